```python
import jax, jax.numpy as jnp
from jax import lax
import numpy as np

D_MODEL = 1024
BATCH = 16
SEQ = 4096
DEPTH = 1

HEAD_DIM = 64
N_HEADS_SB = 8
N_HEADS_FOX = 8
D_SB = N_HEADS_SB * HEAD_DIM
D_FOX = N_HEADS_FOX * HEAD_DIM
D_FF = -(-8 * D_MODEL // (3 * 256)) * 256
Q_BLOCK = 128
RMS_EPS = 1e-6
N_MOD = 6
FORGET_BIAS_INIT = 3.0
IN_SPLITS = (D_SB, 2 * D_SB, 3 * D_SB,
             3 * D_SB + D_FOX, 3 * D_SB + 2 * D_FOX, 3 * D_SB + 3 * D_FOX,
             3 * D_SB + 3 * D_FOX + N_HEADS_FOX,
             3 * D_SB + 3 * D_FOX + N_HEADS_FOX + D_MODEL)
D_IN_PROJ = 3 * D_SB + 3 * D_FOX + N_HEADS_FOX + 2 * D_MODEL

kernel_name = 'hybrid_stickbreak_fox_gated_block'


def rms_norm(x, g):
    xf = x.astype(jnp.float32)
    y = xf * lax.rsqrt(jnp.mean(xf * xf, axis=-1, keepdims=True) + RMS_EPS)
    return (y * g.astype(jnp.float32)).astype(x.dtype)


def to_blocks(t):
    b, s, h, d = t.shape
    return t.reshape(b, s // Q_BLOCK, Q_BLOCK, h, d).transpose(1, 0, 3, 2, 4)


def from_blocks(t):
    nb, b, h, q, d = t.shape
    return t.transpose(1, 0, 3, 2, 4).reshape(b, nb * q, h * d)


def stick_breaking_attention(q, k, v):
    s = q.shape[1]
    nb = s // Q_BLOCK
    kh = k.transpose(0, 2, 1, 3).astype(jnp.float32)
    vh = v.transpose(0, 2, 1, 3).astype(jnp.float32)
    key_pos = jnp.arange(s, dtype=jnp.int32)
    scale = HEAD_DIM ** -0.5

    def block(args):
        q_blk, start = args
        z = jnp.einsum('bhqd,bhkd->bhqk', q_blk.astype(jnp.float32), kh) * scale
        q_pos = start + jnp.arange(Q_BLOCK, dtype=jnp.int32)
        past = key_pos[None, :] < q_pos[:, None]
        log_keep = jnp.where(past, jax.nn.log_sigmoid(-z), 0.0)
        after = lax.cumsum(log_keep, axis=3, reverse=True) - log_keep
        w = jnp.where(past, jnp.exp(jax.nn.log_sigmoid(z) + after), 0.0)
        return jnp.einsum('bhqk,bhkd->bhqd', w, vh)

    starts = jnp.arange(nb, dtype=jnp.int32) * Q_BLOCK
    out = lax.map(block, (to_blocks(q), starts))
    return from_blocks(out).astype(q.dtype)


def forgetting_attention(q, k, v, log_f):
    b, s, h, _ = q.shape
    nb = s // Q_BLOCK
    cum = jnp.cumsum(log_f.astype(jnp.float32), axis=1)
    cum_k = cum.transpose(0, 2, 1)
    cum_q = cum.reshape(b, nb, Q_BLOCK, h).transpose(1, 0, 3, 2)
    kh = k.transpose(0, 2, 1, 3).astype(jnp.float32)
    vh = v.transpose(0, 2, 1, 3).astype(jnp.float32)
    key_pos = jnp.arange(s, dtype=jnp.int32)
    scale = HEAD_DIM ** -0.5

    def block(args):
        q_blk, cq, start = args
        logits = jnp.einsum('bhqd,bhkd->bhqk', q_blk.astype(jnp.float32), kh) * scale
        logits = logits + cq[..., :, None] - cum_k[:, :, None, :]
        q_pos = start + jnp.arange(Q_BLOCK, dtype=jnp.int32)
        causal = key_pos[None, :] <= q_pos[:, None]
        p = jax.nn.softmax(jnp.where(causal, logits, -jnp.inf), axis=-1)
        return jnp.einsum('bhqk,bhkd->bhqd', p, vh)

    starts = jnp.arange(nb, dtype=jnp.int32) * Q_BLOCK
    out = lax.map(block, (to_blocks(q), cum_q, starts))
    return from_blocks(out).astype(q.dtype)


def _fwd_setup_inputs(seed: int = 0) -> dict:
    key = jax.random.key(seed)
    ks = jax.random.split(key, 17)
    f32 = jnp.float32

    def nrm(k, shape, fan_in):
        return jax.random.normal(k, shape, f32) * fan_in ** -0.5

    def gain(k, shape):
        return 1.0 + 0.05 * jax.random.normal(k, shape, f32)

    return {
        'x': jax.random.normal(ks[0], (BATCH, SEQ, D_MODEL), f32),
        'c': jax.random.normal(ks[1], (BATCH, D_MODEL), f32),
        'w_ada': nrm(ks[2], (DEPTH, D_MODEL, N_MOD * D_MODEL), D_MODEL),
        'b_ada': 0.02 * jax.random.normal(ks[3], (DEPTH, N_MOD * D_MODEL), f32),
        'g_mix': gain(ks[4], (DEPTH, D_MODEL)),
        'w_in': nrm(ks[5], (DEPTH, D_MODEL, D_IN_PROJ), D_MODEL),
        'b_forget': FORGET_BIAS_INIT + 0.1 * jax.random.normal(ks[6], (DEPTH, N_HEADS_FOX), f32),
        'b_gate': 0.02 * jax.random.normal(ks[7], (DEPTH, 2 * D_MODEL), f32),
        'w_branch_sb': nrm(ks[8], (DEPTH, D_SB, D_MODEL), D_SB),
        'w_branch_fox': nrm(ks[9], (DEPTH, D_FOX, D_MODEL), D_FOX),
        'w_out': nrm(ks[10], (DEPTH, D_MODEL, D_MODEL), D_MODEL),
        'g_ffn': gain(ks[11], (DEPTH, D_MODEL)),
        'w_ffn_gate': nrm(ks[12], (DEPTH, D_MODEL, D_FF), D_MODEL),
        'w_ffn_up': nrm(ks[13], (DEPTH, D_MODEL, D_FF), D_MODEL),
        'w_ffn_down': nrm(ks[14], (DEPTH, D_FF, D_MODEL), D_FF),
        'g_final': gain(ks[15], (D_MODEL,)),
    }


def _fwd_reference(x, c, w_ada, b_ada, g_mix, w_in, b_forget, b_gate, w_branch_sb, w_branch_fox,
              w_out, g_ffn, w_ffn_gate, w_ffn_up, w_ffn_down, g_final):
    b, s, _ = x.shape
    c_act = jax.nn.silu(c)
    for l in range(DEPTH):
        mod = c_act @ w_ada[l] + b_ada[l]
        shift1, scale1, gate1, shift2, scale2, gate2 = [m[:, None, :] for m in jnp.split(mod, N_MOD, axis=-1)]

        h = rms_norm(x, g_mix[l]) * (1.0 + scale1) + shift1
        proj = h @ w_in[l]
        q_sb, k_sb, v_sb, q_fx, k_fx, v_fx, f_logit, gl_sb, gl_fx = jnp.split(proj, IN_SPLITS, axis=-1)
        heads_sb = lambda t: t.reshape(b, s, N_HEADS_SB, HEAD_DIM)
        heads_fx = lambda t: t.reshape(b, s, N_HEADS_FOX, HEAD_DIM)

        y_sb = stick_breaking_attention(heads_sb(q_sb), heads_sb(k_sb), heads_sb(v_sb))
        log_f = jax.nn.log_sigmoid(f_logit.astype(jnp.float32) + b_forget[l])
        y_fx = forgetting_attention(heads_fx(q_fx), heads_fx(k_fx), heads_fx(v_fx), log_f)

        gates = jax.nn.sigmoid(jnp.concatenate([gl_sb, gl_fx], axis=-1) + b_gate[l])
        g_sb, g_fx = jnp.split(gates, 2, axis=-1)
        merged = g_sb * (y_sb @ w_branch_sb[l]) + g_fx * (y_fx @ w_branch_fox[l])
        x = x + gate1 * (merged @ w_out[l])

        h2 = rms_norm(x, g_ffn[l]) * (1.0 + scale2) + shift2
        ffn = (jax.nn.silu(h2 @ w_ffn_gate[l]) * (h2 @ w_ffn_up[l])) @ w_ffn_down[l]
        x = x + gate2 * ffn
    return rms_norm(x, g_final)


import jax as _jax
import jax.numpy as _jnp

TWIN_FORMAT = 'train_step'
FWD_PARAMS = ['x', 'c', 'w_ada', 'b_ada', 'g_mix', 'w_in', 'b_forget', 'b_gate', 'w_branch_sb', 'w_branch_fox', 'w_out', 'g_ffn', 'w_ffn_gate', 'w_ffn_up', 'w_ffn_down', 'g_final']
TWIN_WEIGHTS = ['w_ada', 'b_ada', 'g_mix', 'w_in', 'b_forget', 'b_gate', 'w_branch_sb', 'w_branch_fox', 'w_out', 'g_ffn', 'w_ffn_gate', 'w_ffn_up', 'w_ffn_down', 'g_final']
TWIN_DIFF_INPUT = 'x'
TWIN_INPUTS = ['x', 'c', 'w_ada', 'b_ada', 'g_mix', 'w_in', 'b_forget', 'b_gate', 'w_branch_sb', 'w_branch_fox', 'w_out', 'g_ffn', 'w_ffn_gate', 'w_ffn_up', 'w_ffn_down', 'g_final', 'loss_target', 'm_w_ada', 'm_b_ada', 'm_g_mix', 'm_w_in', 'm_b_forget', 'm_b_gate', 'm_w_branch_sb', 'm_w_branch_fox', 'm_w_out', 'm_g_ffn', 'm_w_ffn_gate', 'm_w_ffn_up', 'm_w_ffn_down', 'm_g_final', 'v_w_ada', 'v_b_ada', 'v_g_mix', 'v_w_in', 'v_b_forget', 'v_b_gate', 'v_w_branch_sb', 'v_w_branch_fox', 'v_w_out', 'v_g_ffn', 'v_w_ffn_gate', 'v_w_ffn_up', 'v_w_ffn_down', 'v_g_final']
TWIN_OUTPUTS = ['loss', 'grad_x', 'grad_w_ada', 'grad_b_ada', 'grad_g_mix', 'grad_w_in', 'grad_b_forget', 'grad_b_gate', 'grad_w_branch_sb', 'grad_w_branch_fox', 'grad_w_out', 'grad_g_ffn', 'grad_w_ffn_gate', 'grad_w_ffn_up', 'grad_w_ffn_down', 'grad_g_final', 'delta_w_ada', 'delta_b_ada', 'delta_g_mix', 'delta_w_in', 'delta_b_forget', 'delta_b_gate', 'delta_w_branch_sb', 'delta_w_branch_fox', 'delta_w_out', 'delta_g_ffn', 'delta_w_ffn_gate', 'delta_w_ffn_up', 'delta_w_ffn_down', 'delta_g_final', 'new_m_w_ada', 'new_m_b_ada', 'new_m_g_mix', 'new_m_w_in', 'new_m_b_forget', 'new_m_b_gate', 'new_m_w_branch_sb', 'new_m_w_branch_fox', 'new_m_w_out', 'new_m_g_ffn', 'new_m_w_ffn_gate', 'new_m_w_ffn_up', 'new_m_w_ffn_down', 'new_m_g_final', 'new_v_w_ada', 'new_v_b_ada', 'new_v_g_mix', 'new_v_w_in', 'new_v_b_forget', 'new_v_b_gate', 'new_v_w_branch_sb', 'new_v_w_branch_fox', 'new_v_w_out', 'new_v_g_ffn', 'new_v_w_ffn_gate', 'new_v_w_ffn_up', 'new_v_w_ffn_down', 'new_v_g_final']
TWIN_LEAF_KINDS = {'loss': 'loss', 'grad_x': 'grad_x', 'grad_w_ada': 'grad_w', 'grad_b_ada': 'grad_w', 'grad_g_mix': 'grad_w', 'grad_w_in': 'grad_w', 'grad_b_forget': 'grad_w', 'grad_b_gate': 'grad_w', 'grad_w_branch_sb': 'grad_w', 'grad_w_branch_fox': 'grad_w', 'grad_w_out': 'grad_w', 'grad_g_ffn': 'grad_w', 'grad_w_ffn_gate': 'grad_w', 'grad_w_ffn_up': 'grad_w', 'grad_w_ffn_down': 'grad_w', 'grad_g_final': 'grad_w', 'delta_w_ada': 'delta_w', 'delta_b_ada': 'delta_w', 'delta_g_mix': 'delta_w', 'delta_w_in': 'delta_w', 'delta_b_forget': 'delta_w', 'delta_b_gate': 'delta_w', 'delta_w_branch_sb': 'delta_w', 'delta_w_branch_fox': 'delta_w', 'delta_w_out': 'delta_w', 'delta_g_ffn': 'delta_w', 'delta_w_ffn_gate': 'delta_w', 'delta_w_ffn_up': 'delta_w', 'delta_w_ffn_down': 'delta_w', 'delta_g_final': 'delta_w', 'new_m_w_ada': 'new_m', 'new_m_b_ada': 'new_m', 'new_m_g_mix': 'new_m', 'new_m_w_in': 'new_m', 'new_m_b_forget': 'new_m', 'new_m_b_gate': 'new_m', 'new_m_w_branch_sb': 'new_m', 'new_m_w_branch_fox': 'new_m', 'new_m_w_out': 'new_m', 'new_m_g_ffn': 'new_m', 'new_m_w_ffn_gate': 'new_m', 'new_m_w_ffn_up': 'new_m', 'new_m_w_ffn_down': 'new_m', 'new_m_g_final': 'new_m', 'new_v_w_ada': 'new_v', 'new_v_b_ada': 'new_v', 'new_v_g_mix': 'new_v', 'new_v_w_in': 'new_v', 'new_v_b_forget': 'new_v', 'new_v_b_gate': 'new_v', 'new_v_w_branch_sb': 'new_v', 'new_v_w_branch_fox': 'new_v', 'new_v_w_out': 'new_v', 'new_v_g_ffn': 'new_v', 'new_v_w_ffn_gate': 'new_v', 'new_v_w_ffn_up': 'new_v', 'new_v_w_ffn_down': 'new_v', 'new_v_g_final': 'new_v'}


def _forward(args):
    return _fwd_reference(*[args[k] for k in FWD_PARAMS])


def _output_shape():
    out = _jax.eval_shape(lambda: _forward(_fwd_setup_inputs(0)))
    return out.shape, out.dtype

N_MICROBATCH = 1
ADAM_LR = 0.001
ADAM_B1 = 0.9
ADAM_B2 = 0.999
ADAM_EPS = 1e-08
ADAM_WD = 0.01
ADAM_STEP = 10
PER_EXAMPLE_BATCH_AXIS = {'x': 0, 'c': 0, 'loss_target': 0}
SHARED_INPUTS = []
_WEIGHT_DTYPES = {'w_ada': _jnp.float32, 'b_ada': _jnp.float32, 'g_mix': _jnp.float32, 'w_in': _jnp.float32, 'b_forget': _jnp.float32, 'b_gate': _jnp.float32, 'w_branch_sb': _jnp.float32, 'w_branch_fox': _jnp.float32, 'w_out': _jnp.float32, 'g_ffn': _jnp.float32, 'w_ffn_gate': _jnp.float32, 'w_ffn_up': _jnp.float32, 'w_ffn_down': _jnp.float32, 'g_final': _jnp.float32}
MOMENT_SCALE = {'w_ada': 2.650588e-01, 'b_ada': 5.080729e-01, 'g_mix': 1.381297e-01, 'w_in': 8.855584e-02, 'b_forget': 1.879426e-01, 'b_gate': 4.287328e-02, 'w_branch_sb': 1.247815e-01, 'w_branch_fox': 1.099068e-01, 'w_out': 1.669325e-01, 'g_ffn': 1.760736e-01, 'w_ffn_gate': 8.987796e-02, 'w_ffn_up': 9.213137e-02, 'w_ffn_down': 1.554428e-01, 'g_final': 6.599273e+01}


def _to_microbatches(a, axis):
    t = _jnp.moveaxis(a, axis, 0)
    t = t.reshape((N_MICROBATCH, t.shape[0] // N_MICROBATCH) + t.shape[1:])
    return _jnp.moveaxis(t, 1, axis + 1)


def setup_inputs(seed: int = 0) -> dict:
    inp = _fwd_setup_inputs(seed)
    key = _jax.random.fold_in(_jax.random.key(seed), 7919)
    shape, _ = _output_shape()
    out = dict(inp)
    out["loss_target"] = _jax.random.normal(_jax.random.fold_in(key, 0), shape, _jnp.float32)
    for i, name in enumerate(TWIN_WEIGHTS):
        w = inp[name].astype(_jnp.float32)
        if MOMENT_SCALE is None:
            s = _jnp.sqrt(_jnp.mean(_jnp.square(w)) + 1e-30)
        else:
            s = MOMENT_SCALE[name]
        km, kv = _jax.random.split(_jax.random.fold_in(key, i + 1))
        out[name] = w
        out["m_" + name] = s * _jax.random.normal(km, w.shape, _jnp.float32)
        out["v_" + name] = (s * s) * _jax.random.uniform(kv, w.shape, _jnp.float32, 0.5, 1.5)
    if N_MICROBATCH > 1:
        for name, axis in PER_EXAMPLE_BATCH_AXIS.items():
            out[name] = _to_microbatches(out[name], axis)
    return {'x': out['x'], 'c': out['c'], 'w_ada': out['w_ada'], 'b_ada': out['b_ada'], 'g_mix': out['g_mix'], 'w_in': out['w_in'], 'b_forget': out['b_forget'], 'b_gate': out['b_gate'], 'w_branch_sb': out['w_branch_sb'], 'w_branch_fox': out['w_branch_fox'], 'w_out': out['w_out'], 'g_ffn': out['g_ffn'], 'w_ffn_gate': out['w_ffn_gate'], 'w_ffn_up': out['w_ffn_up'], 'w_ffn_down': out['w_ffn_down'], 'g_final': out['g_final'], 'loss_target': out['loss_target'], 'm_w_ada': out['m_w_ada'], 'm_b_ada': out['m_b_ada'], 'm_g_mix': out['m_g_mix'], 'm_w_in': out['m_w_in'], 'm_b_forget': out['m_b_forget'], 'm_b_gate': out['m_b_gate'], 'm_w_branch_sb': out['m_w_branch_sb'], 'm_w_branch_fox': out['m_w_branch_fox'], 'm_w_out': out['m_w_out'], 'm_g_ffn': out['m_g_ffn'], 'm_w_ffn_gate': out['m_w_ffn_gate'], 'm_w_ffn_up': out['m_w_ffn_up'], 'm_w_ffn_down': out['m_w_ffn_down'], 'm_g_final': out['m_g_final'], 'v_w_ada': out['v_w_ada'], 'v_b_ada': out['v_b_ada'], 'v_g_mix': out['v_g_mix'], 'v_w_in': out['v_w_in'], 'v_b_forget': out['v_b_forget'], 'v_b_gate': out['v_b_gate'], 'v_w_branch_sb': out['v_w_branch_sb'], 'v_w_branch_fox': out['v_w_branch_fox'], 'v_w_out': out['v_w_out'], 'v_g_ffn': out['v_g_ffn'], 'v_w_ffn_gate': out['v_w_ffn_gate'], 'v_w_ffn_up': out['v_w_ffn_up'], 'v_w_ffn_down': out['v_w_ffn_down'], 'v_g_final': out['v_g_final']}


def _loss(weights, diff, rest, loss_target):
    with _jax.named_scope("forward"):
        args = {**rest, TWIN_DIFF_INPUT: diff, **{k: w.astype(_WEIGHT_DTYPES[k]) for k, w in weights.items()}}
        y = _forward(args)
    with _jax.named_scope("loss_head"):
        err = _jnp.square(y.astype(_jnp.float32) - loss_target)
        return 0.5 * _jnp.sum(_jnp.mean(err, axis=-1)) if err.ndim else 0.5 * err


def _adamw(w, g, m, v):
    m = ADAM_B1 * m + (1.0 - ADAM_B1) * g
    v = ADAM_B2 * v + (1.0 - ADAM_B2) * _jnp.square(g)
    m_hat = m / (1.0 - ADAM_B1 ** ADAM_STEP)
    v_hat = v / (1.0 - ADAM_B2 ** ADAM_STEP)
    delta = -ADAM_LR * (m_hat / (_jnp.sqrt(v_hat) + ADAM_EPS) + ADAM_WD * w)
    return delta, m, v


def reference(x, c, w_ada, b_ada, g_mix, w_in, b_forget, b_gate, w_branch_sb, w_branch_fox, w_out, g_ffn, w_ffn_gate, w_ffn_up, w_ffn_down, g_final, loss_target, m_w_ada, m_b_ada, m_g_mix, m_w_in, m_b_forget, m_b_gate, m_w_branch_sb, m_w_branch_fox, m_w_out, m_g_ffn, m_w_ffn_gate, m_w_ffn_up, m_w_ffn_down, m_g_final, v_w_ada, v_b_ada, v_g_mix, v_w_in, v_b_forget, v_b_gate, v_w_branch_sb, v_w_branch_fox, v_w_out, v_g_ffn, v_w_ffn_gate, v_w_ffn_up, v_w_ffn_down, v_g_final):
    given = dict(x=x, c=c, w_ada=w_ada, b_ada=b_ada, g_mix=g_mix, w_in=w_in, b_forget=b_forget, b_gate=b_gate, w_branch_sb=w_branch_sb, w_branch_fox=w_branch_fox, w_out=w_out, g_ffn=g_ffn, w_ffn_gate=w_ffn_gate, w_ffn_up=w_ffn_up, w_ffn_down=w_ffn_down, g_final=g_final, loss_target=loss_target, m_w_ada=m_w_ada, m_b_ada=m_b_ada, m_g_mix=m_g_mix, m_w_in=m_w_in, m_b_forget=m_b_forget, m_b_gate=m_b_gate, m_w_branch_sb=m_w_branch_sb, m_w_branch_fox=m_w_branch_fox, m_w_out=m_w_out, m_g_ffn=m_g_ffn, m_w_ffn_gate=m_w_ffn_gate, m_w_ffn_up=m_w_ffn_up, m_w_ffn_down=m_w_ffn_down, m_g_final=m_g_final, v_w_ada=v_w_ada, v_b_ada=v_b_ada, v_g_mix=v_g_mix, v_w_in=v_w_in, v_b_forget=v_b_forget, v_b_gate=v_b_gate, v_w_branch_sb=v_w_branch_sb, v_w_branch_fox=v_w_branch_fox, v_w_out=v_w_out, v_g_ffn=v_g_ffn, v_w_ffn_gate=v_w_ffn_gate, v_w_ffn_up=v_w_ffn_up, v_w_ffn_down=v_w_ffn_down, v_g_final=v_g_final)
    weights = {n: given[n] for n in TWIN_WEIGHTS}
    shared = {n: given[n] for n in SHARED_INPUTS}
    per_example = {n: given[n] for n in ['x', 'c']}
    grad_fn = _jax.value_and_grad(_loss, argnums=(0, 1))

    def one_microbatch(ex, loss_target):
        ex = dict(ex)
        diff = ex.pop(TWIN_DIFF_INPUT)
        return grad_fn(weights, diff, {**shared, **ex}, loss_target)

    if N_MICROBATCH == 1:
        loss, (grad_w, grad_x) = one_microbatch(per_example, given["loss_target"])
    else:
        def body(carry, xs):
            loss_sum, grad_sum = carry
            l_k, (gw_k, gx_k) = one_microbatch(xs[0], xs[1])
            with _jax.named_scope("update"):
                return (loss_sum + l_k, _jax.tree.map(_jnp.add, grad_sum, gw_k)), gx_k

        init = (_jnp.zeros((), _jnp.float32), _jax.tree.map(_jnp.zeros_like, weights))
        (loss, grad_w), grad_x = _jax.lax.scan(body, init, (per_example, given["loss_target"]))
    with _jax.named_scope("update"):
        delta_w, new_m, new_v = {}, {}, {}
        for n in TWIN_WEIGHTS:
            delta_w[n], new_m[n], new_v[n] = _adamw(weights[n], grad_w[n], given["m_" + n], given["v_" + n])
    return (loss, grad_x, *[grad_w[n] for n in TWIN_WEIGHTS], *[delta_w[n] for n in TWIN_WEIGHTS],
            *[new_m[n] for n in TWIN_WEIGHTS], *[new_v[n] for n in TWIN_WEIGHTS])
```

```python
import jax
import jax.numpy as jnp
from jax import lax
from jax.experimental import pallas as pl
from jax.experimental.pallas import tpu as pltpu

F32 = jnp.float32
BF16 = jnp.bfloat16
HEAD_DIM = 64
LANES = 128
N_DEV = 8
AXES = ("x", "y", "c")
RMS_EPS = 1e-6
ADAM_LR, ADAM_B1, ADAM_B2, ADAM_EPS, ADAM_WD, ADAM_STEP = 0.001, 0.9, 0.999, 1e-08, 0.01, 10
NEG = -1e30
MESH = pl.DeviceIdType.MESH
ADAM_ROWS = 512


def _pick(n, cands):
    for c in cands:
        if n % c == 0:
            return c
    raise ValueError(f"no tile for {n} in {cands}")


def _arb(n):
    return pltpu.CompilerParams(dimension_semantics=("arbitrary",) * n)


def _dot(a, b):
    return jnp.dot(a, b, preferred_element_type=F32)


def _dot_nt(a, b):
    return lax.dot_general(a, b, (((1,), (1,)), ((), ())), preferred_element_type=F32)


def _dot_tn(a, b):
    return lax.dot_general(a, b, (((0,), (0,)), ((), ())), preferred_element_type=F32)


def _split2(v):
    hi = v.astype(BF16)
    return hi, (v - hi.astype(F32)).astype(BF16)


def _dot_split2(v, m):
    hi, lo = _split2(v)
    return _dot(hi, m) + _dot(lo, m)


def _tri_dot3(m, v):
    h1 = v.astype(BF16)
    r1 = v - h1.astype(F32)
    h2 = r1.astype(BF16)
    h3 = (r1 - h2.astype(F32)).astype(BF16)
    return _dot(m, h1) + _dot(m, h2) + _dot(m, h3)


def _sigmoid(v):
    return 1.0 / (1.0 + jnp.exp(-v))


def _silu(v):
    return v * _sigmoid(v)


def _mm(pairs, out_dtype, name, tm=512, tn=512):
    m, n = pairs[0][0].shape[0], pairs[0][1].shape[1]
    tm = _pick(m, (tm, 256, 128, 64, 32, 16, 8))
    tn = _pick(n, (tn, 256, 128))
    npairs = len(pairs)

    def body(*refs):
        acc = None
        for t in range(npairs):
            p = _dot(refs[2 * t][...].astype(BF16), refs[2 * t + 1][...].astype(BF16))
            acc = p if acc is None else acc + p
        refs[2 * npairs][...] = acc.astype(out_dtype)

    in_specs, args = [], []
    for a, b in pairs:
        k = a.shape[1]
        in_specs += [pl.BlockSpec((tm, k), lambda j, i: (i, 0)), pl.BlockSpec((k, tn), lambda j, i: (0, j))]
        args += [a, b]
    return pl.pallas_call(
        body, grid=(n // tn, m // tm), in_specs=in_specs,
        out_specs=pl.BlockSpec((tm, tn), lambda j, i: (i, j)),
        out_shape=jax.ShapeDtypeStruct((m, n), out_dtype), name=name, compiler_params=_arb(2),
    )(*args)


def _mm_tn(a, b, name, tm=512, tn=512, tk=1024):
    t, m = a.shape
    n = b.shape[1]
    tm, tn, tk = _pick(m, (tm, 256, 128)), _pick(n, (tn, 256, 128)), _pick(t, (tk, 512, 256, 128))

    def body(a_ref, b_ref, o_ref):
        @pl.when(pl.program_id(2) == 0)
        def _():
            o_ref[...] = jnp.zeros_like(o_ref)

        o_ref[...] += _dot_tn(a_ref[...].astype(BF16), b_ref[...].astype(BF16))

    return pl.pallas_call(
        body, grid=(m // tm, n // tn, t // tk),
        in_specs=[pl.BlockSpec((tk, tm), lambda i, j, k: (k, i)), pl.BlockSpec((tk, tn), lambda i, j, k: (k, j))],
        out_specs=pl.BlockSpec((tm, tn), lambda i, j, k: (i, j)),
        out_shape=jax.ShapeDtypeStruct((m, n), F32), name=name, compiler_params=_arb(3),
    )(a, b)


def _rowmap(fn, ins, outs, name, ts=512):
    bsz, seq = next(a.shape[:2] for a in ins if a.ndim == 3 and a.shape[1] != 1)
    ts = _pick(seq, (ts, 256, 128, 64, 32, 16, 8))
    n_in = len(ins)

    def in_spec(a):
        if a.ndim == 2:
            return pl.BlockSpec(a.shape, lambda b, s: (0, 0))
        if a.shape[1] == 1:
            return pl.BlockSpec((1, 1, a.shape[2]), lambda b, s: (b, 0, 0))
        return pl.BlockSpec((1, ts, a.shape[2]), lambda b, s: (b, s, 0))

    def out_spec(kind, w):
        if kind == "row":
            return pl.BlockSpec((1, ts, w), lambda b, s: (b, s, 0))
        if kind == "batch":
            return pl.BlockSpec((1, 1, w), lambda b, s: (b, 0, 0))
        return pl.BlockSpec((1, w), lambda b, s: (0, 0))

    def out_shape(kind, w, dt):
        shp = {"row": (bsz, seq, w), "batch": (bsz, 1, w), "global": (1, w)}[kind]
        return jax.ShapeDtypeStruct(shp, dt)

    def body(*refs):
        b, s = pl.program_id(0), pl.program_id(1)
        vals = [r[...] if a.ndim == 2 else r[0] for r, a in zip(refs[:n_in], ins)]
        res = fn(*vals)
        for o_ref, (kind, _, dt), v in zip(refs[n_in:], outs, res):
            if kind == "row":
                o_ref[0] = v.astype(dt)
            elif kind == "batch":
                @pl.when(s == 0)
                def _():
                    o_ref[...] = jnp.zeros_like(o_ref)

                o_ref[0] += v
            else:
                @pl.when((s == 0) & (b == 0))
                def _():
                    o_ref[...] = jnp.zeros_like(o_ref)

                o_ref[...] += v

    return pl.pallas_call(
        body, grid=(bsz, seq // ts), in_specs=[in_spec(a) for a in ins],
        out_specs=[out_spec(k, w) for k, w, _ in outs],
        out_shape=[out_shape(*o) for o in outs], name=name, compiler_params=_arb(2),
    )(*ins)


def _small(fn, ins, out_shapes, name):
    n_in = len(ins)

    def body(*refs):
        res = fn(*[r[...] for r in refs[:n_in]])
        for o_ref, v in zip(refs[n_in:], res):
            o_ref[...] = v

    return pl.pallas_call(body, out_shape=[jax.ShapeDtypeStruct(s, d) for s, d in out_shapes], name=name)(*ins)


def _mesh_pos():
    mx, my, mc = lax.axis_index("x"), lax.axis_index("y"), lax.axis_index("c")
    return mx, my, mc, 4 * mx + 2 * my + mc


def _peer(mx, my, mc, k):
    px = 1 - mx if k & 4 else mx
    py = 1 - my if k & 2 else my
    pc = 1 - mc if k & 1 else mc
    return (px, py, pc), 4 * px + 2 * py + pc


def _exchange(x, gather, name):
    blk = x.shape if gather else x.shape[1:]

    def body(x_ref, out_ref, send_sems, recv_sems, local_sem):
        mx, my, mc, me = _mesh_pos()
        own = pltpu.make_async_copy(x_ref if gather else x_ref.at[me], out_ref.at[me], local_sem)
        own.start()
        sends, recvs = [], []
        for k in range(1, N_DEV):
            peer, pid = _peer(mx, my, mc, k)
            src = x_ref if gather else x_ref.at[pid]
            sends.append(pltpu.make_async_remote_copy(
                src_ref=src, dst_ref=out_ref.at[me], send_sem=send_sems.at[k], recv_sem=recv_sems.at[k],
                device_id=peer, device_id_type=MESH))
            recvs.append(pltpu.make_async_remote_copy(
                src_ref=src, dst_ref=out_ref.at[pid], send_sem=send_sems.at[k], recv_sem=recv_sems.at[k],
                device_id=peer, device_id_type=MESH))
        for cp in sends:
            cp.start()
        for cp in recvs:
            cp.wait_recv()
        for cp in sends:
            cp.wait_send()
        own.wait()

    return pl.pallas_call(
        body, out_shape=jax.ShapeDtypeStruct((N_DEV,) + tuple(blk), x.dtype),
        in_specs=[pl.BlockSpec(memory_space=pl.ANY)], out_specs=pl.BlockSpec(memory_space=pl.ANY),
        scratch_shapes=[pltpu.SemaphoreType.DMA((N_DEV,)), pltpu.SemaphoreType.DMA((N_DEV,)), pltpu.SemaphoreType.DMA],
        name=name,
    )(x)


def _head_masks():
    lane = lax.broadcasted_iota(jnp.int32, (1, LANES), 1)
    return [lane < HEAD_DIM, lane >= HEAD_DIM]


def _tile_iotas(tq, tk):
    return (lax.broadcasted_iota(jnp.int32, (tq, tk), 0), lax.broadcasted_iota(jnp.int32, (tq, tk), 1))


def _tri(tk, cmp):
    r = lax.broadcasted_iota(jnp.int32, (tk, tk), 0)
    c = lax.broadcasted_iota(jnp.int32, (tk, tk), 1)
    return jnp.where(cmp(r, c), 1.0, 0.0).astype(BF16)


def _neg_softplus(z):
    return -(jnp.maximum(z, 0.0) + jnp.log(1.0 + jnp.exp(-jnp.abs(z))))


def _pair_specs(tq, seq, np_, off):
    return [pl.BlockSpec((1, tq, LANES), lambda b, p, i: (b, i, off + p)),
            pl.BlockSpec((1, seq, LANES), lambda b, p, i: (b, 0, off + np_ + p)),
            pl.BlockSpec((1, seq, LANES), lambda b, p, i: (b, 0, off + 2 * np_ + p))]


def _sb_fwd(qkv, np_, tq, tk, name):
    bsz, seq, _ = qkv.shape

    def body(q_ref, k_ref, v_ref, y_ref, tot_ref):
        i = pl.program_id(2)
        r, c = _tile_iotas(tq, tk)
        t_idx = i * tq + r
        msuf = _tri(tk, lambda a, b: a > b)
        q = q_ref[0]
        nkb = ((i + 1) * tq) // tk
        accs, tots = [], []
        for hm in _head_masks():
            qm = jnp.where(hm, q, jnp.zeros_like(q))

            def step(jj, carry, qm=qm):
                a_c, acc = carry
                k0 = pl.multiple_of((nkb - 1 - jj) * tk, tk)
                kb = k_ref[0, pl.ds(k0, tk), :]
                vb = v_ref[0, pl.ds(k0, tk), :]
                z = _dot_nt(qm, kb) * (HEAD_DIM ** -0.5)
                valid = (k0 + c) < t_idx
                lk = jnp.where(valid, _neg_softplus(z), 0.0)
                logw = z + lk + _dot_split2(lk, msuf) + a_c
                w = jnp.exp(jnp.where(valid, logw, NEG))
                acc = acc + _dot(w.astype(BF16), vb)
                return a_c + jnp.sum(lk, axis=1, keepdims=True), acc

            a_c, acc = lax.fori_loop(0, nkb, step, (jnp.zeros((tq, 1), F32), jnp.zeros((tq, LANES), F32)))
            accs.append(acc)
            tots.append(a_c)
        lo = _head_masks()[0]
        y_ref[0] = jnp.where(lo, accs[0], accs[1])
        tot_ref[0, 0] = jnp.where(lo, tots[0], tots[1])

    return pl.pallas_call(
        body, grid=(bsz, np_, seq // tq), in_specs=_pair_specs(tq, seq, np_, 0),
        out_specs=[pl.BlockSpec((1, tq, LANES), lambda b, p, i: (b, i, p)),
                   pl.BlockSpec((1, 1, tq, LANES), lambda b, p, i: (b, p, i, 0))],
        out_shape=[jax.ShapeDtypeStruct((bsz, seq, np_ * LANES), F32),
                   jax.ShapeDtypeStruct((bsz, np_, seq, LANES), F32)],
        name=name, compiler_params=_arb(3),
    )(qkv, qkv, qkv)


def _sb_bwd(qkv, dy, tot, np_, tq, tk, name):
    bsz, seq, _ = qkv.shape
    nq = seq // tq

    def body(q_ref, k_ref, v_ref, dy_ref, tot_ref, dq_ref, dk_ref, dv_ref, dk_acc, dv_acc):
        i = pl.program_id(2)

        @pl.when(i == 0)
        def _():
            dk_acc[...] = jnp.zeros_like(dk_acc)
            dv_acc[...] = jnp.zeros_like(dv_acc)

        r, c = _tile_iotas(tq, tk)
        t_idx = i * tq + r
        mincl = _tri(tk, lambda a, b: a <= b)
        mexcl = _tri(tk, lambda a, b: a < b)
        q = q_ref[0]
        dyb = dy_ref[0].astype(BF16)
        tot_all = tot_ref[0, 0]
        nkb = ((i + 1) * tq) // tk
        dqs = []
        for h, hm in enumerate(_head_masks()):
            qm = jnp.where(hm, q, jnp.zeros_like(q))
            dym = jnp.where(hm, dyb, jnp.zeros_like(dyb))
            tot_h = tot_all[:, h * HEAD_DIM:h * HEAD_DIM + 1]

            def step(j, carry, qm=qm, dym=dym, tot_h=tot_h):
                c_l, c_g, dq = carry
                k0 = pl.multiple_of(j * tk, tk)
                kb = k_ref[0, pl.ds(k0, tk), :]
                vb = v_ref[0, pl.ds(k0, tk), :]
                z = _dot_nt(qm, kb) * (HEAD_DIM ** -0.5)
                valid = (k0 + c) < t_idx
                lk = jnp.where(valid, _neg_softplus(z), 0.0)
                after = tot_h - c_l - _dot_split2(lk, mincl)
                w = jnp.exp(jnp.where(valid, z + lk + after, NEG))
                g = w * _dot_nt(dym, vb)
                g_before = c_g + _dot_split2(g, mexcl)
                beta = jnp.exp(jnp.minimum(z + lk, 0.0))
                dz = jnp.where(valid, g - beta * (g + g_before), 0.0) * (HEAD_DIM ** -0.5)
                dzb = dz.astype(BF16)
                dk_acc[pl.ds(k0, tk), :] += _dot_tn(dzb, qm)
                dv_acc[pl.ds(k0, tk), :] += _dot_tn(w.astype(BF16), dym)
                return (c_l + jnp.sum(lk, axis=1, keepdims=True), c_g + jnp.sum(g, axis=1, keepdims=True),
                        dq + _dot(dzb, kb))

            zero = jnp.zeros((tq, 1), F32)
            _, _, dq = lax.fori_loop(0, nkb, step, (zero, zero, jnp.zeros((tq, LANES), F32)))
            dqs.append(dq)
        dq_ref[0] = jnp.where(_head_masks()[0], dqs[0], dqs[1]).astype(BF16)

        @pl.when(i == nq - 1)
        def _():
            dk_ref[0] = dk_acc[...].astype(BF16)
            dv_ref[0] = dv_acc[...].astype(BF16)

    tile = pl.BlockSpec((1, tq, LANES), lambda b, p, i: (b, i, p))
    whole = pl.BlockSpec((1, seq, LANES), lambda b, p, i: (b, 0, p))
    out = jax.ShapeDtypeStruct((bsz, seq, np_ * LANES), BF16)
    return pl.pallas_call(
        body, grid=(bsz, np_, nq),
        in_specs=_pair_specs(tq, seq, np_, 0) + [tile, pl.BlockSpec((1, 1, tq, LANES), lambda b, p, i: (b, p, i, 0))],
        out_specs=[tile, whole, whole], out_shape=[out, out, out],
        scratch_shapes=[pltpu.VMEM((seq, LANES), F32), pltpu.VMEM((seq, LANES), F32)],
        name=name, compiler_params=_arb(3),
    )(qkv, qkv, qkv, dy, tot)


def _fox_fwd(qkv, cumq, cumk, np_, tq, tk, name):
    bsz, seq, _ = qkv.shape

    def body(q_ref, k_ref, v_ref, cq_ref, ck_ref, y_ref, lse_ref):
        i = pl.program_id(2)
        r, c = _tile_iotas(tq, tk)
        t_idx = i * tq + r
        q = q_ref[0]
        cq_all = cq_ref[0, 0]
        nkb = ((i + 1) * tq) // tk
        accs, lses = [], []
        for h, hm in enumerate(_head_masks()):
            qm = jnp.where(hm, q, jnp.zeros_like(q))
            cq = cq_all[:, h * HEAD_DIM:h * HEAD_DIM + 1]

            def step(j, carry, qm=qm, cq=cq, h=h):
                m, l, acc = carry
                k0 = pl.multiple_of(j * tk, tk)
                kb = k_ref[0, pl.ds(k0, tk), :]
                vb = v_ref[0, pl.ds(k0, tk), :]
                ck = ck_ref[0, 0, h:h + 1, pl.ds(k0, tk)]
                s = _dot_nt(qm, kb) * (HEAD_DIM ** -0.5) + (cq - ck)
                s = jnp.where((k0 + c) <= t_idx, s, NEG)
                m_new = jnp.maximum(m, jnp.max(s, axis=1, keepdims=True))
                p = jnp.exp(s - m_new)
                alpha = jnp.exp(m - m_new)
                return (m_new, alpha * l + jnp.sum(p, axis=1, keepdims=True),
                        alpha * acc + _dot(p.astype(BF16), vb))

            m, l, acc = lax.fori_loop(0, nkb, step, (jnp.full((tq, 1), NEG, F32), jnp.zeros((tq, 1), F32),
                                                      jnp.zeros((tq, LANES), F32)))
            accs.append(acc / l)
            lses.append(m + jnp.log(l))
        lo = _head_masks()[0]
        y_ref[0] = jnp.where(lo, accs[0], accs[1])
        lse_ref[0, 0] = jnp.where(lo, lses[0], lses[1])

    row4 = pl.BlockSpec((1, 1, tq, LANES), lambda b, p, i: (b, p, i, 0))
    return pl.pallas_call(
        body, grid=(bsz, np_, seq // tq),
        in_specs=_pair_specs(tq, seq, np_, 3 * np_) + [row4, pl.BlockSpec((1, 1, 8, seq), lambda b, p, i: (b, p, 0, 0))],
        out_specs=[pl.BlockSpec((1, tq, LANES), lambda b, p, i: (b, i, p)), row4],
        out_shape=[jax.ShapeDtypeStruct((bsz, seq, np_ * LANES), F32),
                   jax.ShapeDtypeStruct((bsz, np_, seq, LANES), F32)],
        name=name, compiler_params=_arb(3),
    )(qkv, qkv, qkv, cumq, cumk)


def _fox_bwd(qkv, dy, y, lse, cumq, cumk, np_, tq, tk, name):
    bsz, seq, _ = qkv.shape
    nq = seq // tq

    def body(q_ref, k_ref, v_ref, dy_ref, y_ref, lse_ref, cq_ref, ck_ref,
             dq_ref, dk_ref, dv_ref, dck_ref, dcq_ref, dk_acc, dv_acc, dck_acc):
        i = pl.program_id(2)

        @pl.when(i == 0)
        def _():
            dk_acc[...] = jnp.zeros_like(dk_acc)
            dv_acc[...] = jnp.zeros_like(dv_acc)
            dck_acc[...] = jnp.zeros_like(dck_acc)

        r, c = _tile_iotas(tq, tk)
        t_idx = i * tq + r
        q = q_ref[0]
        dyf = dy_ref[0]
        dyb = dyf.astype(BF16)
        dyy = dyf * y_ref[0]
        cq_all = cq_ref[0, 0]
        lse_all = lse_ref[0, 0]
        nkb = ((i + 1) * tq) // tk
        dqs, rows = [], []
        for h, hm in enumerate(_head_masks()):
            qm = jnp.where(hm, q, jnp.zeros_like(q))
            dym = jnp.where(hm, dyb, jnp.zeros_like(dyb))
            delta = jnp.sum(jnp.where(hm, dyy, 0.0), axis=1, keepdims=True)
            cq = cq_all[:, h * HEAD_DIM:h * HEAD_DIM + 1]
            lse_h = lse_all[:, h * HEAD_DIM:h * HEAD_DIM + 1]

            def step(j, carry, qm=qm, dym=dym, delta=delta, cq=cq, lse_h=lse_h, h=h):
                dq, row = carry
                k0 = pl.multiple_of(j * tk, tk)
                kb = k_ref[0, pl.ds(k0, tk), :]
                vb = v_ref[0, pl.ds(k0, tk), :]
                ck = ck_ref[0, 0, h:h + 1, pl.ds(k0, tk)]
                s = _dot_nt(qm, kb) * (HEAD_DIM ** -0.5) + (cq - ck)
                s = jnp.where((k0 + c) <= t_idx, s, NEG)
                p = jnp.exp(s - lse_h)
                ds = p * (_dot_nt(dym, vb) - delta)
                dsb = (ds * (HEAD_DIM ** -0.5)).astype(BF16)
                dk_acc[pl.ds(k0, tk), :] += _dot_tn(dsb, qm)
                dv_acc[pl.ds(k0, tk), :] += _dot_tn(p.astype(BF16), dym)
                dck_acc[h:h + 1, pl.ds(k0, tk)] += -jnp.sum(ds, axis=0, keepdims=True)
                return dq + _dot(dsb, kb), row + jnp.sum(ds, axis=1, keepdims=True)

            dq, row = lax.fori_loop(0, nkb, step, (jnp.zeros((tq, LANES), F32), jnp.zeros((tq, 1), F32)))
            dqs.append(dq)
            rows.append(row)
        lo = _head_masks()[0]
        dq_ref[0] = jnp.where(lo, dqs[0], dqs[1]).astype(BF16)
        dcq_ref[0, 0] = jnp.where(lo, rows[0], rows[1])

        @pl.when(i == nq - 1)
        def _():
            dk_ref[0] = dk_acc[...].astype(BF16)
            dv_ref[0] = dv_acc[...].astype(BF16)
            dck_ref[0, 0] = dck_acc[...]

    tile = pl.BlockSpec((1, tq, LANES), lambda b, p, i: (b, i, p))
    whole = pl.BlockSpec((1, seq, LANES), lambda b, p, i: (b, 0, p))
    row4 = pl.BlockSpec((1, 1, tq, LANES), lambda b, p, i: (b, p, i, 0))
    key4 = pl.BlockSpec((1, 1, 8, seq), lambda b, p, i: (b, p, 0, 0))
    out = jax.ShapeDtypeStruct((bsz, seq, np_ * LANES), BF16)
    return pl.pallas_call(
        body, grid=(bsz, np_, nq),
        in_specs=_pair_specs(tq, seq, np_, 3 * np_) + [tile, tile, row4, row4, key4],
        out_specs=[tile, whole, whole, key4, row4],
        out_shape=[out, out, out, jax.ShapeDtypeStruct((bsz, np_, 8, seq), F32),
                   jax.ShapeDtypeStruct((bsz, np_, seq, LANES), F32)],
        scratch_shapes=[pltpu.VMEM((seq, LANES), F32), pltpu.VMEM((seq, LANES), F32), pltpu.VMEM((8, seq), F32)],
        name=name, compiler_params=_arb(3),
    )(qkv, qkv, qkv, dy, y, lse, cumq, cumk)


def _cum_fwd(fl, bf, name, tb=256):
    bsz, seq, _ = fl.shape
    tb = _pick(seq, (tb, 128))

    def body(fl_ref, bf_ref, o_ref):
        tri = _tri(tb, lambda a, b: b <= a)

        def step(j, carry):
            r0 = pl.multiple_of(j * tb, tb)
            blk = _tri_dot3(tri, _neg_softplus(-(fl_ref[0, pl.ds(r0, tb), :] + bf_ref[...]))) + carry
            o_ref[0, pl.ds(r0, tb), :] = blk
            return blk[tb - 1:tb, :]

        lax.fori_loop(0, seq // tb, step, jnp.zeros((1, LANES), F32))

    return pl.pallas_call(
        body, grid=(bsz,),
        in_specs=[pl.BlockSpec((1, seq, LANES), lambda b: (b, 0, 0)), pl.BlockSpec((1, LANES), lambda b: (0, 0))],
        out_specs=pl.BlockSpec((1, seq, LANES), lambda b: (b, 0, 0)),
        out_shape=jax.ShapeDtypeStruct(fl.shape, F32), name=name, compiler_params=_arb(1),
    )(fl, bf)


def _cum_bwd(dcum, fl, bf, name, tb=256):
    bsz, seq, _ = fl.shape
    tb = _pick(seq, (tb, 128))
    nb = seq // tb

    def body(dc_ref, fl_ref, bf_ref, o_ref, db_ref):
        @pl.when(pl.program_id(0) == 0)
        def _():
            db_ref[...] = jnp.zeros_like(db_ref)

        tri = _tri(tb, lambda a, b: b >= a)

        def step(jj, carry):
            tail, tot = carry
            r0 = pl.multiple_of((nb - 1 - jj) * tb, tb)
            dlf = _tri_dot3(tri, dc_ref[0, pl.ds(r0, tb), :]) + tail
            dfl = dlf * _sigmoid(-(fl_ref[0, pl.ds(r0, tb), :] + bf_ref[...]))
            o_ref[0, pl.ds(r0, tb), :] = dfl
            return dlf[0:1, :], tot + jnp.sum(dfl, axis=0, keepdims=True)

        zero = jnp.zeros((1, LANES), F32)
        _, tot = lax.fori_loop(0, nb, step, (zero, zero))
        db_ref[...] += tot

    whole = pl.BlockSpec((1, seq, LANES), lambda b: (b, 0, 0))
    vec = pl.BlockSpec((1, LANES), lambda b: (0, 0))
    return pl.pallas_call(
        body, grid=(bsz,), in_specs=[whole, whole, vec], out_specs=[whole, vec],
        out_shape=[jax.ShapeDtypeStruct(fl.shape, F32), jax.ShapeDtypeStruct((1, LANES), F32)],
        name=name, compiler_params=_arb(1),
    )(dcum, fl, bf)


def _adamw_math(w, g, m, v):
    m = ADAM_B1 * m + (1.0 - ADAM_B1) * g
    v = ADAM_B2 * v + (1.0 - ADAM_B2) * (g * g)
    m_hat = m / (1.0 - ADAM_B1 ** ADAM_STEP)
    v_hat = v / (1.0 - ADAM_B2 ** ADAM_STEP)
    return -ADAM_LR * (m_hat / (jnp.sqrt(v_hat) + ADAM_EPS) + ADAM_WD * w), m, v


def _adamw(gparts, w, m, v, name, tr=512):
    nslots, rows, cols = gparts.shape
    tr = _pick(rows, (tr, 256, 128, 64, 32, 16, 8))

    def body(g_ref, w_ref, m_ref, v_ref, go_ref, d_ref, mo_ref, vo_ref):
        g = g_ref[0]
        for k in range(1, nslots):
            g = g + g_ref[k]
        go_ref[...] = g
        d_ref[...], mo_ref[...], vo_ref[...] = _adamw_math(w_ref[...], g, m_ref[...], v_ref[...])

    blk = pl.BlockSpec((tr, cols), lambda i: (i, 0))
    shp = jax.ShapeDtypeStruct((rows, cols), F32)
    return pl.pallas_call(
        body, grid=(rows // tr,), in_specs=[pl.BlockSpec((nslots, tr, cols), lambda i: (0, i, 0)), blk, blk, blk],
        out_specs=[blk] * 4, out_shape=[shp] * 4, name=name, compiler_params=_arb(1),
    )(gparts, w, m, v)


def _rows128(a):
    return a.reshape(-1, LANES)


def _pad_rows(a, mult):
    extra = (-a.shape[0]) % mult
    return a if extra == 0 else jnp.concatenate([a, jnp.zeros((extra, a.shape[1]), a.dtype)], axis=0)


def _pack(arrs, mult):
    return _pad_rows(jnp.concatenate([_rows128(a) for a in arrs], axis=0), mult)


def _unpack(flat, shapes):
    out, off = [], 0
    for shp in shapes:
        n = 1
        for s in shp:
            n *= s
        out.append(flat[off:off + n // LANES].reshape(shp))
        off += n // LANES
    return out


def _col_blocks(full):
    k, n = full.shape
    return full.reshape(k, N_DEV, n // N_DEV).transpose(1, 0, 2)


def _from_col_blocks(blocks):
    _, k, n = blocks.shape
    return blocks.transpose(1, 0, 2).reshape(k, N_DEV * n)


def kernel(x, c, w_ada, b_ada, g_mix, w_in, b_forget, b_gate, w_branch_sb, w_branch_fox, w_out, g_ffn, w_ffn_gate, w_ffn_up, w_ffn_down, g_final, loss_target, m_w_ada, m_b_ada, m_g_mix, m_w_in, m_b_forget, m_b_gate, m_w_branch_sb, m_w_branch_fox, m_w_out, m_g_ffn, m_w_ffn_gate, m_w_ffn_up, m_w_ffn_down, m_g_final, v_w_ada, v_b_ada, v_g_mix, v_w_in, v_b_forget, v_b_gate, v_w_branch_sb, v_w_branch_fox, v_w_out, v_g_ffn, v_w_ffn_gate, v_w_ffn_up, v_w_ffn_down, v_g_final):
    bsz, seq, d = x.shape
    tok = bsz * seq
    nh = b_forget.shape[-1]
    d_in = w_in.shape[-1] * N_DEV
    d_att = (d_in - nh - 2 * d) // 6
    assert d_att == nh * HEAD_DIM and nh % 2 == 0
    np_ = nh // 2
    d_ff = w_ffn_gate.shape[-1] * N_DEV
    n_mod = w_ada.shape[-1] * N_DEV // d
    me = 4 * lax.axis_index("x") + 2 * lax.axis_index("y") + lax.axis_index("c")
    tq = tk = _pick(seq, (256, 128))

    big = [w_in[0], w_branch_sb[0], w_branch_fox[0], w_out[0], w_ffn_gate[0], w_ffn_up[0], w_ffn_down[0]]
    big_m = [m_w_in[0], m_w_branch_sb[0], m_w_branch_fox[0], m_w_out[0], m_w_ffn_gate[0], m_w_ffn_up[0], m_w_ffn_down[0]]
    big_v = [v_w_in[0], v_w_branch_sb[0], v_w_branch_fox[0], v_w_out[0], v_w_ffn_gate[0], v_w_ffn_up[0], v_w_ffn_down[0]]
    big_shapes = [a.shape for a in big]

    wg = _exchange(_pack([a.astype(BF16) for a in big], 16), True, "gather_weights")
    sizes = [a.shape[0] * a.shape[1] // LANES for a in big]
    offs = [sum(sizes[:i]) for i in range(len(sizes))]

    def gathered(i):
        blocks = wg[:, offs[i]:offs[i] + sizes[i]].reshape((N_DEV,) + big_shapes[i])
        return blocks

    w_in_f = _from_col_blocks(gathered(0))
    w_sb_f = _from_col_blocks(gathered(1))
    w_fx_f = _from_col_blocks(gathered(2))
    w_out_f = gathered(3).reshape(d, d)
    w_gu_f = jnp.concatenate([_from_col_blocks(gathered(4)), _from_col_blocks(gathered(5))], axis=1)
    w_dn_f = gathered(6).reshape(d_ff, d)
    w_qkv = w_in_f[:, :6 * d_att]
    w_f = jnp.concatenate([w_in_f[:, 6 * d_att:6 * d_att + nh], jnp.zeros((d, LANES - nh), BF16)], axis=1)
    w_gl = w_in_f[:, 6 * d_att + nh:]

    c_all = _exchange(_rows128(c), True, "gather_c").reshape(N_DEV * bsz, d)
    nb_all = N_DEV * bsz
    ada_cols = w_ada.shape[-1]
    b_ada_loc = lax.dynamic_slice(b_ada, (0, me * ada_cols), (1, ada_cols))

    def mod_fn(c_v, w_v, b_v):
        return [jnp.dot(_silu(c_v), w_v, precision=lax.Precision.HIGHEST, preferred_element_type=F32) + b_v]

    (mod_part,) = _small(mod_fn, [c_all, w_ada[0], b_ada_loc], [((nb_all, ada_cols), F32)], "ada_mod")
    mod_all = _from_col_blocks(_exchange(_rows128(mod_part), True, "gather_mod").reshape(N_DEV, nb_all, ada_cols))
    mod = lax.dynamic_slice(mod_all, (me * bsz, 0), (bsz, n_mod * d))
    shift1, scale1, gate1, shift2, scale2, gate2 = [mod[:, i * d:(i + 1) * d].reshape(bsz, 1, d) for i in range(6)]

    def norm_mod_fn(x_v, sc, sh, g):
        n = x_v * lax.rsqrt(jnp.mean(x_v * x_v, axis=-1, keepdims=True) + RMS_EPS) * g
        return [n * (1.0 + sc) + sh]

    (h,) = _rowmap(norm_mod_fn, [x, scale1, shift1, g_mix], [("row", d, BF16)], "norm1")
    h2d = h.reshape(tok, d)
    qkv = _mm([(h2d, w_qkv)], BF16, "proj_qkv").reshape(bsz, seq, 6 * d_att)
    gl = _mm([(h2d, w_gl)], F32, "proj_gates").reshape(bsz, seq, 2 * d)
    fl = _mm([(h2d, w_f)], F32, "proj_forget").reshape(bsz, seq, LANES)

    bf_pad = jnp.concatenate([b_forget, jnp.zeros((1, LANES - nh), F32)], axis=1)
    cum = _cum_fwd(fl, bf_pad, "cum_fwd")
    cum_h = cum[:, :, :nh]
    cumq = jnp.repeat(cum_h, HEAD_DIM, axis=-1).reshape(bsz, seq, np_, LANES).transpose(0, 2, 1, 3)
    cumk = jnp.concatenate([cum_h.transpose(0, 2, 1).reshape(bsz, np_, 2, seq),
                            jnp.zeros((bsz, np_, 6, seq), F32)], axis=2)

    y_sb, tot_sb = _sb_fwd(qkv, np_, tq, tk, "sb_fwd")
    y_fx, lse_fx = _fox_fwd(qkv, cumq, cumk, np_, tq, tk, "fox_fwd")

    u_sb = _mm([(y_sb.reshape(tok, d_att), w_sb_f)], F32, "branch_sb").reshape(bsz, seq, d)
    u_fx = _mm([(y_fx.reshape(tok, d_att), w_fx_f)], F32, "branch_fox").reshape(bsz, seq, d)

    def merge_fn(gl_v, us, uf, bg):
        gates = _sigmoid(gl_v + bg)
        return [gates[:, :d] * us + gates[:, d:] * uf]

    (merged,) = _rowmap(merge_fn, [gl, u_sb, u_fx, b_gate], [("row", d, BF16)], "merge")
    mo = _mm([(merged.reshape(tok, d), w_out_f)], F32, "out_proj").reshape(bsz, seq, d)

    def resid_norm_fn(x_v, mo_v, g1, sc, sh, g):
        x1_v = x_v + g1 * mo_v
        n = x1_v * lax.rsqrt(jnp.mean(x1_v * x1_v, axis=-1, keepdims=True) + RMS_EPS) * g
        return [x1_v, n * (1.0 + sc) + sh]

    x1, h2 = _rowmap(resid_norm_fn, [x, mo, gate1, scale2, shift2, g_ffn], [("row", d, F32), ("row", d, BF16)], "norm2")
    h2_2d = h2.reshape(tok, d)
    au = _mm([(h2_2d, w_gu_f)], F32, "ffn_in").reshape(bsz, seq, 2 * d_ff)

    def swiglu_fn(au_v):
        return [_silu(au_v[:, :d_ff]) * au_v[:, d_ff:]]

    (f,) = _rowmap(swiglu_fn, [au], [("row", d_ff, BF16)], "swiglu", ts=256)
    ffn = _mm([(f.reshape(tok, d_ff), w_dn_f)], F32, "ffn_out").reshape(bsz, seq, d)

    def head_fn(x1_v, ffn_v, g2, gf, tgt):
        x2 = x1_v + g2 * ffn_v
        rstd = lax.rsqrt(jnp.mean(x2 * x2, axis=-1, keepdims=True) + RMS_EPS)
        xh = x2 * rstd
        err = xh * gf - tgt
        loss_rows = 0.5 * jnp.mean(err * err, axis=-1, keepdims=True)
        dy = err * (1.0 / d)
        dxh = dy * gf
        dx2 = rstd * (dxh - xh * jnp.mean(dxh * xh, axis=-1, keepdims=True))
        return [dx2, dx2 * g2, jnp.sum(loss_rows, axis=0, keepdims=True) * jnp.ones((1, LANES), F32),
                jnp.sum(dy * xh, axis=0, keepdims=True), jnp.sum(dx2 * ffn_v, axis=0, keepdims=True)]

    dx2, dffn, loss_vec, dg_final, dgate2 = _rowmap(
        head_fn, [x1, ffn, gate2, g_final.reshape(1, d), loss_target],
        [("row", d, F32), ("row", d, BF16), ("global", LANES, F32), ("global", d, F32), ("batch", d, F32)], "head")
    loss = lax.psum(loss_vec[0, 0], AXES)

    dffn2d = dffn.reshape(tok, d)
    df = _mm([(dffn2d, w_dn_f.T)], F32, "ffn_out_dx").reshape(bsz, seq, d_ff)
    dw_dn = _mm_tn(f.reshape(tok, d_ff), dffn2d, "ffn_out_dw")

    def swiglu_bwd_fn(au_v, df_v):
        a, u = au_v[:, :d_ff], au_v[:, d_ff:]
        sig = _sigmoid(a)
        return [jnp.concatenate([df_v * u * sig * (1.0 + a * (1.0 - sig)), df_v * a * sig], axis=1)]

    (dau,) = _rowmap(swiglu_bwd_fn, [au, df], [("row", 2 * d_ff, BF16)], "swiglu_bwd", ts=256)
    dau2d = dau.reshape(tok, 2 * d_ff)
    dw_gu = _mm_tn(h2_2d, dau2d, "ffn_in_dw")
    dh2 = _mm([(dau2d, w_gu_f.T)], F32, "ffn_in_dx").reshape(bsz, seq, d)

    def norm2_bwd_fn(dh_v, x1_v, dx2_v, mo_v, sc, g1, g):
        rstd = lax.rsqrt(jnp.mean(x1_v * x1_v, axis=-1, keepdims=True) + RMS_EPS)
        xh = x1_v * rstd
        dn = dh_v * (1.0 + sc)
        dxh = dn * g
        dx1 = dx2_v + rstd * (dxh - xh * jnp.mean(dxh * xh, axis=-1, keepdims=True))
        return [dx1, dx1 * g1, jnp.sum(dh_v * (xh * g), axis=0, keepdims=True), jnp.sum(dh_v, axis=0, keepdims=True),
                jnp.sum(dn * xh, axis=0, keepdims=True), jnp.sum(dx1 * mo_v, axis=0, keepdims=True)]

    dx1, dmo, dscale2, dshift2, dg_ffn, dgate1 = _rowmap(
        norm2_bwd_fn, [dh2, x1, dx2, mo, scale2, gate1, g_ffn],
        [("row", d, F32), ("row", d, BF16), ("batch", d, F32), ("batch", d, F32), ("global", d, F32), ("batch", d, F32)],
        "norm2_bwd")

    dmo2d = dmo.reshape(tok, d)
    dmerged = _mm([(dmo2d, w_out_f.T)], F32, "out_proj_dx").reshape(bsz, seq, d)
    dw_out = _mm_tn(merged.reshape(tok, d), dmo2d, "out_proj_dw")

    def merge_bwd_fn(dm, gl_v, us, uf, bg):
        gates = _sigmoid(gl_v + bg)
        gs, gf = gates[:, :d], gates[:, d:]
        dgl = jnp.concatenate([dm * us * gs * (1.0 - gs), dm * uf * gf * (1.0 - gf)], axis=1)
        return [dm * gs, dm * gf, dgl, jnp.sum(dgl, axis=0, keepdims=True)]

    du_sb, du_fx, dgl, db_gate = _rowmap(
        merge_bwd_fn, [dmerged, gl, u_sb, u_fx, b_gate],
        [("row", d, BF16), ("row", d, BF16), ("row", 2 * d, BF16), ("global", 2 * d, F32)], "merge_bwd", ts=256)
    du_sb2d, du_fx2d = du_sb.reshape(tok, d), du_fx.reshape(tok, d)
    dw_sb = _mm_tn(y_sb.reshape(tok, d_att), du_sb2d, "branch_sb_dw")
    dw_fx = _mm_tn(y_fx.reshape(tok, d_att), du_fx2d, "branch_fox_dw")
    dy_sb = _mm([(du_sb2d, w_sb_f.T)], F32, "branch_sb_dx").reshape(bsz, seq, d_att)
    dy_fx = _mm([(du_fx2d, w_fx_f.T)], F32, "branch_fox_dx").reshape(bsz, seq, d_att)

    dq_sb, dk_sb, dv_sb = _sb_bwd(qkv, dy_sb, tot_sb, np_, tq, tk, "sb_bwd")
    dq_fx, dk_fx, dv_fx, dck, dcq = _fox_bwd(qkv, dy_fx, y_fx, lse_fx, cumq, cumk, np_, tq, tk, "fox_bwd")
    dcum = dck[:, :, :2, :].reshape(bsz, nh, seq).transpose(0, 2, 1)
    dcum = dcum + dcq[:, :, :, ::HEAD_DIM].transpose(0, 2, 1, 3).reshape(bsz, seq, nh)
    dcum = jnp.concatenate([dcum, jnp.zeros((bsz, seq, LANES - nh), F32)], axis=2)
    dfl, db_f = _cum_bwd(dcum, fl, bf_pad, "cum_bwd")

    dqkv = jnp.concatenate([dq_sb, dk_sb, dv_sb, dq_fx, dk_fx, dv_fx], axis=2).reshape(tok, 6 * d_att)
    dgl2d, dfl2d = dgl.reshape(tok, 2 * d), dfl.reshape(tok, LANES)
    dw_in = jnp.concatenate([_mm_tn(h2d, dqkv, "proj_qkv_dw"), _mm_tn(h2d, dfl2d, "proj_forget_dw")[:, :nh],
                             _mm_tn(h2d, dgl2d, "proj_gates_dw")], axis=1)
    dh = _mm([(dqkv, w_qkv.T), (dgl2d, w_gl.T), (dfl2d, w_f.T)], F32, "proj_dx").reshape(bsz, seq, d)

    def norm1_bwd_fn(dh_v, x_v, dx1_v, sc, g):
        rstd = lax.rsqrt(jnp.mean(x_v * x_v, axis=-1, keepdims=True) + RMS_EPS)
        xh = x_v * rstd
        dn = dh_v * (1.0 + sc)
        dxh = dn * g
        dx = dx1_v + rstd * (dxh - xh * jnp.mean(dxh * xh, axis=-1, keepdims=True))
        return [dx, jnp.sum(dh_v * (xh * g), axis=0, keepdims=True), jnp.sum(dh_v, axis=0, keepdims=True),
                jnp.sum(dn * xh, axis=0, keepdims=True)]

    grad_x, dscale1, dshift1, dg_mix = _rowmap(
        norm1_bwd_fn, [dh, x, dx1, scale1, g_mix],
        [("row", d, F32), ("batch", d, F32), ("batch", d, F32), ("global", d, F32)], "norm1_bwd")

    dmod = jnp.concatenate([dshift1, dscale1, dgate1, dshift2, dscale2, dgate2], axis=2).reshape(bsz, n_mod * d)
    db_f_pad = db_f
    partial = [dg_mix, db_f_pad, db_gate, dg_ffn, dg_final]
    n_dmod_rows = bsz * n_mod * d // LANES
    small_sent = _pack([dmod] + partial, 8)
    small_all = _exchange(small_sent, True, "gather_small")
    small_w = [b_ada, g_mix, jnp.concatenate([b_forget, jnp.zeros((1, LANES - nh), F32)], axis=1), b_gate, g_ffn,
               g_final.reshape(1, d)]
    small_m = [m_b_ada, m_g_mix, jnp.concatenate([m_b_forget, jnp.zeros((1, LANES - nh), F32)], axis=1), m_b_gate,
               m_g_ffn, m_g_final.reshape(1, d)]
    small_v = [v_b_ada, v_g_mix, jnp.concatenate([v_b_forget, jnp.zeros((1, LANES - nh), F32)], axis=1), v_b_gate,
               v_g_ffn, v_g_final.reshape(1, d)]
    small_shapes = [a.shape for a in small_w]
    n_ada_rows = n_mod * d // LANES
    n_part_rows = sum(a.shape[1] // LANES for a in partial)
    sw, sm, sv = _pack(small_w, 8), _pack(small_m, 8), _pack(small_v, 8)
    n_small_rows = sw.shape[0]

    def small_fn(all_v, w_v, m_v, v_v):
        g_ada = None
        g_rest = None
        for k in range(N_DEV):
            for b in range(bsz):
                part = all_v[k, b * n_ada_rows:(b + 1) * n_ada_rows]
                g_ada = part if g_ada is None else g_ada + part
            rest = all_v[k, n_dmod_rows:n_dmod_rows + n_part_rows]
            g_rest = rest if g_rest is None else g_rest + rest
        pieces = [g_ada, g_rest]
        if n_small_rows > n_ada_rows + n_part_rows:
            pieces.append(jnp.zeros((n_small_rows - n_ada_rows - n_part_rows, LANES), F32))
        g = jnp.concatenate(pieces, axis=0)
        return [g, *_adamw_math(w_v, g, m_v, v_v)]

    shp = ((n_small_rows, LANES), F32)
    small_out = _small(small_fn, [small_all, sw, sm, sv], [shp] * 4, "small_update")
    small_g, small_d, small_nm, small_nv = [_unpack(o, small_shapes) for o in small_out]

    def fix_small(lst):
        b_ada_o, g_mix_o, b_f_o, b_gate_o, g_ffn_o, g_final_o = lst
        return [b_ada_o, g_mix_o, b_f_o[:, :nh], b_gate_o, g_ffn_o, g_final_o.reshape(d)]

    small_g, small_d, small_nm, small_nv = [fix_small(l) for l in (small_g, small_d, small_nm, small_nv)]

    dmod_all = small_all[:, :n_dmod_rows].reshape(nb_all, n_mod * d)
    dmod_cols = lax.dynamic_slice(dmod_all, (0, me * ada_cols), (nb_all, ada_cols))

    def ada_dw_fn(c_v, dm_v):
        return [lax.dot_general(_silu(c_v), dm_v, (((0,), (0,)), ((), ())), precision=lax.Precision.HIGHEST,
                                preferred_element_type=F32)]

    (dw_ada,) = _small(ada_dw_fn, [c_all, dmod_cols], [((d, ada_cols), F32)], "ada_dw")
    ada_out = _adamw(dw_ada[None], w_ada[0], m_w_ada[0], v_w_ada[0], "adamw_ada")
    ada_g, ada_d, ada_nm, ada_nv = [o[None] for o in ada_out]

    dw_gate, dw_up = dw_gu[:, :d_ff], dw_gu[:, d_ff:]
    blocks = [_col_blocks(dw_in), _col_blocks(dw_sb), _col_blocks(dw_fx), dw_out.reshape(N_DEV, d // N_DEV, d),
              _col_blocks(dw_gate), _col_blocks(dw_up), dw_dn.reshape(N_DEV, d_ff // N_DEV, d)]
    sent = jnp.concatenate([b.reshape(N_DEV, -1, LANES) for b in blocks], axis=1)
    sent = jnp.concatenate([sent, jnp.zeros((N_DEV, (-sent.shape[1]) % ADAM_ROWS, LANES), F32)], axis=1)
    got = _exchange(sent, False, "exchange_grads")
    big_out = _adamw(got, _pack(big, ADAM_ROWS), _pack(big_m, ADAM_ROWS), _pack(big_v, ADAM_ROWS), "adamw_big")
    big_g, big_d, big_nm, big_nv = [[a[None] for a in _unpack(o, big_shapes)] for o in big_out]

    def ordered(ada, small, bigs):
        b_ada_o, g_mix_o, b_f_o, b_gate_o, g_ffn_o, g_final_o = small
        w_in_o, w_sb_o, w_fx_o, w_out_o, w_gate_o, w_up_o, w_dn_o = bigs
        return [ada, b_ada_o, g_mix_o, w_in_o, b_f_o, b_gate_o, w_sb_o, w_fx_o, w_out_o, g_ffn_o, w_gate_o, w_up_o,
                w_dn_o, g_final_o]

    return (loss, grad_x, *ordered(ada_g, small_g, big_g), *ordered(ada_d, small_d, big_d),
            *ordered(ada_nm, small_nm, big_nm), *ordered(ada_nv, small_nv, big_nv))
```

```python
import jax
import jax.numpy as jnp
from jax import lax
from jax.experimental import pallas as pl
from jax.experimental.pallas import tpu as pltpu

F32 = jnp.float32
BF16 = jnp.bfloat16
HEAD_DIM = 64
LANES = 128
N_DEV = 8
AXES = ("x", "y", "c")
RMS_EPS = 1e-6
ADAM_LR, ADAM_B1, ADAM_B2, ADAM_EPS, ADAM_WD, ADAM_STEP = 0.001, 0.9, 0.999, 1e-08, 0.01, 10
NEG = -1e30
MESH = pl.DeviceIdType.MESH


def _pick(n, cands):
    for c in cands:
        if n % c == 0:
            return c
    raise ValueError(f"no tile for {n} in {cands}")


def _arb(n):
    return pltpu.CompilerParams(dimension_semantics=("arbitrary",) * n)


def _dot(a, b):
    return jnp.dot(a, b, preferred_element_type=F32)


def _dot_nt(a, b):
    return lax.dot_general(a, b, (((1,), (1,)), ((), ())), preferred_element_type=F32)


def _dot_tn(a, b):
    return lax.dot_general(a, b, (((0,), (0,)), ((), ())), preferred_element_type=F32)


def _split2(v):
    hi = v.astype(BF16)
    return hi, (v - hi.astype(F32)).astype(BF16)


def _dot_split2(v, m):
    hi, lo = _split2(v)
    return _dot(hi, m) + _dot(lo, m)


def _tri_dot3(m, v):
    h1 = v.astype(BF16)
    r1 = v - h1.astype(F32)
    h2 = r1.astype(BF16)
    h3 = (r1 - h2.astype(F32)).astype(BF16)
    return _dot(m, h1) + _dot(m, h2) + _dot(m, h3)


def _sigmoid(v):
    return 1.0 / (1.0 + jnp.exp(-v))


def _silu(v):
    return v * _sigmoid(v)


def _n_tile(n):
    for c in (1024, 512):
        if n % c == 0:
            return c
    return n if n <= 3072 else _pick(n, (256, 128))


def _mm(pairs, out_dtype, name, tm=512):
    m, n = pairs[0][0].shape[0], pairs[0][1].shape[1]
    tm = _pick(m, (tm, 256, 128, 64, 32, 16, 8))
    tn = _n_tile(n)
    npairs = len(pairs)

    def body(*refs):
        acc = None
        for t in range(npairs):
            p = _dot(refs[2 * t][...].astype(BF16), refs[2 * t + 1][...].astype(BF16))
            acc = p if acc is None else acc + p
        refs[2 * npairs][...] = acc.astype(out_dtype)

    in_specs, args = [], []
    for a, b in pairs:
        k = a.shape[1]
        in_specs += [pl.BlockSpec((tm, k), lambda j, i: (i, 0)), pl.BlockSpec((k, tn), lambda j, i: (0, j))]
        args += [a, b]
    return pl.pallas_call(
        body, grid=(n // tn, m // tm), in_specs=in_specs,
        out_specs=pl.BlockSpec((tm, tn), lambda j, i: (i, j)),
        out_shape=jax.ShapeDtypeStruct((m, n), out_dtype), name=name, compiler_params=_arb(2),
    )(*args)


def _mm_tn(a, b, name, tm=512, tn=512, tk=1024):
    t, m = a.shape
    n = b.shape[1]
    tm, tn, tk = _pick(m, (tm, 256, 128)), _pick(n, (tn, 256, 128)), _pick(t, (tk, 512, 256, 128))

    def body(a_ref, b_ref, o_ref):
        @pl.when(pl.program_id(2) == 0)
        def _():
            o_ref[...] = jnp.zeros_like(o_ref)

        o_ref[...] += _dot_tn(a_ref[...].astype(BF16), b_ref[...].astype(BF16))

    return pl.pallas_call(
        body, grid=(m // tm, n // tn, t // tk),
        in_specs=[pl.BlockSpec((tk, tm), lambda i, j, k: (k, i)), pl.BlockSpec((tk, tn), lambda i, j, k: (k, j))],
        out_specs=pl.BlockSpec((tm, tn), lambda i, j, k: (i, j)),
        out_shape=jax.ShapeDtypeStruct((m, n), F32), name=name, compiler_params=_arb(3),
    )(a, b)


def _rowmap(fn, ins, outs, name, ts=512):
    bsz, seq = next(a.shape[:2] for a in ins if a.ndim == 3 and a.shape[1] != 1)
    ts = _pick(seq, (ts, 256, 128, 64, 32, 16, 8))
    n_in = len(ins)

    def in_spec(a):
        if a.ndim == 2:
            return pl.BlockSpec(a.shape, lambda b, s: (0, 0))
        if a.shape[1] == 1:
            return pl.BlockSpec((1, 1, a.shape[2]), lambda b, s: (b, 0, 0))
        return pl.BlockSpec((1, ts, a.shape[2]), lambda b, s: (b, s, 0))

    def out_spec(kind, w):
        if kind == "row":
            return pl.BlockSpec((1, ts, w), lambda b, s: (b, s, 0))
        if kind == "batch":
            return pl.BlockSpec((1, 1, w), lambda b, s: (b, 0, 0))
        return pl.BlockSpec((1, w), lambda b, s: (0, 0))

    def out_shape(kind, w, dt):
        shp = {"row": (bsz, seq, w), "batch": (bsz, 1, w), "global": (1, w)}[kind]
        return jax.ShapeDtypeStruct(shp, dt)

    def body(*refs):
        b, s = pl.program_id(0), pl.program_id(1)
        vals = [r[...] if a.ndim == 2 else r[0] for r, a in zip(refs[:n_in], ins)]
        res = fn(*vals)
        for o_ref, (kind, _, dt), v in zip(refs[n_in:], outs, res):
            if kind == "row":
                o_ref[0] = v.astype(dt)
            elif kind == "batch":
                @pl.when(s == 0)
                def _():
                    o_ref[...] = jnp.zeros_like(o_ref)

                o_ref[0] += v
            else:
                @pl.when((s == 0) & (b == 0))
                def _():
                    o_ref[...] = jnp.zeros_like(o_ref)

                o_ref[...] += v

    return pl.pallas_call(
        body, grid=(bsz, seq // ts), in_specs=[in_spec(a) for a in ins],
        out_specs=[out_spec(k, w) for k, w, _ in outs],
        out_shape=[out_shape(*o) for o in outs], name=name, compiler_params=_arb(2),
    )(*ins)


def _small(fn, ins, out_shapes, name):
    n_in = len(ins)

    def body(*refs):
        res = fn(*[r[...] for r in refs[:n_in]])
        for o_ref, v in zip(refs[n_in:], res):
            o_ref[...] = v

    return pl.pallas_call(body, out_shape=[jax.ShapeDtypeStruct(s, d) for s, d in out_shapes], name=name)(*ins)


def _mesh_pos():
    mx, my, mc = lax.axis_index("x"), lax.axis_index("y"), lax.axis_index("c")
    return mx, my, mc, 4 * mx + 2 * my + mc


def _peer(mx, my, mc, k):
    px = 1 - mx if k & 4 else mx
    py = 1 - my if k & 2 else my
    pc = 1 - mc if k & 1 else mc
    return (px, py, pc), 4 * px + 2 * py + pc


def _exchange(arrs, gather, name):
    n_arr = len(arrs)

    def body(*refs):
        x_refs, out_refs = refs[:n_arr], refs[n_arr:2 * n_arr]
        send_sems, recv_sems, local_sems = refs[2 * n_arr:]
        mx, my, mc, me = _mesh_pos()
        owns, sends, recvs = [], [], []
        for a, (x_ref, out_ref) in enumerate(zip(x_refs, out_refs)):
            owns.append(pltpu.make_async_copy(x_ref if gather else x_ref.at[me], out_ref.at[me], local_sems.at[a]))
            for k in range(1, N_DEV):
                peer, pid = _peer(mx, my, mc, k)
                src = x_ref if gather else x_ref.at[pid]
                sends.append(pltpu.make_async_remote_copy(
                    src_ref=src, dst_ref=out_ref.at[me], send_sem=send_sems.at[a, k], recv_sem=recv_sems.at[a, k],
                    device_id=peer, device_id_type=MESH))
                recvs.append(pltpu.make_async_remote_copy(
                    src_ref=src, dst_ref=out_ref.at[pid], send_sem=send_sems.at[a, k], recv_sem=recv_sems.at[a, k],
                    device_id=peer, device_id_type=MESH))
        for cp in owns + sends:
            cp.start()
        for cp in recvs:
            cp.wait_recv()
        for cp in sends:
            cp.wait_send()
        for cp in owns:
            cp.wait()

    any_spec = pl.BlockSpec(memory_space=pl.ANY)
    return pl.pallas_call(
        body, out_shape=[jax.ShapeDtypeStruct((N_DEV,) + tuple(x.shape if gather else x.shape[1:]), x.dtype) for x in arrs],
        in_specs=[any_spec] * n_arr, out_specs=[any_spec] * n_arr,
        scratch_shapes=[pltpu.SemaphoreType.DMA((n_arr, N_DEV)), pltpu.SemaphoreType.DMA((n_arr, N_DEV)),
                        pltpu.SemaphoreType.DMA((n_arr,))],
        name=name,
    )(*arrs)


QK_SCALE = HEAD_DIM ** -0.5


def _low_lanes():
    return lax.broadcasted_iota(jnp.int32, (1, LANES), 1) < HEAD_DIM


def _stack_heads(v, scale=None):
    lo = _low_lanes()
    zero = jnp.zeros_like(v)
    s = jnp.concatenate([jnp.where(lo, v, zero), jnp.where(lo, zero, v)], axis=0)
    return s if scale is None else s * scale


def _stack_cols(v):
    return jnp.concatenate([v[:, 0:1], v[:, HEAD_DIM:HEAD_DIM + 1]], axis=0)


def _unstack(v, tq):
    return jnp.where(_low_lanes(), v[:tq], v[tq:])


def _tile_pos(tq, tk, q0):
    rows = lax.broadcasted_iota(jnp.int32, (2 * tq, tk), 0)
    cols = lax.broadcasted_iota(jnp.int32, (2 * tq, tk), 1)
    return q0 + jnp.where(rows >= tq, rows - tq, rows), cols, rows < tq


def _tri(tk, cmp):
    r = lax.broadcasted_iota(jnp.int32, (tk, tk), 0)
    c = lax.broadcasted_iota(jnp.int32, (tk, tk), 1)
    return jnp.where(cmp(r, c), 1.0, 0.0).astype(BF16)


def _softplus(z):
    return jnp.maximum(z, 0.0) + jnp.log(1.0 + jnp.exp(-jnp.abs(z)))


def _pair_specs(tq, seq, np_, off):
    return [pl.BlockSpec((1, tq, LANES), lambda b, p, i: (b, i, off + p)),
            pl.BlockSpec((1, seq, LANES), lambda b, p, i: (b, 0, off + np_ + p)),
            pl.BlockSpec((1, seq, LANES), lambda b, p, i: (b, 0, off + 2 * np_ + p))]


def _key_tiles(tile, carry, q0, tq, tk, upward):
    nfull = q0 // tk
    edge = range(tq // tk)
    if upward:
        carry = lax.fori_loop(0, nfull, lambda j, cr: tile(pl.multiple_of(j * tk, tk), cr, False), carry)
        for jm in edge:
            carry = tile(pl.multiple_of(q0 + jm * tk, tk), carry, True)
        return carry
    for jm in reversed(edge):
        carry = tile(pl.multiple_of(q0 + jm * tk, tk), carry, True)
    return lax.fori_loop(0, nfull, lambda jj, cr: tile(pl.multiple_of((nfull - 1 - jj) * tk, tk), cr, False), carry)


def _sb_fwd(qkv, np_, tq, tk, name):
    bsz, seq, _ = qkv.shape

    def body(q_ref, k_ref, v_ref, y_ref, tot_ref):
        q0 = pl.program_id(2) * tq
        tpos, cols, _ = _tile_pos(tq, tk, q0)
        msuf = _tri(tk, lambda a, b: a > b)
        qs = _stack_heads(q_ref[0], QK_SCALE)

        def tile(k0, carry, masked):
            tot, acc = carry
            z = _dot_nt(qs, k_ref[0, pl.ds(k0, tk), :])
            sp = _softplus(z)
            if masked:
                seen = (k0 + cols) < tpos
                sp = jnp.where(seen, sp, 0.0)
            logw = z - sp - _dot_split2(sp, msuf) - tot
            if masked:
                logw = jnp.where(seen, logw, NEG)
            return (tot + jnp.sum(sp, axis=1, keepdims=True),
                    acc + _dot(jnp.exp(logw).astype(BF16), v_ref[0, pl.ds(k0, tk), :]))

        init = (jnp.zeros((2 * tq, 1), F32), jnp.zeros((2 * tq, LANES), F32))
        tot, acc = _key_tiles(tile, init, q0, tq, tk, upward=False)
        y_ref[0] = _unstack(acc, tq)
        tot_ref[0, 0] = _unstack(tot, tq)

    return pl.pallas_call(
        body, grid=(bsz, np_, seq // tq), in_specs=_pair_specs(tq, seq, np_, 0),
        out_specs=[pl.BlockSpec((1, tq, LANES), lambda b, p, i: (b, i, p)),
                   pl.BlockSpec((1, 1, tq, LANES), lambda b, p, i: (b, p, i, 0))],
        out_shape=[jax.ShapeDtypeStruct((bsz, seq, np_ * LANES), F32),
                   jax.ShapeDtypeStruct((bsz, np_, seq, LANES), F32)],
        name=name, compiler_params=_arb(3),
    )(qkv, qkv, qkv)


def _sb_bwd(qkv, dy, tot, np_, tq, tk, name):
    bsz, seq, _ = qkv.shape
    nq = seq // tq

    def body(q_ref, k_ref, v_ref, dy_ref, tot_ref, dq_ref, dk_ref, dv_ref, dk_acc, dv_acc):
        i = pl.program_id(2)
        q0 = i * tq

        @pl.when(i == 0)
        def _():
            dk_acc[...] = jnp.zeros_like(dk_acc)
            dv_acc[...] = jnp.zeros_like(dv_acc)

        tpos, cols, _ = _tile_pos(tq, tk, q0)
        mincl = _tri(tk, lambda a, b: a <= b)
        mexcl = _tri(tk, lambda a, b: a < b)
        qs = _stack_heads(q_ref[0], QK_SCALE)
        dys = _stack_heads(dy_ref[0].astype(BF16))
        tots = _stack_cols(tot_ref[0, 0])

        def tile(k0, carry, masked):
            c_sp, c_g, dq = carry
            kb = k_ref[0, pl.ds(k0, tk), :]
            z = _dot_nt(qs, kb)
            sp = _softplus(z)
            if masked:
                seen = (k0 + cols) < tpos
                sp = jnp.where(seen, sp, 0.0)
            logw = z - sp - (tots - c_sp - _dot_split2(sp, mincl))
            if masked:
                logw = jnp.where(seen, logw, NEG)
            w = jnp.exp(logw)
            g = w * _dot_nt(dys, v_ref[0, pl.ds(k0, tk), :])
            beta = jnp.exp(jnp.minimum(z - sp, 0.0))
            dz = g - beta * (g + c_g + _dot_split2(g, mexcl))
            if masked:
                dz = jnp.where(seen, dz, 0.0)
            dzb = dz.astype(BF16)
            dk_acc[pl.ds(k0, tk), :] += _dot_tn(dzb, qs)
            dv_acc[pl.ds(k0, tk), :] += _dot_tn(w.astype(BF16), dys)
            return (c_sp + jnp.sum(sp, axis=1, keepdims=True), c_g + jnp.sum(g, axis=1, keepdims=True),
                    dq + _dot(dzb, kb))

        zero = jnp.zeros((2 * tq, 1), F32)
        _, _, dq = _key_tiles(tile, (zero, zero, jnp.zeros((2 * tq, LANES), F32)), q0, tq, tk, upward=True)
        dq_ref[0] = (_unstack(dq, tq) * QK_SCALE).astype(BF16)

        @pl.when(i == nq - 1)
        def _():
            dk_ref[0] = dk_acc[...].astype(BF16)
            dv_ref[0] = dv_acc[...].astype(BF16)

    tile_spec = pl.BlockSpec((1, tq, LANES), lambda b, p, i: (b, i, p))
    whole = pl.BlockSpec((1, seq, LANES), lambda b, p, i: (b, 0, p))
    out = jax.ShapeDtypeStruct((bsz, seq, np_ * LANES), BF16)
    return pl.pallas_call(
        body, grid=(bsz, np_, nq),
        in_specs=_pair_specs(tq, seq, np_, 0) + [tile_spec, pl.BlockSpec((1, 1, tq, LANES), lambda b, p, i: (b, p, i, 0))],
        out_specs=[tile_spec, whole, whole], out_shape=[out, out, out],
        scratch_shapes=[pltpu.VMEM((seq, LANES), F32), pltpu.VMEM((seq, LANES), F32)],
        name=name, compiler_params=_arb(3),
    )(qkv, qkv, qkv, dy, tot)


def _fox_fwd(qkv, cumq, cumk, np_, tq, tk, name):
    bsz, seq, _ = qkv.shape

    def body(q_ref, k_ref, v_ref, cq_ref, ck_ref, y_ref, lse_ref):
        q0 = pl.program_id(2) * tq
        tpos, cols, top = _tile_pos(tq, tk, q0)
        qs = _stack_heads(q_ref[0], QK_SCALE)
        cq = _stack_cols(cq_ref[0, 0])

        def tile(k0, carry, masked):
            m, l, acc = carry
            ck = jnp.where(top, ck_ref[0, 0, 0:1, pl.ds(k0, tk)], ck_ref[0, 0, 1:2, pl.ds(k0, tk)])
            s = _dot_nt(qs, k_ref[0, pl.ds(k0, tk), :]) + (cq - ck)
            if masked:
                s = jnp.where((k0 + cols) <= tpos, s, NEG)
            m_new = jnp.maximum(m, jnp.max(s, axis=1, keepdims=True))
            p = jnp.exp(s - m_new)
            alpha = jnp.exp(m - m_new)
            return (m_new, alpha * l + jnp.sum(p, axis=1, keepdims=True),
                    alpha * acc + _dot(p.astype(BF16), v_ref[0, pl.ds(k0, tk), :]))

        init = (jnp.full((2 * tq, 1), NEG, F32), jnp.zeros((2 * tq, 1), F32), jnp.zeros((2 * tq, LANES), F32))
        m, l, acc = _key_tiles(tile, init, q0, tq, tk, upward=True)
        y_ref[0] = _unstack(acc / l, tq)
        lse_ref[0, 0] = _unstack(m + jnp.log(l), tq)

    row4 = pl.BlockSpec((1, 1, tq, LANES), lambda b, p, i: (b, p, i, 0))
    return pl.pallas_call(
        body, grid=(bsz, np_, seq // tq),
        in_specs=_pair_specs(tq, seq, np_, 3 * np_) + [row4, pl.BlockSpec((1, 1, 8, seq), lambda b, p, i: (b, p, 0, 0))],
        out_specs=[pl.BlockSpec((1, tq, LANES), lambda b, p, i: (b, i, p)), row4],
        out_shape=[jax.ShapeDtypeStruct((bsz, seq, np_ * LANES), F32),
                   jax.ShapeDtypeStruct((bsz, np_, seq, LANES), F32)],
        name=name, compiler_params=_arb(3),
    )(qkv, qkv, qkv, cumq, cumk)


def _fox_bwd(qkv, dy, y, lse, cumq, cumk, np_, tq, tk, name):
    bsz, seq, _ = qkv.shape
    nq = seq // tq

    def body(q_ref, k_ref, v_ref, dy_ref, y_ref, lse_ref, cq_ref, ck_ref,
             dq_ref, dk_ref, dv_ref, dck_ref, dcq_ref, dk_acc, dv_acc, dck_acc):
        i = pl.program_id(2)
        q0 = i * tq

        @pl.when(i == 0)
        def _():
            dk_acc[...] = jnp.zeros_like(dk_acc)
            dv_acc[...] = jnp.zeros_like(dv_acc)
            dck_acc[...] = jnp.zeros_like(dck_acc)

        tpos, cols, top = _tile_pos(tq, tk, q0)
        qs = _stack_heads(q_ref[0], QK_SCALE)
        dyf = dy_ref[0]
        dys = _stack_heads(dyf.astype(BF16))
        dyy = dyf * y_ref[0]
        lo = _low_lanes()
        delta = jnp.concatenate([jnp.sum(jnp.where(lo, dyy, 0.0), axis=1, keepdims=True),
                                 jnp.sum(jnp.where(lo, 0.0, dyy), axis=1, keepdims=True)], axis=0)
        cq = _stack_cols(cq_ref[0, 0])
        lse_s = _stack_cols(lse_ref[0, 0])

        def tile(k0, carry, masked):
            dq, row = carry
            kb = k_ref[0, pl.ds(k0, tk), :]
            ck = jnp.where(top, ck_ref[0, 0, 0:1, pl.ds(k0, tk)], ck_ref[0, 0, 1:2, pl.ds(k0, tk)])
            s = _dot_nt(qs, kb) + (cq - ck)
            if masked:
                s = jnp.where((k0 + cols) <= tpos, s, NEG)
            p = jnp.exp(s - lse_s)
            ds = p * (_dot_nt(dys, v_ref[0, pl.ds(k0, tk), :]) - delta)
            dsb = ds.astype(BF16)
            dk_acc[pl.ds(k0, tk), :] += _dot_tn(dsb, qs)
            dv_acc[pl.ds(k0, tk), :] += _dot_tn(p.astype(BF16), dys)
            dck_acc[0:1, pl.ds(k0, tk)] += -jnp.sum(ds[:tq], axis=0, keepdims=True)
            dck_acc[1:2, pl.ds(k0, tk)] += -jnp.sum(ds[tq:], axis=0, keepdims=True)
            return dq + _dot(dsb, kb), row + jnp.sum(ds, axis=1, keepdims=True)

        init = (jnp.zeros((2 * tq, LANES), F32), jnp.zeros((2 * tq, 1), F32))
        dq, row = _key_tiles(tile, init, q0, tq, tk, upward=True)
        dq_ref[0] = (_unstack(dq, tq) * QK_SCALE).astype(BF16)
        dcq_ref[0, 0] = _unstack(row, tq)

        @pl.when(i == nq - 1)
        def _():
            dk_ref[0] = dk_acc[...].astype(BF16)
            dv_ref[0] = dv_acc[...].astype(BF16)
            dck_ref[0, 0] = dck_acc[...]

    tile_spec = pl.BlockSpec((1, tq, LANES), lambda b, p, i: (b, i, p))
    whole = pl.BlockSpec((1, seq, LANES), lambda b, p, i: (b, 0, p))
    row4 = pl.BlockSpec((1, 1, tq, LANES), lambda b, p, i: (b, p, i, 0))
    key4 = pl.BlockSpec((1, 1, 8, seq), lambda b, p, i: (b, p, 0, 0))
    out = jax.ShapeDtypeStruct((bsz, seq, np_ * LANES), BF16)
    return pl.pallas_call(
        body, grid=(bsz, np_, nq),
        in_specs=_pair_specs(tq, seq, np_, 3 * np_) + [tile_spec, tile_spec, row4, row4, key4],
        out_specs=[tile_spec, whole, whole, key4, row4],
        out_shape=[out, out, out, jax.ShapeDtypeStruct((bsz, np_, 8, seq), F32),
                   jax.ShapeDtypeStruct((bsz, np_, seq, LANES), F32)],
        scratch_shapes=[pltpu.VMEM((seq, LANES), F32), pltpu.VMEM((seq, LANES), F32), pltpu.VMEM((8, seq), F32)],
        name=name, compiler_params=_arb(3),
    )(qkv, qkv, qkv, dy, y, lse, cumq, cumk)


def _cum_fwd(fl, bf, np_, name, tb=256):
    bsz, seq, _ = fl.shape
    tb = _pick(seq, (tb, 128))

    def body(fl_ref, bf_ref, o_ref, q_ref):
        tri = _tri(tb, lambda a, b: b <= a)
        lo = _low_lanes()

        def step(j, carry):
            r0 = pl.multiple_of(j * tb, tb)
            blk = _tri_dot3(tri, -_softplus(-(fl_ref[0, pl.ds(r0, tb), :] + bf_ref[...]))) + carry
            o_ref[0, pl.ds(r0, tb), :] = blk
            for p in range(np_):
                q_ref[0, p, pl.ds(r0, tb), :] = jnp.where(lo, blk[:, 2 * p:2 * p + 1], blk[:, 2 * p + 1:2 * p + 2])
            return blk[tb - 1:tb, :]

        lax.fori_loop(0, seq // tb, step, jnp.zeros((1, LANES), F32))

    return pl.pallas_call(
        body, grid=(bsz,),
        in_specs=[pl.BlockSpec((1, seq, LANES), lambda b: (b, 0, 0)), pl.BlockSpec((1, LANES), lambda b: (0, 0))],
        out_specs=[pl.BlockSpec((1, seq, LANES), lambda b: (b, 0, 0)),
                   pl.BlockSpec((1, np_, seq, LANES), lambda b: (b, 0, 0, 0))],
        out_shape=[jax.ShapeDtypeStruct(fl.shape, F32), jax.ShapeDtypeStruct((bsz, np_, seq, LANES), F32)],
        name=name, compiler_params=_arb(1),
    )(fl, bf)


def _cum_bwd(dck, dcq, fl, bf, np_, name, tb=256):
    bsz, seq, _ = fl.shape
    tb = _pick(seq, (tb, 128))
    nb = seq // tb

    def body(dck_ref, dcq_ref, fl_ref, bf_ref, o_ref, db_ref):
        @pl.when(pl.program_id(0) == 0)
        def _():
            db_ref[...] = jnp.zeros_like(db_ref)

        tri = _tri(tb, lambda a, b: b >= a)
        lane = lax.broadcasted_iota(jnp.int32, (1, LANES), 1)

        def step(jj, carry):
            tail, tot = carry
            r0 = pl.multiple_of((nb - 1 - jj) * tb, tb)
            dc = dck_ref[0, pl.ds(r0, tb), :]
            for p in range(np_):
                pair = dcq_ref[0, p, pl.ds(r0, tb), :]
                dc = dc + jnp.where(lane == 2 * p, pair, 0.0) + jnp.where(lane == 2 * p + 1, pltpu.roll(pair, HEAD_DIM, 1), 0.0)
            dlf = _tri_dot3(tri, dc) + tail
            dfl = dlf * _sigmoid(-(fl_ref[0, pl.ds(r0, tb), :] + bf_ref[...]))
            o_ref[0, pl.ds(r0, tb), :] = dfl
            return dlf[0:1, :], tot + jnp.sum(dfl, axis=0, keepdims=True)

        zero = jnp.zeros((1, LANES), F32)
        _, tot = lax.fori_loop(0, nb, step, (zero, zero))
        db_ref[...] += tot

    whole = pl.BlockSpec((1, seq, LANES), lambda b: (b, 0, 0))
    vec = pl.BlockSpec((1, LANES), lambda b: (0, 0))
    return pl.pallas_call(
        body, grid=(bsz,), in_specs=[whole, pl.BlockSpec((1, np_, seq, LANES), lambda b: (b, 0, 0, 0)), whole, vec],
        out_specs=[whole, vec],
        out_shape=[jax.ShapeDtypeStruct(fl.shape, F32), jax.ShapeDtypeStruct((1, LANES), F32)],
        name=name, compiler_params=_arb(1),
    )(dck, dcq, fl, bf)


def _adamw_math(w, g, m, v):
    m = ADAM_B1 * m + (1.0 - ADAM_B1) * g
    v = ADAM_B2 * v + (1.0 - ADAM_B2) * (g * g)
    m_hat = m / (1.0 - ADAM_B1 ** ADAM_STEP)
    v_hat = v / (1.0 - ADAM_B2 ** ADAM_STEP)
    return -ADAM_LR * (m_hat / (jnp.sqrt(v_hat) + ADAM_EPS) + ADAM_WD * w), m, v


def _adamw(gparts, w, m, v, name, tr=512):
    nslots, rows, cols = gparts.shape
    tr = _pick(rows, (tr, 256, 128, 64, 32, 16, 8))

    def body(g_ref, w_ref, m_ref, v_ref, go_ref, d_ref, mo_ref, vo_ref):
        g = g_ref[0]
        for k in range(1, nslots):
            g = g + g_ref[k]
        go_ref[...] = g
        d_ref[...], mo_ref[...], vo_ref[...] = _adamw_math(w_ref[...], g, m_ref[...], v_ref[...])

    blk = pl.BlockSpec((tr, cols), lambda i: (i, 0))
    shp = jax.ShapeDtypeStruct((rows, cols), F32)
    return pl.pallas_call(
        body, grid=(rows // tr,), in_specs=[pl.BlockSpec((nslots, tr, cols), lambda i: (0, i, 0)), blk, blk, blk],
        out_specs=[blk] * 4, out_shape=[shp] * 4, name=name, compiler_params=_arb(1),
    )(gparts, w, m, v)


def _rows128(a):
    return a.reshape(-1, LANES)


def _pad_rows(a, mult):
    extra = (-a.shape[0]) % mult
    return a if extra == 0 else jnp.concatenate([a, jnp.zeros((extra, a.shape[1]), a.dtype)], axis=0)


def _pack(arrs, mult):
    return _pad_rows(jnp.concatenate([_rows128(a) for a in arrs], axis=0), mult)


def _unpack(flat, shapes):
    out, off = [], 0
    for shp in shapes:
        n = 1
        for s in shp:
            n *= s
        out.append(flat[off:off + n // LANES].reshape(shp))
        off += n // LANES
    return out


def _col_blocks(full):
    k, n = full.shape
    return full.reshape(k, N_DEV, n // N_DEV).transpose(1, 0, 2)


def _from_col_blocks(blocks):
    _, k, n = blocks.shape
    return blocks.transpose(1, 0, 2).reshape(k, N_DEV * n)


def kernel(x, c, w_ada, b_ada, g_mix, w_in, b_forget, b_gate, w_branch_sb, w_branch_fox, w_out, g_ffn, w_ffn_gate, w_ffn_up, w_ffn_down, g_final, loss_target, m_w_ada, m_b_ada, m_g_mix, m_w_in, m_b_forget, m_b_gate, m_w_branch_sb, m_w_branch_fox, m_w_out, m_g_ffn, m_w_ffn_gate, m_w_ffn_up, m_w_ffn_down, m_g_final, v_w_ada, v_b_ada, v_g_mix, v_w_in, v_b_forget, v_b_gate, v_w_branch_sb, v_w_branch_fox, v_w_out, v_g_ffn, v_w_ffn_gate, v_w_ffn_up, v_w_ffn_down, v_g_final):
    bsz, seq, d = x.shape
    tok = bsz * seq
    nh = b_forget.shape[-1]
    d_in = w_in.shape[-1] * N_DEV
    d_att = (d_in - nh - 2 * d) // 6
    assert d_att == nh * HEAD_DIM and nh % 2 == 0
    np_ = nh // 2
    d_ff = w_ffn_gate.shape[-1] * N_DEV
    n_mod = w_ada.shape[-1] * N_DEV // d
    me = 4 * lax.axis_index("x") + 2 * lax.axis_index("y") + lax.axis_index("c")
    tq_att = _pick(seq, (512, 256, 128))
    tk_att = _pick(tq_att, (256, 128))

    big = [w_in[0], w_branch_sb[0], w_branch_fox[0], w_out[0], w_ffn_gate[0], w_ffn_up[0], w_ffn_down[0]]
    big_m = [m_w_in[0], m_w_branch_sb[0], m_w_branch_fox[0], m_w_out[0], m_w_ffn_gate[0], m_w_ffn_up[0], m_w_ffn_down[0]]
    big_v = [v_w_in[0], v_w_branch_sb[0], v_w_branch_fox[0], v_w_out[0], v_w_ffn_gate[0], v_w_ffn_up[0], v_w_ffn_down[0]]

    gath = _exchange([a.astype(BF16) for a in big], True, "gather_weights")
    w_in_f = _from_col_blocks(gath[0])
    w_sb_f = _from_col_blocks(gath[1])
    w_fx_f = _from_col_blocks(gath[2])
    w_out_f = gath[3].reshape(d, d)
    w_gu_f = jnp.concatenate([_from_col_blocks(gath[4]), _from_col_blocks(gath[5])], axis=1)
    w_dn_f = gath[6].reshape(d_ff, d)
    w_qkv = w_in_f[:, :6 * d_att]
    w_f = jnp.concatenate([w_in_f[:, 6 * d_att:6 * d_att + nh], jnp.zeros((d, LANES - nh), BF16)], axis=1)
    w_gl = w_in_f[:, 6 * d_att + nh:]

    c_all = _exchange([_rows128(c)], True, "gather_c")[0].reshape(N_DEV * bsz, d)
    nb_all = N_DEV * bsz
    ada_cols = w_ada.shape[-1]
    b_ada_loc = lax.dynamic_slice(b_ada, (0, me * ada_cols), (1, ada_cols))

    def mod_fn(c_v, w_v, b_v):
        return [jnp.dot(_silu(c_v), w_v, precision=lax.Precision.HIGHEST, preferred_element_type=F32) + b_v]

    (mod_part,) = _small(mod_fn, [c_all, w_ada[0], b_ada_loc], [((nb_all, ada_cols), F32)], "ada_mod")
    mod_all = _from_col_blocks(_exchange([_rows128(mod_part)], True, "gather_mod")[0].reshape(N_DEV, nb_all, ada_cols))
    mod = lax.dynamic_slice(mod_all, (me * bsz, 0), (bsz, n_mod * d))
    shift1, scale1, gate1, shift2, scale2, gate2 = [mod[:, i * d:(i + 1) * d].reshape(bsz, 1, d) for i in range(6)]

    def norm_mod_fn(x_v, sc, sh, g):
        n = x_v * lax.rsqrt(jnp.mean(x_v * x_v, axis=-1, keepdims=True) + RMS_EPS) * g
        return [n * (1.0 + sc) + sh]

    (h,) = _rowmap(norm_mod_fn, [x, scale1, shift1, g_mix], [("row", d, BF16)], "norm1")
    h2d = h.reshape(tok, d)
    qkv = _mm([(h2d, w_qkv)], BF16, "proj_qkv").reshape(bsz, seq, 6 * d_att)
    gl = _mm([(h2d, w_gl)], F32, "proj_gates").reshape(bsz, seq, 2 * d)
    fl = _mm([(h2d, w_f)], F32, "proj_forget").reshape(bsz, seq, LANES)

    bf_pad = jnp.concatenate([b_forget, jnp.zeros((1, LANES - nh), F32)], axis=1)
    cum, cumq = _cum_fwd(fl, bf_pad, np_, "cum_fwd")
    cumk = jnp.concatenate([cum[:, :, :nh].transpose(0, 2, 1).reshape(bsz, np_, 2, seq),
                            jnp.zeros((bsz, np_, 6, seq), F32)], axis=2)

    y_sb, tot_sb = _sb_fwd(qkv, np_, tq_att, tk_att, "sb_fwd")
    y_fx, lse_fx = _fox_fwd(qkv, cumq, cumk, np_, tq_att, tk_att, "fox_fwd")

    u_sb = _mm([(y_sb.reshape(tok, d_att), w_sb_f)], F32, "branch_sb").reshape(bsz, seq, d)
    u_fx = _mm([(y_fx.reshape(tok, d_att), w_fx_f)], F32, "branch_fox").reshape(bsz, seq, d)

    def merge_fn(gl_v, us, uf, bg):
        gates = _sigmoid(gl_v + bg)
        return [gates[:, :d] * us + gates[:, d:] * uf]

    (merged,) = _rowmap(merge_fn, [gl, u_sb, u_fx, b_gate], [("row", d, BF16)], "merge")
    mo = _mm([(merged.reshape(tok, d), w_out_f)], F32, "out_proj").reshape(bsz, seq, d)

    def resid_norm_fn(x_v, mo_v, g1, sc, sh, g):
        x1_v = x_v + g1 * mo_v
        n = x1_v * lax.rsqrt(jnp.mean(x1_v * x1_v, axis=-1, keepdims=True) + RMS_EPS) * g
        return [x1_v, n * (1.0 + sc) + sh]

    x1, h2 = _rowmap(resid_norm_fn, [x, mo, gate1, scale2, shift2, g_ffn], [("row", d, F32), ("row", d, BF16)], "norm2")
    h2_2d = h2.reshape(tok, d)
    au = _mm([(h2_2d, w_gu_f)], F32, "ffn_in").reshape(bsz, seq, 2 * d_ff)

    def swiglu_fn(au_v):
        return [_silu(au_v[:, :d_ff]) * au_v[:, d_ff:]]

    (f,) = _rowmap(swiglu_fn, [au], [("row", d_ff, BF16)], "swiglu", ts=256)
    ffn = _mm([(f.reshape(tok, d_ff), w_dn_f)], F32, "ffn_out").reshape(bsz, seq, d)

    def head_fn(x1_v, ffn_v, g2, gf, tgt):
        x2 = x1_v + g2 * ffn_v
        rstd = lax.rsqrt(jnp.mean(x2 * x2, axis=-1, keepdims=True) + RMS_EPS)
        xh = x2 * rstd
        err = xh * gf - tgt
        loss_rows = 0.5 * jnp.mean(err * err, axis=-1, keepdims=True)
        dy = err * (1.0 / d)
        dxh = dy * gf
        dx2 = rstd * (dxh - xh * jnp.mean(dxh * xh, axis=-1, keepdims=True))
        return [dx2, dx2 * g2, jnp.sum(loss_rows, axis=0, keepdims=True) * jnp.ones((1, LANES), F32),
                jnp.sum(dy * xh, axis=0, keepdims=True), jnp.sum(dx2 * ffn_v, axis=0, keepdims=True)]

    dx2, dffn, loss_vec, dg_final, dgate2 = _rowmap(
        head_fn, [x1, ffn, gate2, g_final.reshape(1, d), loss_target],
        [("row", d, F32), ("row", d, BF16), ("global", LANES, F32), ("global", d, F32), ("batch", d, F32)], "head")
    loss = lax.psum(loss_vec[0, 0], AXES)

    dffn2d = dffn.reshape(tok, d)
    df = _mm([(dffn2d, w_dn_f.T)], F32, "ffn_out_dx").reshape(bsz, seq, d_ff)
    dw_dn = _mm_tn(f.reshape(tok, d_ff), dffn2d, "ffn_out_dw")

    def swiglu_bwd_fn(au_v, df_v):
        a, u = au_v[:, :d_ff], au_v[:, d_ff:]
        sig = _sigmoid(a)
        return [jnp.concatenate([df_v * u * sig * (1.0 + a * (1.0 - sig)), df_v * a * sig], axis=1)]

    (dau,) = _rowmap(swiglu_bwd_fn, [au, df], [("row", 2 * d_ff, BF16)], "swiglu_bwd", ts=256)
    dau2d = dau.reshape(tok, 2 * d_ff)
    dw_gu = _mm_tn(h2_2d, dau2d, "ffn_in_dw")
    dh2 = _mm([(dau2d, w_gu_f.T)], F32, "ffn_in_dx").reshape(bsz, seq, d)

    def norm2_bwd_fn(dh_v, x1_v, dx2_v, mo_v, sc, g1, g):
        rstd = lax.rsqrt(jnp.mean(x1_v * x1_v, axis=-1, keepdims=True) + RMS_EPS)
        xh = x1_v * rstd
        dn = dh_v * (1.0 + sc)
        dxh = dn * g
        dx1 = dx2_v + rstd * (dxh - xh * jnp.mean(dxh * xh, axis=-1, keepdims=True))
        return [dx1, dx1 * g1, jnp.sum(dh_v * (xh * g), axis=0, keepdims=True), jnp.sum(dh_v, axis=0, keepdims=True),
                jnp.sum(dn * xh, axis=0, keepdims=True), jnp.sum(dx1 * mo_v, axis=0, keepdims=True)]

    dx1, dmo, dscale2, dshift2, dg_ffn, dgate1 = _rowmap(
        norm2_bwd_fn, [dh2, x1, dx2, mo, scale2, gate1, g_ffn],
        [("row", d, F32), ("row", d, BF16), ("batch", d, F32), ("batch", d, F32), ("global", d, F32), ("batch", d, F32)],
        "norm2_bwd")

    dmo2d = dmo.reshape(tok, d)
    dmerged = _mm([(dmo2d, w_out_f.T)], F32, "out_proj_dx").reshape(bsz, seq, d)
    dw_out = _mm_tn(merged.reshape(tok, d), dmo2d, "out_proj_dw")

    def merge_bwd_fn(dm, gl_v, us, uf, bg):
        gates = _sigmoid(gl_v + bg)
        gs, gf = gates[:, :d], gates[:, d:]
        dgl = jnp.concatenate([dm * us * gs * (1.0 - gs), dm * uf * gf * (1.0 - gf)], axis=1)
        return [dm * gs, dm * gf, dgl, jnp.sum(dgl, axis=0, keepdims=True)]

    du_sb, du_fx, dgl, db_gate = _rowmap(
        merge_bwd_fn, [dmerged, gl, u_sb, u_fx, b_gate],
        [("row", d, BF16), ("row", d, BF16), ("row", 2 * d, BF16), ("global", 2 * d, F32)], "merge_bwd", ts=256)
    du_sb2d, du_fx2d = du_sb.reshape(tok, d), du_fx.reshape(tok, d)
    dw_sb = _mm_tn(y_sb.reshape(tok, d_att), du_sb2d, "branch_sb_dw")
    dw_fx = _mm_tn(y_fx.reshape(tok, d_att), du_fx2d, "branch_fox_dw")
    dy_sb = _mm([(du_sb2d, w_sb_f.T)], F32, "branch_sb_dx").reshape(bsz, seq, d_att)
    dy_fx = _mm([(du_fx2d, w_fx_f.T)], F32, "branch_fox_dx").reshape(bsz, seq, d_att)

    dq_sb, dk_sb, dv_sb = _sb_bwd(qkv, dy_sb, tot_sb, np_, tq_att, tk_att, "sb_bwd")
    dq_fx, dk_fx, dv_fx, dck, dcq = _fox_bwd(qkv, dy_fx, y_fx, lse_fx, cumq, cumk, np_, tq_att, tk_att, "fox_bwd")
    dck_rows = dck[:, :, :2, :].reshape(bsz, nh, seq).transpose(0, 2, 1)
    dck_rows = jnp.concatenate([dck_rows, jnp.zeros((bsz, seq, LANES - nh), F32)], axis=2)
    dfl, db_f = _cum_bwd(dck_rows, dcq, fl, bf_pad, np_, "cum_bwd")

    dqkv = jnp.concatenate([dq_sb, dk_sb, dv_sb, dq_fx, dk_fx, dv_fx], axis=2).reshape(tok, 6 * d_att)
    dgl2d, dfl2d = dgl.reshape(tok, 2 * d), dfl.reshape(tok, LANES)
    dw_in = jnp.concatenate([_mm_tn(h2d, dqkv, "proj_qkv_dw"), _mm_tn(h2d, dfl2d, "proj_forget_dw")[:, :nh],
                             _mm_tn(h2d, dgl2d, "proj_gates_dw")], axis=1)
    dh = _mm([(dqkv, w_qkv.T), (dgl2d, w_gl.T), (dfl2d, w_f.T)], F32, "proj_dx").reshape(bsz, seq, d)

    def norm1_bwd_fn(dh_v, x_v, dx1_v, sc, g):
        rstd = lax.rsqrt(jnp.mean(x_v * x_v, axis=-1, keepdims=True) + RMS_EPS)
        xh = x_v * rstd
        dn = dh_v * (1.0 + sc)
        dxh = dn * g
        dx = dx1_v + rstd * (dxh - xh * jnp.mean(dxh * xh, axis=-1, keepdims=True))
        return [dx, jnp.sum(dh_v * (xh * g), axis=0, keepdims=True), jnp.sum(dh_v, axis=0, keepdims=True),
                jnp.sum(dn * xh, axis=0, keepdims=True)]

    grad_x, dscale1, dshift1, dg_mix = _rowmap(
        norm1_bwd_fn, [dh, x, dx1, scale1, g_mix],
        [("row", d, F32), ("batch", d, F32), ("batch", d, F32), ("global", d, F32)], "norm1_bwd")

    dmod = jnp.concatenate([dshift1, dscale1, dgate1, dshift2, dscale2, dgate2], axis=2).reshape(bsz, n_mod * d)
    partial = [dg_mix, db_f, db_gate, dg_ffn, dg_final]
    n_dmod_rows = bsz * n_mod * d // LANES
    small_sent = _pack([dmod] + partial, 8)
    small_all = _exchange([small_sent], True, "gather_small")[0]
    small_w = [b_ada, g_mix, jnp.concatenate([b_forget, jnp.zeros((1, LANES - nh), F32)], axis=1), b_gate, g_ffn,
               g_final.reshape(1, d)]
    small_m = [m_b_ada, m_g_mix, jnp.concatenate([m_b_forget, jnp.zeros((1, LANES - nh), F32)], axis=1), m_b_gate,
               m_g_ffn, m_g_final.reshape(1, d)]
    small_v = [v_b_ada, v_g_mix, jnp.concatenate([v_b_forget, jnp.zeros((1, LANES - nh), F32)], axis=1), v_b_gate,
               v_g_ffn, v_g_final.reshape(1, d)]
    small_shapes = [a.shape for a in small_w]
    n_ada_rows = n_mod * d // LANES
    n_part_rows = sum(a.shape[1] // LANES for a in partial)
    sw, sm, sv = _pack(small_w, 8), _pack(small_m, 8), _pack(small_v, 8)
    n_small_rows = sw.shape[0]

    def small_fn(all_v, w_v, m_v, v_v):
        g_ada = None
        g_rest = None
        for k in range(N_DEV):
            for b in range(bsz):
                part = all_v[k, b * n_ada_rows:(b + 1) * n_ada_rows]
                g_ada = part if g_ada is None else g_ada + part
            rest = all_v[k, n_dmod_rows:n_dmod_rows + n_part_rows]
            g_rest = rest if g_rest is None else g_rest + rest
        pieces = [g_ada, g_rest]
        if n_small_rows > n_ada_rows + n_part_rows:
            pieces.append(jnp.zeros((n_small_rows - n_ada_rows - n_part_rows, LANES), F32))
        g = jnp.concatenate(pieces, axis=0)
        return [g, *_adamw_math(w_v, g, m_v, v_v)]

    shp = ((n_small_rows, LANES), F32)
    small_out = _small(small_fn, [small_all, sw, sm, sv], [shp] * 4, "small_update")
    small_g, small_d, small_nm, small_nv = [_unpack(o, small_shapes) for o in small_out]

    def fix_small(lst):
        b_ada_o, g_mix_o, b_f_o, b_gate_o, g_ffn_o, g_final_o = lst
        return [b_ada_o, g_mix_o, b_f_o[:, :nh], b_gate_o, g_ffn_o, g_final_o.reshape(d)]

    small_g, small_d, small_nm, small_nv = [fix_small(l) for l in (small_g, small_d, small_nm, small_nv)]

    dmod_all = small_all[:, :n_dmod_rows].reshape(nb_all, n_mod * d)
    dmod_cols = lax.dynamic_slice(dmod_all, (0, me * ada_cols), (nb_all, ada_cols))

    def ada_dw_fn(c_v, dm_v):
        return [lax.dot_general(_silu(c_v), dm_v, (((0,), (0,)), ((), ())), precision=lax.Precision.HIGHEST,
                                preferred_element_type=F32)]

    (dw_ada,) = _small(ada_dw_fn, [c_all, dmod_cols], [((d, ada_cols), F32)], "ada_dw")
    ada_out = _adamw(dw_ada[None], w_ada[0], m_w_ada[0], v_w_ada[0], "adamw_ada")
    ada_g, ada_d, ada_nm, ada_nv = [o[None] for o in ada_out]

    blocks = [_col_blocks(dw_in), _col_blocks(dw_sb), _col_blocks(dw_fx), dw_out.reshape(N_DEV, d // N_DEV, d),
              _col_blocks(dw_gu[:, :d_ff]), _col_blocks(dw_gu[:, d_ff:]), dw_dn.reshape(N_DEV, d_ff // N_DEV, d)]
    got = _exchange(blocks, False, "exchange_grads")
    names = ["w_in", "w_sb", "w_fox", "w_out", "w_gate", "w_up", "w_down"]
    big_out = [_adamw(g, w, m, v, "adamw_" + n, tr=256) for g, w, m, v, n in zip(got, big, big_m, big_v, names)]
    big_g, big_d, big_nm, big_nv = [[o[i][None] for o in big_out] for i in range(4)]

    def ordered(ada, small, bigs):
        b_ada_o, g_mix_o, b_f_o, b_gate_o, g_ffn_o, g_final_o = small
        w_in_o, w_sb_o, w_fx_o, w_out_o, w_gate_o, w_up_o, w_dn_o = bigs
        return [ada, b_ada_o, g_mix_o, w_in_o, b_f_o, b_gate_o, w_sb_o, w_fx_o, w_out_o, g_ffn_o, w_gate_o, w_up_o,
                w_dn_o, g_final_o]

    return (loss, grad_x, *ordered(ada_g, small_g, big_g), *ordered(ada_d, small_d, big_d),
            *ordered(ada_nm, small_nm, big_nm), *ordered(ada_nv, small_nv, big_nv))
```

```python
import jax
import jax.numpy as jnp
from jax import lax
from jax.experimental import pallas as pl
from jax.experimental.pallas import tpu as pltpu

F32 = jnp.float32
BF16 = jnp.bfloat16
HEAD_DIM = 64
LANES = 128
N_DEV = 8
AXES = ("x", "y", "c")
RMS_EPS = 1e-6
ADAM_LR, ADAM_B1, ADAM_B2, ADAM_EPS, ADAM_WD, ADAM_STEP = 0.001, 0.9, 0.999, 1e-08, 0.01, 10
NEG = -1e30
MESH = pl.DeviceIdType.MESH


def _pick(n, cands):
    for c in cands:
        if n % c == 0:
            return c
    raise ValueError(f"no tile for {n} in {cands}")


def _arb(n):
    return pltpu.CompilerParams(dimension_semantics=("arbitrary",) * n)


def _dot(a, b):
    return jnp.dot(a, b, preferred_element_type=F32)


def _dot_nt(a, b):
    return lax.dot_general(a, b, (((1,), (1,)), ((), ())), preferred_element_type=F32)


def _dot_tn(a, b):
    return lax.dot_general(a, b, (((0,), (0,)), ((), ())), preferred_element_type=F32)


def _split2(v):
    hi = v.astype(BF16)
    return hi, (v - hi.astype(F32)).astype(BF16)


def _dot_split2(v, m):
    hi, lo = _split2(v)
    return _dot(hi, m) + _dot(lo, m)


def _tri_dot3(m, v):
    h1 = v.astype(BF16)
    r1 = v - h1.astype(F32)
    h2 = r1.astype(BF16)
    h3 = (r1 - h2.astype(F32)).astype(BF16)
    return _dot(m, h1) + _dot(m, h2) + _dot(m, h3)


def _sigmoid(v):
    return 1.0 / (1.0 + jnp.exp(-v))


def _silu(v):
    return v * _sigmoid(v)


def _n_tile(n):
    for c in (1024, 512):
        if n % c == 0:
            return c
    return n if n <= 3072 else _pick(n, (256, 128))


def _mm(pairs, out_dtype, name, tm=512):
    m, n = pairs[0][0].shape[0], pairs[0][1].shape[1]
    tm = _pick(m, (tm, 256, 128, 64, 32, 16, 8))
    tn = _n_tile(n)
    npairs = len(pairs)

    def body(*refs):
        acc = None
        for t in range(npairs):
            p = _dot(refs[2 * t][...].astype(BF16), refs[2 * t + 1][...].astype(BF16))
            acc = p if acc is None else acc + p
        refs[2 * npairs][...] = acc.astype(out_dtype)

    in_specs, args = [], []
    for a, b in pairs:
        k = a.shape[1]
        in_specs += [pl.BlockSpec((tm, k), lambda j, i: (i, 0)), pl.BlockSpec((k, tn), lambda j, i: (0, j))]
        args += [a, b]
    return pl.pallas_call(
        body, grid=(n // tn, m // tm), in_specs=in_specs,
        out_specs=pl.BlockSpec((tm, tn), lambda j, i: (i, j)),
        out_shape=jax.ShapeDtypeStruct((m, n), out_dtype), name=name, compiler_params=_arb(2),
    )(*args)


def _mm_tn(a, b, name, tm=512, tn=512, tk=1024):
    t, m = a.shape
    n = b.shape[1]
    tm, tn, tk = _pick(m, (tm, 256, 128)), _pick(n, (tn, 256, 128)), _pick(t, (tk, 512, 256, 128))

    def body(a_ref, b_ref, o_ref):
        @pl.when(pl.program_id(2) == 0)
        def _():
            o_ref[...] = jnp.zeros_like(o_ref)

        o_ref[...] += _dot_tn(a_ref[...].astype(BF16), b_ref[...].astype(BF16))

    return pl.pallas_call(
        body, grid=(m // tm, n // tn, t // tk),
        in_specs=[pl.BlockSpec((tk, tm), lambda i, j, k: (k, i)), pl.BlockSpec((tk, tn), lambda i, j, k: (k, j))],
        out_specs=pl.BlockSpec((tm, tn), lambda i, j, k: (i, j)),
        out_shape=jax.ShapeDtypeStruct((m, n), F32), name=name, compiler_params=_arb(3),
    )(a, b)


def _rowmap(fn, ins, outs, name, ts=512):
    bsz, seq = next(a.shape[:2] for a in ins if a.ndim == 3 and a.shape[1] != 1)
    ts = _pick(seq, (ts, 256, 128, 64, 32, 16, 8))
    n_in = len(ins)

    def in_spec(a):
        if a.ndim == 2:
            return pl.BlockSpec(a.shape, lambda b, s: (0, 0))
        if a.shape[1] == 1:
            return pl.BlockSpec((1, 1, a.shape[2]), lambda b, s: (b, 0, 0))
        return pl.BlockSpec((1, ts, a.shape[2]), lambda b, s: (b, s, 0))

    def out_spec(kind, w):
        if kind == "row":
            return pl.BlockSpec((1, ts, w), lambda b, s: (b, s, 0))
        if kind == "batch":
            return pl.BlockSpec((1, 1, w), lambda b, s: (b, 0, 0))
        return pl.BlockSpec((1, w), lambda b, s: (0, 0))

    def out_shape(kind, w, dt):
        shp = {"row": (bsz, seq, w), "batch": (bsz, 1, w), "global": (1, w)}[kind]
        return jax.ShapeDtypeStruct(shp, dt)

    def body(*refs):
        b, s = pl.program_id(0), pl.program_id(1)
        vals = [r[...] if a.ndim == 2 else r[0] for r, a in zip(refs[:n_in], ins)]
        res = fn(*vals)
        for o_ref, (kind, _, dt), v in zip(refs[n_in:], outs, res):
            if kind == "row":
                o_ref[0] = v.astype(dt)
            elif kind == "batch":
                @pl.when(s == 0)
                def _():
                    o_ref[...] = jnp.zeros_like(o_ref)

                o_ref[0] += v
            else:
                @pl.when((s == 0) & (b == 0))
                def _():
                    o_ref[...] = jnp.zeros_like(o_ref)

                o_ref[...] += v

    return pl.pallas_call(
        body, grid=(bsz, seq // ts), in_specs=[in_spec(a) for a in ins],
        out_specs=[out_spec(k, w) for k, w, _ in outs],
        out_shape=[out_shape(*o) for o in outs], name=name, compiler_params=_arb(2),
    )(*ins)


def _small(fn, ins, out_shapes, name):
    n_in = len(ins)

    def body(*refs):
        res = fn(*[r[...] for r in refs[:n_in]])
        for o_ref, v in zip(refs[n_in:], res):
            o_ref[...] = v

    return pl.pallas_call(body, out_shape=[jax.ShapeDtypeStruct(s, d) for s, d in out_shapes], name=name)(*ins)


def _mesh_pos():
    mx, my, mc = lax.axis_index("x"), lax.axis_index("y"), lax.axis_index("c")
    return mx, my, mc, 4 * mx + 2 * my + mc


def _peer(mx, my, mc, k):
    px = 1 - mx if k & 4 else mx
    py = 1 - my if k & 2 else my
    pc = 1 - mc if k & 1 else mc
    return (px, py, pc), 4 * px + 2 * py + pc


ANY_SPEC = pl.BlockSpec(memory_space=pl.ANY)


def _exchange_shapes(arrs, gather):
    return [jax.ShapeDtypeStruct((N_DEV,) + tuple(x.shape if gather else x.shape[1:]), x.dtype) for x in arrs]


def _exchange_sems(n_arr):
    return [pltpu.SemaphoreType.DMA((n_arr, N_DEV)), pltpu.SemaphoreType.DMA((n_arr, N_DEV)),
            pltpu.SemaphoreType.DMA((n_arr,))]


def _exchange_start(x_refs, out_refs, sems, gather, wait=False):
    send_sems, recv_sems, local_sems = sems
    mx, my, mc, me = _mesh_pos()
    owns, sends, recvs = [], [], []
    for a, (x_ref, out_ref) in enumerate(zip(x_refs, out_refs)):
        owns.append(pltpu.make_async_copy(x_ref if gather else x_ref.at[me], out_ref.at[me], local_sems.at[a]))
        for k in range(1, N_DEV):
            peer, pid = _peer(mx, my, mc, k)
            src = x_ref if gather else x_ref.at[pid]
            sends.append(pltpu.make_async_remote_copy(
                src_ref=src, dst_ref=out_ref.at[me], send_sem=send_sems.at[a, k], recv_sem=recv_sems.at[a, k],
                device_id=peer, device_id_type=MESH))
            if wait:
                recvs.append(pltpu.make_async_remote_copy(
                    src_ref=src, dst_ref=out_ref.at[pid], send_sem=send_sems.at[a, k], recv_sem=recv_sems.at[a, k],
                    device_id=peer, device_id_type=MESH))
    if not wait:
        for cp in owns + sends:
            cp.start()
        return
    for cp in recvs:
        cp.wait_recv()
    for cp in sends:
        cp.wait_send()
    for cp in owns:
        cp.wait()


def _exchange(arrs, gather, name):
    n_arr = len(arrs)

    def body(*refs):
        x_refs, out_refs, sems = refs[:n_arr], refs[n_arr:2 * n_arr], refs[2 * n_arr:]
        _exchange_start(x_refs, out_refs, sems, gather)
        _exchange_start(x_refs, out_refs, sems, gather, wait=True)

    return pl.pallas_call(
        body, out_shape=_exchange_shapes(arrs, gather), in_specs=[ANY_SPEC] * n_arr, out_specs=[ANY_SPEC] * n_arr,
        scratch_shapes=_exchange_sems(n_arr), name=name,
    )(*arrs)


QK_SCALE = HEAD_DIM ** -0.5


def _low_lanes():
    return lax.broadcasted_iota(jnp.int32, (1, LANES), 1) < HEAD_DIM


def _stack_heads(v, scale=None):
    lo = _low_lanes()
    zero = jnp.zeros_like(v)
    s = jnp.concatenate([jnp.where(lo, v, zero), jnp.where(lo, zero, v)], axis=0)
    return s if scale is None else s * scale


def _stack_cols(v):
    return jnp.concatenate([v[:, 0:1], v[:, HEAD_DIM:HEAD_DIM + 1]], axis=0)


def _unstack(v, tq):
    return jnp.where(_low_lanes(), v[:tq], v[tq:])


def _tile_pos(tq, tk, q0):
    rows = lax.broadcasted_iota(jnp.int32, (2 * tq, tk), 0)
    cols = lax.broadcasted_iota(jnp.int32, (2 * tq, tk), 1)
    return q0 + jnp.where(rows >= tq, rows - tq, rows), cols, rows < tq


def _tri(tk, cmp):
    r = lax.broadcasted_iota(jnp.int32, (tk, tk), 0)
    c = lax.broadcasted_iota(jnp.int32, (tk, tk), 1)
    return jnp.where(cmp(r, c), 1.0, 0.0).astype(BF16)


def _softplus(z):
    return jnp.maximum(z, 0.0) + jnp.log(1.0 + jnp.exp(-jnp.abs(z)))


def _pair_specs(tq, seq, np_, off):
    return [pl.BlockSpec((1, tq, LANES), lambda b, p, i: (b, i, off + p)),
            pl.BlockSpec((1, seq, LANES), lambda b, p, i: (b, 0, off + np_ + p)),
            pl.BlockSpec((1, seq, LANES), lambda b, p, i: (b, 0, off + 2 * np_ + p))]


def _key_tiles(tile, carry, q0, tq, tk, upward):
    nfull = q0 // tk
    edge = range(tq // tk)
    if upward:
        carry = lax.fori_loop(0, nfull, lambda j, cr: tile(pl.multiple_of(j * tk, tk), cr, False), carry)
        for jm in edge:
            carry = tile(pl.multiple_of(q0 + jm * tk, tk), carry, True)
        return carry
    for jm in reversed(edge):
        carry = tile(pl.multiple_of(q0 + jm * tk, tk), carry, True)
    return lax.fori_loop(0, nfull, lambda jj, cr: tile(pl.multiple_of((nfull - 1 - jj) * tk, tk), cr, False), carry)


def _grid_ends(bsz, np_, nq):
    b, p, i = pl.program_id(0), pl.program_id(1), pl.program_id(2)
    return (b == 0) & (p == 0) & (i == 0), (b == bsz - 1) & (p == np_ - 1) & (i == nq - 1)


def _sb_fwd(qkv, np_, tq, tk, name, gathered=()):
    bsz, seq, _ = qkv.shape
    n_g = len(gathered)

    def body(*refs):
        q_ref, k_ref, v_ref = refs[:3]
        x_refs, (y_ref, tot_ref) = refs[3:3 + n_g], refs[3 + n_g:5 + n_g]
        out_refs, sems = refs[5 + n_g:5 + 2 * n_g], refs[5 + 2 * n_g:]
        first, last = _grid_ends(bsz, np_, seq // tq)
        if n_g:
            @pl.when(first)
            def _():
                _exchange_start(x_refs, out_refs, sems, True)

        q0 = pl.program_id(2) * tq
        tpos, cols, _ = _tile_pos(tq, tk, q0)
        msuf = _tri(tk, lambda a, b: a > b)
        qs = _stack_heads(q_ref[0], QK_SCALE)

        def tile(k0, carry, masked):
            tot, acc = carry
            z = _dot_nt(qs, k_ref[0, pl.ds(k0, tk), :])
            sp = _softplus(z)
            if masked:
                seen = (k0 + cols) < tpos
                sp = jnp.where(seen, sp, 0.0)
            logw = z - sp - _dot_split2(sp, msuf) - tot
            if masked:
                logw = jnp.where(seen, logw, NEG)
            return (tot + jnp.sum(sp, axis=1, keepdims=True),
                    acc + _dot(jnp.exp(logw).astype(BF16), v_ref[0, pl.ds(k0, tk), :]))

        init = (jnp.zeros((2 * tq, 1), F32), jnp.zeros((2 * tq, LANES), F32))
        tot, acc = _key_tiles(tile, init, q0, tq, tk, upward=False)
        y_ref[0] = _unstack(acc, tq)
        tot_ref[0, 0] = _unstack(tot, tq)
        if n_g:
            @pl.when(last)
            def _():
                _exchange_start(x_refs, out_refs, sems, True, wait=True)

    y, tot, *got = pl.pallas_call(
        body, grid=(bsz, np_, seq // tq), in_specs=_pair_specs(tq, seq, np_, 0) + [ANY_SPEC] * n_g,
        out_specs=[pl.BlockSpec((1, tq, LANES), lambda b, p, i: (b, i, p)),
                   pl.BlockSpec((1, 1, tq, LANES), lambda b, p, i: (b, p, i, 0))] + [ANY_SPEC] * n_g,
        out_shape=[jax.ShapeDtypeStruct((bsz, seq, np_ * LANES), F32),
                   jax.ShapeDtypeStruct((bsz, np_, seq, LANES), F32)] + _exchange_shapes(gathered, True),
        scratch_shapes=_exchange_sems(n_g) if n_g else [],
        name=name, compiler_params=_arb(3),
    )(qkv, qkv, qkv, *gathered)
    return y, tot, got


def _sb_bwd(qkv, dy, tot, np_, tq, tk, name, swapped=()):
    bsz, seq, _ = qkv.shape
    nq = seq // tq
    n_s = len(swapped)

    def body(*refs):
        q_ref, k_ref, v_ref, dy_ref, tot_ref = refs[:5]
        x_refs, (dq_ref, dk_ref, dv_ref) = refs[5:5 + n_s], refs[5 + n_s:8 + n_s]
        out_refs = refs[8 + n_s:8 + 2 * n_s]
        dk_acc, dv_acc = refs[8 + 2 * n_s:10 + 2 * n_s]
        sems = refs[10 + 2 * n_s:]
        first, last = _grid_ends(bsz, np_, nq)
        if n_s:
            @pl.when(first)
            def _():
                _exchange_start(x_refs, out_refs, sems, False)

        i = pl.program_id(2)
        q0 = i * tq

        @pl.when(i == 0)
        def _():
            dk_acc[...] = jnp.zeros_like(dk_acc)
            dv_acc[...] = jnp.zeros_like(dv_acc)

        tpos, cols, _ = _tile_pos(tq, tk, q0)
        mincl = _tri(tk, lambda a, b: a <= b)
        mexcl = _tri(tk, lambda a, b: a < b)
        qs = _stack_heads(q_ref[0], QK_SCALE)
        dys = _stack_heads(dy_ref[0].astype(BF16))
        tots = _stack_cols(tot_ref[0, 0])

        def tile(k0, carry, masked):
            c_sp, c_g, dq = carry
            kb = k_ref[0, pl.ds(k0, tk), :]
            z = _dot_nt(qs, kb)
            sp = _softplus(z)
            if masked:
                seen = (k0 + cols) < tpos
                sp = jnp.where(seen, sp, 0.0)
            logw = z - sp - (tots - c_sp - _dot_split2(sp, mincl))
            if masked:
                logw = jnp.where(seen, logw, NEG)
            w = jnp.exp(logw)
            g = w * _dot_nt(dys, v_ref[0, pl.ds(k0, tk), :])
            beta = jnp.exp(jnp.minimum(z - sp, 0.0))
            dz = g - beta * (g + c_g + _dot_split2(g, mexcl))
            if masked:
                dz = jnp.where(seen, dz, 0.0)
            dzb = dz.astype(BF16)
            dk_acc[pl.ds(k0, tk), :] += _dot_tn(dzb, qs)
            dv_acc[pl.ds(k0, tk), :] += _dot_tn(w.astype(BF16), dys)
            return (c_sp + jnp.sum(sp, axis=1, keepdims=True), c_g + jnp.sum(g, axis=1, keepdims=True),
                    dq + _dot(dzb, kb))

        zero = jnp.zeros((2 * tq, 1), F32)
        _, _, dq = _key_tiles(tile, (zero, zero, jnp.zeros((2 * tq, LANES), F32)), q0, tq, tk, upward=True)
        dq_ref[0] = (_unstack(dq, tq) * QK_SCALE).astype(BF16)

        @pl.when(i == nq - 1)
        def _():
            dk_ref[0] = dk_acc[...].astype(BF16)
            dv_ref[0] = dv_acc[...].astype(BF16)

        if n_s:
            @pl.when(last)
            def _():
                _exchange_start(x_refs, out_refs, sems, False, wait=True)

    tile_spec = pl.BlockSpec((1, tq, LANES), lambda b, p, i: (b, i, p))
    whole = pl.BlockSpec((1, seq, LANES), lambda b, p, i: (b, 0, p))
    out = jax.ShapeDtypeStruct((bsz, seq, np_ * LANES), BF16)
    dq, dk, dv, *got = pl.pallas_call(
        body, grid=(bsz, np_, nq),
        in_specs=_pair_specs(tq, seq, np_, 0) + [tile_spec, pl.BlockSpec((1, 1, tq, LANES), lambda b, p, i: (b, p, i, 0))]
        + [ANY_SPEC] * n_s,
        out_specs=[tile_spec, whole, whole] + [ANY_SPEC] * n_s,
        out_shape=[out, out, out] + _exchange_shapes(swapped, False),
        scratch_shapes=[pltpu.VMEM((seq, LANES), F32), pltpu.VMEM((seq, LANES), F32)] + (_exchange_sems(n_s) if n_s else []),
        name=name, compiler_params=_arb(3),
    )(qkv, qkv, qkv, dy, tot, *swapped)
    return dq, dk, dv, got


def _fox_fwd(qkv, cumq, cumk, np_, tq, tk, name):
    bsz, seq, _ = qkv.shape

    def body(q_ref, k_ref, v_ref, cq_ref, ck_ref, y_ref, lse_ref):
        q0 = pl.program_id(2) * tq
        tpos, cols, top = _tile_pos(tq, tk, q0)
        qs = _stack_heads(q_ref[0], QK_SCALE)
        cq = _stack_cols(cq_ref[0, 0])

        def tile(k0, carry, masked):
            m, l, acc = carry
            ck = jnp.where(top, ck_ref[0, 0, 0:1, pl.ds(k0, tk)], ck_ref[0, 0, 1:2, pl.ds(k0, tk)])
            s = _dot_nt(qs, k_ref[0, pl.ds(k0, tk), :]) + (cq - ck)
            if masked:
                s = jnp.where((k0 + cols) <= tpos, s, NEG)
            m_new = jnp.maximum(m, jnp.max(s, axis=1, keepdims=True))
            p = jnp.exp(s - m_new)
            alpha = jnp.exp(m - m_new)
            return (m_new, alpha * l + jnp.sum(p, axis=1, keepdims=True),
                    alpha * acc + _dot(p.astype(BF16), v_ref[0, pl.ds(k0, tk), :]))

        init = (jnp.full((2 * tq, 1), NEG, F32), jnp.zeros((2 * tq, 1), F32), jnp.zeros((2 * tq, LANES), F32))
        m, l, acc = _key_tiles(tile, init, q0, tq, tk, upward=True)
        y_ref[0] = _unstack(acc / l, tq)
        lse_ref[0, 0] = _unstack(m + jnp.log(l), tq)

    row4 = pl.BlockSpec((1, 1, tq, LANES), lambda b, p, i: (b, p, i, 0))
    return pl.pallas_call(
        body, grid=(bsz, np_, seq // tq),
        in_specs=_pair_specs(tq, seq, np_, 3 * np_) + [row4, pl.BlockSpec((1, 1, 8, seq), lambda b, p, i: (b, p, 0, 0))],
        out_specs=[pl.BlockSpec((1, tq, LANES), lambda b, p, i: (b, i, p)), row4],
        out_shape=[jax.ShapeDtypeStruct((bsz, seq, np_ * LANES), F32),
                   jax.ShapeDtypeStruct((bsz, np_, seq, LANES), F32)],
        name=name, compiler_params=_arb(3),
    )(qkv, qkv, qkv, cumq, cumk)


def _fox_bwd(qkv, dy, y, lse, cumq, cumk, np_, tq, tk, name):
    bsz, seq, _ = qkv.shape
    nq = seq // tq

    def body(q_ref, k_ref, v_ref, dy_ref, y_ref, lse_ref, cq_ref, ck_ref,
             dq_ref, dk_ref, dv_ref, dck_ref, dcq_ref, dk_acc, dv_acc, dck_acc):
        i = pl.program_id(2)
        q0 = i * tq

        @pl.when(i == 0)
        def _():
            dk_acc[...] = jnp.zeros_like(dk_acc)
            dv_acc[...] = jnp.zeros_like(dv_acc)
            dck_acc[...] = jnp.zeros_like(dck_acc)

        tpos, cols, top = _tile_pos(tq, tk, q0)
        qs = _stack_heads(q_ref[0], QK_SCALE)
        dyf = dy_ref[0]
        dys = _stack_heads(dyf.astype(BF16))
        dyy = dyf * y_ref[0]
        lo = _low_lanes()
        delta = jnp.concatenate([jnp.sum(jnp.where(lo, dyy, 0.0), axis=1, keepdims=True),
                                 jnp.sum(jnp.where(lo, 0.0, dyy), axis=1, keepdims=True)], axis=0)
        cq = _stack_cols(cq_ref[0, 0])
        lse_s = _stack_cols(lse_ref[0, 0])

        def tile(k0, carry, masked):
            dq, row = carry
            kb = k_ref[0, pl.ds(k0, tk), :]
            ck = jnp.where(top, ck_ref[0, 0, 0:1, pl.ds(k0, tk)], ck_ref[0, 0, 1:2, pl.ds(k0, tk)])
            s = _dot_nt(qs, kb) + (cq - ck)
            if masked:
                s = jnp.where((k0 + cols) <= tpos, s, NEG)
            p = jnp.exp(s - lse_s)
            ds = p * (_dot_nt(dys, v_ref[0, pl.ds(k0, tk), :]) - delta)
            dsb = ds.astype(BF16)
            dk_acc[pl.ds(k0, tk), :] += _dot_tn(dsb, qs)
            dv_acc[pl.ds(k0, tk), :] += _dot_tn(p.astype(BF16), dys)
            dck_acc[0:1, pl.ds(k0, tk)] += -jnp.sum(ds[:tq], axis=0, keepdims=True)
            dck_acc[1:2, pl.ds(k0, tk)] += -jnp.sum(ds[tq:], axis=0, keepdims=True)
            return dq + _dot(dsb, kb), row + jnp.sum(ds, axis=1, keepdims=True)

        init = (jnp.zeros((2 * tq, LANES), F32), jnp.zeros((2 * tq, 1), F32))
        dq, row = _key_tiles(tile, init, q0, tq, tk, upward=True)
        dq_ref[0] = (_unstack(dq, tq) * QK_SCALE).astype(BF16)
        dcq_ref[0, 0] = _unstack(row, tq)

        @pl.when(i == nq - 1)
        def _():
            dk_ref[0] = dk_acc[...].astype(BF16)
            dv_ref[0] = dv_acc[...].astype(BF16)
            dck_ref[0, 0] = dck_acc[...]

    tile_spec = pl.BlockSpec((1, tq, LANES), lambda b, p, i: (b, i, p))
    whole = pl.BlockSpec((1, seq, LANES), lambda b, p, i: (b, 0, p))
    row4 = pl.BlockSpec((1, 1, tq, LANES), lambda b, p, i: (b, p, i, 0))
    key4 = pl.BlockSpec((1, 1, 8, seq), lambda b, p, i: (b, p, 0, 0))
    out = jax.ShapeDtypeStruct((bsz, seq, np_ * LANES), BF16)
    return pl.pallas_call(
        body, grid=(bsz, np_, nq),
        in_specs=_pair_specs(tq, seq, np_, 3 * np_) + [tile_spec, tile_spec, row4, row4, key4],
        out_specs=[tile_spec, whole, whole, key4, row4],
        out_shape=[out, out, out, jax.ShapeDtypeStruct((bsz, np_, 8, seq), F32),
                   jax.ShapeDtypeStruct((bsz, np_, seq, LANES), F32)],
        scratch_shapes=[pltpu.VMEM((seq, LANES), F32), pltpu.VMEM((seq, LANES), F32), pltpu.VMEM((8, seq), F32)],
        name=name, compiler_params=_arb(3),
    )(qkv, qkv, qkv, dy, y, lse, cumq, cumk)


def _cum_fwd(fl, bf, np_, name, tb=256):
    bsz, seq, _ = fl.shape
    tb = _pick(seq, (tb, 128))

    def body(fl_ref, bf_ref, o_ref, q_ref):
        tri = _tri(tb, lambda a, b: b <= a)
        lo = _low_lanes()

        def step(j, carry):
            r0 = pl.multiple_of(j * tb, tb)
            blk = _tri_dot3(tri, -_softplus(-(fl_ref[0, pl.ds(r0, tb), :] + bf_ref[...]))) + carry
            o_ref[0, pl.ds(r0, tb), :] = blk
            for p in range(np_):
                q_ref[0, p, pl.ds(r0, tb), :] = jnp.where(lo, blk[:, 2 * p:2 * p + 1], blk[:, 2 * p + 1:2 * p + 2])
            return blk[tb - 1:tb, :]

        lax.fori_loop(0, seq // tb, step, jnp.zeros((1, LANES), F32))

    return pl.pallas_call(
        body, grid=(bsz,),
        in_specs=[pl.BlockSpec((1, seq, LANES), lambda b: (b, 0, 0)), pl.BlockSpec((1, LANES), lambda b: (0, 0))],
        out_specs=[pl.BlockSpec((1, seq, LANES), lambda b: (b, 0, 0)),
                   pl.BlockSpec((1, np_, seq, LANES), lambda b: (b, 0, 0, 0))],
        out_shape=[jax.ShapeDtypeStruct(fl.shape, F32), jax.ShapeDtypeStruct((bsz, np_, seq, LANES), F32)],
        name=name, compiler_params=_arb(1),
    )(fl, bf)


def _cum_bwd(dck, dcq, fl, bf, np_, name, tb=256):
    bsz, seq, _ = fl.shape
    tb = _pick(seq, (tb, 128))
    nb = seq // tb

    def body(dck_ref, dcq_ref, fl_ref, bf_ref, o_ref, db_ref):
        @pl.when(pl.program_id(0) == 0)
        def _():
            db_ref[...] = jnp.zeros_like(db_ref)

        tri = _tri(tb, lambda a, b: b >= a)
        lane = lax.broadcasted_iota(jnp.int32, (1, LANES), 1)

        def step(jj, carry):
            tail, tot = carry
            r0 = pl.multiple_of((nb - 1 - jj) * tb, tb)
            dc = dck_ref[0, pl.ds(r0, tb), :]
            for p in range(np_):
                pair = dcq_ref[0, p, pl.ds(r0, tb), :]
                dc = dc + jnp.where(lane == 2 * p, pair, 0.0) + jnp.where(lane == 2 * p + 1, pltpu.roll(pair, HEAD_DIM, 1), 0.0)
            dlf = _tri_dot3(tri, dc) + tail
            dfl = dlf * _sigmoid(-(fl_ref[0, pl.ds(r0, tb), :] + bf_ref[...]))
            o_ref[0, pl.ds(r0, tb), :] = dfl
            return dlf[0:1, :], tot + jnp.sum(dfl, axis=0, keepdims=True)

        zero = jnp.zeros((1, LANES), F32)
        _, tot = lax.fori_loop(0, nb, step, (zero, zero))
        db_ref[...] += tot

    whole = pl.BlockSpec((1, seq, LANES), lambda b: (b, 0, 0))
    vec = pl.BlockSpec((1, LANES), lambda b: (0, 0))
    return pl.pallas_call(
        body, grid=(bsz,), in_specs=[whole, pl.BlockSpec((1, np_, seq, LANES), lambda b: (b, 0, 0, 0)), whole, vec],
        out_specs=[whole, vec],
        out_shape=[jax.ShapeDtypeStruct(fl.shape, F32), jax.ShapeDtypeStruct((1, LANES), F32)],
        name=name, compiler_params=_arb(1),
    )(dck, dcq, fl, bf)


def _adamw_math(w, g, m, v):
    m = ADAM_B1 * m + (1.0 - ADAM_B1) * g
    v = ADAM_B2 * v + (1.0 - ADAM_B2) * (g * g)
    m_hat = m / (1.0 - ADAM_B1 ** ADAM_STEP)
    v_hat = v / (1.0 - ADAM_B2 ** ADAM_STEP)
    return -ADAM_LR * (m_hat / (jnp.sqrt(v_hat) + ADAM_EPS) + ADAM_WD * w), m, v


def _adamw(gparts, w, m, v, name, tr=512):
    nslots, rows, cols = gparts.shape
    tr = _pick(rows, (tr, 256, 128, 64, 32, 16, 8))

    def body(g_ref, w_ref, m_ref, v_ref, go_ref, d_ref, mo_ref, vo_ref):
        g = g_ref[0]
        for k in range(1, nslots):
            g = g + g_ref[k]
        go_ref[...] = g
        d_ref[...], mo_ref[...], vo_ref[...] = _adamw_math(w_ref[...], g, m_ref[...], v_ref[...])

    blk = pl.BlockSpec((tr, cols), lambda i: (i, 0))
    shp = jax.ShapeDtypeStruct((rows, cols), F32)
    return pl.pallas_call(
        body, grid=(rows // tr,), in_specs=[pl.BlockSpec((nslots, tr, cols), lambda i: (0, i, 0)), blk, blk, blk],
        out_specs=[blk] * 4, out_shape=[shp] * 4, name=name, compiler_params=_arb(1),
    )(gparts, w, m, v)


def _rows128(a):
    return a.reshape(-1, LANES)


def _pad_rows(a, mult):
    extra = (-a.shape[0]) % mult
    return a if extra == 0 else jnp.concatenate([a, jnp.zeros((extra, a.shape[1]), a.dtype)], axis=0)


def _pack(arrs, mult):
    return _pad_rows(jnp.concatenate([_rows128(a) for a in arrs], axis=0), mult)


def _unpack(flat, shapes):
    out, off = [], 0
    for shp in shapes:
        n = 1
        for s in shp:
            n *= s
        out.append(flat[off:off + n // LANES].reshape(shp))
        off += n // LANES
    return out


def _col_blocks(full):
    k, n = full.shape
    return full.reshape(k, N_DEV, n // N_DEV).transpose(1, 0, 2)


def _from_col_blocks(blocks):
    _, k, n = blocks.shape
    return blocks.transpose(1, 0, 2).reshape(k, N_DEV * n)


def kernel(x, c, w_ada, b_ada, g_mix, w_in, b_forget, b_gate, w_branch_sb, w_branch_fox, w_out, g_ffn, w_ffn_gate, w_ffn_up, w_ffn_down, g_final, loss_target, m_w_ada, m_b_ada, m_g_mix, m_w_in, m_b_forget, m_b_gate, m_w_branch_sb, m_w_branch_fox, m_w_out, m_g_ffn, m_w_ffn_gate, m_w_ffn_up, m_w_ffn_down, m_g_final, v_w_ada, v_b_ada, v_g_mix, v_w_in, v_b_forget, v_b_gate, v_w_branch_sb, v_w_branch_fox, v_w_out, v_g_ffn, v_w_ffn_gate, v_w_ffn_up, v_w_ffn_down, v_g_final):
    bsz, seq, d = x.shape
    tok = bsz * seq
    nh = b_forget.shape[-1]
    d_in = w_in.shape[-1] * N_DEV
    d_att = (d_in - nh - 2 * d) // 6
    assert d_att == nh * HEAD_DIM and nh % 2 == 0
    np_ = nh // 2
    d_ff = w_ffn_gate.shape[-1] * N_DEV
    n_mod = w_ada.shape[-1] * N_DEV // d
    me = 4 * lax.axis_index("x") + 2 * lax.axis_index("y") + lax.axis_index("c")
    tq_att = _pick(seq, (512, 256, 128))
    tk_att = _pick(tq_att, (256, 128))

    big = [w_in[0], w_branch_sb[0], w_branch_fox[0], w_out[0], w_ffn_gate[0], w_ffn_up[0], w_ffn_down[0]]
    big_m = [m_w_in[0], m_w_branch_sb[0], m_w_branch_fox[0], m_w_out[0], m_w_ffn_gate[0], m_w_ffn_up[0], m_w_ffn_down[0]]
    big_v = [v_w_in[0], v_w_branch_sb[0], v_w_branch_fox[0], v_w_out[0], v_w_ffn_gate[0], v_w_ffn_up[0], v_w_ffn_down[0]]

    w_in_f = _from_col_blocks(_exchange([big[0].astype(BF16)], True, "gather_w_in")[0])
    w_qkv = w_in_f[:, :6 * d_att]
    w_f = jnp.concatenate([w_in_f[:, 6 * d_att:6 * d_att + nh], jnp.zeros((d, LANES - nh), BF16)], axis=1)
    w_gl = w_in_f[:, 6 * d_att + nh:]

    c_all = _exchange([_rows128(c)], True, "gather_c")[0].reshape(N_DEV * bsz, d)
    nb_all = N_DEV * bsz
    ada_cols = w_ada.shape[-1]
    b_ada_loc = lax.dynamic_slice(b_ada, (0, me * ada_cols), (1, ada_cols))

    def mod_fn(c_v, w_v, b_v):
        return [jnp.dot(_silu(c_v), w_v, precision=lax.Precision.HIGHEST, preferred_element_type=F32) + b_v]

    (mod_part,) = _small(mod_fn, [c_all, w_ada[0], b_ada_loc], [((nb_all, ada_cols), F32)], "ada_mod")
    mod_all = _from_col_blocks(_exchange([_rows128(mod_part)], True, "gather_mod")[0].reshape(N_DEV, nb_all, ada_cols))
    mod = lax.dynamic_slice(mod_all, (me * bsz, 0), (bsz, n_mod * d))
    shift1, scale1, gate1, shift2, scale2, gate2 = [mod[:, i * d:(i + 1) * d].reshape(bsz, 1, d) for i in range(6)]

    def norm_mod_fn(x_v, sc, sh, g):
        n = x_v * lax.rsqrt(jnp.mean(x_v * x_v, axis=-1, keepdims=True) + RMS_EPS) * g
        return [n * (1.0 + sc) + sh]

    (h,) = _rowmap(norm_mod_fn, [x, scale1, shift1, g_mix], [("row", d, BF16)], "norm1")
    h2d = h.reshape(tok, d)
    qkv = _mm([(h2d, w_qkv)], BF16, "proj_qkv").reshape(bsz, seq, 6 * d_att)
    gl = _mm([(h2d, w_gl)], F32, "proj_gates").reshape(bsz, seq, 2 * d)
    fl = _mm([(h2d, w_f)], F32, "proj_forget").reshape(bsz, seq, LANES)

    bf_pad = jnp.concatenate([b_forget, jnp.zeros((1, LANES - nh), F32)], axis=1)
    cum, cumq = _cum_fwd(fl, bf_pad, np_, "cum_fwd")
    cumk = jnp.concatenate([cum[:, :, :nh].transpose(0, 2, 1).reshape(bsz, np_, 2, seq),
                            jnp.zeros((bsz, np_, 6, seq), F32)], axis=2)

    y_sb, tot_sb, gath = _sb_fwd(qkv, np_, tq_att, tk_att, "sb_fwd", gathered=[a.astype(BF16) for a in big[1:]])
    w_sb_f = _from_col_blocks(gath[0])
    w_fx_f = _from_col_blocks(gath[1])
    w_out_f = gath[2].reshape(d, d)
    w_gu_f = jnp.concatenate([_from_col_blocks(gath[3]), _from_col_blocks(gath[4])], axis=1)
    w_dn_f = gath[5].reshape(d_ff, d)
    y_fx, lse_fx = _fox_fwd(qkv, cumq, cumk, np_, tq_att, tk_att, "fox_fwd")

    u_sb = _mm([(y_sb.reshape(tok, d_att), w_sb_f)], F32, "branch_sb").reshape(bsz, seq, d)
    u_fx = _mm([(y_fx.reshape(tok, d_att), w_fx_f)], F32, "branch_fox").reshape(bsz, seq, d)

    def merge_fn(gl_v, us, uf, bg):
        gates = _sigmoid(gl_v + bg)
        return [gates[:, :d] * us + gates[:, d:] * uf]

    (merged,) = _rowmap(merge_fn, [gl, u_sb, u_fx, b_gate], [("row", d, BF16)], "merge")
    mo = _mm([(merged.reshape(tok, d), w_out_f)], F32, "out_proj").reshape(bsz, seq, d)

    def resid_norm_fn(x_v, mo_v, g1, sc, sh, g):
        x1_v = x_v + g1 * mo_v
        n = x1_v * lax.rsqrt(jnp.mean(x1_v * x1_v, axis=-1, keepdims=True) + RMS_EPS) * g
        return [x1_v, n * (1.0 + sc) + sh]

    x1, h2 = _rowmap(resid_norm_fn, [x, mo, gate1, scale2, shift2, g_ffn], [("row", d, F32), ("row", d, BF16)], "norm2")
    h2_2d = h2.reshape(tok, d)
    au = _mm([(h2_2d, w_gu_f)], F32, "ffn_in").reshape(bsz, seq, 2 * d_ff)

    def swiglu_fn(au_v):
        return [_silu(au_v[:, :d_ff]) * au_v[:, d_ff:]]

    (f,) = _rowmap(swiglu_fn, [au], [("row", d_ff, BF16)], "swiglu", ts=256)
    ffn = _mm([(f.reshape(tok, d_ff), w_dn_f)], F32, "ffn_out").reshape(bsz, seq, d)

    def head_fn(x1_v, ffn_v, g2, gf, tgt):
        x2 = x1_v + g2 * ffn_v
        rstd = lax.rsqrt(jnp.mean(x2 * x2, axis=-1, keepdims=True) + RMS_EPS)
        xh = x2 * rstd
        err = xh * gf - tgt
        loss_rows = 0.5 * jnp.mean(err * err, axis=-1, keepdims=True)
        dy = err * (1.0 / d)
        dxh = dy * gf
        dx2 = rstd * (dxh - xh * jnp.mean(dxh * xh, axis=-1, keepdims=True))
        return [dx2, dx2 * g2, jnp.sum(loss_rows, axis=0, keepdims=True) * jnp.ones((1, LANES), F32),
                jnp.sum(dy * xh, axis=0, keepdims=True), jnp.sum(dx2 * ffn_v, axis=0, keepdims=True)]

    dx2, dffn, loss_vec, dg_final, dgate2 = _rowmap(
        head_fn, [x1, ffn, gate2, g_final.reshape(1, d), loss_target],
        [("row", d, F32), ("row", d, BF16), ("global", LANES, F32), ("global", d, F32), ("batch", d, F32)], "head")
    loss = lax.psum(loss_vec[0, 0], AXES)

    dffn2d = dffn.reshape(tok, d)
    df = _mm([(dffn2d, w_dn_f.T)], F32, "ffn_out_dx").reshape(bsz, seq, d_ff)
    dw_dn = _mm_tn(f.reshape(tok, d_ff), dffn2d, "ffn_out_dw")

    def swiglu_bwd_fn(au_v, df_v):
        a, u = au_v[:, :d_ff], au_v[:, d_ff:]
        sig = _sigmoid(a)
        return [jnp.concatenate([df_v * u * sig * (1.0 + a * (1.0 - sig)), df_v * a * sig], axis=1)]

    (dau,) = _rowmap(swiglu_bwd_fn, [au, df], [("row", 2 * d_ff, BF16)], "swiglu_bwd", ts=256)
    dau2d = dau.reshape(tok, 2 * d_ff)
    dw_gu = _mm_tn(h2_2d, dau2d, "ffn_in_dw")
    dh2 = _mm([(dau2d, w_gu_f.T)], F32, "ffn_in_dx").reshape(bsz, seq, d)

    def norm2_bwd_fn(dh_v, x1_v, dx2_v, mo_v, sc, g1, g):
        rstd = lax.rsqrt(jnp.mean(x1_v * x1_v, axis=-1, keepdims=True) + RMS_EPS)
        xh = x1_v * rstd
        dn = dh_v * (1.0 + sc)
        dxh = dn * g
        dx1 = dx2_v + rstd * (dxh - xh * jnp.mean(dxh * xh, axis=-1, keepdims=True))
        return [dx1, dx1 * g1, jnp.sum(dh_v * (xh * g), axis=0, keepdims=True), jnp.sum(dh_v, axis=0, keepdims=True),
                jnp.sum(dn * xh, axis=0, keepdims=True), jnp.sum(dx1 * mo_v, axis=0, keepdims=True)]

    dx1, dmo, dscale2, dshift2, dg_ffn, dgate1 = _rowmap(
        norm2_bwd_fn, [dh2, x1, dx2, mo, scale2, gate1, g_ffn],
        [("row", d, F32), ("row", d, BF16), ("batch", d, F32), ("batch", d, F32), ("global", d, F32), ("batch", d, F32)],
        "norm2_bwd")

    dmo2d = dmo.reshape(tok, d)
    dmerged = _mm([(dmo2d, w_out_f.T)], F32, "out_proj_dx").reshape(bsz, seq, d)
    dw_out = _mm_tn(merged.reshape(tok, d), dmo2d, "out_proj_dw")

    def merge_bwd_fn(dm, gl_v, us, uf, bg):
        gates = _sigmoid(gl_v + bg)
        gs, gf = gates[:, :d], gates[:, d:]
        dgl = jnp.concatenate([dm * us * gs * (1.0 - gs), dm * uf * gf * (1.0 - gf)], axis=1)
        return [dm * gs, dm * gf, dgl, jnp.sum(dgl, axis=0, keepdims=True)]

    du_sb, du_fx, dgl, db_gate = _rowmap(
        merge_bwd_fn, [dmerged, gl, u_sb, u_fx, b_gate],
        [("row", d, BF16), ("row", d, BF16), ("row", 2 * d, BF16), ("global", 2 * d, F32)], "merge_bwd", ts=256)
    du_sb2d, du_fx2d = du_sb.reshape(tok, d), du_fx.reshape(tok, d)
    dw_sb = _mm_tn(y_sb.reshape(tok, d_att), du_sb2d, "branch_sb_dw")
    dw_fx = _mm_tn(y_fx.reshape(tok, d_att), du_fx2d, "branch_fox_dw")
    dy_sb = _mm([(du_sb2d, w_sb_f.T)], F32, "branch_sb_dx").reshape(bsz, seq, d_att)
    dy_fx = _mm([(du_fx2d, w_fx_f.T)], F32, "branch_fox_dx").reshape(bsz, seq, d_att)

    blocks = [_col_blocks(dw_sb), _col_blocks(dw_fx), dw_out.reshape(N_DEV, d // N_DEV, d),
              _col_blocks(dw_gu[:, :d_ff]), _col_blocks(dw_gu[:, d_ff:]), dw_dn.reshape(N_DEV, d_ff // N_DEV, d)]
    dq_sb, dk_sb, dv_sb, got = _sb_bwd(qkv, dy_sb, tot_sb, np_, tq_att, tk_att, "sb_bwd", swapped=blocks)
    dq_fx, dk_fx, dv_fx, dck, dcq = _fox_bwd(qkv, dy_fx, y_fx, lse_fx, cumq, cumk, np_, tq_att, tk_att, "fox_bwd")
    dck_rows = dck[:, :, :2, :].reshape(bsz, nh, seq).transpose(0, 2, 1)
    dck_rows = jnp.concatenate([dck_rows, jnp.zeros((bsz, seq, LANES - nh), F32)], axis=2)
    dfl, db_f = _cum_bwd(dck_rows, dcq, fl, bf_pad, np_, "cum_bwd")

    dqkv = jnp.concatenate([dq_sb, dk_sb, dv_sb, dq_fx, dk_fx, dv_fx], axis=2).reshape(tok, 6 * d_att)
    dgl2d, dfl2d = dgl.reshape(tok, 2 * d), dfl.reshape(tok, LANES)
    dw_in = jnp.concatenate([_mm_tn(h2d, dqkv, "proj_qkv_dw"), _mm_tn(h2d, dfl2d, "proj_forget_dw")[:, :nh],
                             _mm_tn(h2d, dgl2d, "proj_gates_dw")], axis=1)
    dh = _mm([(dqkv, w_qkv.T), (dgl2d, w_gl.T), (dfl2d, w_f.T)], F32, "proj_dx").reshape(bsz, seq, d)

    def norm1_bwd_fn(dh_v, x_v, dx1_v, sc, g):
        rstd = lax.rsqrt(jnp.mean(x_v * x_v, axis=-1, keepdims=True) + RMS_EPS)
        xh = x_v * rstd
        dn = dh_v * (1.0 + sc)
        dxh = dn * g
        dx = dx1_v + rstd * (dxh - xh * jnp.mean(dxh * xh, axis=-1, keepdims=True))
        return [dx, jnp.sum(dh_v * (xh * g), axis=0, keepdims=True), jnp.sum(dh_v, axis=0, keepdims=True),
                jnp.sum(dn * xh, axis=0, keepdims=True)]

    grad_x, dscale1, dshift1, dg_mix = _rowmap(
        norm1_bwd_fn, [dh, x, dx1, scale1, g_mix],
        [("row", d, F32), ("batch", d, F32), ("batch", d, F32), ("global", d, F32)], "norm1_bwd")

    dmod = jnp.concatenate([dshift1, dscale1, dgate1, dshift2, dscale2, dgate2], axis=2).reshape(bsz, n_mod * d)
    partial = [dg_mix, db_f, db_gate, dg_ffn, dg_final]
    n_dmod_rows = bsz * n_mod * d // LANES
    small_sent = _pack([dmod] + partial, 8)
    small_all = _exchange([small_sent], True, "gather_small")[0]
    small_w = [b_ada, g_mix, jnp.concatenate([b_forget, jnp.zeros((1, LANES - nh), F32)], axis=1), b_gate, g_ffn,
               g_final.reshape(1, d)]
    small_m = [m_b_ada, m_g_mix, jnp.concatenate([m_b_forget, jnp.zeros((1, LANES - nh), F32)], axis=1), m_b_gate,
               m_g_ffn, m_g_final.reshape(1, d)]
    small_v = [v_b_ada, v_g_mix, jnp.concatenate([v_b_forget, jnp.zeros((1, LANES - nh), F32)], axis=1), v_b_gate,
               v_g_ffn, v_g_final.reshape(1, d)]
    small_shapes = [a.shape for a in small_w]
    n_ada_rows = n_mod * d // LANES
    n_part_rows = sum(a.shape[1] // LANES for a in partial)
    sw, sm, sv = _pack(small_w, 8), _pack(small_m, 8), _pack(small_v, 8)
    n_small_rows = sw.shape[0]

    def small_fn(all_v, w_v, m_v, v_v):
        g_ada = None
        g_rest = None
        for k in range(N_DEV):
            for b in range(bsz):
                part = all_v[k, b * n_ada_rows:(b + 1) * n_ada_rows]
                g_ada = part if g_ada is None else g_ada + part
            rest = all_v[k, n_dmod_rows:n_dmod_rows + n_part_rows]
            g_rest = rest if g_rest is None else g_rest + rest
        pieces = [g_ada, g_rest]
        if n_small_rows > n_ada_rows + n_part_rows:
            pieces.append(jnp.zeros((n_small_rows - n_ada_rows - n_part_rows, LANES), F32))
        g = jnp.concatenate(pieces, axis=0)
        return [g, *_adamw_math(w_v, g, m_v, v_v)]

    shp = ((n_small_rows, LANES), F32)
    small_out = _small(small_fn, [small_all, sw, sm, sv], [shp] * 4, "small_update")
    small_g, small_d, small_nm, small_nv = [_unpack(o, small_shapes) for o in small_out]

    def fix_small(lst):
        b_ada_o, g_mix_o, b_f_o, b_gate_o, g_ffn_o, g_final_o = lst
        return [b_ada_o, g_mix_o, b_f_o[:, :nh], b_gate_o, g_ffn_o, g_final_o.reshape(d)]

    small_g, small_d, small_nm, small_nv = [fix_small(l) for l in (small_g, small_d, small_nm, small_nv)]

    dmod_all = small_all[:, :n_dmod_rows].reshape(nb_all, n_mod * d)
    dmod_cols = lax.dynamic_slice(dmod_all, (0, me * ada_cols), (nb_all, ada_cols))

    def ada_dw_fn(c_v, dm_v):
        return [lax.dot_general(_silu(c_v), dm_v, (((0,), (0,)), ((), ())), precision=lax.Precision.HIGHEST,
                                preferred_element_type=F32)]

    (dw_ada,) = _small(ada_dw_fn, [c_all, dmod_cols], [((d, ada_cols), F32)], "ada_dw")
    ada_out = _adamw(dw_ada[None], w_ada[0], m_w_ada[0], v_w_ada[0], "adamw_ada")
    ada_g, ada_d, ada_nm, ada_nv = [o[None] for o in ada_out]

    got = list(_exchange([_col_blocks(dw_in)], False, "exchange_dw_in")) + list(got)
    names = ["w_in", "w_sb", "w_fox", "w_out", "w_gate", "w_up", "w_down"]
    big_out = [_adamw(g, w, m, v, "adamw_" + n, tr=256) for g, w, m, v, n in zip(got, big, big_m, big_v, names)]
    big_g, big_d, big_nm, big_nv = [[o[i][None] for o in big_out] for i in range(4)]

    def ordered(ada, small, bigs):
        b_ada_o, g_mix_o, b_f_o, b_gate_o, g_ffn_o, g_final_o = small
        w_in_o, w_sb_o, w_fx_o, w_out_o, w_gate_o, w_up_o, w_dn_o = bigs
        return [ada, b_ada_o, g_mix_o, w_in_o, b_f_o, b_gate_o, w_sb_o, w_fx_o, w_out_o, g_ffn_o, w_gate_o, w_up_o,
                w_dn_o, g_final_o]

    return (loss, grad_x, *ordered(ada_g, small_g, big_g), *ordered(ada_d, small_d, big_d),
            *ordered(ada_nm, small_nm, big_nm), *ordered(ada_nv, small_nv, big_nv))
```

```python
import jax
import jax.numpy as jnp
from jax import lax
from jax.experimental import pallas as pl
from jax.experimental.pallas import tpu as pltpu

F32 = jnp.float32
BF16 = jnp.bfloat16
HEAD_DIM = 64
LANES = 128
N_DEV = 8
AXES = ("x", "y", "c")
RMS_EPS = 1e-6
ADAM_LR, ADAM_B1, ADAM_B2, ADAM_EPS, ADAM_WD, ADAM_STEP = 0.001, 0.9, 0.999, 1e-08, 0.01, 10
NEG = -1e30
MESH = pl.DeviceIdType.MESH


def _pick(n, cands):
    for c in cands:
        if n % c == 0:
            return c
    raise ValueError(f"no tile for {n} in {cands}")


def _arb(n):
    return pltpu.CompilerParams(dimension_semantics=("arbitrary",) * n)


def _dot(a, b):
    return jnp.dot(a, b, preferred_element_type=F32)


def _dot_nt(a, b):
    return lax.dot_general(a, b, (((1,), (1,)), ((), ())), preferred_element_type=F32)


def _dot_tn(a, b):
    return lax.dot_general(a, b, (((0,), (0,)), ((), ())), preferred_element_type=F32)


def _split2(v):
    hi = v.astype(BF16)
    return hi, (v - hi.astype(F32)).astype(BF16)


def _dot_split2(v, m):
    hi, lo = _split2(v)
    return _dot(hi, m) + _dot(lo, m)


def _tri_dot3(m, v):
    h1 = v.astype(BF16)
    r1 = v - h1.astype(F32)
    h2 = r1.astype(BF16)
    h3 = (r1 - h2.astype(F32)).astype(BF16)
    return _dot(m, h1) + _dot(m, h2) + _dot(m, h3)


def _sigmoid(v):
    return 1.0 / (1.0 + jnp.exp(-v))


def _silu(v):
    return v * _sigmoid(v)


VMEM_BLOCK_BUDGET = 44 << 20


def _col_tiles(n):
    return [n // q for q in range(1, n // LANES + 1) if n % q == 0 and (n // q) % LANES == 0]


def _size(dt):
    return jnp.dtype(dt).itemsize


def _mm(pairs, out_dtypes, name, tm=512, extras=(), epilogue=None):
    m, n = pairs[0][0].shape[0], pairs[0][1].shape[1]
    tm = _pick(m, (tm, 256, 128, 64, 32, 16, 8))
    lhs = []
    for a, _ in pairs:
        if not any(a is x for x in lhs):
            lhs.append(a)
    odts = out_dtypes if epilogue is not None else [out_dtypes]
    n_acc = len(pairs) if epilogue is not None else 1
    per_col = (sum(b.shape[0] * _size(b.dtype) for _, b in pairs) * 2
               + tm * 2 * (sum(_size(d) for d in odts) + sum(_size(e.dtype) for e in extras)) + tm * 4 * n_acc)
    fixed = 2 * sum(tm * a.shape[1] * _size(a.dtype) for a in lhs)
    tn = next((c for c in _col_tiles(n) if fixed + per_col * c <= VMEM_BLOCK_BUDGET), LANES)
    n_l, n_p, n_e = len(lhs), len(pairs), len(extras)

    def body(*refs):
        l_refs, b_refs, e_refs, o_refs = refs[:n_l], refs[n_l:n_l + n_p], refs[n_l + n_p:n_l + n_p + n_e], refs[n_l + n_p + n_e:]
        vals = [r[...].astype(BF16) for r in l_refs]
        accs = []
        for (a, _), b_ref in zip(pairs, b_refs):
            av = vals[next(i for i, x in enumerate(lhs) if x is a)]
            accs.append(_dot(av, b_ref[...].astype(BF16)))
        if epilogue is None:
            outs = [sum(accs[1:], accs[0])]
        else:
            outs = epilogue(accs, [r[...] for r in e_refs])
        for o_ref, v, dt in zip(o_refs, outs, odts):
            o_ref[...] = v.astype(dt)

    tile = pl.BlockSpec((tm, tn), lambda j, i: (i, j))
    res = pl.pallas_call(
        body, grid=(n // tn, m // tm),
        in_specs=[pl.BlockSpec((tm, a.shape[1]), lambda j, i: (i, 0)) for a in lhs]
        + [pl.BlockSpec((b.shape[0], tn), lambda j, i: (0, j)) for _, b in pairs] + [tile] * n_e,
        out_specs=[tile] * len(odts), out_shape=[jax.ShapeDtypeStruct((m, n), d) for d in odts],
        name=name, compiler_params=_arb(2),
    )(*lhs, *[b for _, b in pairs], *extras)
    return res if epilogue is not None else res[0]


def _mm_tn(a, b, name):
    t, m = a.shape
    n = b.shape[1]

    def fits(tm, tn, tk):
        return 2 * (tk * tm * _size(a.dtype) + tk * tn * _size(b.dtype) + tm * tn * 4) <= VMEM_BLOCK_BUDGET

    tm, tn, tk = next((tm, tn, tk) for tn in _col_tiles(n) for tm in _col_tiles(m) if tm <= 1536
                      for tk in (1024, 512, 256, 128) if t % tk == 0 and fits(tm, tn, tk))

    def body(a_ref, b_ref, o_ref):
        @pl.when(pl.program_id(2) == 0)
        def _():
            o_ref[...] = jnp.zeros_like(o_ref)

        o_ref[...] += _dot_tn(a_ref[...].astype(BF16), b_ref[...].astype(BF16))

    return pl.pallas_call(
        body, grid=(m // tm, n // tn, t // tk),
        in_specs=[pl.BlockSpec((tk, tm), lambda i, j, k: (k, i)), pl.BlockSpec((tk, tn), lambda i, j, k: (k, j))],
        out_specs=pl.BlockSpec((tm, tn), lambda i, j, k: (i, j)),
        out_shape=jax.ShapeDtypeStruct((m, n), F32), name=name, compiler_params=_arb(3),
    )(a, b)


def _rowmap(fn, ins, outs, name, ts=512):
    bsz, seq = next(a.shape[:2] for a in ins if a.ndim == 3 and a.shape[1] != 1)
    ts = _pick(seq, (ts, 256, 128, 64, 32, 16, 8))
    n_in = len(ins)

    def in_spec(a):
        if a.ndim == 2:
            return pl.BlockSpec(a.shape, lambda b, s: (0, 0))
        if a.shape[1] == 1:
            return pl.BlockSpec((1, 1, a.shape[2]), lambda b, s: (b, 0, 0))
        return pl.BlockSpec((1, ts, a.shape[2]), lambda b, s: (b, s, 0))

    def out_spec(kind, w):
        if kind == "row":
            return pl.BlockSpec((1, ts, w), lambda b, s: (b, s, 0))
        if kind == "batch":
            return pl.BlockSpec((1, 1, w), lambda b, s: (b, 0, 0))
        return pl.BlockSpec((1, w), lambda b, s: (0, 0))

    def out_shape(kind, w, dt):
        shp = {"row": (bsz, seq, w), "batch": (bsz, 1, w), "global": (1, w)}[kind]
        return jax.ShapeDtypeStruct(shp, dt)

    def body(*refs):
        b, s = pl.program_id(0), pl.program_id(1)
        vals = [r[...] if a.ndim == 2 else r[0] for r, a in zip(refs[:n_in], ins)]
        res = fn(*vals)
        for o_ref, (kind, _, dt), v in zip(refs[n_in:], outs, res):
            if kind == "row":
                o_ref[0] = v.astype(dt)
            elif kind == "batch":
                @pl.when(s == 0)
                def _():
                    o_ref[...] = jnp.zeros_like(o_ref)

                o_ref[0] += v
            else:
                @pl.when((s == 0) & (b == 0))
                def _():
                    o_ref[...] = jnp.zeros_like(o_ref)

                o_ref[...] += v

    return pl.pallas_call(
        body, grid=(bsz, seq // ts), in_specs=[in_spec(a) for a in ins],
        out_specs=[out_spec(k, w) for k, w, _ in outs],
        out_shape=[out_shape(*o) for o in outs], name=name, compiler_params=_arb(2),
    )(*ins)


def _small(fn, ins, out_shapes, name):
    n_in = len(ins)

    def body(*refs):
        res = fn(*[r[...] for r in refs[:n_in]])
        for o_ref, v in zip(refs[n_in:], res):
            o_ref[...] = v

    return pl.pallas_call(body, out_shape=[jax.ShapeDtypeStruct(s, d) for s, d in out_shapes], name=name)(*ins)


def _mesh_pos():
    mx, my, mc = lax.axis_index("x"), lax.axis_index("y"), lax.axis_index("c")
    return mx, my, mc, 4 * mx + 2 * my + mc


def _peer(mx, my, mc, k):
    px = 1 - mx if k & 4 else mx
    py = 1 - my if k & 2 else my
    pc = 1 - mc if k & 1 else mc
    return (px, py, pc), 4 * px + 2 * py + pc


ANY_SPEC = pl.BlockSpec(memory_space=pl.ANY)


def _exchange_shapes(arrs, gather):
    return [jax.ShapeDtypeStruct((N_DEV,) + tuple(x.shape if gather else x.shape[1:]), x.dtype) for x in arrs]


def _exchange_sems(n_arr):
    return [pltpu.SemaphoreType.DMA((n_arr, N_DEV)), pltpu.SemaphoreType.DMA((n_arr, N_DEV)),
            pltpu.SemaphoreType.DMA((n_arr,))]


def _exchange_start(x_refs, out_refs, sems, gather, wait=False):
    send_sems, recv_sems, local_sems = sems
    mx, my, mc, me = _mesh_pos()
    owns, sends, recvs = [], [], []
    for a, (x_ref, out_ref) in enumerate(zip(x_refs, out_refs)):
        owns.append(pltpu.make_async_copy(x_ref if gather else x_ref.at[me], out_ref.at[me], local_sems.at[a]))
        for k in range(1, N_DEV):
            peer, pid = _peer(mx, my, mc, k)
            src = x_ref if gather else x_ref.at[pid]
            sends.append(pltpu.make_async_remote_copy(
                src_ref=src, dst_ref=out_ref.at[me], send_sem=send_sems.at[a, k], recv_sem=recv_sems.at[a, k],
                device_id=peer, device_id_type=MESH))
            if wait:
                recvs.append(pltpu.make_async_remote_copy(
                    src_ref=src, dst_ref=out_ref.at[pid], send_sem=send_sems.at[a, k], recv_sem=recv_sems.at[a, k],
                    device_id=peer, device_id_type=MESH))
    if not wait:
        for cp in owns + sends:
            cp.start()
        return
    for cp in recvs:
        cp.wait_recv()
    for cp in sends:
        cp.wait_send()
    for cp in owns:
        cp.wait()


def _exchange(arrs, gather, name):
    n_arr = len(arrs)

    def body(*refs):
        x_refs, out_refs, sems = refs[:n_arr], refs[n_arr:2 * n_arr], refs[2 * n_arr:]
        _exchange_start(x_refs, out_refs, sems, gather)
        _exchange_start(x_refs, out_refs, sems, gather, wait=True)

    return pl.pallas_call(
        body, out_shape=_exchange_shapes(arrs, gather), in_specs=[ANY_SPEC] * n_arr, out_specs=[ANY_SPEC] * n_arr,
        scratch_shapes=_exchange_sems(n_arr), name=name,
    )(*arrs)


QK_SCALE = HEAD_DIM ** -0.5


def _low_lanes():
    return lax.broadcasted_iota(jnp.int32, (1, LANES), 1) < HEAD_DIM


def _stack_heads(v, scale=None):
    lo = _low_lanes()
    zero = jnp.zeros_like(v)
    s = jnp.concatenate([jnp.where(lo, v, zero), jnp.where(lo, zero, v)], axis=0)
    return s if scale is None else s * scale


def _stack_cols(v):
    return jnp.concatenate([v[:, 0:1], v[:, HEAD_DIM:HEAD_DIM + 1]], axis=0)


def _unstack(v, tq):
    return jnp.where(_low_lanes(), v[:tq], v[tq:])


def _tile_pos(tq, tk, q0):
    rows = lax.broadcasted_iota(jnp.int32, (2 * tq, tk), 0)
    cols = lax.broadcasted_iota(jnp.int32, (2 * tq, tk), 1)
    return q0 + jnp.where(rows >= tq, rows - tq, rows), cols, rows < tq


def _tri(tk, cmp):
    r = lax.broadcasted_iota(jnp.int32, (tk, tk), 0)
    c = lax.broadcasted_iota(jnp.int32, (tk, tk), 1)
    return jnp.where(cmp(r, c), 1.0, 0.0).astype(BF16)


def _softplus(z):
    return jnp.maximum(z, 0.0) + jnp.log(1.0 + jnp.exp(-jnp.abs(z)))


def _pair_specs(tq, seq, np_, off):
    return [pl.BlockSpec((1, tq, LANES), lambda b, p, i: (b, i, off + p)),
            pl.BlockSpec((1, seq, LANES), lambda b, p, i: (b, 0, off + np_ + p)),
            pl.BlockSpec((1, seq, LANES), lambda b, p, i: (b, 0, off + 2 * np_ + p))]


def _key_tiles(tile, carry, q0, tq, tk, upward):
    nfull = q0 // tk
    edge = range(tq // tk)
    if upward:
        carry = lax.fori_loop(0, nfull, lambda j, cr: tile(pl.multiple_of(j * tk, tk), cr, False), carry)
        for jm in edge:
            carry = tile(pl.multiple_of(q0 + jm * tk, tk), carry, True)
        return carry
    for jm in reversed(edge):
        carry = tile(pl.multiple_of(q0 + jm * tk, tk), carry, True)
    return lax.fori_loop(0, nfull, lambda jj, cr: tile(pl.multiple_of((nfull - 1 - jj) * tk, tk), cr, False), carry)


def _grid_ends(bsz, np_, nq):
    b, p, i = pl.program_id(0), pl.program_id(1), pl.program_id(2)
    return (b == 0) & (p == 0) & (i == 0), (b == bsz - 1) & (p == np_ - 1) & (i == nq - 1)


def _sb_fwd(qkv, np_, tq, tk, name, gathered=()):
    bsz, seq, _ = qkv.shape
    n_g = len(gathered)

    def body(*refs):
        q_ref, k_ref, v_ref = refs[:3]
        x_refs, (y_ref, tot_ref) = refs[3:3 + n_g], refs[3 + n_g:5 + n_g]
        out_refs, sems = refs[5 + n_g:5 + 2 * n_g], refs[5 + 2 * n_g:]
        first, last = _grid_ends(bsz, np_, seq // tq)
        if n_g:
            @pl.when(first)
            def _():
                _exchange_start(x_refs, out_refs, sems, True)

        q0 = pl.program_id(2) * tq
        tpos, cols, _ = _tile_pos(tq, tk, q0)
        msuf = _tri(tk, lambda a, b: a > b)
        qs = _stack_heads(q_ref[0], QK_SCALE)

        def tile(k0, carry, masked):
            tot, acc = carry
            z = _dot_nt(qs, k_ref[0, pl.ds(k0, tk), :])
            sp = _softplus(z)
            if masked:
                seen = (k0 + cols) < tpos
                sp = jnp.where(seen, sp, 0.0)
            logw = z - sp - _dot_split2(sp, msuf) - tot
            if masked:
                logw = jnp.where(seen, logw, NEG)
            return (tot + jnp.sum(sp, axis=1, keepdims=True),
                    acc + _dot(jnp.exp(logw).astype(BF16), v_ref[0, pl.ds(k0, tk), :]))

        init = (jnp.zeros((2 * tq, 1), F32), jnp.zeros((2 * tq, LANES), F32))
        tot, acc = _key_tiles(tile, init, q0, tq, tk, upward=False)
        y_ref[0] = _unstack(acc, tq)
        tot_ref[0, 0] = _unstack(tot, tq)
        if n_g:
            @pl.when(last)
            def _():
                _exchange_start(x_refs, out_refs, sems, True, wait=True)

    y, tot, *got = pl.pallas_call(
        body, grid=(bsz, np_, seq // tq), in_specs=_pair_specs(tq, seq, np_, 0) + [ANY_SPEC] * n_g,
        out_specs=[pl.BlockSpec((1, tq, LANES), lambda b, p, i: (b, i, p)),
                   pl.BlockSpec((1, 1, tq, LANES), lambda b, p, i: (b, p, i, 0))] + [ANY_SPEC] * n_g,
        out_shape=[jax.ShapeDtypeStruct((bsz, seq, np_ * LANES), F32),
                   jax.ShapeDtypeStruct((bsz, np_, seq, LANES), F32)] + _exchange_shapes(gathered, True),
        scratch_shapes=_exchange_sems(n_g) if n_g else [],
        name=name, compiler_params=_arb(3),
    )(qkv, qkv, qkv, *gathered)
    return y, tot, got


def _sb_bwd(qkv, dy, tot, np_, tq, tk, name, swapped=()):
    bsz, seq, _ = qkv.shape
    nq = seq // tq
    n_s = len(swapped)

    def body(*refs):
        q_ref, k_ref, v_ref, dy_ref, tot_ref = refs[:5]
        x_refs, (dq_ref, dk_ref, dv_ref) = refs[5:5 + n_s], refs[5 + n_s:8 + n_s]
        out_refs = refs[8 + n_s:8 + 2 * n_s]
        dk_acc, dv_acc = refs[8 + 2 * n_s:10 + 2 * n_s]
        sems = refs[10 + 2 * n_s:]
        first, last = _grid_ends(bsz, np_, nq)
        if n_s:
            @pl.when(first)
            def _():
                _exchange_start(x_refs, out_refs, sems, False)

        i = pl.program_id(2)
        q0 = i * tq

        @pl.when(i == 0)
        def _():
            dk_acc[...] = jnp.zeros_like(dk_acc)
            dv_acc[...] = jnp.zeros_like(dv_acc)

        tpos, cols, _ = _tile_pos(tq, tk, q0)
        mincl = _tri(tk, lambda a, b: a <= b)
        mexcl = _tri(tk, lambda a, b: a < b)
        qs = _stack_heads(q_ref[0], QK_SCALE)
        dys = _stack_heads(dy_ref[0].astype(BF16))
        tots = _stack_cols(tot_ref[0, 0])

        def tile(k0, carry, masked):
            c_sp, c_g, dq = carry
            kb = k_ref[0, pl.ds(k0, tk), :]
            z = _dot_nt(qs, kb)
            sp = _softplus(z)
            if masked:
                seen = (k0 + cols) < tpos
                sp = jnp.where(seen, sp, 0.0)
            logw = z - sp - (tots - c_sp - _dot_split2(sp, mincl))
            if masked:
                logw = jnp.where(seen, logw, NEG)
            w = jnp.exp(logw)
            g = w * _dot_nt(dys, v_ref[0, pl.ds(k0, tk), :])
            beta = jnp.exp(jnp.minimum(z - sp, 0.0))
            dz = g - beta * (g + c_g + _dot_split2(g, mexcl))
            if masked:
                dz = jnp.where(seen, dz, 0.0)
            dzb = dz.astype(BF16)
            dk_acc[pl.ds(k0, tk), :] += _dot_tn(dzb, qs)
            dv_acc[pl.ds(k0, tk), :] += _dot_tn(w.astype(BF16), dys)
            return (c_sp + jnp.sum(sp, axis=1, keepdims=True), c_g + jnp.sum(g, axis=1, keepdims=True),
                    dq + _dot(dzb, kb))

        zero = jnp.zeros((2 * tq, 1), F32)
        _, _, dq = _key_tiles(tile, (zero, zero, jnp.zeros((2 * tq, LANES), F32)), q0, tq, tk, upward=True)
        dq_ref[0] = (_unstack(dq, tq) * QK_SCALE).astype(BF16)

        @pl.when(i == nq - 1)
        def _():
            dk_ref[0] = dk_acc[...].astype(BF16)
            dv_ref[0] = dv_acc[...].astype(BF16)

        if n_s:
            @pl.when(last)
            def _():
                _exchange_start(x_refs, out_refs, sems, False, wait=True)

    tile_spec = pl.BlockSpec((1, tq, LANES), lambda b, p, i: (b, i, p))
    whole = pl.BlockSpec((1, seq, LANES), lambda b, p, i: (b, 0, p))
    out = jax.ShapeDtypeStruct((bsz, seq, np_ * LANES), BF16)
    dq, dk, dv, *got = pl.pallas_call(
        body, grid=(bsz, np_, nq),
        in_specs=_pair_specs(tq, seq, np_, 0) + [tile_spec, pl.BlockSpec((1, 1, tq, LANES), lambda b, p, i: (b, p, i, 0))]
        + [ANY_SPEC] * n_s,
        out_specs=[tile_spec, whole, whole] + [ANY_SPEC] * n_s,
        out_shape=[out, out, out] + _exchange_shapes(swapped, False),
        scratch_shapes=[pltpu.VMEM((seq, LANES), F32), pltpu.VMEM((seq, LANES), F32)] + (_exchange_sems(n_s) if n_s else []),
        name=name, compiler_params=_arb(3),
    )(qkv, qkv, qkv, dy, tot, *swapped)
    return dq, dk, dv, got


def _fox_fwd(qkv, cumq, cumk, np_, tq, tk, name):
    bsz, seq, _ = qkv.shape

    def body(q_ref, k_ref, v_ref, cq_ref, ck_ref, y_ref, lse_ref):
        q0 = pl.program_id(2) * tq
        tpos, cols, top = _tile_pos(tq, tk, q0)
        qs = _stack_heads(q_ref[0], QK_SCALE)
        cq = _stack_cols(cq_ref[0, 0])

        def tile(k0, carry, masked):
            m, l, acc = carry
            ck = jnp.where(top, ck_ref[0, 0, 0:1, pl.ds(k0, tk)], ck_ref[0, 0, 1:2, pl.ds(k0, tk)])
            s = _dot_nt(qs, k_ref[0, pl.ds(k0, tk), :]) + (cq - ck)
            if masked:
                s = jnp.where((k0 + cols) <= tpos, s, NEG)
            m_new = jnp.maximum(m, jnp.max(s, axis=1, keepdims=True))
            p = jnp.exp(s - m_new)
            alpha = jnp.exp(m - m_new)
            return (m_new, alpha * l + jnp.sum(p, axis=1, keepdims=True),
                    alpha * acc + _dot(p.astype(BF16), v_ref[0, pl.ds(k0, tk), :]))

        init = (jnp.full((2 * tq, 1), NEG, F32), jnp.zeros((2 * tq, 1), F32), jnp.zeros((2 * tq, LANES), F32))
        m, l, acc = _key_tiles(tile, init, q0, tq, tk, upward=True)
        y_ref[0] = _unstack(acc / l, tq)
        lse_ref[0, 0] = _unstack(m + jnp.log(l), tq)

    row4 = pl.BlockSpec((1, 1, tq, LANES), lambda b, p, i: (b, p, i, 0))
    return pl.pallas_call(
        body, grid=(bsz, np_, seq // tq),
        in_specs=_pair_specs(tq, seq, np_, 3 * np_) + [row4, pl.BlockSpec((1, 1, 8, seq), lambda b, p, i: (b, p, 0, 0))],
        out_specs=[pl.BlockSpec((1, tq, LANES), lambda b, p, i: (b, i, p)), row4],
        out_shape=[jax.ShapeDtypeStruct((bsz, seq, np_ * LANES), F32),
                   jax.ShapeDtypeStruct((bsz, np_, seq, LANES), F32)],
        name=name, compiler_params=_arb(3),
    )(qkv, qkv, qkv, cumq, cumk)


def _fox_bwd(qkv, dy, y, lse, cumq, cumk, np_, tq, tk, name):
    bsz, seq, _ = qkv.shape
    nq = seq // tq

    def body(q_ref, k_ref, v_ref, dy_ref, y_ref, lse_ref, cq_ref, ck_ref,
             dq_ref, dk_ref, dv_ref, dck_ref, dcq_ref, dk_acc, dv_acc, dck_acc):
        i = pl.program_id(2)
        q0 = i * tq

        @pl.when(i == 0)
        def _():
            dk_acc[...] = jnp.zeros_like(dk_acc)
            dv_acc[...] = jnp.zeros_like(dv_acc)
            dck_acc[...] = jnp.zeros_like(dck_acc)

        tpos, cols, top = _tile_pos(tq, tk, q0)
        qs = _stack_heads(q_ref[0], QK_SCALE)
        dyf = dy_ref[0]
        dys = _stack_heads(dyf.astype(BF16))
        dyy = dyf * y_ref[0]
        lo = _low_lanes()
        delta = jnp.concatenate([jnp.sum(jnp.where(lo, dyy, 0.0), axis=1, keepdims=True),
                                 jnp.sum(jnp.where(lo, 0.0, dyy), axis=1, keepdims=True)], axis=0)
        cq = _stack_cols(cq_ref[0, 0])
        lse_s = _stack_cols(lse_ref[0, 0])

        def tile(k0, carry, masked):
            dq, row = carry
            kb = k_ref[0, pl.ds(k0, tk), :]
            ck = jnp.where(top, ck_ref[0, 0, 0:1, pl.ds(k0, tk)], ck_ref[0, 0, 1:2, pl.ds(k0, tk)])
            s = _dot_nt(qs, kb) + (cq - ck)
            if masked:
                s = jnp.where((k0 + cols) <= tpos, s, NEG)
            p = jnp.exp(s - lse_s)
            ds = p * (_dot_nt(dys, v_ref[0, pl.ds(k0, tk), :]) - delta)
            dsb = ds.astype(BF16)
            dk_acc[pl.ds(k0, tk), :] += _dot_tn(dsb, qs)
            dv_acc[pl.ds(k0, tk), :] += _dot_tn(p.astype(BF16), dys)
            dck_acc[0:1, pl.ds(k0, tk)] += -jnp.sum(ds[:tq], axis=0, keepdims=True)
            dck_acc[1:2, pl.ds(k0, tk)] += -jnp.sum(ds[tq:], axis=0, keepdims=True)
            return dq + _dot(dsb, kb), row + jnp.sum(ds, axis=1, keepdims=True)

        init = (jnp.zeros((2 * tq, LANES), F32), jnp.zeros((2 * tq, 1), F32))
        dq, row = _key_tiles(tile, init, q0, tq, tk, upward=True)
        dq_ref[0] = (_unstack(dq, tq) * QK_SCALE).astype(BF16)
        dcq_ref[0, 0] = _unstack(row, tq)

        @pl.when(i == nq - 1)
        def _():
            dk_ref[0] = dk_acc[...].astype(BF16)
            dv_ref[0] = dv_acc[...].astype(BF16)
            dck_ref[0, 0] = dck_acc[...]

    tile_spec = pl.BlockSpec((1, tq, LANES), lambda b, p, i: (b, i, p))
    whole = pl.BlockSpec((1, seq, LANES), lambda b, p, i: (b, 0, p))
    row4 = pl.BlockSpec((1, 1, tq, LANES), lambda b, p, i: (b, p, i, 0))
    key4 = pl.BlockSpec((1, 1, 8, seq), lambda b, p, i: (b, p, 0, 0))
    out = jax.ShapeDtypeStruct((bsz, seq, np_ * LANES), BF16)
    return pl.pallas_call(
        body, grid=(bsz, np_, nq),
        in_specs=_pair_specs(tq, seq, np_, 3 * np_) + [tile_spec, tile_spec, row4, row4, key4],
        out_specs=[tile_spec, whole, whole, key4, row4],
        out_shape=[out, out, out, jax.ShapeDtypeStruct((bsz, np_, 8, seq), F32),
                   jax.ShapeDtypeStruct((bsz, np_, seq, LANES), F32)],
        scratch_shapes=[pltpu.VMEM((seq, LANES), F32), pltpu.VMEM((seq, LANES), F32), pltpu.VMEM((8, seq), F32)],
        name=name, compiler_params=_arb(3),
    )(qkv, qkv, qkv, dy, y, lse, cumq, cumk)


def _cum_fwd(fl, bf, np_, name, tb=256):
    bsz, seq, _ = fl.shape
    tb = _pick(seq, (tb, 128))

    def body(fl_ref, bf_ref, o_ref, q_ref):
        tri = _tri(tb, lambda a, b: b <= a)
        lo = _low_lanes()

        def step(j, carry):
            r0 = pl.multiple_of(j * tb, tb)
            blk = _tri_dot3(tri, -_softplus(-(fl_ref[0, pl.ds(r0, tb), :] + bf_ref[...]))) + carry
            o_ref[0, pl.ds(r0, tb), :] = blk
            for p in range(np_):
                q_ref[0, p, pl.ds(r0, tb), :] = jnp.where(lo, blk[:, 2 * p:2 * p + 1], blk[:, 2 * p + 1:2 * p + 2])
            return blk[tb - 1:tb, :]

        lax.fori_loop(0, seq // tb, step, jnp.zeros((1, LANES), F32))

    return pl.pallas_call(
        body, grid=(bsz,),
        in_specs=[pl.BlockSpec((1, seq, LANES), lambda b: (b, 0, 0)), pl.BlockSpec((1, LANES), lambda b: (0, 0))],
        out_specs=[pl.BlockSpec((1, seq, LANES), lambda b: (b, 0, 0)),
                   pl.BlockSpec((1, np_, seq, LANES), lambda b: (b, 0, 0, 0))],
        out_shape=[jax.ShapeDtypeStruct(fl.shape, F32), jax.ShapeDtypeStruct((bsz, np_, seq, LANES), F32)],
        name=name, compiler_params=_arb(1),
    )(fl, bf)


def _cum_bwd(dck, dcq, fl, bf, np_, name, tb=256):
    bsz, seq, _ = fl.shape
    tb = _pick(seq, (tb, 128))
    nb = seq // tb

    def body(dck_ref, dcq_ref, fl_ref, bf_ref, o_ref, db_ref):
        @pl.when(pl.program_id(0) == 0)
        def _():
            db_ref[...] = jnp.zeros_like(db_ref)

        tri = _tri(tb, lambda a, b: b >= a)
        lane = lax.broadcasted_iota(jnp.int32, (1, LANES), 1)

        def step(jj, carry):
            tail, tot = carry
            r0 = pl.multiple_of((nb - 1 - jj) * tb, tb)
            dc = dck_ref[0, pl.ds(r0, tb), :]
            for p in range(np_):
                pair = dcq_ref[0, p, pl.ds(r0, tb), :]
                dc = dc + jnp.where(lane == 2 * p, pair, 0.0) + jnp.where(lane == 2 * p + 1, pltpu.roll(pair, HEAD_DIM, 1), 0.0)
            dlf = _tri_dot3(tri, dc) + tail
            dfl = dlf * _sigmoid(-(fl_ref[0, pl.ds(r0, tb), :] + bf_ref[...]))
            o_ref[0, pl.ds(r0, tb), :] = dfl
            return dlf[0:1, :], tot + jnp.sum(dfl, axis=0, keepdims=True)

        zero = jnp.zeros((1, LANES), F32)
        _, tot = lax.fori_loop(0, nb, step, (zero, zero))
        db_ref[...] += tot

    whole = pl.BlockSpec((1, seq, LANES), lambda b: (b, 0, 0))
    vec = pl.BlockSpec((1, LANES), lambda b: (0, 0))
    return pl.pallas_call(
        body, grid=(bsz,), in_specs=[whole, pl.BlockSpec((1, np_, seq, LANES), lambda b: (b, 0, 0, 0)), whole, vec],
        out_specs=[whole, vec],
        out_shape=[jax.ShapeDtypeStruct(fl.shape, F32), jax.ShapeDtypeStruct((1, LANES), F32)],
        name=name, compiler_params=_arb(1),
    )(dck, dcq, fl, bf)


def _adamw_math(w, g, m, v):
    m = ADAM_B1 * m + (1.0 - ADAM_B1) * g
    v = ADAM_B2 * v + (1.0 - ADAM_B2) * (g * g)
    m_hat = m / (1.0 - ADAM_B1 ** ADAM_STEP)
    v_hat = v / (1.0 - ADAM_B2 ** ADAM_STEP)
    return -ADAM_LR * (m_hat / (jnp.sqrt(v_hat) + ADAM_EPS) + ADAM_WD * w), m, v


def _adamw(gparts, w, m, v, name, tr=512):
    nslots, rows, cols = gparts.shape
    tr = _pick(rows, (tr, 256, 128, 64, 32, 16, 8))

    def body(g_ref, w_ref, m_ref, v_ref, go_ref, d_ref, mo_ref, vo_ref):
        g = g_ref[0]
        for k in range(1, nslots):
            g = g + g_ref[k]
        go_ref[...] = g
        d_ref[...], mo_ref[...], vo_ref[...] = _adamw_math(w_ref[...], g, m_ref[...], v_ref[...])

    blk = pl.BlockSpec((tr, cols), lambda i: (i, 0))
    shp = jax.ShapeDtypeStruct((rows, cols), F32)
    return pl.pallas_call(
        body, grid=(rows // tr,), in_specs=[pl.BlockSpec((nslots, tr, cols), lambda i: (0, i, 0)), blk, blk, blk],
        out_specs=[blk] * 4, out_shape=[shp] * 4, name=name, compiler_params=_arb(1),
    )(gparts, w, m, v)


def _rows128(a):
    return a.reshape(-1, LANES)


def _pad_rows(a, mult):
    extra = (-a.shape[0]) % mult
    return a if extra == 0 else jnp.concatenate([a, jnp.zeros((extra, a.shape[1]), a.dtype)], axis=0)


def _pack(arrs, mult):
    return _pad_rows(jnp.concatenate([_rows128(a) for a in arrs], axis=0), mult)


def _unpack(flat, shapes):
    out, off = [], 0
    for shp in shapes:
        n = 1
        for s in shp:
            n *= s
        out.append(flat[off:off + n // LANES].reshape(shp))
        off += n // LANES
    return out


def _col_blocks(full):
    k, n = full.shape
    return full.reshape(k, N_DEV, n // N_DEV).transpose(1, 0, 2)


def _from_col_blocks(blocks):
    _, k, n = blocks.shape
    return blocks.transpose(1, 0, 2).reshape(k, N_DEV * n)


def kernel(x, c, w_ada, b_ada, g_mix, w_in, b_forget, b_gate, w_branch_sb, w_branch_fox, w_out, g_ffn, w_ffn_gate, w_ffn_up, w_ffn_down, g_final, loss_target, m_w_ada, m_b_ada, m_g_mix, m_w_in, m_b_forget, m_b_gate, m_w_branch_sb, m_w_branch_fox, m_w_out, m_g_ffn, m_w_ffn_gate, m_w_ffn_up, m_w_ffn_down, m_g_final, v_w_ada, v_b_ada, v_g_mix, v_w_in, v_b_forget, v_b_gate, v_w_branch_sb, v_w_branch_fox, v_w_out, v_g_ffn, v_w_ffn_gate, v_w_ffn_up, v_w_ffn_down, v_g_final):
    bsz, seq, d = x.shape
    tok = bsz * seq
    nh = b_forget.shape[-1]
    d_in = w_in.shape[-1] * N_DEV
    d_att = (d_in - nh - 2 * d) // 6
    assert d_att == nh * HEAD_DIM and nh % 2 == 0
    np_ = nh // 2
    d_ff = w_ffn_gate.shape[-1] * N_DEV
    n_mod = w_ada.shape[-1] * N_DEV // d
    me = 4 * lax.axis_index("x") + 2 * lax.axis_index("y") + lax.axis_index("c")
    tq_att = _pick(seq, (512, 256, 128))
    tk_att = _pick(tq_att, (256, 128))

    big = [w_in[0], w_branch_sb[0], w_branch_fox[0], w_out[0], w_ffn_gate[0], w_ffn_up[0], w_ffn_down[0]]
    big_m = [m_w_in[0], m_w_branch_sb[0], m_w_branch_fox[0], m_w_out[0], m_w_ffn_gate[0], m_w_ffn_up[0], m_w_ffn_down[0]]
    big_v = [v_w_in[0], v_w_branch_sb[0], v_w_branch_fox[0], v_w_out[0], v_w_ffn_gate[0], v_w_ffn_up[0], v_w_ffn_down[0]]

    w_in_f = _from_col_blocks(_exchange([big[0].astype(BF16)], True, "gather_w_in")[0])
    w_qkv = w_in_f[:, :6 * d_att]
    w_f = jnp.concatenate([w_in_f[:, 6 * d_att:6 * d_att + nh], jnp.zeros((d, LANES - nh), BF16)], axis=1)
    w_gl = w_in_f[:, 6 * d_att + nh:]

    c_all = _exchange([_rows128(c)], True, "gather_c")[0].reshape(N_DEV * bsz, d)
    nb_all = N_DEV * bsz
    ada_cols = w_ada.shape[-1]
    b_ada_loc = lax.dynamic_slice(b_ada, (0, me * ada_cols), (1, ada_cols))

    def mod_fn(c_v, w_v, b_v):
        return [jnp.dot(_silu(c_v), w_v, precision=lax.Precision.HIGHEST, preferred_element_type=F32) + b_v]

    (mod_part,) = _small(mod_fn, [c_all, w_ada[0], b_ada_loc], [((nb_all, ada_cols), F32)], "ada_mod")
    mod_all = _from_col_blocks(_exchange([_rows128(mod_part)], True, "gather_mod")[0].reshape(N_DEV, nb_all, ada_cols))
    mod = lax.dynamic_slice(mod_all, (me * bsz, 0), (bsz, n_mod * d))
    shift1, scale1, gate1, shift2, scale2, gate2 = [mod[:, i * d:(i + 1) * d].reshape(bsz, 1, d) for i in range(6)]

    def norm_mod_fn(x_v, sc, sh, g):
        n = x_v * lax.rsqrt(jnp.mean(x_v * x_v, axis=-1, keepdims=True) + RMS_EPS) * g
        return [n * (1.0 + sc) + sh]

    (h,) = _rowmap(norm_mod_fn, [x, scale1, shift1, g_mix], [("row", d, BF16)], "norm1")
    h2d = h.reshape(tok, d)
    qkv = _mm([(h2d, w_qkv)], BF16, "proj_qkv").reshape(bsz, seq, 6 * d_att)
    gl = _mm([(h2d, w_gl)], F32, "proj_gates").reshape(bsz, seq, 2 * d)
    fl = _mm([(h2d, w_f)], F32, "proj_forget").reshape(bsz, seq, LANES)

    bf_pad = jnp.concatenate([b_forget, jnp.zeros((1, LANES - nh), F32)], axis=1)
    cum, cumq = _cum_fwd(fl, bf_pad, np_, "cum_fwd")
    cumk = jnp.concatenate([cum[:, :, :nh].transpose(0, 2, 1).reshape(bsz, np_, 2, seq),
                            jnp.zeros((bsz, np_, 6, seq), F32)], axis=2)

    y_sb, tot_sb, gath = _sb_fwd(qkv, np_, tq_att, tk_att, "sb_fwd", gathered=[a.astype(BF16) for a in big[1:]])
    w_sb_f = _from_col_blocks(gath[0])
    w_fx_f = _from_col_blocks(gath[1])
    w_out_f = gath[2].reshape(d, d)
    w_g_f, w_u_f = _from_col_blocks(gath[3]), _from_col_blocks(gath[4])
    w_dn_f = gath[5].reshape(d_ff, d)
    y_fx, lse_fx = _fox_fwd(qkv, cumq, cumk, np_, tq_att, tk_att, "fox_fwd")

    u_sb = _mm([(y_sb.reshape(tok, d_att), w_sb_f)], F32, "branch_sb").reshape(bsz, seq, d)
    u_fx = _mm([(y_fx.reshape(tok, d_att), w_fx_f)], F32, "branch_fox").reshape(bsz, seq, d)

    def merge_fn(gl_v, us, uf, bg):
        gates = _sigmoid(gl_v + bg)
        return [gates[:, :d] * us + gates[:, d:] * uf]

    (merged,) = _rowmap(merge_fn, [gl, u_sb, u_fx, b_gate], [("row", d, BF16)], "merge")
    mo = _mm([(merged.reshape(tok, d), w_out_f)], F32, "out_proj").reshape(bsz, seq, d)

    def resid_norm_fn(x_v, mo_v, g1, sc, sh, g):
        x1_v = x_v + g1 * mo_v
        n = x1_v * lax.rsqrt(jnp.mean(x1_v * x1_v, axis=-1, keepdims=True) + RMS_EPS) * g
        return [x1_v, n * (1.0 + sc) + sh]

    x1, h2 = _rowmap(resid_norm_fn, [x, mo, gate1, scale2, shift2, g_ffn], [("row", d, F32), ("row", d, BF16)], "norm2")
    h2_2d = h2.reshape(tok, d)
    def swiglu_fn(accs, _):
        a, u = accs
        return [_silu(a) * u, a, u]

    f, a_s, u_s = _mm([(h2_2d, w_g_f), (h2_2d, w_u_f)], [BF16, BF16, BF16], "ffn_in", epilogue=swiglu_fn)
    ffn = _mm([(f, w_dn_f)], F32, "ffn_out").reshape(bsz, seq, d)

    def head_fn(x1_v, ffn_v, g2, gf, tgt):
        x2 = x1_v + g2 * ffn_v
        rstd = lax.rsqrt(jnp.mean(x2 * x2, axis=-1, keepdims=True) + RMS_EPS)
        xh = x2 * rstd
        err = xh * gf - tgt
        loss_rows = 0.5 * jnp.mean(err * err, axis=-1, keepdims=True)
        dy = err * (1.0 / d)
        dxh = dy * gf
        dx2 = rstd * (dxh - xh * jnp.mean(dxh * xh, axis=-1, keepdims=True))
        return [dx2, dx2 * g2, jnp.sum(loss_rows, axis=0, keepdims=True) * jnp.ones((1, LANES), F32),
                jnp.sum(dy * xh, axis=0, keepdims=True), jnp.sum(dx2 * ffn_v, axis=0, keepdims=True)]

    dx2, dffn, loss_vec, dg_final, dgate2 = _rowmap(
        head_fn, [x1, ffn, gate2, g_final.reshape(1, d), loss_target],
        [("row", d, F32), ("row", d, BF16), ("global", LANES, F32), ("global", d, F32), ("batch", d, F32)], "head")
    loss = lax.psum(loss_vec[0, 0], AXES)

    dffn2d = dffn.reshape(tok, d)
    dw_dn = _mm_tn(f, dffn2d, "ffn_out_dw")

    def swiglu_bwd_fn(accs, saved):
        df_v, a, u = accs[0], saved[0].astype(F32), saved[1].astype(F32)
        sig = _sigmoid(a)
        return [df_v * u * sig * (1.0 + a * (1.0 - sig)), df_v * a * sig]

    da, du = _mm([(dffn2d, w_dn_f.T)], [BF16, BF16], "ffn_out_dx", extras=[a_s, u_s], epilogue=swiglu_bwd_fn)
    dw_gate, dw_up = _mm_tn(h2_2d, da, "ffn_gate_dw"), _mm_tn(h2_2d, du, "ffn_up_dw")
    dh2 = _mm([(da, w_g_f.T), (du, w_u_f.T)], F32, "ffn_in_dx").reshape(bsz, seq, d)

    def norm2_bwd_fn(dh_v, x1_v, dx2_v, mo_v, sc, g1, g):
        rstd = lax.rsqrt(jnp.mean(x1_v * x1_v, axis=-1, keepdims=True) + RMS_EPS)
        xh = x1_v * rstd
        dn = dh_v * (1.0 + sc)
        dxh = dn * g
        dx1 = dx2_v + rstd * (dxh - xh * jnp.mean(dxh * xh, axis=-1, keepdims=True))
        return [dx1, dx1 * g1, jnp.sum(dh_v * (xh * g), axis=0, keepdims=True), jnp.sum(dh_v, axis=0, keepdims=True),
                jnp.sum(dn * xh, axis=0, keepdims=True), jnp.sum(dx1 * mo_v, axis=0, keepdims=True)]

    dx1, dmo, dscale2, dshift2, dg_ffn, dgate1 = _rowmap(
        norm2_bwd_fn, [dh2, x1, dx2, mo, scale2, gate1, g_ffn],
        [("row", d, F32), ("row", d, BF16), ("batch", d, F32), ("batch", d, F32), ("global", d, F32), ("batch", d, F32)],
        "norm2_bwd")

    dmo2d = dmo.reshape(tok, d)
    dmerged = _mm([(dmo2d, w_out_f.T)], F32, "out_proj_dx").reshape(bsz, seq, d)
    dw_out = _mm_tn(merged.reshape(tok, d), dmo2d, "out_proj_dw")

    def merge_bwd_fn(dm, gl_v, us, uf, bg):
        gates = _sigmoid(gl_v + bg)
        gs, gf = gates[:, :d], gates[:, d:]
        dgl = jnp.concatenate([dm * us * gs * (1.0 - gs), dm * uf * gf * (1.0 - gf)], axis=1)
        return [dm * gs, dm * gf, dgl, jnp.sum(dgl, axis=0, keepdims=True)]

    du_sb, du_fx, dgl, db_gate = _rowmap(
        merge_bwd_fn, [dmerged, gl, u_sb, u_fx, b_gate],
        [("row", d, BF16), ("row", d, BF16), ("row", 2 * d, BF16), ("global", 2 * d, F32)], "merge_bwd", ts=256)
    du_sb2d, du_fx2d = du_sb.reshape(tok, d), du_fx.reshape(tok, d)
    dw_sb = _mm_tn(y_sb.reshape(tok, d_att), du_sb2d, "branch_sb_dw")
    dw_fx = _mm_tn(y_fx.reshape(tok, d_att), du_fx2d, "branch_fox_dw")
    dy_sb = _mm([(du_sb2d, w_sb_f.T)], F32, "branch_sb_dx").reshape(bsz, seq, d_att)
    dy_fx = _mm([(du_fx2d, w_fx_f.T)], F32, "branch_fox_dx").reshape(bsz, seq, d_att)

    blocks = [_col_blocks(dw_sb), _col_blocks(dw_fx), dw_out.reshape(N_DEV, d // N_DEV, d),
              _col_blocks(dw_gate), _col_blocks(dw_up), dw_dn.reshape(N_DEV, d_ff // N_DEV, d)]
    dq_sb, dk_sb, dv_sb, got = _sb_bwd(qkv, dy_sb, tot_sb, np_, tq_att, tk_att, "sb_bwd", swapped=blocks)
    dq_fx, dk_fx, dv_fx, dck, dcq = _fox_bwd(qkv, dy_fx, y_fx, lse_fx, cumq, cumk, np_, tq_att, tk_att, "fox_bwd")
    dck_rows = dck[:, :, :2, :].reshape(bsz, nh, seq).transpose(0, 2, 1)
    dck_rows = jnp.concatenate([dck_rows, jnp.zeros((bsz, seq, LANES - nh), F32)], axis=2)
    dfl, db_f = _cum_bwd(dck_rows, dcq, fl, bf_pad, np_, "cum_bwd")

    dqkv = jnp.concatenate([dq_sb, dk_sb, dv_sb, dq_fx, dk_fx, dv_fx], axis=2).reshape(tok, 6 * d_att)
    dgl2d, dfl2d = dgl.reshape(tok, 2 * d), dfl.reshape(tok, LANES)
    dw_in = jnp.concatenate([_mm_tn(h2d, dqkv, "proj_qkv_dw"), _mm_tn(h2d, dfl2d, "proj_forget_dw")[:, :nh],
                             _mm_tn(h2d, dgl2d, "proj_gates_dw")], axis=1)
    dh = _mm([(dqkv, w_qkv.T), (dgl2d, w_gl.T), (dfl2d, w_f.T)], F32, "proj_dx").reshape(bsz, seq, d)

    def norm1_bwd_fn(dh_v, x_v, dx1_v, sc, g):
        rstd = lax.rsqrt(jnp.mean(x_v * x_v, axis=-1, keepdims=True) + RMS_EPS)
        xh = x_v * rstd
        dn = dh_v * (1.0 + sc)
        dxh = dn * g
        dx = dx1_v + rstd * (dxh - xh * jnp.mean(dxh * xh, axis=-1, keepdims=True))
        return [dx, jnp.sum(dh_v * (xh * g), axis=0, keepdims=True), jnp.sum(dh_v, axis=0, keepdims=True),
                jnp.sum(dn * xh, axis=0, keepdims=True)]

    grad_x, dscale1, dshift1, dg_mix = _rowmap(
        norm1_bwd_fn, [dh, x, dx1, scale1, g_mix],
        [("row", d, F32), ("batch", d, F32), ("batch", d, F32), ("global", d, F32)], "norm1_bwd")

    dmod = jnp.concatenate([dshift1, dscale1, dgate1, dshift2, dscale2, dgate2], axis=2).reshape(bsz, n_mod * d)
    partial = [dg_mix, db_f, db_gate, dg_ffn, dg_final]
    n_dmod_rows = bsz * n_mod * d // LANES
    small_sent = _pack([dmod] + partial, 8)
    small_all = _exchange([small_sent], True, "gather_small")[0]
    small_w = [b_ada, g_mix, jnp.concatenate([b_forget, jnp.zeros((1, LANES - nh), F32)], axis=1), b_gate, g_ffn,
               g_final.reshape(1, d)]
    small_m = [m_b_ada, m_g_mix, jnp.concatenate([m_b_forget, jnp.zeros((1, LANES - nh), F32)], axis=1), m_b_gate,
               m_g_ffn, m_g_final.reshape(1, d)]
    small_v = [v_b_ada, v_g_mix, jnp.concatenate([v_b_forget, jnp.zeros((1, LANES - nh), F32)], axis=1), v_b_gate,
               v_g_ffn, v_g_final.reshape(1, d)]
    small_shapes = [a.shape for a in small_w]
    n_ada_rows = n_mod * d // LANES
    n_part_rows = sum(a.shape[1] // LANES for a in partial)
    sw, sm, sv = _pack(small_w, 8), _pack(small_m, 8), _pack(small_v, 8)
    n_small_rows = sw.shape[0]

    def small_fn(all_v, w_v, m_v, v_v):
        g_ada = None
        g_rest = None
        for k in range(N_DEV):
            for b in range(bsz):
                part = all_v[k, b * n_ada_rows:(b + 1) * n_ada_rows]
                g_ada = part if g_ada is None else g_ada + part
            rest = all_v[k, n_dmod_rows:n_dmod_rows + n_part_rows]
            g_rest = rest if g_rest is None else g_rest + rest
        pieces = [g_ada, g_rest]
        if n_small_rows > n_ada_rows + n_part_rows:
            pieces.append(jnp.zeros((n_small_rows - n_ada_rows - n_part_rows, LANES), F32))
        g = jnp.concatenate(pieces, axis=0)
        return [g, *_adamw_math(w_v, g, m_v, v_v)]

    shp = ((n_small_rows, LANES), F32)
    small_out = _small(small_fn, [small_all, sw, sm, sv], [shp] * 4, "small_update")
    small_g, small_d, small_nm, small_nv = [_unpack(o, small_shapes) for o in small_out]

    def fix_small(lst):
        b_ada_o, g_mix_o, b_f_o, b_gate_o, g_ffn_o, g_final_o = lst
        return [b_ada_o, g_mix_o, b_f_o[:, :nh], b_gate_o, g_ffn_o, g_final_o.reshape(d)]

    small_g, small_d, small_nm, small_nv = [fix_small(l) for l in (small_g, small_d, small_nm, small_nv)]

    dmod_all = small_all[:, :n_dmod_rows].reshape(nb_all, n_mod * d)
    dmod_cols = lax.dynamic_slice(dmod_all, (0, me * ada_cols), (nb_all, ada_cols))

    def ada_dw_fn(c_v, dm_v):
        return [lax.dot_general(_silu(c_v), dm_v, (((0,), (0,)), ((), ())), precision=lax.Precision.HIGHEST,
                                preferred_element_type=F32)]

    (dw_ada,) = _small(ada_dw_fn, [c_all, dmod_cols], [((d, ada_cols), F32)], "ada_dw")
    ada_out = _adamw(dw_ada[None], w_ada[0], m_w_ada[0], v_w_ada[0], "adamw_ada")
    ada_g, ada_d, ada_nm, ada_nv = [o[None] for o in ada_out]

    got = list(_exchange([_col_blocks(dw_in)], False, "exchange_dw_in")) + list(got)
    names = ["w_in", "w_sb", "w_fox", "w_out", "w_gate", "w_up", "w_down"]
    big_out = [_adamw(g, w, m, v, "adamw_" + n, tr=256) for g, w, m, v, n in zip(got, big, big_m, big_v, names)]
    big_g, big_d, big_nm, big_nv = [[o[i][None] for o in big_out] for i in range(4)]

    def ordered(ada, small, bigs):
        b_ada_o, g_mix_o, b_f_o, b_gate_o, g_ffn_o, g_final_o = small
        w_in_o, w_sb_o, w_fx_o, w_out_o, w_gate_o, w_up_o, w_dn_o = bigs
        return [ada, b_ada_o, g_mix_o, w_in_o, b_f_o, b_gate_o, w_sb_o, w_fx_o, w_out_o, g_ffn_o, w_gate_o, w_up_o,
                w_dn_o, g_final_o]

    return (loss, grad_x, *ordered(ada_g, small_g, big_g), *ordered(ada_d, small_d, big_d),
            *ordered(ada_nm, small_nm, big_nm), *ordered(ada_nv, small_nv, big_nv))
```

```python
import jax
import jax.numpy as jnp
from jax import lax
from jax.experimental import pallas as pl
from jax.experimental.pallas import tpu as pltpu

F32 = jnp.float32
BF16 = jnp.bfloat16
HEAD_DIM = 64
LANES = 128
N_DEV = 8
AXES = ("x", "y", "c")
RMS_EPS = 1e-6
ADAM_LR, ADAM_B1, ADAM_B2, ADAM_EPS, ADAM_WD, ADAM_STEP = 0.001, 0.9, 0.999, 1e-08, 0.01, 10
NEG = -1e30
MESH = pl.DeviceIdType.MESH


def _pick(n, cands):
    for c in cands:
        if n % c == 0:
            return c
    raise ValueError(f"no tile for {n} in {cands}")


def _arb(n):
    return pltpu.CompilerParams(dimension_semantics=("arbitrary",) * n)


def _dot(a, b):
    return jnp.dot(a, b, preferred_element_type=F32)


def _dot_nt(a, b):
    return lax.dot_general(a, b, (((1,), (1,)), ((), ())), preferred_element_type=F32)


def _dot_tn(a, b):
    return lax.dot_general(a, b, (((0,), (0,)), ((), ())), preferred_element_type=F32)


def _split2(v):
    hi = v.astype(BF16)
    return hi, (v - hi.astype(F32)).astype(BF16)


def _dot_split2(v, m):
    hi, lo = _split2(v)
    return _dot(hi, m) + _dot(lo, m)


def _tri_dot3(m, v):
    h1 = v.astype(BF16)
    r1 = v - h1.astype(F32)
    h2 = r1.astype(BF16)
    h3 = (r1 - h2.astype(F32)).astype(BF16)
    return _dot(m, h1) + _dot(m, h2) + _dot(m, h3)


def _sigmoid(v):
    return 1.0 / (1.0 + jnp.exp(-v))


def _silu(v):
    return v * _sigmoid(v)


VMEM_BLOCK_BUDGET = 44 << 20


def _col_tiles(n):
    return [n // q for q in range(1, n // LANES + 1) if n % q == 0 and (n // q) % LANES == 0]


def _size(dt):
    return jnp.dtype(dt).itemsize


def _mm(pairs, out_dtypes, name, tm=512, extras=(), epilogue=None):
    m, n = pairs[0][0].shape[0], pairs[0][1].shape[1]
    tm = _pick(m, (tm, 256, 128, 64, 32, 16, 8))
    lhs = []
    for a, _ in pairs:
        if not any(a is x for x in lhs):
            lhs.append(a)
    odts = out_dtypes if epilogue is not None else [out_dtypes]
    n_acc = len(pairs) if epilogue is not None else 1
    per_col = (sum(b.shape[0] * _size(b.dtype) for _, b in pairs) * 2
               + tm * 2 * (sum(_size(d) for d in odts) + sum(_size(e.dtype) for e in extras)) + tm * 4 * n_acc)
    fixed = 2 * sum(tm * a.shape[1] * _size(a.dtype) for a in lhs)
    tn = next((c for c in _col_tiles(n) if fixed + per_col * c <= VMEM_BLOCK_BUDGET), LANES)
    n_l, n_p, n_e = len(lhs), len(pairs), len(extras)

    def body(*refs):
        l_refs, b_refs, e_refs, o_refs = refs[:n_l], refs[n_l:n_l + n_p], refs[n_l + n_p:n_l + n_p + n_e], refs[n_l + n_p + n_e:]
        vals = [r[...].astype(BF16) for r in l_refs]
        accs = []
        for (a, _), b_ref in zip(pairs, b_refs):
            av = vals[next(i for i, x in enumerate(lhs) if x is a)]
            accs.append(_dot(av, b_ref[...].astype(BF16)))
        if epilogue is None:
            outs = [sum(accs[1:], accs[0])]
        else:
            outs = epilogue(accs, [r[...] for r in e_refs])
        for o_ref, v, dt in zip(o_refs, outs, odts):
            o_ref[...] = v.astype(dt)

    tile = pl.BlockSpec((tm, tn), lambda j, i: (i, j))
    res = pl.pallas_call(
        body, grid=(n // tn, m // tm),
        in_specs=[pl.BlockSpec((tm, a.shape[1]), lambda j, i: (i, 0)) for a in lhs]
        + [pl.BlockSpec((b.shape[0], tn), lambda j, i: (0, j)) for _, b in pairs] + [tile] * n_e,
        out_specs=[tile] * len(odts), out_shape=[jax.ShapeDtypeStruct((m, n), d) for d in odts],
        name=name, compiler_params=_arb(2),
    )(*lhs, *[b for _, b in pairs], *extras)
    return res if epilogue is not None else res[0]


def _mm_tn(a, b, name):
    t, m = a.shape
    n = b.shape[1]

    def fits(tm, tn, tk):
        return 2 * (tk * tm * _size(a.dtype) + tk * tn * _size(b.dtype) + tm * tn * 4) <= VMEM_BLOCK_BUDGET

    tm, tn, tk = next((tm, tn, tk) for tn in _col_tiles(n) for tm in _col_tiles(m) if tm <= 1536
                      for tk in (1024, 512, 256, 128) if t % tk == 0 and fits(tm, tn, tk))

    def body(a_ref, b_ref, o_ref):
        @pl.when(pl.program_id(2) == 0)
        def _():
            o_ref[...] = jnp.zeros_like(o_ref)

        o_ref[...] += _dot_tn(a_ref[...].astype(BF16), b_ref[...].astype(BF16))

    return pl.pallas_call(
        body, grid=(m // tm, n // tn, t // tk),
        in_specs=[pl.BlockSpec((tk, tm), lambda i, j, k: (k, i)), pl.BlockSpec((tk, tn), lambda i, j, k: (k, j))],
        out_specs=pl.BlockSpec((tm, tn), lambda i, j, k: (i, j)),
        out_shape=jax.ShapeDtypeStruct((m, n), F32), name=name, compiler_params=_arb(3),
    )(a, b)


def _rowmap(fn, ins, outs, name, ts=512):
    bsz, seq = next(a.shape[:2] for a in ins if a.ndim == 3 and a.shape[1] != 1)
    ts = _pick(seq, (ts, 256, 128, 64, 32, 16, 8))
    n_in = len(ins)

    def in_spec(a):
        if a.ndim == 2:
            return pl.BlockSpec(a.shape, lambda b, s: (0, 0))
        if a.shape[1] == 1:
            return pl.BlockSpec((1, 1, a.shape[2]), lambda b, s: (b, 0, 0))
        return pl.BlockSpec((1, ts, a.shape[2]), lambda b, s: (b, s, 0))

    def out_spec(kind, w):
        if kind == "row":
            return pl.BlockSpec((1, ts, w), lambda b, s: (b, s, 0))
        if kind == "batch":
            return pl.BlockSpec((1, 1, w), lambda b, s: (b, 0, 0))
        return pl.BlockSpec((1, w), lambda b, s: (0, 0))

    def out_shape(kind, w, dt):
        shp = {"row": (bsz, seq, w), "batch": (bsz, 1, w), "global": (1, w)}[kind]
        return jax.ShapeDtypeStruct(shp, dt)

    def body(*refs):
        b, s = pl.program_id(0), pl.program_id(1)
        vals = [r[...] if a.ndim == 2 else r[0] for r, a in zip(refs[:n_in], ins)]
        res = fn(*vals)
        for o_ref, (kind, _, dt), v in zip(refs[n_in:], outs, res):
            if kind == "row":
                o_ref[0] = v.astype(dt)
            elif kind == "batch":
                @pl.when(s == 0)
                def _():
                    o_ref[...] = jnp.zeros_like(o_ref)

                o_ref[0] += v
            else:
                @pl.when((s == 0) & (b == 0))
                def _():
                    o_ref[...] = jnp.zeros_like(o_ref)

                o_ref[...] += v

    return pl.pallas_call(
        body, grid=(bsz, seq // ts), in_specs=[in_spec(a) for a in ins],
        out_specs=[out_spec(k, w) for k, w, _ in outs],
        out_shape=[out_shape(*o) for o in outs], name=name, compiler_params=_arb(2),
    )(*ins)


def _small(fn, ins, out_shapes, name):
    n_in = len(ins)

    def body(*refs):
        res = fn(*[r[...] for r in refs[:n_in]])
        for o_ref, v in zip(refs[n_in:], res):
            o_ref[...] = v

    return pl.pallas_call(body, out_shape=[jax.ShapeDtypeStruct(s, d) for s, d in out_shapes], name=name)(*ins)


def _mesh_pos():
    mx, my, mc = lax.axis_index("x"), lax.axis_index("y"), lax.axis_index("c")
    return mx, my, mc, 4 * mx + 2 * my + mc


def _peer(mx, my, mc, k):
    px = 1 - mx if k & 4 else mx
    py = 1 - my if k & 2 else my
    pc = 1 - mc if k & 1 else mc
    return (px, py, pc), 4 * px + 2 * py + pc


ANY_SPEC = pl.BlockSpec(memory_space=pl.ANY)


def _exchange_shapes(arrs, gather):
    return [jax.ShapeDtypeStruct((N_DEV,) + tuple(x.shape if gather else x.shape[1:]), x.dtype) for x in arrs]


def _exchange_sems(n_arr):
    return [pltpu.SemaphoreType.DMA((n_arr, N_DEV)), pltpu.SemaphoreType.DMA((n_arr, N_DEV)),
            pltpu.SemaphoreType.DMA((n_arr,))]


def _exchange_start(x_refs, out_refs, sems, gather, wait=False):
    send_sems, recv_sems, local_sems = sems
    mx, my, mc, me = _mesh_pos()
    owns, sends, recvs = [], [], []
    for a, (x_ref, out_ref) in enumerate(zip(x_refs, out_refs)):
        owns.append(pltpu.make_async_copy(x_ref if gather else x_ref.at[me], out_ref.at[me], local_sems.at[a]))
        for k in range(1, N_DEV):
            peer, pid = _peer(mx, my, mc, k)
            src = x_ref if gather else x_ref.at[pid]
            sends.append(pltpu.make_async_remote_copy(
                src_ref=src, dst_ref=out_ref.at[me], send_sem=send_sems.at[a, k], recv_sem=recv_sems.at[a, k],
                device_id=peer, device_id_type=MESH))
            if wait:
                recvs.append(pltpu.make_async_remote_copy(
                    src_ref=src, dst_ref=out_ref.at[pid], send_sem=send_sems.at[a, k], recv_sem=recv_sems.at[a, k],
                    device_id=peer, device_id_type=MESH))
    if not wait:
        for cp in owns + sends:
            cp.start()
        return
    for cp in recvs:
        cp.wait_recv()
    for cp in sends:
        cp.wait_send()
    for cp in owns:
        cp.wait()


def _exchange(arrs, gather, name):
    n_arr = len(arrs)

    def body(*refs):
        x_refs, out_refs, sems = refs[:n_arr], refs[n_arr:2 * n_arr], refs[2 * n_arr:]
        _exchange_start(x_refs, out_refs, sems, gather)
        _exchange_start(x_refs, out_refs, sems, gather, wait=True)

    return pl.pallas_call(
        body, out_shape=_exchange_shapes(arrs, gather), in_specs=[ANY_SPEC] * n_arr, out_specs=[ANY_SPEC] * n_arr,
        scratch_shapes=_exchange_sems(n_arr), name=name,
    )(*arrs)


QK_SCALE = HEAD_DIM ** -0.5


def _low_lanes():
    return lax.broadcasted_iota(jnp.int32, (1, LANES), 1) < HEAD_DIM


def _stack_heads(v, scale=None):
    lo = _low_lanes()
    zero = jnp.zeros_like(v)
    s = jnp.concatenate([jnp.where(lo, v, zero), jnp.where(lo, zero, v)], axis=0)
    return s if scale is None else s * scale


def _stack_cols(v):
    return jnp.concatenate([v[:, 0:1], v[:, HEAD_DIM:HEAD_DIM + 1]], axis=0)


def _unstack(v, tq):
    return jnp.where(_low_lanes(), v[:tq], v[tq:])


def _tile_pos(tq, tk, q0):
    rows = lax.broadcasted_iota(jnp.int32, (2 * tq, tk), 0)
    cols = lax.broadcasted_iota(jnp.int32, (2 * tq, tk), 1)
    return q0 + jnp.where(rows >= tq, rows - tq, rows), cols, rows < tq


def _tri(tk, cmp):
    r = lax.broadcasted_iota(jnp.int32, (tk, tk), 0)
    c = lax.broadcasted_iota(jnp.int32, (tk, tk), 1)
    return jnp.where(cmp(r, c), 1.0, 0.0).astype(BF16)


def _softplus(z):
    return jnp.maximum(z, 0.0) + jnp.log(1.0 + jnp.exp(-jnp.abs(z)))


PREFIX_BLOCK = 256


def _running(v, tri, later):
    blk = tri.shape[0]
    nb = v.shape[1] // blk
    parts = [v[:, b * blk:(b + 1) * blk] for b in range(nb)]
    outs, run = [None] * nb, None
    for b in (reversed(range(nb)) if later else range(nb)):
        inside = _dot_split2(parts[b], tri)
        outs[b] = inside if run is None else inside + run
        total = jnp.sum(parts[b], axis=1, keepdims=True)
        run = total if run is None else run + total
    return (outs[0] if nb == 1 else jnp.concatenate(outs, axis=1)), run


def _pair_specs(tq, seq, np_, off):
    return [pl.BlockSpec((1, tq, LANES), lambda b, p, i: (b, i, off + p)),
            pl.BlockSpec((1, seq, LANES), lambda b, p, i: (b, 0, off + np_ + p)),
            pl.BlockSpec((1, seq, LANES), lambda b, p, i: (b, 0, off + 2 * np_ + p))]


def _key_tiles(tile, carry, q0, tq, tk, upward):
    nfull = q0 // tk
    edge = range(tq // tk)
    if upward:
        carry = lax.fori_loop(0, nfull, lambda j, cr: tile(pl.multiple_of(j * tk, tk), cr, False), carry)
        for jm in edge:
            carry = tile(pl.multiple_of(q0 + jm * tk, tk), carry, True)
        return carry
    for jm in reversed(edge):
        carry = tile(pl.multiple_of(q0 + jm * tk, tk), carry, True)
    return lax.fori_loop(0, nfull, lambda jj, cr: tile(pl.multiple_of((nfull - 1 - jj) * tk, tk), cr, False), carry)


def _grid_ends(bsz, np_, nq):
    b, p, i = pl.program_id(0), pl.program_id(1), pl.program_id(2)
    return (b == 0) & (p == 0) & (i == 0), (b == bsz - 1) & (p == np_ - 1) & (i == nq - 1)


def _sb_fwd(qkv, np_, tq, tk, name, gathered=()):
    bsz, seq, _ = qkv.shape
    n_g = len(gathered)

    def body(*refs):
        q_ref, k_ref, v_ref = refs[:3]
        x_refs, (y_ref, tot_ref) = refs[3:3 + n_g], refs[3 + n_g:5 + n_g]
        out_refs, sems = refs[5 + n_g:5 + 2 * n_g], refs[5 + 2 * n_g:]
        first, last = _grid_ends(bsz, np_, seq // tq)
        if n_g:
            @pl.when(first)
            def _():
                _exchange_start(x_refs, out_refs, sems, True)

        q0 = pl.program_id(2) * tq
        tpos, cols, _ = _tile_pos(tq, tk, q0)
        msuf = _tri(min(tk, PREFIX_BLOCK), lambda a, b: a > b)
        qs = _stack_heads(q_ref[0], QK_SCALE)

        def tile(k0, carry, masked):
            tot, acc = carry
            z = _dot_nt(qs, k_ref[0, pl.ds(k0, tk), :])
            sp = _softplus(z)
            if masked:
                seen = (k0 + cols) < tpos
                sp = jnp.where(seen, sp, 0.0)
            sp_after, sp_tot = _running(sp, msuf, later=True)
            logw = z - sp - sp_after - tot
            if masked:
                logw = jnp.where(seen, logw, NEG)
            return tot + sp_tot, acc + _dot(jnp.exp(logw).astype(BF16), v_ref[0, pl.ds(k0, tk), :])

        init = (jnp.zeros((2 * tq, 1), F32), jnp.zeros((2 * tq, LANES), F32))
        tot, acc = _key_tiles(tile, init, q0, tq, tk, upward=False)
        y_ref[0] = _unstack(acc, tq)
        tot_ref[0, 0] = _unstack(tot, tq)
        if n_g:
            @pl.when(last)
            def _():
                _exchange_start(x_refs, out_refs, sems, True, wait=True)

    y, tot, *got = pl.pallas_call(
        body, grid=(bsz, np_, seq // tq), in_specs=_pair_specs(tq, seq, np_, 0) + [ANY_SPEC] * n_g,
        out_specs=[pl.BlockSpec((1, tq, LANES), lambda b, p, i: (b, i, p)),
                   pl.BlockSpec((1, 1, tq, LANES), lambda b, p, i: (b, p, i, 0))] + [ANY_SPEC] * n_g,
        out_shape=[jax.ShapeDtypeStruct((bsz, seq, np_ * LANES), F32),
                   jax.ShapeDtypeStruct((bsz, np_, seq, LANES), F32)] + _exchange_shapes(gathered, True),
        scratch_shapes=_exchange_sems(n_g) if n_g else [],
        name=name, compiler_params=_arb(3),
    )(qkv, qkv, qkv, *gathered)
    return y, tot, got


def _sb_bwd(qkv, dy, tot, np_, tq, tk, name, swapped=()):
    bsz, seq, _ = qkv.shape
    nq = seq // tq
    n_s = len(swapped)

    def body(*refs):
        q_ref, k_ref, v_ref, dy_ref, tot_ref = refs[:5]
        x_refs, (dq_ref, dk_ref, dv_ref) = refs[5:5 + n_s], refs[5 + n_s:8 + n_s]
        out_refs = refs[8 + n_s:8 + 2 * n_s]
        dk_acc, dv_acc = refs[8 + 2 * n_s:10 + 2 * n_s]
        sems = refs[10 + 2 * n_s:]
        first, last = _grid_ends(bsz, np_, nq)
        if n_s:
            @pl.when(first)
            def _():
                _exchange_start(x_refs, out_refs, sems, False)

        i = pl.program_id(2)
        q0 = i * tq

        @pl.when(i == 0)
        def _():
            dk_acc[...] = jnp.zeros_like(dk_acc)
            dv_acc[...] = jnp.zeros_like(dv_acc)

        tpos, cols, _ = _tile_pos(tq, tk, q0)
        mincl = _tri(min(tk, PREFIX_BLOCK), lambda a, b: a <= b)
        mexcl = _tri(min(tk, PREFIX_BLOCK), lambda a, b: a < b)
        qs = _stack_heads(q_ref[0], QK_SCALE)
        dys = _stack_heads(dy_ref[0].astype(BF16))
        tots = _stack_cols(tot_ref[0, 0])

        def tile(k0, carry, masked):
            c_sp, c_g, dq = carry
            kb = k_ref[0, pl.ds(k0, tk), :]
            z = _dot_nt(qs, kb)
            sp = _softplus(z)
            if masked:
                seen = (k0 + cols) < tpos
                sp = jnp.where(seen, sp, 0.0)
            sp_upto, sp_tot = _running(sp, mincl, later=False)
            logw = z - sp - (tots - c_sp - sp_upto)
            if masked:
                logw = jnp.where(seen, logw, NEG)
            w = jnp.exp(logw)
            g = w * _dot_nt(dys, v_ref[0, pl.ds(k0, tk), :])
            g_before, g_tot = _running(g, mexcl, later=False)
            beta = jnp.exp(jnp.minimum(z - sp, 0.0))
            dz = g - beta * (g + c_g + g_before)
            if masked:
                dz = jnp.where(seen, dz, 0.0)
            dzb = dz.astype(BF16)
            dk_acc[pl.ds(k0, tk), :] += _dot_tn(dzb, qs)
            dv_acc[pl.ds(k0, tk), :] += _dot_tn(w.astype(BF16), dys)
            return c_sp + sp_tot, c_g + g_tot, dq + _dot(dzb, kb)

        zero = jnp.zeros((2 * tq, 1), F32)
        _, _, dq = _key_tiles(tile, (zero, zero, jnp.zeros((2 * tq, LANES), F32)), q0, tq, tk, upward=True)
        dq_ref[0] = (_unstack(dq, tq) * QK_SCALE).astype(BF16)

        @pl.when(i == nq - 1)
        def _():
            dk_ref[0] = dk_acc[...].astype(BF16)
            dv_ref[0] = dv_acc[...].astype(BF16)

        if n_s:
            @pl.when(last)
            def _():
                _exchange_start(x_refs, out_refs, sems, False, wait=True)

    tile_spec = pl.BlockSpec((1, tq, LANES), lambda b, p, i: (b, i, p))
    whole = pl.BlockSpec((1, seq, LANES), lambda b, p, i: (b, 0, p))
    out = jax.ShapeDtypeStruct((bsz, seq, np_ * LANES), BF16)
    dq, dk, dv, *got = pl.pallas_call(
        body, grid=(bsz, np_, nq),
        in_specs=_pair_specs(tq, seq, np_, 0) + [tile_spec, pl.BlockSpec((1, 1, tq, LANES), lambda b, p, i: (b, p, i, 0))]
        + [ANY_SPEC] * n_s,
        out_specs=[tile_spec, whole, whole] + [ANY_SPEC] * n_s,
        out_shape=[out, out, out] + _exchange_shapes(swapped, False),
        scratch_shapes=[pltpu.VMEM((seq, LANES), F32), pltpu.VMEM((seq, LANES), F32)] + (_exchange_sems(n_s) if n_s else []),
        name=name, compiler_params=_arb(3),
    )(qkv, qkv, qkv, dy, tot, *swapped)
    return dq, dk, dv, got


def _fox_fwd(qkv, cumq, cumk, np_, tq, tk, name):
    bsz, seq, _ = qkv.shape

    def body(q_ref, k_ref, v_ref, cq_ref, ck_ref, y_ref, lse_ref):
        q0 = pl.program_id(2) * tq
        tpos, cols, top = _tile_pos(tq, tk, q0)
        qs = _stack_heads(q_ref[0], QK_SCALE)
        cq = _stack_cols(cq_ref[0, 0])

        def tile(k0, carry, masked):
            m, l, acc = carry
            ck = jnp.where(top, ck_ref[0, 0, 0:1, pl.ds(k0, tk)], ck_ref[0, 0, 1:2, pl.ds(k0, tk)])
            s = _dot_nt(qs, k_ref[0, pl.ds(k0, tk), :]) + (cq - ck)
            if masked:
                s = jnp.where((k0 + cols) <= tpos, s, NEG)
            m_new = jnp.maximum(m, jnp.max(s, axis=1, keepdims=True))
            p = jnp.exp(s - m_new)
            alpha = jnp.exp(m - m_new)
            return (m_new, alpha * l + jnp.sum(p, axis=1, keepdims=True),
                    alpha * acc + _dot(p.astype(BF16), v_ref[0, pl.ds(k0, tk), :]))

        init = (jnp.full((2 * tq, 1), NEG, F32), jnp.zeros((2 * tq, 1), F32), jnp.zeros((2 * tq, LANES), F32))
        m, l, acc = _key_tiles(tile, init, q0, tq, tk, upward=True)
        y_ref[0] = _unstack(acc / l, tq)
        lse_ref[0, 0] = _unstack(m + jnp.log(l), tq)

    row4 = pl.BlockSpec((1, 1, tq, LANES), lambda b, p, i: (b, p, i, 0))
    return pl.pallas_call(
        body, grid=(bsz, np_, seq // tq),
        in_specs=_pair_specs(tq, seq, np_, 3 * np_) + [row4, pl.BlockSpec((1, 1, 8, seq), lambda b, p, i: (b, p, 0, 0))],
        out_specs=[pl.BlockSpec((1, tq, LANES), lambda b, p, i: (b, i, p)), row4],
        out_shape=[jax.ShapeDtypeStruct((bsz, seq, np_ * LANES), F32),
                   jax.ShapeDtypeStruct((bsz, np_, seq, LANES), F32)],
        name=name, compiler_params=_arb(3),
    )(qkv, qkv, qkv, cumq, cumk)


def _fox_bwd(qkv, dy, y, lse, cumq, cumk, np_, tq, tk, name):
    bsz, seq, _ = qkv.shape
    nq = seq // tq

    def body(q_ref, k_ref, v_ref, dy_ref, y_ref, lse_ref, cq_ref, ck_ref,
             dq_ref, dk_ref, dv_ref, dck_ref, dcq_ref, dk_acc, dv_acc, dck_acc):
        i = pl.program_id(2)
        q0 = i * tq

        @pl.when(i == 0)
        def _():
            dk_acc[...] = jnp.zeros_like(dk_acc)
            dv_acc[...] = jnp.zeros_like(dv_acc)
            dck_acc[...] = jnp.zeros_like(dck_acc)

        tpos, cols, top = _tile_pos(tq, tk, q0)
        qs = _stack_heads(q_ref[0], QK_SCALE)
        dyf = dy_ref[0]
        dys = _stack_heads(dyf.astype(BF16))
        dyy = dyf * y_ref[0]
        lo = _low_lanes()
        delta = jnp.concatenate([jnp.sum(jnp.where(lo, dyy, 0.0), axis=1, keepdims=True),
                                 jnp.sum(jnp.where(lo, 0.0, dyy), axis=1, keepdims=True)], axis=0)
        cq = _stack_cols(cq_ref[0, 0])
        lse_s = _stack_cols(lse_ref[0, 0])

        def tile(k0, carry, masked):
            dq, row = carry
            kb = k_ref[0, pl.ds(k0, tk), :]
            ck = jnp.where(top, ck_ref[0, 0, 0:1, pl.ds(k0, tk)], ck_ref[0, 0, 1:2, pl.ds(k0, tk)])
            s = _dot_nt(qs, kb) + (cq - ck)
            if masked:
                s = jnp.where((k0 + cols) <= tpos, s, NEG)
            p = jnp.exp(s - lse_s)
            ds = p * (_dot_nt(dys, v_ref[0, pl.ds(k0, tk), :]) - delta)
            dsb = ds.astype(BF16)
            dk_acc[pl.ds(k0, tk), :] += _dot_tn(dsb, qs)
            dv_acc[pl.ds(k0, tk), :] += _dot_tn(p.astype(BF16), dys)
            dck_acc[0:1, pl.ds(k0, tk)] += -jnp.sum(ds[:tq], axis=0, keepdims=True)
            dck_acc[1:2, pl.ds(k0, tk)] += -jnp.sum(ds[tq:], axis=0, keepdims=True)
            return dq + _dot(dsb, kb), row + jnp.sum(ds, axis=1, keepdims=True)

        init = (jnp.zeros((2 * tq, LANES), F32), jnp.zeros((2 * tq, 1), F32))
        dq, row = _key_tiles(tile, init, q0, tq, tk, upward=True)
        dq_ref[0] = (_unstack(dq, tq) * QK_SCALE).astype(BF16)
        dcq_ref[0, 0] = _unstack(row, tq)

        @pl.when(i == nq - 1)
        def _():
            dk_ref[0] = dk_acc[...].astype(BF16)
            dv_ref[0] = dv_acc[...].astype(BF16)
            dck_ref[0, 0] = dck_acc[...]

    tile_spec = pl.BlockSpec((1, tq, LANES), lambda b, p, i: (b, i, p))
    whole = pl.BlockSpec((1, seq, LANES), lambda b, p, i: (b, 0, p))
    row4 = pl.BlockSpec((1, 1, tq, LANES), lambda b, p, i: (b, p, i, 0))
    key4 = pl.BlockSpec((1, 1, 8, seq), lambda b, p, i: (b, p, 0, 0))
    out = jax.ShapeDtypeStruct((bsz, seq, np_ * LANES), BF16)
    return pl.pallas_call(
        body, grid=(bsz, np_, nq),
        in_specs=_pair_specs(tq, seq, np_, 3 * np_) + [tile_spec, tile_spec, row4, row4, key4],
        out_specs=[tile_spec, whole, whole, key4, row4],
        out_shape=[out, out, out, jax.ShapeDtypeStruct((bsz, np_, 8, seq), F32),
                   jax.ShapeDtypeStruct((bsz, np_, seq, LANES), F32)],
        scratch_shapes=[pltpu.VMEM((seq, LANES), F32), pltpu.VMEM((seq, LANES), F32), pltpu.VMEM((8, seq), F32)],
        name=name, compiler_params=_arb(3),
    )(qkv, qkv, qkv, dy, y, lse, cumq, cumk)


def _cum_fwd(fl, bf, np_, name, tb=256):
    bsz, seq, _ = fl.shape
    tb = _pick(seq, (tb, 128))

    def body(fl_ref, bf_ref, o_ref, q_ref):
        tri = _tri(tb, lambda a, b: b <= a)
        lo = _low_lanes()

        def step(j, carry):
            r0 = pl.multiple_of(j * tb, tb)
            blk = _tri_dot3(tri, -_softplus(-(fl_ref[0, pl.ds(r0, tb), :] + bf_ref[...]))) + carry
            o_ref[0, pl.ds(r0, tb), :] = blk
            for p in range(np_):
                q_ref[0, p, pl.ds(r0, tb), :] = jnp.where(lo, blk[:, 2 * p:2 * p + 1], blk[:, 2 * p + 1:2 * p + 2])
            return blk[tb - 1:tb, :]

        lax.fori_loop(0, seq // tb, step, jnp.zeros((1, LANES), F32))

    return pl.pallas_call(
        body, grid=(bsz,),
        in_specs=[pl.BlockSpec((1, seq, LANES), lambda b: (b, 0, 0)), pl.BlockSpec((1, LANES), lambda b: (0, 0))],
        out_specs=[pl.BlockSpec((1, seq, LANES), lambda b: (b, 0, 0)),
                   pl.BlockSpec((1, np_, seq, LANES), lambda b: (b, 0, 0, 0))],
        out_shape=[jax.ShapeDtypeStruct(fl.shape, F32), jax.ShapeDtypeStruct((bsz, np_, seq, LANES), F32)],
        name=name, compiler_params=_arb(1),
    )(fl, bf)


def _cum_bwd(dck, dcq, fl, bf, np_, name, tb=256):
    bsz, seq, _ = fl.shape
    tb = _pick(seq, (tb, 128))
    nb = seq // tb

    def body(dck_ref, dcq_ref, fl_ref, bf_ref, o_ref, db_ref):
        @pl.when(pl.program_id(0) == 0)
        def _():
            db_ref[...] = jnp.zeros_like(db_ref)

        tri = _tri(tb, lambda a, b: b >= a)
        lane = lax.broadcasted_iota(jnp.int32, (1, LANES), 1)

        def step(jj, carry):
            tail, tot = carry
            r0 = pl.multiple_of((nb - 1 - jj) * tb, tb)
            dc = dck_ref[0, pl.ds(r0, tb), :]
            for p in range(np_):
                pair = dcq_ref[0, p, pl.ds(r0, tb), :]
                dc = dc + jnp.where(lane == 2 * p, pair, 0.0) + jnp.where(lane == 2 * p + 1, pltpu.roll(pair, HEAD_DIM, 1), 0.0)
            dlf = _tri_dot3(tri, dc) + tail
            dfl = dlf * _sigmoid(-(fl_ref[0, pl.ds(r0, tb), :] + bf_ref[...]))
            o_ref[0, pl.ds(r0, tb), :] = dfl
            return dlf[0:1, :], tot + jnp.sum(dfl, axis=0, keepdims=True)

        zero = jnp.zeros((1, LANES), F32)
        _, tot = lax.fori_loop(0, nb, step, (zero, zero))
        db_ref[...] += tot

    whole = pl.BlockSpec((1, seq, LANES), lambda b: (b, 0, 0))
    vec = pl.BlockSpec((1, LANES), lambda b: (0, 0))
    return pl.pallas_call(
        body, grid=(bsz,), in_specs=[whole, pl.BlockSpec((1, np_, seq, LANES), lambda b: (b, 0, 0, 0)), whole, vec],
        out_specs=[whole, vec],
        out_shape=[jax.ShapeDtypeStruct(fl.shape, F32), jax.ShapeDtypeStruct((1, LANES), F32)],
        name=name, compiler_params=_arb(1),
    )(dck, dcq, fl, bf)


def _adamw_math(w, g, m, v):
    m = ADAM_B1 * m + (1.0 - ADAM_B1) * g
    v = ADAM_B2 * v + (1.0 - ADAM_B2) * (g * g)
    m_hat = m / (1.0 - ADAM_B1 ** ADAM_STEP)
    v_hat = v / (1.0 - ADAM_B2 ** ADAM_STEP)
    return -ADAM_LR * (m_hat / (jnp.sqrt(v_hat) + ADAM_EPS) + ADAM_WD * w), m, v


def _adamw(gparts, w, m, v, name, tr=512):
    nslots, rows, cols = gparts.shape
    tr = _pick(rows, (tr, 256, 128, 64, 32, 16, 8))

    def body(g_ref, w_ref, m_ref, v_ref, go_ref, d_ref, mo_ref, vo_ref):
        g = g_ref[0]
        for k in range(1, nslots):
            g = g + g_ref[k]
        go_ref[...] = g
        d_ref[...], mo_ref[...], vo_ref[...] = _adamw_math(w_ref[...], g, m_ref[...], v_ref[...])

    blk = pl.BlockSpec((tr, cols), lambda i: (i, 0))
    shp = jax.ShapeDtypeStruct((rows, cols), F32)
    return pl.pallas_call(
        body, grid=(rows // tr,), in_specs=[pl.BlockSpec((nslots, tr, cols), lambda i: (0, i, 0)), blk, blk, blk],
        out_specs=[blk] * 4, out_shape=[shp] * 4, name=name, compiler_params=_arb(1),
    )(gparts, w, m, v)


def _rows128(a):
    return a.reshape(-1, LANES)


def _pad_rows(a, mult):
    extra = (-a.shape[0]) % mult
    return a if extra == 0 else jnp.concatenate([a, jnp.zeros((extra, a.shape[1]), a.dtype)], axis=0)


def _pack(arrs, mult):
    return _pad_rows(jnp.concatenate([_rows128(a) for a in arrs], axis=0), mult)


def _unpack(flat, shapes):
    out, off = [], 0
    for shp in shapes:
        n = 1
        for s in shp:
            n *= s
        out.append(flat[off:off + n // LANES].reshape(shp))
        off += n // LANES
    return out


def _col_blocks(full):
    k, n = full.shape
    return full.reshape(k, N_DEV, n // N_DEV).transpose(1, 0, 2)


def _from_col_blocks(blocks):
    _, k, n = blocks.shape
    return blocks.transpose(1, 0, 2).reshape(k, N_DEV * n)


def kernel(x, c, w_ada, b_ada, g_mix, w_in, b_forget, b_gate, w_branch_sb, w_branch_fox, w_out, g_ffn, w_ffn_gate, w_ffn_up, w_ffn_down, g_final, loss_target, m_w_ada, m_b_ada, m_g_mix, m_w_in, m_b_forget, m_b_gate, m_w_branch_sb, m_w_branch_fox, m_w_out, m_g_ffn, m_w_ffn_gate, m_w_ffn_up, m_w_ffn_down, m_g_final, v_w_ada, v_b_ada, v_g_mix, v_w_in, v_b_forget, v_b_gate, v_w_branch_sb, v_w_branch_fox, v_w_out, v_g_ffn, v_w_ffn_gate, v_w_ffn_up, v_w_ffn_down, v_g_final):
    bsz, seq, d = x.shape
    tok = bsz * seq
    nh = b_forget.shape[-1]
    d_in = w_in.shape[-1] * N_DEV
    d_att = (d_in - nh - 2 * d) // 6
    assert d_att == nh * HEAD_DIM and nh % 2 == 0
    np_ = nh // 2
    d_ff = w_ffn_gate.shape[-1] * N_DEV
    n_mod = w_ada.shape[-1] * N_DEV // d
    me = 4 * lax.axis_index("x") + 2 * lax.axis_index("y") + lax.axis_index("c")
    tq_att = _pick(seq, (512, 256, 128))
    tk_att = tq_att
    tq_fox = tk_fox = tq_att

    big = [w_in[0], w_branch_sb[0], w_branch_fox[0], w_out[0], w_ffn_gate[0], w_ffn_up[0], w_ffn_down[0]]
    big_m = [m_w_in[0], m_w_branch_sb[0], m_w_branch_fox[0], m_w_out[0], m_w_ffn_gate[0], m_w_ffn_up[0], m_w_ffn_down[0]]
    big_v = [v_w_in[0], v_w_branch_sb[0], v_w_branch_fox[0], v_w_out[0], v_w_ffn_gate[0], v_w_ffn_up[0], v_w_ffn_down[0]]

    w_in_f = _from_col_blocks(_exchange([big[0].astype(BF16)], True, "gather_w_in")[0])
    w_qkv = w_in_f[:, :6 * d_att]
    w_f = jnp.concatenate([w_in_f[:, 6 * d_att:6 * d_att + nh], jnp.zeros((d, LANES - nh), BF16)], axis=1)
    w_gl = w_in_f[:, 6 * d_att + nh:]

    c_all = _exchange([_rows128(c)], True, "gather_c")[0].reshape(N_DEV * bsz, d)
    nb_all = N_DEV * bsz
    ada_cols = w_ada.shape[-1]
    b_ada_loc = lax.dynamic_slice(b_ada, (0, me * ada_cols), (1, ada_cols))

    def mod_fn(c_v, w_v, b_v):
        return [jnp.dot(_silu(c_v), w_v, precision=lax.Precision.HIGHEST, preferred_element_type=F32) + b_v]

    (mod_part,) = _small(mod_fn, [c_all, w_ada[0], b_ada_loc], [((nb_all, ada_cols), F32)], "ada_mod")
    mod_all = _from_col_blocks(_exchange([_rows128(mod_part)], True, "gather_mod")[0].reshape(N_DEV, nb_all, ada_cols))
    mod = lax.dynamic_slice(mod_all, (me * bsz, 0), (bsz, n_mod * d))
    shift1, scale1, gate1, shift2, scale2, gate2 = [mod[:, i * d:(i + 1) * d].reshape(bsz, 1, d) for i in range(6)]

    def norm_mod_fn(x_v, sc, sh, g):
        n = x_v * lax.rsqrt(jnp.mean(x_v * x_v, axis=-1, keepdims=True) + RMS_EPS) * g
        return [n * (1.0 + sc) + sh]

    (h,) = _rowmap(norm_mod_fn, [x, scale1, shift1, g_mix], [("row", d, BF16)], "norm1")
    h2d = h.reshape(tok, d)
    qkv = _mm([(h2d, w_qkv)], BF16, "proj_qkv").reshape(bsz, seq, 6 * d_att)
    gl = _mm([(h2d, w_gl)], F32, "proj_gates").reshape(bsz, seq, 2 * d)
    fl = _mm([(h2d, w_f)], F32, "proj_forget").reshape(bsz, seq, LANES)

    bf_pad = jnp.concatenate([b_forget, jnp.zeros((1, LANES - nh), F32)], axis=1)
    cum, cumq = _cum_fwd(fl, bf_pad, np_, "cum_fwd")
    cumk = jnp.concatenate([cum[:, :, :nh].transpose(0, 2, 1).reshape(bsz, np_, 2, seq),
                            jnp.zeros((bsz, np_, 6, seq), F32)], axis=2)

    y_sb, tot_sb, gath = _sb_fwd(qkv, np_, tq_att, tk_att, "sb_fwd", gathered=[a.astype(BF16) for a in big[1:]])
    w_sb_f = _from_col_blocks(gath[0])
    w_fx_f = _from_col_blocks(gath[1])
    w_out_f = gath[2].reshape(d, d)
    w_g_f, w_u_f = _from_col_blocks(gath[3]), _from_col_blocks(gath[4])
    w_dn_f = gath[5].reshape(d_ff, d)
    y_fx, lse_fx = _fox_fwd(qkv, cumq, cumk, np_, tq_fox, tk_fox, "fox_fwd")

    u_sb = _mm([(y_sb.reshape(tok, d_att), w_sb_f)], F32, "branch_sb").reshape(bsz, seq, d)
    u_fx = _mm([(y_fx.reshape(tok, d_att), w_fx_f)], F32, "branch_fox").reshape(bsz, seq, d)

    def merge_fn(gl_v, us, uf, bg):
        gates = _sigmoid(gl_v + bg)
        return [gates[:, :d] * us + gates[:, d:] * uf]

    (merged,) = _rowmap(merge_fn, [gl, u_sb, u_fx, b_gate], [("row", d, BF16)], "merge")
    mo = _mm([(merged.reshape(tok, d), w_out_f)], F32, "out_proj").reshape(bsz, seq, d)

    def resid_norm_fn(x_v, mo_v, g1, sc, sh, g):
        x1_v = x_v + g1 * mo_v
        n = x1_v * lax.rsqrt(jnp.mean(x1_v * x1_v, axis=-1, keepdims=True) + RMS_EPS) * g
        return [x1_v, n * (1.0 + sc) + sh]

    x1, h2 = _rowmap(resid_norm_fn, [x, mo, gate1, scale2, shift2, g_ffn], [("row", d, F32), ("row", d, BF16)], "norm2")
    h2_2d = h2.reshape(tok, d)
    def swiglu_fn(accs, _):
        a, u = accs
        return [_silu(a) * u, a, u]

    f, a_s, u_s = _mm([(h2_2d, w_g_f), (h2_2d, w_u_f)], [BF16, BF16, BF16], "ffn_in", epilogue=swiglu_fn)
    ffn = _mm([(f, w_dn_f)], F32, "ffn_out").reshape(bsz, seq, d)

    def head_fn(x1_v, ffn_v, g2, gf, tgt):
        x2 = x1_v + g2 * ffn_v
        rstd = lax.rsqrt(jnp.mean(x2 * x2, axis=-1, keepdims=True) + RMS_EPS)
        xh = x2 * rstd
        err = xh * gf - tgt
        loss_rows = 0.5 * jnp.mean(err * err, axis=-1, keepdims=True)
        dy = err * (1.0 / d)
        dxh = dy * gf
        dx2 = rstd * (dxh - xh * jnp.mean(dxh * xh, axis=-1, keepdims=True))
        return [dx2, dx2 * g2, jnp.sum(loss_rows, axis=0, keepdims=True) * jnp.ones((1, LANES), F32),
                jnp.sum(dy * xh, axis=0, keepdims=True), jnp.sum(dx2 * ffn_v, axis=0, keepdims=True)]

    dx2, dffn, loss_vec, dg_final, dgate2 = _rowmap(
        head_fn, [x1, ffn, gate2, g_final.reshape(1, d), loss_target],
        [("row", d, F32), ("row", d, BF16), ("global", LANES, F32), ("global", d, F32), ("batch", d, F32)], "head")
    loss = lax.psum(loss_vec[0, 0], AXES)

    dffn2d = dffn.reshape(tok, d)
    dw_dn = _mm_tn(f, dffn2d, "ffn_out_dw")

    def swiglu_bwd_fn(accs, saved):
        df_v, a, u = accs[0], saved[0].astype(F32), saved[1].astype(F32)
        sig = _sigmoid(a)
        return [df_v * u * sig * (1.0 + a * (1.0 - sig)), df_v * a * sig]

    da, du = _mm([(dffn2d, w_dn_f.T)], [BF16, BF16], "ffn_out_dx", extras=[a_s, u_s], epilogue=swiglu_bwd_fn)
    dw_gate, dw_up = _mm_tn(h2_2d, da, "ffn_gate_dw"), _mm_tn(h2_2d, du, "ffn_up_dw")
    dh2 = _mm([(da, w_g_f.T), (du, w_u_f.T)], F32, "ffn_in_dx").reshape(bsz, seq, d)

    def norm2_bwd_fn(dh_v, x1_v, dx2_v, mo_v, sc, g1, g):
        rstd = lax.rsqrt(jnp.mean(x1_v * x1_v, axis=-1, keepdims=True) + RMS_EPS)
        xh = x1_v * rstd
        dn = dh_v * (1.0 + sc)
        dxh = dn * g
        dx1 = dx2_v + rstd * (dxh - xh * jnp.mean(dxh * xh, axis=-1, keepdims=True))
        return [dx1, dx1 * g1, jnp.sum(dh_v * (xh * g), axis=0, keepdims=True), jnp.sum(dh_v, axis=0, keepdims=True),
                jnp.sum(dn * xh, axis=0, keepdims=True), jnp.sum(dx1 * mo_v, axis=0, keepdims=True)]

    dx1, dmo, dscale2, dshift2, dg_ffn, dgate1 = _rowmap(
        norm2_bwd_fn, [dh2, x1, dx2, mo, scale2, gate1, g_ffn],
        [("row", d, F32), ("row", d, BF16), ("batch", d, F32), ("batch", d, F32), ("global", d, F32), ("batch", d, F32)],
        "norm2_bwd")

    dmo2d = dmo.reshape(tok, d)
    dmerged = _mm([(dmo2d, w_out_f.T)], F32, "out_proj_dx").reshape(bsz, seq, d)
    dw_out = _mm_tn(merged.reshape(tok, d), dmo2d, "out_proj_dw")

    def merge_bwd_fn(dm, gl_v, us, uf, bg):
        gates = _sigmoid(gl_v + bg)
        gs, gf = gates[:, :d], gates[:, d:]
        dgl = jnp.concatenate([dm * us * gs * (1.0 - gs), dm * uf * gf * (1.0 - gf)], axis=1)
        return [dm * gs, dm * gf, dgl, jnp.sum(dgl, axis=0, keepdims=True)]

    du_sb, du_fx, dgl, db_gate = _rowmap(
        merge_bwd_fn, [dmerged, gl, u_sb, u_fx, b_gate],
        [("row", d, BF16), ("row", d, BF16), ("row", 2 * d, BF16), ("global", 2 * d, F32)], "merge_bwd", ts=256)
    du_sb2d, du_fx2d = du_sb.reshape(tok, d), du_fx.reshape(tok, d)
    dw_sb = _mm_tn(y_sb.reshape(tok, d_att), du_sb2d, "branch_sb_dw")
    dw_fx = _mm_tn(y_fx.reshape(tok, d_att), du_fx2d, "branch_fox_dw")
    dy_sb = _mm([(du_sb2d, w_sb_f.T)], F32, "branch_sb_dx").reshape(bsz, seq, d_att)
    dy_fx = _mm([(du_fx2d, w_fx_f.T)], F32, "branch_fox_dx").reshape(bsz, seq, d_att)

    blocks = [_col_blocks(dw_sb), _col_blocks(dw_fx), dw_out.reshape(N_DEV, d // N_DEV, d),
              _col_blocks(dw_gate), _col_blocks(dw_up), dw_dn.reshape(N_DEV, d_ff // N_DEV, d)]
    dq_sb, dk_sb, dv_sb, got = _sb_bwd(qkv, dy_sb, tot_sb, np_, tq_att, tk_att, "sb_bwd", swapped=blocks)
    dq_fx, dk_fx, dv_fx, dck, dcq = _fox_bwd(qkv, dy_fx, y_fx, lse_fx, cumq, cumk, np_, tq_fox, tk_fox, "fox_bwd")
    dck_rows = dck[:, :, :2, :].reshape(bsz, nh, seq).transpose(0, 2, 1)
    dck_rows = jnp.concatenate([dck_rows, jnp.zeros((bsz, seq, LANES - nh), F32)], axis=2)
    dfl, db_f = _cum_bwd(dck_rows, dcq, fl, bf_pad, np_, "cum_bwd")

    dqkv = jnp.concatenate([dq_sb, dk_sb, dv_sb, dq_fx, dk_fx, dv_fx], axis=2).reshape(tok, 6 * d_att)
    dgl2d, dfl2d = dgl.reshape(tok, 2 * d), dfl.reshape(tok, LANES)
    dw_in = jnp.concatenate([_mm_tn(h2d, dqkv, "proj_qkv_dw"), _mm_tn(h2d, dfl2d, "proj_forget_dw")[:, :nh],
                             _mm_tn(h2d, dgl2d, "proj_gates_dw")], axis=1)
    dh = _mm([(dqkv, w_qkv.T), (dgl2d, w_gl.T), (dfl2d, w_f.T)], F32, "proj_dx").reshape(bsz, seq, d)

    def norm1_bwd_fn(dh_v, x_v, dx1_v, sc, g):
        rstd = lax.rsqrt(jnp.mean(x_v * x_v, axis=-1, keepdims=True) + RMS_EPS)
        xh = x_v * rstd
        dn = dh_v * (1.0 + sc)
        dxh = dn * g
        dx = dx1_v + rstd * (dxh - xh * jnp.mean(dxh * xh, axis=-1, keepdims=True))
        return [dx, jnp.sum(dh_v * (xh * g), axis=0, keepdims=True), jnp.sum(dh_v, axis=0, keepdims=True),
                jnp.sum(dn * xh, axis=0, keepdims=True)]

    grad_x, dscale1, dshift1, dg_mix = _rowmap(
        norm1_bwd_fn, [dh, x, dx1, scale1, g_mix],
        [("row", d, F32), ("batch", d, F32), ("batch", d, F32), ("global", d, F32)], "norm1_bwd")

    dmod = jnp.concatenate([dshift1, dscale1, dgate1, dshift2, dscale2, dgate2], axis=2).reshape(bsz, n_mod * d)
    partial = [dg_mix, db_f, db_gate, dg_ffn, dg_final]
    n_dmod_rows = bsz * n_mod * d // LANES
    small_sent = _pack([dmod] + partial, 8)
    small_all = _exchange([small_sent], True, "gather_small")[0]
    small_w = [b_ada, g_mix, jnp.concatenate([b_forget, jnp.zeros((1, LANES - nh), F32)], axis=1), b_gate, g_ffn,
               g_final.reshape(1, d)]
    small_m = [m_b_ada, m_g_mix, jnp.concatenate([m_b_forget, jnp.zeros((1, LANES - nh), F32)], axis=1), m_b_gate,
               m_g_ffn, m_g_final.reshape(1, d)]
    small_v = [v_b_ada, v_g_mix, jnp.concatenate([v_b_forget, jnp.zeros((1, LANES - nh), F32)], axis=1), v_b_gate,
               v_g_ffn, v_g_final.reshape(1, d)]
    small_shapes = [a.shape for a in small_w]
    n_ada_rows = n_mod * d // LANES
    n_part_rows = sum(a.shape[1] // LANES for a in partial)
    sw, sm, sv = _pack(small_w, 8), _pack(small_m, 8), _pack(small_v, 8)
    n_small_rows = sw.shape[0]

    def small_fn(all_v, w_v, m_v, v_v):
        g_ada = None
        g_rest = None
        for k in range(N_DEV):
            for b in range(bsz):
                part = all_v[k, b * n_ada_rows:(b + 1) * n_ada_rows]
                g_ada = part if g_ada is None else g_ada + part
            rest = all_v[k, n_dmod_rows:n_dmod_rows + n_part_rows]
            g_rest = rest if g_rest is None else g_rest + rest
        pieces = [g_ada, g_rest]
        if n_small_rows > n_ada_rows + n_part_rows:
            pieces.append(jnp.zeros((n_small_rows - n_ada_rows - n_part_rows, LANES), F32))
        g = jnp.concatenate(pieces, axis=0)
        return [g, *_adamw_math(w_v, g, m_v, v_v)]

    shp = ((n_small_rows, LANES), F32)
    small_out = _small(small_fn, [small_all, sw, sm, sv], [shp] * 4, "small_update")
    small_g, small_d, small_nm, small_nv = [_unpack(o, small_shapes) for o in small_out]

    def fix_small(lst):
        b_ada_o, g_mix_o, b_f_o, b_gate_o, g_ffn_o, g_final_o = lst
        return [b_ada_o, g_mix_o, b_f_o[:, :nh], b_gate_o, g_ffn_o, g_final_o.reshape(d)]

    small_g, small_d, small_nm, small_nv = [fix_small(l) for l in (small_g, small_d, small_nm, small_nv)]

    dmod_all = small_all[:, :n_dmod_rows].reshape(nb_all, n_mod * d)
    dmod_cols = lax.dynamic_slice(dmod_all, (0, me * ada_cols), (nb_all, ada_cols))

    def ada_dw_fn(c_v, dm_v):
        return [lax.dot_general(_silu(c_v), dm_v, (((0,), (0,)), ((), ())), precision=lax.Precision.HIGHEST,
                                preferred_element_type=F32)]

    (dw_ada,) = _small(ada_dw_fn, [c_all, dmod_cols], [((d, ada_cols), F32)], "ada_dw")
    ada_out = _adamw(dw_ada[None], w_ada[0], m_w_ada[0], v_w_ada[0], "adamw_ada")
    ada_g, ada_d, ada_nm, ada_nv = [o[None] for o in ada_out]

    got = list(_exchange([_col_blocks(dw_in)], False, "exchange_dw_in")) + list(got)
    names = ["w_in", "w_sb", "w_fox", "w_out", "w_gate", "w_up", "w_down"]
    big_out = [_adamw(g, w, m, v, "adamw_" + n, tr=256) for g, w, m, v, n in zip(got, big, big_m, big_v, names)]
    big_g, big_d, big_nm, big_nv = [[o[i][None] for o in big_out] for i in range(4)]

    def ordered(ada, small, bigs):
        b_ada_o, g_mix_o, b_f_o, b_gate_o, g_ffn_o, g_final_o = small
        w_in_o, w_sb_o, w_fx_o, w_out_o, w_gate_o, w_up_o, w_dn_o = bigs
        return [ada, b_ada_o, g_mix_o, w_in_o, b_f_o, b_gate_o, w_sb_o, w_fx_o, w_out_o, g_ffn_o, w_gate_o, w_up_o,
                w_dn_o, g_final_o]

    return (loss, grad_x, *ordered(ada_g, small_g, big_g), *ordered(ada_d, small_d, big_d),
            *ordered(ada_nm, small_nm, big_nm), *ordered(ada_nv, small_nv, big_nv))
```

```python
import jax
import jax.numpy as jnp
from jax import lax
from jax.experimental import pallas as pl
from jax.experimental.pallas import tpu as pltpu

F32 = jnp.float32
BF16 = jnp.bfloat16
HEAD_DIM = 64
LANES = 128
N_DEV = 8
RMS_EPS = 1e-6
ADAM_LR, ADAM_B1, ADAM_B2, ADAM_EPS, ADAM_WD, ADAM_STEP = 0.001, 0.9, 0.999, 1e-08, 0.01, 10
NEG = -1e30
MESH = pl.DeviceIdType.MESH


def _pick(n, cands):
    for c in cands:
        if n % c == 0:
            return c
    raise ValueError(f"no tile for {n} in {cands}")


def _arb(n):
    return pltpu.CompilerParams(dimension_semantics=("arbitrary",) * n)


def _dot(a, b):
    return jnp.dot(a, b, preferred_element_type=F32)


def _dot_nt(a, b):
    return lax.dot_general(a, b, (((1,), (1,)), ((), ())), preferred_element_type=F32)


def _dot_tn(a, b):
    return lax.dot_general(a, b, (((0,), (0,)), ((), ())), preferred_element_type=F32)


def _split2(v):
    hi = v.astype(BF16)
    return hi, (v - hi.astype(F32)).astype(BF16)


def _dot_split2(v, m):
    hi, lo = _split2(v)
    return _dot(hi, m) + _dot(lo, m)


def _tri_dot3(m, v):
    h1 = v.astype(BF16)
    r1 = v - h1.astype(F32)
    h2 = r1.astype(BF16)
    h3 = (r1 - h2.astype(F32)).astype(BF16)
    return _dot(m, h1) + _dot(m, h2) + _dot(m, h3)


def _sigmoid(v):
    return 1.0 / (1.0 + jnp.exp(-v))


def _silu(v):
    return v * _sigmoid(v)


VMEM_BLOCK_BUDGET = 44 << 20


def _col_tiles(n):
    return [n // q for q in range(1, n // LANES + 1) if n % q == 0 and (n // q) % LANES == 0]


def _size(dt):
    return jnp.dtype(dt).itemsize


def _mm(pairs, out_dtypes, name, tm=512, extras=(), epilogue=None):
    m, n = pairs[0][0].shape[0], pairs[0][1].shape[1]
    tm = _pick(m, (tm, 256, 128, 64, 32, 16, 8))
    lhs = []
    for a, _ in pairs:
        if not any(a is x for x in lhs):
            lhs.append(a)
    odts = out_dtypes if epilogue is not None else [out_dtypes]
    n_acc = len(pairs) if epilogue is not None else 1
    per_col = (sum(b.shape[0] * _size(b.dtype) for _, b in pairs) * 2
               + tm * 2 * (sum(_size(d) for d in odts) + sum(_size(e.dtype) for e in extras)) + tm * 4 * n_acc)
    fixed = 2 * sum(tm * a.shape[1] * _size(a.dtype) for a in lhs)
    tn = next((c for c in _col_tiles(n) if fixed + per_col * c <= VMEM_BLOCK_BUDGET), LANES)
    n_l, n_p, n_e = len(lhs), len(pairs), len(extras)

    def body(*refs):
        l_refs, b_refs, e_refs, o_refs = refs[:n_l], refs[n_l:n_l + n_p], refs[n_l + n_p:n_l + n_p + n_e], refs[n_l + n_p + n_e:]
        vals = [r[...].astype(BF16) for r in l_refs]
        accs = []
        for (a, _), b_ref in zip(pairs, b_refs):
            av = vals[next(i for i, x in enumerate(lhs) if x is a)]
            accs.append(_dot(av, b_ref[...].astype(BF16)))
        if epilogue is None:
            outs = [sum(accs[1:], accs[0])]
        else:
            outs = epilogue(accs, [r[...] for r in e_refs])
        for o_ref, v, dt in zip(o_refs, outs, odts):
            o_ref[...] = v.astype(dt)

    tile = pl.BlockSpec((tm, tn), lambda j, i: (i, j))
    res = pl.pallas_call(
        body, grid=(n // tn, m // tm),
        in_specs=[pl.BlockSpec((tm, a.shape[1]), lambda j, i: (i, 0)) for a in lhs]
        + [pl.BlockSpec((b.shape[0], tn), lambda j, i: (0, j)) for _, b in pairs] + [tile] * n_e,
        out_specs=[tile] * len(odts), out_shape=[jax.ShapeDtypeStruct((m, n), d) for d in odts],
        name=name, compiler_params=_arb(2),
    )(*lhs, *[b for _, b in pairs], *extras)
    return res if epilogue is not None else res[0]


def _mm_tn(a, b, name):
    t, m = a.shape
    n = b.shape[1]

    def fits(tm, tn, tk):
        return 2 * (tk * tm * _size(a.dtype) + tk * tn * _size(b.dtype) + tm * tn * 4) <= VMEM_BLOCK_BUDGET

    tm, tn, tk = next((tm, tn, tk) for tn in _col_tiles(n) for tm in _col_tiles(m) if tm <= 1536
                      for tk in (1024, 512, 256, 128) if t % tk == 0 and fits(tm, tn, tk))

    def body(a_ref, b_ref, o_ref):
        @pl.when(pl.program_id(2) == 0)
        def _():
            o_ref[...] = jnp.zeros_like(o_ref)

        o_ref[...] += _dot_tn(a_ref[...].astype(BF16), b_ref[...].astype(BF16))

    return pl.pallas_call(
        body, grid=(m // tm, n // tn, t // tk),
        in_specs=[pl.BlockSpec((tk, tm), lambda i, j, k: (k, i)), pl.BlockSpec((tk, tn), lambda i, j, k: (k, j))],
        out_specs=pl.BlockSpec((tm, tn), lambda i, j, k: (i, j)),
        out_shape=jax.ShapeDtypeStruct((m, n), F32), name=name, compiler_params=_arb(3),
    )(a, b)


def _rowmap(fn, ins, outs, name, ts=512):
    bsz, seq = next(a.shape[:2] for a in ins if a.ndim == 3 and a.shape[1] != 1)
    ts = _pick(seq, (ts, 256, 128, 64, 32, 16, 8))
    n_in = len(ins)

    def in_spec(a):
        if a.ndim == 2:
            return pl.BlockSpec(a.shape, lambda b, s: (0, 0))
        if a.shape[1] == 1:
            return pl.BlockSpec((1, 1, a.shape[2]), lambda b, s: (b, 0, 0))
        return pl.BlockSpec((1, ts, a.shape[2]), lambda b, s: (b, s, 0))

    def out_spec(kind, w):
        if kind == "row":
            return pl.BlockSpec((1, ts, w), lambda b, s: (b, s, 0))
        if kind == "batch":
            return pl.BlockSpec((1, 1, w), lambda b, s: (b, 0, 0))
        return pl.BlockSpec((1, w), lambda b, s: (0, 0))

    def out_shape(kind, w, dt):
        shp = {"row": (bsz, seq, w), "batch": (bsz, 1, w), "global": (1, w)}[kind]
        return jax.ShapeDtypeStruct(shp, dt)

    def body(*refs):
        b, s = pl.program_id(0), pl.program_id(1)
        vals = [r[...] if a.ndim == 2 else r[0] for r, a in zip(refs[:n_in], ins)]
        res = fn(*vals)
        for o_ref, (kind, _, dt), v in zip(refs[n_in:], outs, res):
            if kind == "row":
                o_ref[0] = v.astype(dt)
            elif kind == "batch":
                @pl.when(s == 0)
                def _():
                    o_ref[...] = jnp.zeros_like(o_ref)

                o_ref[0] += v
            else:
                @pl.when((s == 0) & (b == 0))
                def _():
                    o_ref[...] = jnp.zeros_like(o_ref)

                o_ref[...] += v

    return pl.pallas_call(
        body, grid=(bsz, seq // ts), in_specs=[in_spec(a) for a in ins],
        out_specs=[out_spec(k, w) for k, w, _ in outs],
        out_shape=[out_shape(*o) for o in outs], name=name, compiler_params=_arb(2),
    )(*ins)


def _small(fn, ins, out_shapes, name):
    n_in = len(ins)

    def body(*refs):
        res = fn(*[r[...] for r in refs[:n_in]])
        for o_ref, v in zip(refs[n_in:], res):
            o_ref[...] = v

    return pl.pallas_call(body, out_shape=[jax.ShapeDtypeStruct(s, d) for s, d in out_shapes], name=name)(*ins)


def _mesh_pos():
    mx, my, mc = lax.axis_index("x"), lax.axis_index("y"), lax.axis_index("c")
    return mx, my, mc, 4 * mx + 2 * my + mc


def _peer(mx, my, mc, k):
    px = 1 - mx if k & 4 else mx
    py = 1 - my if k & 2 else my
    pc = 1 - mc if k & 1 else mc
    return (px, py, pc), 4 * px + 2 * py + pc


ANY_SPEC = pl.BlockSpec(memory_space=pl.ANY)


def _exchange_shapes(arrs, gather):
    return [jax.ShapeDtypeStruct((N_DEV,) + tuple(x.shape if gather else x.shape[1:]), x.dtype) for x in arrs]


def _exchange_sems(n_arr):
    return [pltpu.SemaphoreType.DMA((n_arr, N_DEV)), pltpu.SemaphoreType.DMA((n_arr, N_DEV)),
            pltpu.SemaphoreType.DMA((n_arr,))]


def _exchange_start(x_refs, out_refs, sems, gather, wait=False):
    send_sems, recv_sems, local_sems = sems
    mx, my, mc, me = _mesh_pos()
    owns, sends, recvs = [], [], []
    for a, (x_ref, out_ref) in enumerate(zip(x_refs, out_refs)):
        owns.append(pltpu.make_async_copy(x_ref if gather else x_ref.at[me], out_ref.at[me], local_sems.at[a]))
        for k in range(1, N_DEV):
            peer, pid = _peer(mx, my, mc, k)
            src = x_ref if gather else x_ref.at[pid]
            sends.append(pltpu.make_async_remote_copy(
                src_ref=src, dst_ref=out_ref.at[me], send_sem=send_sems.at[a, k], recv_sem=recv_sems.at[a, k],
                device_id=peer, device_id_type=MESH))
            if wait:
                recvs.append(pltpu.make_async_remote_copy(
                    src_ref=src, dst_ref=out_ref.at[pid], send_sem=send_sems.at[a, k], recv_sem=recv_sems.at[a, k],
                    device_id=peer, device_id_type=MESH))
    if not wait:
        for cp in owns + sends:
            cp.start()
        return
    for cp in recvs:
        cp.wait_recv()
    for cp in sends:
        cp.wait_send()
    for cp in owns:
        cp.wait()


def _exchange(arrs, gather, name):
    n_arr = len(arrs)

    def body(*refs):
        x_refs, out_refs, sems = refs[:n_arr], refs[n_arr:2 * n_arr], refs[2 * n_arr:]
        _exchange_start(x_refs, out_refs, sems, gather)
        _exchange_start(x_refs, out_refs, sems, gather, wait=True)

    return pl.pallas_call(
        body, out_shape=_exchange_shapes(arrs, gather), in_specs=[ANY_SPEC] * n_arr, out_specs=[ANY_SPEC] * n_arr,
        scratch_shapes=_exchange_sems(n_arr), name=name,
    )(*arrs)


QK_SCALE = HEAD_DIM ** -0.5


def _low_lanes():
    return lax.broadcasted_iota(jnp.int32, (1, LANES), 1) < HEAD_DIM


def _stack_heads(v, scale=None):
    lo = _low_lanes()
    zero = jnp.zeros_like(v)
    s = jnp.concatenate([jnp.where(lo, v, zero), jnp.where(lo, zero, v)], axis=0)
    return s if scale is None else s * scale


def _stack_cols(v):
    return jnp.concatenate([v[:, 0:1], v[:, HEAD_DIM:HEAD_DIM + 1]], axis=0)


def _unstack(v, tq):
    return jnp.where(_low_lanes(), v[:tq], v[tq:])


def _tile_pos(tq, tk, q0):
    rows = lax.broadcasted_iota(jnp.int32, (2 * tq, tk), 0)
    cols = lax.broadcasted_iota(jnp.int32, (2 * tq, tk), 1)
    return q0 + jnp.where(rows >= tq, rows - tq, rows), cols, rows < tq


def _tri(tk, cmp):
    r = lax.broadcasted_iota(jnp.int32, (tk, tk), 0)
    c = lax.broadcasted_iota(jnp.int32, (tk, tk), 1)
    return jnp.where(cmp(r, c), 1.0, 0.0).astype(BF16)


def _softplus(z):
    return jnp.maximum(z, 0.0) + jnp.log(1.0 + jnp.exp(-jnp.abs(z)))


PREFIX_BLOCK = 256


def _running(v, tri, later):
    blk = tri.shape[0]
    nb = v.shape[1] // blk
    parts = [v[:, b * blk:(b + 1) * blk] for b in range(nb)]
    outs, run = [None] * nb, None
    for b in (reversed(range(nb)) if later else range(nb)):
        inside = _dot_split2(parts[b], tri)
        outs[b] = inside if run is None else inside + run
        total = jnp.sum(parts[b], axis=1, keepdims=True)
        run = total if run is None else run + total
    return (outs[0] if nb == 1 else jnp.concatenate(outs, axis=1)), run


def _pair_specs(tq, seq, np_, off):
    return [pl.BlockSpec((1, tq, LANES), lambda b, p, i: (b, i, off + p)),
            pl.BlockSpec((1, seq, LANES), lambda b, p, i: (b, 0, off + np_ + p)),
            pl.BlockSpec((1, seq, LANES), lambda b, p, i: (b, 0, off + 2 * np_ + p))]


def _key_tiles(tile, carry, q0, tq, tk, upward):
    nfull = q0 // tk
    edge = range(tq // tk)
    if upward:
        carry = lax.fori_loop(0, nfull, lambda j, cr: tile(pl.multiple_of(j * tk, tk), cr, False), carry)
        for jm in edge:
            carry = tile(pl.multiple_of(q0 + jm * tk, tk), carry, True)
        return carry
    for jm in reversed(edge):
        carry = tile(pl.multiple_of(q0 + jm * tk, tk), carry, True)
    return lax.fori_loop(0, nfull, lambda jj, cr: tile(pl.multiple_of((nfull - 1 - jj) * tk, tk), cr, False), carry)


def _grid_ends(bsz, np_, nq):
    b, p, i = pl.program_id(0), pl.program_id(1), pl.program_id(2)
    return (b == 0) & (p == 0) & (i == 0), (b == bsz - 1) & (p == np_ - 1) & (i == nq - 1)


def _sb_fwd(qkv, np_, tq, tk, name, gathered=()):
    bsz, seq, _ = qkv.shape
    n_g = len(gathered)

    def body(*refs):
        q_ref, k_ref, v_ref = refs[:3]
        x_refs, (y_ref, tot_ref) = refs[3:3 + n_g], refs[3 + n_g:5 + n_g]
        out_refs, sems = refs[5 + n_g:5 + 2 * n_g], refs[5 + 2 * n_g:]
        first, last = _grid_ends(bsz, np_, seq // tq)
        if n_g:
            @pl.when(first)
            def _():
                _exchange_start(x_refs, out_refs, sems, True)

        q0 = pl.program_id(2) * tq
        tpos, cols, _ = _tile_pos(tq, tk, q0)
        msuf = _tri(min(tk, PREFIX_BLOCK), lambda a, b: a > b)
        qs = _stack_heads(q_ref[0], QK_SCALE)

        def tile(k0, carry, masked):
            tot, acc = carry
            z = _dot_nt(qs, k_ref[0, pl.ds(k0, tk), :])
            sp = _softplus(z)
            if masked:
                seen = (k0 + cols) < tpos
                sp = jnp.where(seen, sp, 0.0)
            sp_after, sp_tot = _running(sp, msuf, later=True)
            logw = z - sp - sp_after - tot
            if masked:
                logw = jnp.where(seen, logw, NEG)
            return tot + sp_tot, acc + _dot(jnp.exp(logw).astype(BF16), v_ref[0, pl.ds(k0, tk), :])

        init = (jnp.zeros((2 * tq, 1), F32), jnp.zeros((2 * tq, LANES), F32))
        tot, acc = _key_tiles(tile, init, q0, tq, tk, upward=False)
        y_ref[0] = _unstack(acc, tq)
        tot_ref[0, 0] = _unstack(tot, tq)
        if n_g:
            @pl.when(last)
            def _():
                _exchange_start(x_refs, out_refs, sems, True, wait=True)

    y, tot, *got = pl.pallas_call(
        body, grid=(bsz, np_, seq // tq), in_specs=_pair_specs(tq, seq, np_, 0) + [ANY_SPEC] * n_g,
        out_specs=[pl.BlockSpec((1, tq, LANES), lambda b, p, i: (b, i, p)),
                   pl.BlockSpec((1, 1, tq, LANES), lambda b, p, i: (b, p, i, 0))] + [ANY_SPEC] * n_g,
        out_shape=[jax.ShapeDtypeStruct((bsz, seq, np_ * LANES), F32),
                   jax.ShapeDtypeStruct((bsz, np_, seq, LANES), F32)] + _exchange_shapes(gathered, True),
        scratch_shapes=_exchange_sems(n_g) if n_g else [],
        name=name, compiler_params=_arb(3),
    )(qkv, qkv, qkv, *gathered)
    return y, tot, got


def _sb_bwd(qkv, dy, tot, np_, tq, tk, name, swapped=()):
    bsz, seq, _ = qkv.shape
    nq = seq // tq
    n_s = len(swapped)

    def body(*refs):
        q_ref, k_ref, v_ref, dy_ref, tot_ref = refs[:5]
        x_refs, (dq_ref, dk_ref, dv_ref) = refs[5:5 + n_s], refs[5 + n_s:8 + n_s]
        out_refs = refs[8 + n_s:8 + 2 * n_s]
        dk_acc, dv_acc = refs[8 + 2 * n_s:10 + 2 * n_s]
        sems = refs[10 + 2 * n_s:]
        first, last = _grid_ends(bsz, np_, nq)
        if n_s:
            @pl.when(first)
            def _():
                _exchange_start(x_refs, out_refs, sems, False)

        i = pl.program_id(2)
        q0 = i * tq

        @pl.when(i == 0)
        def _():
            dk_acc[...] = jnp.zeros_like(dk_acc)
            dv_acc[...] = jnp.zeros_like(dv_acc)

        tpos, cols, _ = _tile_pos(tq, tk, q0)
        mincl = _tri(min(tk, PREFIX_BLOCK), lambda a, b: a <= b)
        mexcl = _tri(min(tk, PREFIX_BLOCK), lambda a, b: a < b)
        qs = _stack_heads(q_ref[0], QK_SCALE)
        dys = _stack_heads(dy_ref[0].astype(BF16))
        tots = _stack_cols(tot_ref[0, 0])

        def tile(k0, carry, masked):
            c_sp, c_g, dq = carry
            kb = k_ref[0, pl.ds(k0, tk), :]
            z = _dot_nt(qs, kb)
            sp = _softplus(z)
            if masked:
                seen = (k0 + cols) < tpos
                sp = jnp.where(seen, sp, 0.0)
            sp_upto, sp_tot = _running(sp, mincl, later=False)
            logw = z - sp - (tots - c_sp - sp_upto)
            if masked:
                logw = jnp.where(seen, logw, NEG)
            w = jnp.exp(logw)
            g = w * _dot_nt(dys, v_ref[0, pl.ds(k0, tk), :])
            g_before, g_tot = _running(g, mexcl, later=False)
            beta = jnp.exp(jnp.minimum(z - sp, 0.0))
            dz = g - beta * (g + c_g + g_before)
            if masked:
                dz = jnp.where(seen, dz, 0.0)
            dzb = dz.astype(BF16)
            dk_acc[pl.ds(k0, tk), :] += _dot_tn(dzb, qs)
            dv_acc[pl.ds(k0, tk), :] += _dot_tn(w.astype(BF16), dys)
            return c_sp + sp_tot, c_g + g_tot, dq + _dot(dzb, kb)

        zero = jnp.zeros((2 * tq, 1), F32)
        _, _, dq = _key_tiles(tile, (zero, zero, jnp.zeros((2 * tq, LANES), F32)), q0, tq, tk, upward=True)
        dq_ref[0] = (_unstack(dq, tq) * QK_SCALE).astype(BF16)

        @pl.when(i == nq - 1)
        def _():
            dk_ref[0] = dk_acc[...].astype(BF16)
            dv_ref[0] = dv_acc[...].astype(BF16)

        if n_s:
            @pl.when(last)
            def _():
                _exchange_start(x_refs, out_refs, sems, False, wait=True)

    tile_spec = pl.BlockSpec((1, tq, LANES), lambda b, p, i: (b, i, p))
    whole = pl.BlockSpec((1, seq, LANES), lambda b, p, i: (b, 0, p))
    out = jax.ShapeDtypeStruct((bsz, seq, np_ * LANES), BF16)
    dq, dk, dv, *got = pl.pallas_call(
        body, grid=(bsz, np_, nq),
        in_specs=_pair_specs(tq, seq, np_, 0) + [tile_spec, pl.BlockSpec((1, 1, tq, LANES), lambda b, p, i: (b, p, i, 0))]
        + [ANY_SPEC] * n_s,
        out_specs=[tile_spec, whole, whole] + [ANY_SPEC] * n_s,
        out_shape=[out, out, out] + _exchange_shapes(swapped, False),
        scratch_shapes=[pltpu.VMEM((seq, LANES), F32), pltpu.VMEM((seq, LANES), F32)] + (_exchange_sems(n_s) if n_s else []),
        name=name, compiler_params=_arb(3),
    )(qkv, qkv, qkv, dy, tot, *swapped)
    return dq, dk, dv, got


def _fox_fwd(qkv, cumq, cumk, np_, tq, tk, name):
    bsz, seq, _ = qkv.shape

    def body(q_ref, k_ref, v_ref, cq_ref, ck_ref, y_ref, lse_ref):
        q0 = pl.program_id(2) * tq
        tpos, cols, top = _tile_pos(tq, tk, q0)
        qs = _stack_heads(q_ref[0], QK_SCALE)
        cq = _stack_cols(cq_ref[0, 0])

        def tile(k0, carry, masked):
            m, l, acc = carry
            ck = jnp.where(top, ck_ref[0, 0, 0:1, pl.ds(k0, tk)], ck_ref[0, 0, 1:2, pl.ds(k0, tk)])
            s = _dot_nt(qs, k_ref[0, pl.ds(k0, tk), :]) + (cq - ck)
            if masked:
                s = jnp.where((k0 + cols) <= tpos, s, NEG)
            m_new = jnp.maximum(m, jnp.max(s, axis=1, keepdims=True))
            p = jnp.exp(s - m_new)
            alpha = jnp.exp(m - m_new)
            return (m_new, alpha * l + jnp.sum(p, axis=1, keepdims=True),
                    alpha * acc + _dot(p.astype(BF16), v_ref[0, pl.ds(k0, tk), :]))

        init = (jnp.full((2 * tq, 1), NEG, F32), jnp.zeros((2 * tq, 1), F32), jnp.zeros((2 * tq, LANES), F32))
        m, l, acc = _key_tiles(tile, init, q0, tq, tk, upward=True)
        y_ref[0] = _unstack(acc / l, tq)
        lse_ref[0, 0] = _unstack(m + jnp.log(l), tq)

    row4 = pl.BlockSpec((1, 1, tq, LANES), lambda b, p, i: (b, p, i, 0))
    return pl.pallas_call(
        body, grid=(bsz, np_, seq // tq),
        in_specs=_pair_specs(tq, seq, np_, 3 * np_) + [row4, pl.BlockSpec((1, 1, 8, seq), lambda b, p, i: (b, p, 0, 0))],
        out_specs=[pl.BlockSpec((1, tq, LANES), lambda b, p, i: (b, i, p)), row4],
        out_shape=[jax.ShapeDtypeStruct((bsz, seq, np_ * LANES), F32),
                   jax.ShapeDtypeStruct((bsz, np_, seq, LANES), F32)],
        name=name, compiler_params=_arb(3),
    )(qkv, qkv, qkv, cumq, cumk)


def _fox_bwd(qkv, dy, y, lse, cumq, cumk, np_, tq, tk, name):
    bsz, seq, _ = qkv.shape
    nq = seq // tq

    def body(q_ref, k_ref, v_ref, dy_ref, y_ref, lse_ref, cq_ref, ck_ref,
             dq_ref, dk_ref, dv_ref, dck_ref, dcq_ref, dk_acc, dv_acc, dck_acc):
        i = pl.program_id(2)
        q0 = i * tq

        @pl.when(i == 0)
        def _():
            dk_acc[...] = jnp.zeros_like(dk_acc)
            dv_acc[...] = jnp.zeros_like(dv_acc)
            dck_acc[...] = jnp.zeros_like(dck_acc)

        tpos, cols, top = _tile_pos(tq, tk, q0)
        qs = _stack_heads(q_ref[0], QK_SCALE)
        dyf = dy_ref[0]
        dys = _stack_heads(dyf.astype(BF16))
        dyy = dyf * y_ref[0]
        lo = _low_lanes()
        delta = jnp.concatenate([jnp.sum(jnp.where(lo, dyy, 0.0), axis=1, keepdims=True),
                                 jnp.sum(jnp.where(lo, 0.0, dyy), axis=1, keepdims=True)], axis=0)
        cq = _stack_cols(cq_ref[0, 0])
        lse_s = _stack_cols(lse_ref[0, 0])

        def tile(k0, carry, masked):
            dq, row = carry
            kb = k_ref[0, pl.ds(k0, tk), :]
            ck = jnp.where(top, ck_ref[0, 0, 0:1, pl.ds(k0, tk)], ck_ref[0, 0, 1:2, pl.ds(k0, tk)])
            s = _dot_nt(qs, kb) + (cq - ck)
            if masked:
                s = jnp.where((k0 + cols) <= tpos, s, NEG)
            p = jnp.exp(s - lse_s)
            ds = p * (_dot_nt(dys, v_ref[0, pl.ds(k0, tk), :]) - delta)
            dsb = ds.astype(BF16)
            dk_acc[pl.ds(k0, tk), :] += _dot_tn(dsb, qs)
            dv_acc[pl.ds(k0, tk), :] += _dot_tn(p.astype(BF16), dys)
            dck_acc[0:1, pl.ds(k0, tk)] += -jnp.sum(ds[:tq], axis=0, keepdims=True)
            dck_acc[1:2, pl.ds(k0, tk)] += -jnp.sum(ds[tq:], axis=0, keepdims=True)
            return dq + _dot(dsb, kb), row + jnp.sum(ds, axis=1, keepdims=True)

        init = (jnp.zeros((2 * tq, LANES), F32), jnp.zeros((2 * tq, 1), F32))
        dq, row = _key_tiles(tile, init, q0, tq, tk, upward=True)
        dq_ref[0] = (_unstack(dq, tq) * QK_SCALE).astype(BF16)
        dcq_ref[0, 0] = _unstack(row, tq)

        @pl.when(i == nq - 1)
        def _():
            dk_ref[0] = dk_acc[...].astype(BF16)
            dv_ref[0] = dv_acc[...].astype(BF16)
            dck_ref[0, 0] = dck_acc[...]

    tile_spec = pl.BlockSpec((1, tq, LANES), lambda b, p, i: (b, i, p))
    whole = pl.BlockSpec((1, seq, LANES), lambda b, p, i: (b, 0, p))
    row4 = pl.BlockSpec((1, 1, tq, LANES), lambda b, p, i: (b, p, i, 0))
    key4 = pl.BlockSpec((1, 1, 8, seq), lambda b, p, i: (b, p, 0, 0))
    out = jax.ShapeDtypeStruct((bsz, seq, np_ * LANES), BF16)
    return pl.pallas_call(
        body, grid=(bsz, np_, nq),
        in_specs=_pair_specs(tq, seq, np_, 3 * np_) + [tile_spec, tile_spec, row4, row4, key4],
        out_specs=[tile_spec, whole, whole, key4, row4],
        out_shape=[out, out, out, jax.ShapeDtypeStruct((bsz, np_, 8, seq), F32),
                   jax.ShapeDtypeStruct((bsz, np_, seq, LANES), F32)],
        scratch_shapes=[pltpu.VMEM((seq, LANES), F32), pltpu.VMEM((seq, LANES), F32), pltpu.VMEM((8, seq), F32)],
        name=name, compiler_params=_arb(3),
    )(qkv, qkv, qkv, dy, y, lse, cumq, cumk)


def _cum_fwd(fl, bf, np_, name, tb=256):
    bsz, seq, _ = fl.shape
    tb = _pick(seq, (tb, 128))

    def body(fl_ref, bf_ref, o_ref, q_ref):
        tri = _tri(tb, lambda a, b: b <= a)
        lo = _low_lanes()

        def step(j, carry):
            r0 = pl.multiple_of(j * tb, tb)
            blk = _tri_dot3(tri, -_softplus(-(fl_ref[0, pl.ds(r0, tb), :] + bf_ref[...]))) + carry
            o_ref[0, pl.ds(r0, tb), :] = blk
            for p in range(np_):
                q_ref[0, p, pl.ds(r0, tb), :] = jnp.where(lo, blk[:, 2 * p:2 * p + 1], blk[:, 2 * p + 1:2 * p + 2])
            return blk[tb - 1:tb, :]

        lax.fori_loop(0, seq // tb, step, jnp.zeros((1, LANES), F32))

    return pl.pallas_call(
        body, grid=(bsz,),
        in_specs=[pl.BlockSpec((1, seq, LANES), lambda b: (b, 0, 0)), pl.BlockSpec((1, LANES), lambda b: (0, 0))],
        out_specs=[pl.BlockSpec((1, seq, LANES), lambda b: (b, 0, 0)),
                   pl.BlockSpec((1, np_, seq, LANES), lambda b: (b, 0, 0, 0))],
        out_shape=[jax.ShapeDtypeStruct(fl.shape, F32), jax.ShapeDtypeStruct((bsz, np_, seq, LANES), F32)],
        name=name, compiler_params=_arb(1),
    )(fl, bf)


def _cum_bwd(dck, dcq, fl, bf, np_, name, tb=256):
    bsz, seq, _ = fl.shape
    tb = _pick(seq, (tb, 128))
    nb = seq // tb

    def body(dck_ref, dcq_ref, fl_ref, bf_ref, o_ref, db_ref):
        @pl.when(pl.program_id(0) == 0)
        def _():
            db_ref[...] = jnp.zeros_like(db_ref)

        tri = _tri(tb, lambda a, b: b >= a)
        lane = lax.broadcasted_iota(jnp.int32, (1, LANES), 1)

        def step(jj, carry):
            tail, tot = carry
            r0 = pl.multiple_of((nb - 1 - jj) * tb, tb)
            dc = dck_ref[0, pl.ds(r0, tb), :]
            for p in range(np_):
                pair = dcq_ref[0, p, pl.ds(r0, tb), :]
                dc = dc + jnp.where(lane == 2 * p, pair, 0.0) + jnp.where(lane == 2 * p + 1, pltpu.roll(pair, HEAD_DIM, 1), 0.0)
            dlf = _tri_dot3(tri, dc) + tail
            dfl = dlf * _sigmoid(-(fl_ref[0, pl.ds(r0, tb), :] + bf_ref[...]))
            o_ref[0, pl.ds(r0, tb), :] = dfl
            return dlf[0:1, :], tot + jnp.sum(dfl, axis=0, keepdims=True)

        zero = jnp.zeros((1, LANES), F32)
        _, tot = lax.fori_loop(0, nb, step, (zero, zero))
        db_ref[...] += tot

    whole = pl.BlockSpec((1, seq, LANES), lambda b: (b, 0, 0))
    vec = pl.BlockSpec((1, LANES), lambda b: (0, 0))
    return pl.pallas_call(
        body, grid=(bsz,), in_specs=[whole, pl.BlockSpec((1, np_, seq, LANES), lambda b: (b, 0, 0, 0)), whole, vec],
        out_specs=[whole, vec],
        out_shape=[jax.ShapeDtypeStruct(fl.shape, F32), jax.ShapeDtypeStruct((1, LANES), F32)],
        name=name, compiler_params=_arb(1),
    )(dck, dcq, fl, bf)


def _adamw_math(w, g, m, v):
    m = ADAM_B1 * m + (1.0 - ADAM_B1) * g
    v = ADAM_B2 * v + (1.0 - ADAM_B2) * (g * g)
    m_hat = m / (1.0 - ADAM_B1 ** ADAM_STEP)
    v_hat = v / (1.0 - ADAM_B2 ** ADAM_STEP)
    return -ADAM_LR * (m_hat / (jnp.sqrt(v_hat) + ADAM_EPS) + ADAM_WD * w), m, v


def _adamw(gparts, w, m, v, name, tr=512):
    nslots, rows, cols = gparts.shape
    tr = _pick(rows, (tr, 256, 128, 64, 32, 16, 8))

    def body(g_ref, w_ref, m_ref, v_ref, go_ref, d_ref, mo_ref, vo_ref):
        g = g_ref[0].astype(F32)
        for k in range(1, nslots):
            g = g + g_ref[k].astype(F32)
        go_ref[...] = g
        d_ref[...], mo_ref[...], vo_ref[...] = _adamw_math(w_ref[...], g, m_ref[...], v_ref[...])

    blk = pl.BlockSpec((tr, cols), lambda i: (i, 0))
    shp = jax.ShapeDtypeStruct((rows, cols), F32)
    return pl.pallas_call(
        body, grid=(rows // tr,), in_specs=[pl.BlockSpec((nslots, tr, cols), lambda i: (0, i, 0)), blk, blk, blk],
        out_specs=[blk] * 4, out_shape=[shp] * 4, name=name, compiler_params=_arb(1),
    )(gparts, w, m, v)


def _rows128(a):
    return a.reshape(-1, LANES)


def _pad_rows(a, mult):
    extra = (-a.shape[0]) % mult
    return a if extra == 0 else jnp.concatenate([a, jnp.zeros((extra, a.shape[1]), a.dtype)], axis=0)


def _pack(arrs, mult):
    return _pad_rows(jnp.concatenate([_rows128(a) for a in arrs], axis=0), mult)


def _unpack(flat, shapes):
    out, off = [], 0
    for shp in shapes:
        n = 1
        for s in shp:
            n *= s
        out.append(flat[off:off + n // LANES].reshape(shp))
        off += n // LANES
    return out


def _col_blocks(full):
    k, n = full.shape
    return full.reshape(k, N_DEV, n // N_DEV).transpose(1, 0, 2)


def _from_col_blocks(blocks):
    _, k, n = blocks.shape
    return blocks.transpose(1, 0, 2).reshape(k, N_DEV * n)


def kernel(x, c, w_ada, b_ada, g_mix, w_in, b_forget, b_gate, w_branch_sb, w_branch_fox, w_out, g_ffn, w_ffn_gate, w_ffn_up, w_ffn_down, g_final, loss_target, m_w_ada, m_b_ada, m_g_mix, m_w_in, m_b_forget, m_b_gate, m_w_branch_sb, m_w_branch_fox, m_w_out, m_g_ffn, m_w_ffn_gate, m_w_ffn_up, m_w_ffn_down, m_g_final, v_w_ada, v_b_ada, v_g_mix, v_w_in, v_b_forget, v_b_gate, v_w_branch_sb, v_w_branch_fox, v_w_out, v_g_ffn, v_w_ffn_gate, v_w_ffn_up, v_w_ffn_down, v_g_final):
    bsz, seq, d = x.shape
    tok = bsz * seq
    nh = b_forget.shape[-1]
    d_in = w_in.shape[-1] * N_DEV
    d_att = (d_in - nh - 2 * d) // 6
    assert d_att == nh * HEAD_DIM and nh % 2 == 0
    np_ = nh // 2
    d_ff = w_ffn_gate.shape[-1] * N_DEV
    n_mod = w_ada.shape[-1] * N_DEV // d
    me = 4 * lax.axis_index("x") + 2 * lax.axis_index("y") + lax.axis_index("c")
    tq_att = _pick(seq, (512, 256, 128))
    tk_att = tq_att
    tq_fox = tk_fox = tq_att

    big = [w_in[0], w_branch_sb[0], w_branch_fox[0], w_out[0], w_ffn_gate[0], w_ffn_up[0], w_ffn_down[0]]
    big_m = [m_w_in[0], m_w_branch_sb[0], m_w_branch_fox[0], m_w_out[0], m_w_ffn_gate[0], m_w_ffn_up[0], m_w_ffn_down[0]]
    big_v = [v_w_in[0], v_w_branch_sb[0], v_w_branch_fox[0], v_w_out[0], v_w_ffn_gate[0], v_w_ffn_up[0], v_w_ffn_down[0]]

    w_in_g, c_g = _exchange([big[0].astype(BF16), _rows128(c)], True, "gather_w_in_c")
    w_in_f = _from_col_blocks(w_in_g)
    w_qkv = w_in_f[:, :6 * d_att]
    w_f = jnp.concatenate([w_in_f[:, 6 * d_att:6 * d_att + nh], jnp.zeros((d, LANES - nh), BF16)], axis=1)
    w_gl = w_in_f[:, 6 * d_att + nh:]

    c_all = c_g.reshape(N_DEV * bsz, d)
    nb_all = N_DEV * bsz
    ada_cols = w_ada.shape[-1]
    b_ada_loc = lax.dynamic_slice(b_ada, (0, me * ada_cols), (1, ada_cols))

    def mod_fn(c_v, w_v, b_v):
        return [jnp.dot(_silu(c_v), w_v, precision=lax.Precision.HIGHEST, preferred_element_type=F32) + b_v]

    (mod_part,) = _small(mod_fn, [c_all, w_ada[0], b_ada_loc], [((nb_all, ada_cols), F32)], "ada_mod")
    mod_all = _from_col_blocks(_exchange([_rows128(mod_part)], True, "gather_mod")[0].reshape(N_DEV, nb_all, ada_cols))
    mod = lax.dynamic_slice(mod_all, (me * bsz, 0), (bsz, n_mod * d))
    shift1, scale1, gate1, shift2, scale2, gate2 = [mod[:, i * d:(i + 1) * d].reshape(bsz, 1, d) for i in range(6)]

    def norm_mod_fn(x_v, sc, sh, g):
        n = x_v * lax.rsqrt(jnp.mean(x_v * x_v, axis=-1, keepdims=True) + RMS_EPS) * g
        return [n * (1.0 + sc) + sh]

    (h,) = _rowmap(norm_mod_fn, [x, scale1, shift1, g_mix], [("row", d, BF16)], "norm1")
    h2d = h.reshape(tok, d)
    qkv = _mm([(h2d, w_qkv)], BF16, "proj_qkv").reshape(bsz, seq, 6 * d_att)
    gl = _mm([(h2d, w_gl)], F32, "proj_gates").reshape(bsz, seq, 2 * d)
    fl = _mm([(h2d, w_f)], F32, "proj_forget").reshape(bsz, seq, LANES)

    bf_pad = jnp.concatenate([b_forget, jnp.zeros((1, LANES - nh), F32)], axis=1)
    cum, cumq = _cum_fwd(fl, bf_pad, np_, "cum_fwd")
    cumk = jnp.concatenate([cum[:, :, :nh].transpose(0, 2, 1).reshape(bsz, np_, 2, seq),
                            jnp.zeros((bsz, np_, 6, seq), F32)], axis=2)

    y_sb, tot_sb, gath = _sb_fwd(qkv, np_, tq_att, tk_att, "sb_fwd", gathered=[a.astype(BF16) for a in big[1:]])
    w_sb_f = _from_col_blocks(gath[0])
    w_fx_f = _from_col_blocks(gath[1])
    w_out_f = gath[2].reshape(d, d)
    w_g_f, w_u_f = _from_col_blocks(gath[3]), _from_col_blocks(gath[4])
    w_dn_f = gath[5].reshape(d_ff, d)
    y_fx, lse_fx = _fox_fwd(qkv, cumq, cumk, np_, tq_fox, tk_fox, "fox_fwd")

    u_sb = _mm([(y_sb.reshape(tok, d_att), w_sb_f)], F32, "branch_sb").reshape(bsz, seq, d)
    u_fx = _mm([(y_fx.reshape(tok, d_att), w_fx_f)], F32, "branch_fox").reshape(bsz, seq, d)

    def merge_fn(gl_v, us, uf, bg):
        gates = _sigmoid(gl_v + bg)
        return [gates[:, :d] * us + gates[:, d:] * uf]

    (merged,) = _rowmap(merge_fn, [gl, u_sb, u_fx, b_gate], [("row", d, BF16)], "merge")
    mo = _mm([(merged.reshape(tok, d), w_out_f)], F32, "out_proj").reshape(bsz, seq, d)

    def resid_norm_fn(x_v, mo_v, g1, sc, sh, g):
        x1_v = x_v + g1 * mo_v
        n = x1_v * lax.rsqrt(jnp.mean(x1_v * x1_v, axis=-1, keepdims=True) + RMS_EPS) * g
        return [x1_v, n * (1.0 + sc) + sh]

    x1, h2 = _rowmap(resid_norm_fn, [x, mo, gate1, scale2, shift2, g_ffn], [("row", d, F32), ("row", d, BF16)], "norm2")
    h2_2d = h2.reshape(tok, d)
    def swiglu_fn(accs, _):
        a, u = accs
        return [_silu(a) * u, a, u]

    f, a_s, u_s = _mm([(h2_2d, w_g_f), (h2_2d, w_u_f)], [BF16, BF16, BF16], "ffn_in", epilogue=swiglu_fn)
    ffn = _mm([(f, w_dn_f)], F32, "ffn_out").reshape(bsz, seq, d)

    def head_fn(x1_v, ffn_v, g2, gf, tgt):
        x2 = x1_v + g2 * ffn_v
        rstd = lax.rsqrt(jnp.mean(x2 * x2, axis=-1, keepdims=True) + RMS_EPS)
        xh = x2 * rstd
        err = xh * gf - tgt
        loss_rows = 0.5 * jnp.mean(err * err, axis=-1, keepdims=True)
        dy = err * (1.0 / d)
        dxh = dy * gf
        dx2 = rstd * (dxh - xh * jnp.mean(dxh * xh, axis=-1, keepdims=True))
        return [dx2, dx2 * g2, jnp.sum(loss_rows, axis=0, keepdims=True) * jnp.ones((1, LANES), F32),
                jnp.sum(dy * xh, axis=0, keepdims=True), jnp.sum(dx2 * ffn_v, axis=0, keepdims=True)]

    dx2, dffn, loss_vec, dg_final, dgate2 = _rowmap(
        head_fn, [x1, ffn, gate2, g_final.reshape(1, d), loss_target],
        [("row", d, F32), ("row", d, BF16), ("global", LANES, F32), ("global", d, F32), ("batch", d, F32)], "head")

    dffn2d = dffn.reshape(tok, d)
    dw_dn = _mm_tn(f, dffn2d, "ffn_out_dw")

    def swiglu_bwd_fn(accs, saved):
        df_v, a, u = accs[0], saved[0].astype(F32), saved[1].astype(F32)
        sig = _sigmoid(a)
        return [df_v * u * sig * (1.0 + a * (1.0 - sig)), df_v * a * sig]

    da, du = _mm([(dffn2d, w_dn_f.T)], [BF16, BF16], "ffn_out_dx", extras=[a_s, u_s], epilogue=swiglu_bwd_fn)
    dw_gate, dw_up = _mm_tn(h2_2d, da, "ffn_gate_dw"), _mm_tn(h2_2d, du, "ffn_up_dw")
    dh2 = _mm([(da, w_g_f.T), (du, w_u_f.T)], F32, "ffn_in_dx").reshape(bsz, seq, d)

    def norm2_bwd_fn(dh_v, x1_v, dx2_v, mo_v, sc, g1, g):
        rstd = lax.rsqrt(jnp.mean(x1_v * x1_v, axis=-1, keepdims=True) + RMS_EPS)
        xh = x1_v * rstd
        dn = dh_v * (1.0 + sc)
        dxh = dn * g
        dx1 = dx2_v + rstd * (dxh - xh * jnp.mean(dxh * xh, axis=-1, keepdims=True))
        return [dx1, dx1 * g1, jnp.sum(dh_v * (xh * g), axis=0, keepdims=True), jnp.sum(dh_v, axis=0, keepdims=True),
                jnp.sum(dn * xh, axis=0, keepdims=True), jnp.sum(dx1 * mo_v, axis=0, keepdims=True)]

    dx1, dmo, dscale2, dshift2, dg_ffn, dgate1 = _rowmap(
        norm2_bwd_fn, [dh2, x1, dx2, mo, scale2, gate1, g_ffn],
        [("row", d, F32), ("row", d, BF16), ("batch", d, F32), ("batch", d, F32), ("global", d, F32), ("batch", d, F32)],
        "norm2_bwd")

    dmo2d = dmo.reshape(tok, d)
    dmerged = _mm([(dmo2d, w_out_f.T)], F32, "out_proj_dx").reshape(bsz, seq, d)
    dw_out = _mm_tn(merged.reshape(tok, d), dmo2d, "out_proj_dw")

    def merge_bwd_fn(dm, gl_v, us, uf, bg):
        gates = _sigmoid(gl_v + bg)
        gs, gf = gates[:, :d], gates[:, d:]
        dgl = jnp.concatenate([dm * us * gs * (1.0 - gs), dm * uf * gf * (1.0 - gf)], axis=1)
        return [dm * gs, dm * gf, dgl, jnp.sum(dgl, axis=0, keepdims=True)]

    du_sb, du_fx, dgl, db_gate = _rowmap(
        merge_bwd_fn, [dmerged, gl, u_sb, u_fx, b_gate],
        [("row", d, BF16), ("row", d, BF16), ("row", 2 * d, BF16), ("global", 2 * d, F32)], "merge_bwd", ts=256)
    du_sb2d, du_fx2d = du_sb.reshape(tok, d), du_fx.reshape(tok, d)
    dw_sb = _mm_tn(y_sb.reshape(tok, d_att), du_sb2d, "branch_sb_dw")
    dw_fx = _mm_tn(y_fx.reshape(tok, d_att), du_fx2d, "branch_fox_dw")
    dy_sb = _mm([(du_sb2d, w_sb_f.T)], F32, "branch_sb_dx").reshape(bsz, seq, d_att)
    dy_fx = _mm([(du_fx2d, w_fx_f.T)], F32, "branch_fox_dx").reshape(bsz, seq, d_att)

    blocks = [_col_blocks(dw_sb), _col_blocks(dw_fx), dw_out.reshape(N_DEV, d // N_DEV, d),
              _col_blocks(dw_gate), _col_blocks(dw_up), dw_dn.reshape(N_DEV, d_ff // N_DEV, d)]
    blocks = [b.astype(BF16) for b in blocks]
    dq_sb, dk_sb, dv_sb, got = _sb_bwd(qkv, dy_sb, tot_sb, np_, tq_att, tk_att, "sb_bwd", swapped=blocks)
    dq_fx, dk_fx, dv_fx, dck, dcq = _fox_bwd(qkv, dy_fx, y_fx, lse_fx, cumq, cumk, np_, tq_fox, tk_fox, "fox_bwd")
    dck_rows = dck[:, :, :2, :].reshape(bsz, nh, seq).transpose(0, 2, 1)
    dck_rows = jnp.concatenate([dck_rows, jnp.zeros((bsz, seq, LANES - nh), F32)], axis=2)
    dfl, db_f = _cum_bwd(dck_rows, dcq, fl, bf_pad, np_, "cum_bwd")

    dqkv = jnp.concatenate([dq_sb, dk_sb, dv_sb, dq_fx, dk_fx, dv_fx], axis=2).reshape(tok, 6 * d_att)
    dgl2d, dfl2d = dgl.reshape(tok, 2 * d), dfl.reshape(tok, LANES)
    dw_in = jnp.concatenate([_mm_tn(h2d, dqkv, "proj_qkv_dw"), _mm_tn(h2d, dfl2d, "proj_forget_dw")[:, :nh],
                             _mm_tn(h2d, dgl2d, "proj_gates_dw")], axis=1)
    dh = _mm([(dqkv, w_qkv.T), (dgl2d, w_gl.T), (dfl2d, w_f.T)], F32, "proj_dx").reshape(bsz, seq, d)

    def norm1_bwd_fn(dh_v, x_v, dx1_v, sc, g):
        rstd = lax.rsqrt(jnp.mean(x_v * x_v, axis=-1, keepdims=True) + RMS_EPS)
        xh = x_v * rstd
        dn = dh_v * (1.0 + sc)
        dxh = dn * g
        dx = dx1_v + rstd * (dxh - xh * jnp.mean(dxh * xh, axis=-1, keepdims=True))
        return [dx, jnp.sum(dh_v * (xh * g), axis=0, keepdims=True), jnp.sum(dh_v, axis=0, keepdims=True),
                jnp.sum(dn * xh, axis=0, keepdims=True)]

    grad_x, dscale1, dshift1, dg_mix = _rowmap(
        norm1_bwd_fn, [dh, x, dx1, scale1, g_mix],
        [("row", d, F32), ("batch", d, F32), ("batch", d, F32), ("global", d, F32)], "norm1_bwd")

    dmod = jnp.concatenate([dshift1, dscale1, dgate1, dshift2, dscale2, dgate2], axis=2).reshape(bsz, n_mod * d)
    partial = [dg_mix, db_f, db_gate, dg_ffn, dg_final]
    n_dmod_rows = bsz * n_mod * d // LANES
    small_sent = _pack([dmod] + partial + [loss_vec], 8)
    small_all = _exchange([small_sent], True, "gather_small")[0]
    small_w = [b_ada, g_mix, jnp.concatenate([b_forget, jnp.zeros((1, LANES - nh), F32)], axis=1), b_gate, g_ffn,
               g_final.reshape(1, d)]
    small_m = [m_b_ada, m_g_mix, jnp.concatenate([m_b_forget, jnp.zeros((1, LANES - nh), F32)], axis=1), m_b_gate,
               m_g_ffn, m_g_final.reshape(1, d)]
    small_v = [v_b_ada, v_g_mix, jnp.concatenate([v_b_forget, jnp.zeros((1, LANES - nh), F32)], axis=1), v_b_gate,
               v_g_ffn, v_g_final.reshape(1, d)]
    small_shapes = [a.shape for a in small_w]
    n_ada_rows = n_mod * d // LANES
    n_part_rows = sum(a.shape[1] // LANES for a in partial)
    sw, sm, sv = _pack(small_w, 8), _pack(small_m, 8), _pack(small_v, 8)
    n_small_rows = sw.shape[0]

    def small_fn(all_v, w_v, m_v, v_v):
        g_ada = None
        g_rest = None
        for k in range(N_DEV):
            for b in range(bsz):
                part = all_v[k, b * n_ada_rows:(b + 1) * n_ada_rows]
                g_ada = part if g_ada is None else g_ada + part
            rest = all_v[k, n_dmod_rows:n_dmod_rows + n_part_rows + 1]
            g_rest = rest if g_rest is None else g_rest + rest
        pieces = [g_ada, g_rest[:n_part_rows]]
        if n_small_rows > n_ada_rows + n_part_rows:
            pieces.append(jnp.zeros((n_small_rows - n_ada_rows - n_part_rows, LANES), F32))
        g = jnp.concatenate(pieces, axis=0)
        return [g, *_adamw_math(w_v, g, m_v, v_v), jnp.broadcast_to(g_rest[n_part_rows:], (8, LANES))]

    shp = ((n_small_rows, LANES), F32)
    *small_out, loss_all = _small(small_fn, [small_all, sw, sm, sv], [shp] * 4 + [((8, LANES), F32)], "small_update")
    loss = loss_all[0, 0]
    small_g, small_d, small_nm, small_nv = [_unpack(o, small_shapes) for o in small_out]

    def fix_small(lst):
        b_ada_o, g_mix_o, b_f_o, b_gate_o, g_ffn_o, g_final_o = lst
        return [b_ada_o, g_mix_o, b_f_o[:, :nh], b_gate_o, g_ffn_o, g_final_o.reshape(d)]

    small_g, small_d, small_nm, small_nv = [fix_small(l) for l in (small_g, small_d, small_nm, small_nv)]

    dmod_all = small_all[:, :n_dmod_rows].reshape(nb_all, n_mod * d)
    dmod_cols = lax.dynamic_slice(dmod_all, (0, me * ada_cols), (nb_all, ada_cols))

    def ada_dw_fn(c_v, dm_v):
        return [lax.dot_general(_silu(c_v), dm_v, (((0,), (0,)), ((), ())), precision=lax.Precision.HIGHEST,
                                preferred_element_type=F32)]

    (dw_ada,) = _small(ada_dw_fn, [c_all, dmod_cols], [((d, ada_cols), F32)], "ada_dw")
    ada_out = _adamw(dw_ada[None], w_ada[0], m_w_ada[0], v_w_ada[0], "adamw_ada")
    ada_g, ada_d, ada_nm, ada_nv = [o[None] for o in ada_out]

    got = list(_exchange([_col_blocks(dw_in).astype(BF16)], False, "exchange_dw_in")) + list(got)
    names = ["w_in", "w_sb", "w_fox", "w_out", "w_gate", "w_up", "w_down"]
    big_out = [_adamw(g, w, m, v, "adamw_" + n, tr=256) for g, w, m, v, n in zip(got, big, big_m, big_v, names)]
    big_g, big_d, big_nm, big_nv = [[o[i][None] for o in big_out] for i in range(4)]

    def ordered(ada, small, bigs):
        b_ada_o, g_mix_o, b_f_o, b_gate_o, g_ffn_o, g_final_o = small
        w_in_o, w_sb_o, w_fx_o, w_out_o, w_gate_o, w_up_o, w_dn_o = bigs
        return [ada, b_ada_o, g_mix_o, w_in_o, b_f_o, b_gate_o, w_sb_o, w_fx_o, w_out_o, g_ffn_o, w_gate_o, w_up_o,
                w_dn_o, g_final_o]

    return (loss, grad_x, *ordered(ada_g, small_g, big_g), *ordered(ada_d, small_d, big_d),
            *ordered(ada_nm, small_nm, big_nm), *ordered(ada_nv, small_nv, big_nv))
```

```python
import jax
import jax.numpy as jnp
from jax import lax
from jax.experimental import pallas as pl
from jax.experimental.pallas import tpu as pltpu

F32 = jnp.float32
BF16 = jnp.bfloat16
HEAD_DIM = 64
LANES = 128
N_DEV = 8
RMS_EPS = 1e-6
ADAM_LR, ADAM_B1, ADAM_B2, ADAM_EPS, ADAM_WD, ADAM_STEP = 0.001, 0.9, 0.999, 1e-08, 0.01, 10
NEG = -1e30
MESH = pl.DeviceIdType.MESH


def _pick(n, cands):
    for c in cands:
        if n % c == 0:
            return c
    raise ValueError(f"no tile for {n} in {cands}")


def _arb(n):
    return pltpu.CompilerParams(dimension_semantics=("arbitrary",) * n)


def _dot(a, b):
    return jnp.dot(a, b, preferred_element_type=F32)


def _dot_nt(a, b):
    return lax.dot_general(a, b, (((1,), (1,)), ((), ())), preferred_element_type=F32)


def _dot_tn(a, b):
    return lax.dot_general(a, b, (((0,), (0,)), ((), ())), preferred_element_type=F32)


def _split2(v):
    hi = v.astype(BF16)
    return hi, (v - hi.astype(F32)).astype(BF16)


def _dot_split2(v, m):
    hi, lo = _split2(v)
    return _dot(hi, m) + _dot(lo, m)


def _tri_dot3(m, v):
    h1 = v.astype(BF16)
    r1 = v - h1.astype(F32)
    h2 = r1.astype(BF16)
    h3 = (r1 - h2.astype(F32)).astype(BF16)
    return _dot(m, h1) + _dot(m, h2) + _dot(m, h3)


def _sigmoid(v):
    return 1.0 / (1.0 + jnp.exp(-v))


def _silu(v):
    return v * _sigmoid(v)


VMEM_BLOCK_BUDGET = 44 << 20


def _col_tiles(n):
    return [n // q for q in range(1, n // LANES + 1) if n % q == 0 and (n // q) % LANES == 0]


def _size(dt):
    return jnp.dtype(dt).itemsize


def _mm(pairs, out_dtypes, name, tm=512, extras=(), epilogue=None):
    m, n = pairs[0][0].shape[0], pairs[0][1].shape[1]
    tm = _pick(m, (tm, 256, 128, 64, 32, 16, 8))
    lhs = []
    for a, _ in pairs:
        if not any(a is x for x in lhs):
            lhs.append(a)
    odts = out_dtypes if epilogue is not None else [out_dtypes]
    n_acc = len(pairs) if epilogue is not None else 1
    per_col = (sum(b.shape[0] * _size(b.dtype) for _, b in pairs) * 2
               + tm * 2 * (sum(_size(d) for d in odts) + sum(_size(e.dtype) for e in extras)) + tm * 4 * n_acc)
    fixed = 2 * sum(tm * a.shape[1] * _size(a.dtype) for a in lhs)
    tn = next((c for c in _col_tiles(n) if fixed + per_col * c <= VMEM_BLOCK_BUDGET), LANES)
    n_l, n_p, n_e = len(lhs), len(pairs), len(extras)

    def body(*refs):
        l_refs, b_refs, e_refs, o_refs = refs[:n_l], refs[n_l:n_l + n_p], refs[n_l + n_p:n_l + n_p + n_e], refs[n_l + n_p + n_e:]
        vals = [r[...].astype(BF16) for r in l_refs]
        accs = []
        for (a, _), b_ref in zip(pairs, b_refs):
            av = vals[next(i for i, x in enumerate(lhs) if x is a)]
            accs.append(_dot(av, b_ref[...].astype(BF16)))
        if epilogue is None:
            outs = [sum(accs[1:], accs[0])]
        else:
            outs = epilogue(accs, [r[...] for r in e_refs])
        for o_ref, v, dt in zip(o_refs, outs, odts):
            o_ref[...] = v.astype(dt)

    tile = pl.BlockSpec((tm, tn), lambda j, i: (i, j))
    res = pl.pallas_call(
        body, grid=(n // tn, m // tm),
        in_specs=[pl.BlockSpec((tm, a.shape[1]), lambda j, i: (i, 0)) for a in lhs]
        + [pl.BlockSpec((b.shape[0], tn), lambda j, i: (0, j)) for _, b in pairs] + [tile] * n_e,
        out_specs=[tile] * len(odts), out_shape=[jax.ShapeDtypeStruct((m, n), d) for d in odts],
        name=name, compiler_params=_arb(2),
    )(*lhs, *[b for _, b in pairs], *extras)
    return res if epilogue is not None else res[0]


def _mm_tn(a, b, name):
    t, m = a.shape
    n = b.shape[1]

    def fits(tm, tn, tk):
        return 2 * (tk * tm * _size(a.dtype) + tk * tn * _size(b.dtype) + tm * tn * 4) <= VMEM_BLOCK_BUDGET

    tm, tn, tk = next((tm, tn, tk) for tn in _col_tiles(n) for tm in _col_tiles(m) if tm <= 1536
                      for tk in (1024, 512, 256, 128) if t % tk == 0 and fits(tm, tn, tk))

    def body(a_ref, b_ref, o_ref):
        @pl.when(pl.program_id(2) == 0)
        def _():
            o_ref[...] = jnp.zeros_like(o_ref)

        o_ref[...] += _dot_tn(a_ref[...].astype(BF16), b_ref[...].astype(BF16))

    return pl.pallas_call(
        body, grid=(m // tm, n // tn, t // tk),
        in_specs=[pl.BlockSpec((tk, tm), lambda i, j, k: (k, i)), pl.BlockSpec((tk, tn), lambda i, j, k: (k, j))],
        out_specs=pl.BlockSpec((tm, tn), lambda i, j, k: (i, j)),
        out_shape=jax.ShapeDtypeStruct((m, n), F32), name=name, compiler_params=_arb(3),
    )(a, b)


def _rowmap(fn, ins, outs, name, ts=512, products=()):
    bsz, seq = next(a.shape[:2] for a in ins if a.ndim == 3 and a.shape[1] != 1)
    ts = _pick(seq, (ts, 256, 128, 64, 32, 16, 8))
    n_given = len(ins)
    ins = list(ins) + [t for pair in products for t in pair]
    n_in = len(ins)

    def in_spec(a):
        if a.ndim == 2:
            return pl.BlockSpec(a.shape, lambda b, s: (0, 0))
        if a.shape[1] == 1:
            return pl.BlockSpec((1, 1, a.shape[2]), lambda b, s: (b, 0, 0))
        return pl.BlockSpec((1, ts, a.shape[2]), lambda b, s: (b, s, 0))

    def out_spec(kind, w):
        if kind == "row":
            return pl.BlockSpec((1, ts, w), lambda b, s: (b, s, 0))
        if kind == "batch":
            return pl.BlockSpec((1, 1, w), lambda b, s: (b, 0, 0))
        return pl.BlockSpec((1, w), lambda b, s: (0, 0))

    def out_shape(kind, w, dt):
        shp = {"row": (bsz, seq, w), "batch": (bsz, 1, w), "global": (1, w)}[kind]
        return jax.ShapeDtypeStruct(shp, dt)

    def body(*refs):
        b, s = pl.program_id(0), pl.program_id(1)
        vals = [r[...] if a.ndim == 2 else r[0] for r, a in zip(refs[:n_in], ins)]
        prods = [_dot(vals[t].astype(BF16), vals[t + 1].astype(BF16)) for t in range(n_given, n_in, 2)]
        res = fn(*vals[:n_given], *prods)
        for o_ref, (kind, _, dt), v in zip(refs[n_in:], outs, res):
            if kind == "row":
                o_ref[0] = v.astype(dt)
            elif kind == "batch":
                @pl.when(s == 0)
                def _():
                    o_ref[...] = jnp.zeros_like(o_ref)

                o_ref[0] += v
            else:
                @pl.when((s == 0) & (b == 0))
                def _():
                    o_ref[...] = jnp.zeros_like(o_ref)

                o_ref[...] += v

    return pl.pallas_call(
        body, grid=(bsz, seq // ts), in_specs=[in_spec(a) for a in ins],
        out_specs=[out_spec(k, w) for k, w, _ in outs],
        out_shape=[out_shape(*o) for o in outs], name=name, compiler_params=_arb(2),
    )(*ins)


def _small(fn, ins, out_shapes, name):
    n_in = len(ins)

    def body(*refs):
        res = fn(*[r[...] for r in refs[:n_in]])
        for o_ref, v in zip(refs[n_in:], res):
            o_ref[...] = v

    return pl.pallas_call(body, out_shape=[jax.ShapeDtypeStruct(s, d) for s, d in out_shapes], name=name)(*ins)


def _mesh_pos():
    mx, my, mc = lax.axis_index("x"), lax.axis_index("y"), lax.axis_index("c")
    return mx, my, mc, 4 * mx + 2 * my + mc


def _peer(mx, my, mc, k):
    px = 1 - mx if k & 4 else mx
    py = 1 - my if k & 2 else my
    pc = 1 - mc if k & 1 else mc
    return (px, py, pc), 4 * px + 2 * py + pc


ANY_SPEC = pl.BlockSpec(memory_space=pl.ANY)


def _exchange_shapes(arrs, gather):
    return [jax.ShapeDtypeStruct((N_DEV,) + tuple(x.shape if gather else x.shape[1:]), x.dtype) for x in arrs]


def _exchange_sems(n_arr):
    return [pltpu.SemaphoreType.DMA((n_arr, N_DEV)), pltpu.SemaphoreType.DMA((n_arr, N_DEV)),
            pltpu.SemaphoreType.DMA((n_arr,))]


def _exchange_start(x_refs, out_refs, sems, gather, wait=False):
    send_sems, recv_sems, local_sems = sems
    mx, my, mc, me = _mesh_pos()
    owns, sends, recvs = [], [], []
    for a, (x_ref, out_ref) in enumerate(zip(x_refs, out_refs)):
        owns.append(pltpu.make_async_copy(x_ref if gather else x_ref.at[me], out_ref.at[me], local_sems.at[a]))
        for k in range(1, N_DEV):
            peer, pid = _peer(mx, my, mc, k)
            src = x_ref if gather else x_ref.at[pid]
            sends.append(pltpu.make_async_remote_copy(
                src_ref=src, dst_ref=out_ref.at[me], send_sem=send_sems.at[a, k], recv_sem=recv_sems.at[a, k],
                device_id=peer, device_id_type=MESH))
            if wait:
                recvs.append(pltpu.make_async_remote_copy(
                    src_ref=src, dst_ref=out_ref.at[pid], send_sem=send_sems.at[a, k], recv_sem=recv_sems.at[a, k],
                    device_id=peer, device_id_type=MESH))
    if not wait:
        for cp in owns + sends:
            cp.start()
        return
    for cp in recvs:
        cp.wait_recv()
    for cp in sends:
        cp.wait_send()
    for cp in owns:
        cp.wait()


def _exchange(arrs, gather, name):
    n_arr = len(arrs)

    def body(*refs):
        x_refs, out_refs, sems = refs[:n_arr], refs[n_arr:2 * n_arr], refs[2 * n_arr:]
        _exchange_start(x_refs, out_refs, sems, gather)
        _exchange_start(x_refs, out_refs, sems, gather, wait=True)

    return pl.pallas_call(
        body, out_shape=_exchange_shapes(arrs, gather), in_specs=[ANY_SPEC] * n_arr, out_specs=[ANY_SPEC] * n_arr,
        scratch_shapes=_exchange_sems(n_arr), name=name,
    )(*arrs)


QK_SCALE = HEAD_DIM ** -0.5


def _low_lanes():
    return lax.broadcasted_iota(jnp.int32, (1, LANES), 1) < HEAD_DIM


def _stack_heads(v, scale=None):
    lo = _low_lanes()
    zero = jnp.zeros_like(v)
    s = jnp.concatenate([jnp.where(lo, v, zero), jnp.where(lo, zero, v)], axis=0)
    return s if scale is None else s * scale


def _stack_cols(v):
    return jnp.concatenate([v[:, 0:1], v[:, HEAD_DIM:HEAD_DIM + 1]], axis=0)


def _unstack(v, tq):
    return jnp.where(_low_lanes(), v[:tq], v[tq:])


def _tile_pos(tq, tk, q0):
    rows = lax.broadcasted_iota(jnp.int32, (2 * tq, tk), 0)
    cols = lax.broadcasted_iota(jnp.int32, (2 * tq, tk), 1)
    return q0 + jnp.where(rows >= tq, rows - tq, rows), cols, rows < tq


def _tri(tk, cmp):
    r = lax.broadcasted_iota(jnp.int32, (tk, tk), 0)
    c = lax.broadcasted_iota(jnp.int32, (tk, tk), 1)
    return jnp.where(cmp(r, c), 1.0, 0.0).astype(BF16)


def _softplus(z):
    return jnp.maximum(z, 0.0) + jnp.log(1.0 + jnp.exp(-jnp.abs(z)))


PREFIX_BLOCK = 256


def _running(v, tri, later):
    blk = tri.shape[0]
    nb = v.shape[1] // blk
    parts = [v[:, b * blk:(b + 1) * blk] for b in range(nb)]
    outs, run = [None] * nb, None
    for b in (reversed(range(nb)) if later else range(nb)):
        inside = _dot_split2(parts[b], tri)
        outs[b] = inside if run is None else inside + run
        total = jnp.sum(parts[b], axis=1, keepdims=True)
        run = total if run is None else run + total
    return (outs[0] if nb == 1 else jnp.concatenate(outs, axis=1)), run


def _pair_specs(tq, seq, np_, off):
    return [pl.BlockSpec((1, tq, LANES), lambda b, p, i: (b, i, off + p)),
            pl.BlockSpec((1, seq, LANES), lambda b, p, i: (b, 0, off + np_ + p)),
            pl.BlockSpec((1, seq, LANES), lambda b, p, i: (b, 0, off + 2 * np_ + p))]


def _key_tiles(tile, carry, q0, tq, tk, upward):
    nfull = q0 // tk
    edge = range(tq // tk)
    if upward:
        carry = lax.fori_loop(0, nfull, lambda j, cr: tile(pl.multiple_of(j * tk, tk), cr, False), carry)
        for jm in edge:
            carry = tile(pl.multiple_of(q0 + jm * tk, tk), carry, True)
        return carry
    for jm in reversed(edge):
        carry = tile(pl.multiple_of(q0 + jm * tk, tk), carry, True)
    return lax.fori_loop(0, nfull, lambda jj, cr: tile(pl.multiple_of((nfull - 1 - jj) * tk, tk), cr, False), carry)


def _grid_ends(bsz, np_, nq):
    b, p, i = pl.program_id(0), pl.program_id(1), pl.program_id(2)
    return (b == 0) & (p == 0) & (i == 0), (b == bsz - 1) & (p == np_ - 1) & (i == nq - 1)


def _sb_fwd(qkv, np_, tq, tk, name, gathered=()):
    bsz, seq, _ = qkv.shape
    n_g = len(gathered)

    def body(*refs):
        q_ref, k_ref, v_ref = refs[:3]
        x_refs, (y_ref, tot_ref) = refs[3:3 + n_g], refs[3 + n_g:5 + n_g]
        out_refs, sems = refs[5 + n_g:5 + 2 * n_g], refs[5 + 2 * n_g:]
        first, last = _grid_ends(bsz, np_, seq // tq)
        if n_g:
            @pl.when(first)
            def _():
                _exchange_start(x_refs, out_refs, sems, True)

        q0 = pl.program_id(2) * tq
        tpos, cols, _ = _tile_pos(tq, tk, q0)
        msuf = _tri(min(tk, PREFIX_BLOCK), lambda a, b: a > b)
        qs = _stack_heads(q_ref[0], QK_SCALE)

        def tile(k0, carry, masked):
            tot, acc = carry
            z = _dot_nt(qs, k_ref[0, pl.ds(k0, tk), :])
            sp = _softplus(z)
            if masked:
                seen = (k0 + cols) < tpos
                sp = jnp.where(seen, sp, 0.0)
            sp_after, sp_tot = _running(sp, msuf, later=True)
            logw = z - sp - sp_after - tot
            if masked:
                logw = jnp.where(seen, logw, NEG)
            return tot + sp_tot, acc + _dot(jnp.exp(logw).astype(BF16), v_ref[0, pl.ds(k0, tk), :])

        init = (jnp.zeros((2 * tq, 1), F32), jnp.zeros((2 * tq, LANES), F32))
        tot, acc = _key_tiles(tile, init, q0, tq, tk, upward=False)
        y_ref[0] = _unstack(acc, tq)
        tot_ref[0, 0] = _unstack(tot, tq)
        if n_g:
            @pl.when(last)
            def _():
                _exchange_start(x_refs, out_refs, sems, True, wait=True)

    y, tot, *got = pl.pallas_call(
        body, grid=(bsz, np_, seq // tq), in_specs=_pair_specs(tq, seq, np_, 0) + [ANY_SPEC] * n_g,
        out_specs=[pl.BlockSpec((1, tq, LANES), lambda b, p, i: (b, i, p)),
                   pl.BlockSpec((1, 1, tq, LANES), lambda b, p, i: (b, p, i, 0))] + [ANY_SPEC] * n_g,
        out_shape=[jax.ShapeDtypeStruct((bsz, seq, np_ * LANES), F32),
                   jax.ShapeDtypeStruct((bsz, np_, seq, LANES), F32)] + _exchange_shapes(gathered, True),
        scratch_shapes=_exchange_sems(n_g) if n_g else [],
        name=name, compiler_params=_arb(3),
    )(qkv, qkv, qkv, *gathered)
    return y, tot, got


def _sb_bwd(qkv, dy, tot, np_, tq, tk, name, swapped=()):
    bsz, seq, _ = qkv.shape
    nq = seq // tq
    n_s = len(swapped)

    def body(*refs):
        q_ref, k_ref, v_ref, dy_ref, tot_ref = refs[:5]
        x_refs, (dq_ref, dk_ref, dv_ref) = refs[5:5 + n_s], refs[5 + n_s:8 + n_s]
        out_refs = refs[8 + n_s:8 + 2 * n_s]
        dk_acc, dv_acc = refs[8 + 2 * n_s:10 + 2 * n_s]
        sems = refs[10 + 2 * n_s:]
        first, last = _grid_ends(bsz, np_, nq)
        if n_s:
            @pl.when(first)
            def _():
                _exchange_start(x_refs, out_refs, sems, False)

        i = pl.program_id(2)
        q0 = i * tq

        @pl.when(i == 0)
        def _():
            dk_acc[...] = jnp.zeros_like(dk_acc)
            dv_acc[...] = jnp.zeros_like(dv_acc)

        tpos, cols, _ = _tile_pos(tq, tk, q0)
        mincl = _tri(min(tk, PREFIX_BLOCK), lambda a, b: a <= b)
        mexcl = _tri(min(tk, PREFIX_BLOCK), lambda a, b: a < b)
        qs = _stack_heads(q_ref[0], QK_SCALE)
        dys = _stack_heads(dy_ref[0].astype(BF16))
        tots = _stack_cols(tot_ref[0, 0])

        def tile(k0, carry, masked):
            c_sp, c_g, dq = carry
            kb = k_ref[0, pl.ds(k0, tk), :]
            z = _dot_nt(qs, kb)
            sp = _softplus(z)
            if masked:
                seen = (k0 + cols) < tpos
                sp = jnp.where(seen, sp, 0.0)
            sp_upto, sp_tot = _running(sp, mincl, later=False)
            logw = z - sp - (tots - c_sp - sp_upto)
            if masked:
                logw = jnp.where(seen, logw, NEG)
            w = jnp.exp(logw)
            g = w * _dot_nt(dys, v_ref[0, pl.ds(k0, tk), :])
            g_before, g_tot = _running(g, mexcl, later=False)
            beta = jnp.exp(jnp.minimum(z - sp, 0.0))
            dz = g - beta * (g + c_g + g_before)
            if masked:
                dz = jnp.where(seen, dz, 0.0)
            dzb = dz.astype(BF16)
            dk_acc[pl.ds(k0, tk), :] += _dot_tn(dzb, qs)
            dv_acc[pl.ds(k0, tk), :] += _dot_tn(w.astype(BF16), dys)
            return c_sp + sp_tot, c_g + g_tot, dq + _dot(dzb, kb)

        zero = jnp.zeros((2 * tq, 1), F32)
        _, _, dq = _key_tiles(tile, (zero, zero, jnp.zeros((2 * tq, LANES), F32)), q0, tq, tk, upward=True)
        dq_ref[0] = (_unstack(dq, tq) * QK_SCALE).astype(BF16)

        @pl.when(i == nq - 1)
        def _():
            dk_ref[0] = dk_acc[...].astype(BF16)
            dv_ref[0] = dv_acc[...].astype(BF16)

        if n_s:
            @pl.when(last)
            def _():
                _exchange_start(x_refs, out_refs, sems, False, wait=True)

    tile_spec = pl.BlockSpec((1, tq, LANES), lambda b, p, i: (b, i, p))
    whole = pl.BlockSpec((1, seq, LANES), lambda b, p, i: (b, 0, p))
    out = jax.ShapeDtypeStruct((bsz, seq, np_ * LANES), BF16)
    dq, dk, dv, *got = pl.pallas_call(
        body, grid=(bsz, np_, nq),
        in_specs=_pair_specs(tq, seq, np_, 0) + [tile_spec, pl.BlockSpec((1, 1, tq, LANES), lambda b, p, i: (b, p, i, 0))]
        + [ANY_SPEC] * n_s,
        out_specs=[tile_spec, whole, whole] + [ANY_SPEC] * n_s,
        out_shape=[out, out, out] + _exchange_shapes(swapped, False),
        scratch_shapes=[pltpu.VMEM((seq, LANES), F32), pltpu.VMEM((seq, LANES), F32)] + (_exchange_sems(n_s) if n_s else []),
        name=name, compiler_params=_arb(3),
    )(qkv, qkv, qkv, dy, tot, *swapped)
    return dq, dk, dv, got


def _fox_fwd(qkv, cumq, cumk, np_, tq, tk, name):
    bsz, seq, _ = qkv.shape

    def body(q_ref, k_ref, v_ref, cq_ref, ck_ref, y_ref, lse_ref):
        q0 = pl.program_id(2) * tq
        tpos, cols, top = _tile_pos(tq, tk, q0)
        qs = _stack_heads(q_ref[0], QK_SCALE)
        cq = _stack_cols(cq_ref[0, 0])

        def tile(k0, carry, masked):
            m, l, acc = carry
            ck = jnp.where(top, ck_ref[0, 0, 0:1, pl.ds(k0, tk)], ck_ref[0, 0, 1:2, pl.ds(k0, tk)])
            s = _dot_nt(qs, k_ref[0, pl.ds(k0, tk), :]) + (cq - ck)
            if masked:
                s = jnp.where((k0 + cols) <= tpos, s, NEG)
            m_new = jnp.maximum(m, jnp.max(s, axis=1, keepdims=True))
            p = jnp.exp(s - m_new)
            alpha = jnp.exp(m - m_new)
            return (m_new, alpha * l + jnp.sum(p, axis=1, keepdims=True),
                    alpha * acc + _dot(p.astype(BF16), v_ref[0, pl.ds(k0, tk), :]))

        init = (jnp.full((2 * tq, 1), NEG, F32), jnp.zeros((2 * tq, 1), F32), jnp.zeros((2 * tq, LANES), F32))
        m, l, acc = _key_tiles(tile, init, q0, tq, tk, upward=True)
        y_ref[0] = _unstack(acc / l, tq)
        lse_ref[0, 0] = _unstack(m + jnp.log(l), tq)

    row4 = pl.BlockSpec((1, 1, tq, LANES), lambda b, p, i: (b, p, i, 0))
    return pl.pallas_call(
        body, grid=(bsz, np_, seq // tq),
        in_specs=_pair_specs(tq, seq, np_, 3 * np_) + [row4, pl.BlockSpec((1, 1, 8, seq), lambda b, p, i: (b, p, 0, 0))],
        out_specs=[pl.BlockSpec((1, tq, LANES), lambda b, p, i: (b, i, p)), row4],
        out_shape=[jax.ShapeDtypeStruct((bsz, seq, np_ * LANES), F32),
                   jax.ShapeDtypeStruct((bsz, np_, seq, LANES), F32)],
        name=name, compiler_params=_arb(3),
    )(qkv, qkv, qkv, cumq, cumk)


def _fox_bwd(qkv, dy, y, lse, cumq, cumk, np_, tq, tk, name):
    bsz, seq, _ = qkv.shape
    nq = seq // tq

    def body(q_ref, k_ref, v_ref, dy_ref, y_ref, lse_ref, cq_ref, ck_ref,
             dq_ref, dk_ref, dv_ref, dck_ref, dcq_ref, dk_acc, dv_acc, dck_acc):
        i = pl.program_id(2)
        q0 = i * tq

        @pl.when(i == 0)
        def _():
            dk_acc[...] = jnp.zeros_like(dk_acc)
            dv_acc[...] = jnp.zeros_like(dv_acc)
            dck_acc[...] = jnp.zeros_like(dck_acc)

        tpos, cols, top = _tile_pos(tq, tk, q0)
        qs = _stack_heads(q_ref[0], QK_SCALE)
        dyf = dy_ref[0]
        dys = _stack_heads(dyf.astype(BF16))
        dyy = dyf * y_ref[0]
        lo = _low_lanes()
        delta = jnp.concatenate([jnp.sum(jnp.where(lo, dyy, 0.0), axis=1, keepdims=True),
                                 jnp.sum(jnp.where(lo, 0.0, dyy), axis=1, keepdims=True)], axis=0)
        cq = _stack_cols(cq_ref[0, 0])
        lse_s = _stack_cols(lse_ref[0, 0])

        def tile(k0, carry, masked):
            dq, row = carry
            kb = k_ref[0, pl.ds(k0, tk), :]
            ck = jnp.where(top, ck_ref[0, 0, 0:1, pl.ds(k0, tk)], ck_ref[0, 0, 1:2, pl.ds(k0, tk)])
            s = _dot_nt(qs, kb) + (cq - ck)
            if masked:
                s = jnp.where((k0 + cols) <= tpos, s, NEG)
            p = jnp.exp(s - lse_s)
            ds = p * (_dot_nt(dys, v_ref[0, pl.ds(k0, tk), :]) - delta)
            dsb = ds.astype(BF16)
            dk_acc[pl.ds(k0, tk), :] += _dot_tn(dsb, qs)
            dv_acc[pl.ds(k0, tk), :] += _dot_tn(p.astype(BF16), dys)
            dck_acc[0:1, pl.ds(k0, tk)] += -jnp.sum(ds[:tq], axis=0, keepdims=True)
            dck_acc[1:2, pl.ds(k0, tk)] += -jnp.sum(ds[tq:], axis=0, keepdims=True)
            return dq + _dot(dsb, kb), row + jnp.sum(ds, axis=1, keepdims=True)

        init = (jnp.zeros((2 * tq, LANES), F32), jnp.zeros((2 * tq, 1), F32))
        dq, row = _key_tiles(tile, init, q0, tq, tk, upward=True)
        dq_ref[0] = (_unstack(dq, tq) * QK_SCALE).astype(BF16)
        dcq_ref[0, 0] = _unstack(row, tq)

        @pl.when(i == nq - 1)
        def _():
            dk_ref[0] = dk_acc[...].astype(BF16)
            dv_ref[0] = dv_acc[...].astype(BF16)
            dck_ref[0, 0] = dck_acc[...]

    tile_spec = pl.BlockSpec((1, tq, LANES), lambda b, p, i: (b, i, p))
    whole = pl.BlockSpec((1, seq, LANES), lambda b, p, i: (b, 0, p))
    row4 = pl.BlockSpec((1, 1, tq, LANES), lambda b, p, i: (b, p, i, 0))
    key4 = pl.BlockSpec((1, 1, 8, seq), lambda b, p, i: (b, p, 0, 0))
    out = jax.ShapeDtypeStruct((bsz, seq, np_ * LANES), BF16)
    return pl.pallas_call(
        body, grid=(bsz, np_, nq),
        in_specs=_pair_specs(tq, seq, np_, 3 * np_) + [tile_spec, tile_spec, row4, row4, key4],
        out_specs=[tile_spec, whole, whole, key4, row4],
        out_shape=[out, out, out, jax.ShapeDtypeStruct((bsz, np_, 8, seq), F32),
                   jax.ShapeDtypeStruct((bsz, np_, seq, LANES), F32)],
        scratch_shapes=[pltpu.VMEM((seq, LANES), F32), pltpu.VMEM((seq, LANES), F32), pltpu.VMEM((8, seq), F32)],
        name=name, compiler_params=_arb(3),
    )(qkv, qkv, qkv, dy, y, lse, cumq, cumk)


def _cum_fwd(fl, bf, np_, name, tb=256):
    bsz, seq, _ = fl.shape
    tb = _pick(seq, (tb, 128))

    def body(fl_ref, bf_ref, o_ref, q_ref):
        tri = _tri(tb, lambda a, b: b <= a)
        lo = _low_lanes()

        def step(j, carry):
            r0 = pl.multiple_of(j * tb, tb)
            blk = _tri_dot3(tri, -_softplus(-(fl_ref[0, pl.ds(r0, tb), :] + bf_ref[...]))) + carry
            o_ref[0, pl.ds(r0, tb), :] = blk
            for p in range(np_):
                q_ref[0, p, pl.ds(r0, tb), :] = jnp.where(lo, blk[:, 2 * p:2 * p + 1], blk[:, 2 * p + 1:2 * p + 2])
            return blk[tb - 1:tb, :]

        lax.fori_loop(0, seq // tb, step, jnp.zeros((1, LANES), F32))

    return pl.pallas_call(
        body, grid=(bsz,),
        in_specs=[pl.BlockSpec((1, seq, LANES), lambda b: (b, 0, 0)), pl.BlockSpec((1, LANES), lambda b: (0, 0))],
        out_specs=[pl.BlockSpec((1, seq, LANES), lambda b: (b, 0, 0)),
                   pl.BlockSpec((1, np_, seq, LANES), lambda b: (b, 0, 0, 0))],
        out_shape=[jax.ShapeDtypeStruct(fl.shape, F32), jax.ShapeDtypeStruct((bsz, np_, seq, LANES), F32)],
        name=name, compiler_params=_arb(1),
    )(fl, bf)


def _cum_bwd(dck, dcq, fl, bf, np_, name, tb=256):
    bsz, seq, _ = fl.shape
    tb = _pick(seq, (tb, 128))
    nb = seq // tb

    def body(dck_ref, dcq_ref, fl_ref, bf_ref, o_ref, db_ref):
        @pl.when(pl.program_id(0) == 0)
        def _():
            db_ref[...] = jnp.zeros_like(db_ref)

        tri = _tri(tb, lambda a, b: b >= a)
        lane = lax.broadcasted_iota(jnp.int32, (1, LANES), 1)

        def step(jj, carry):
            tail, tot = carry
            r0 = pl.multiple_of((nb - 1 - jj) * tb, tb)
            dc = dck_ref[0, pl.ds(r0, tb), :]
            for p in range(np_):
                pair = dcq_ref[0, p, pl.ds(r0, tb), :]
                dc = dc + jnp.where(lane == 2 * p, pair, 0.0) + jnp.where(lane == 2 * p + 1, pltpu.roll(pair, HEAD_DIM, 1), 0.0)
            dlf = _tri_dot3(tri, dc) + tail
            dfl = dlf * _sigmoid(-(fl_ref[0, pl.ds(r0, tb), :] + bf_ref[...]))
            o_ref[0, pl.ds(r0, tb), :] = dfl
            return dlf[0:1, :], tot + jnp.sum(dfl, axis=0, keepdims=True)

        zero = jnp.zeros((1, LANES), F32)
        _, tot = lax.fori_loop(0, nb, step, (zero, zero))
        db_ref[...] += tot

    whole = pl.BlockSpec((1, seq, LANES), lambda b: (b, 0, 0))
    vec = pl.BlockSpec((1, LANES), lambda b: (0, 0))
    return pl.pallas_call(
        body, grid=(bsz,), in_specs=[whole, pl.BlockSpec((1, np_, seq, LANES), lambda b: (b, 0, 0, 0)), whole, vec],
        out_specs=[whole, vec],
        out_shape=[jax.ShapeDtypeStruct(fl.shape, F32), jax.ShapeDtypeStruct((1, LANES), F32)],
        name=name, compiler_params=_arb(1),
    )(dck, dcq, fl, bf)


def _adamw_math(w, g, m, v):
    m = ADAM_B1 * m + (1.0 - ADAM_B1) * g
    v = ADAM_B2 * v + (1.0 - ADAM_B2) * (g * g)
    m_hat = m / (1.0 - ADAM_B1 ** ADAM_STEP)
    v_hat = v / (1.0 - ADAM_B2 ** ADAM_STEP)
    return -ADAM_LR * (m_hat / (jnp.sqrt(v_hat) + ADAM_EPS) + ADAM_WD * w), m, v


def _adamw(gparts, w, m, v, name, tr=512):
    nslots, rows, cols = gparts.shape
    tr = _pick(rows, (tr, 256, 128, 64, 32, 16, 8))

    def body(g_ref, w_ref, m_ref, v_ref, go_ref, d_ref, mo_ref, vo_ref):
        g = g_ref[0].astype(F32)
        for k in range(1, nslots):
            g = g + g_ref[k].astype(F32)
        go_ref[...] = g
        d_ref[...], mo_ref[...], vo_ref[...] = _adamw_math(w_ref[...], g, m_ref[...], v_ref[...])

    blk = pl.BlockSpec((tr, cols), lambda i: (i, 0))
    shp = jax.ShapeDtypeStruct((rows, cols), F32)
    return pl.pallas_call(
        body, grid=(rows // tr,), in_specs=[pl.BlockSpec((nslots, tr, cols), lambda i: (0, i, 0)), blk, blk, blk],
        out_specs=[blk] * 4, out_shape=[shp] * 4, name=name, compiler_params=_arb(1),
    )(gparts, w, m, v)


def _rows128(a):
    return a.reshape(-1, LANES)


def _pad_rows(a, mult):
    extra = (-a.shape[0]) % mult
    return a if extra == 0 else jnp.concatenate([a, jnp.zeros((extra, a.shape[1]), a.dtype)], axis=0)


def _pack(arrs, mult):
    return _pad_rows(jnp.concatenate([_rows128(a) for a in arrs], axis=0), mult)


def _unpack(flat, shapes):
    out, off = [], 0
    for shp in shapes:
        n = 1
        for s in shp:
            n *= s
        out.append(flat[off:off + n // LANES].reshape(shp))
        off += n // LANES
    return out


def _col_blocks(full):
    k, n = full.shape
    return full.reshape(k, N_DEV, n // N_DEV).transpose(1, 0, 2)


def _from_col_blocks(blocks):
    _, k, n = blocks.shape
    return blocks.transpose(1, 0, 2).reshape(k, N_DEV * n)


def kernel(x, c, w_ada, b_ada, g_mix, w_in, b_forget, b_gate, w_branch_sb, w_branch_fox, w_out, g_ffn, w_ffn_gate, w_ffn_up, w_ffn_down, g_final, loss_target, m_w_ada, m_b_ada, m_g_mix, m_w_in, m_b_forget, m_b_gate, m_w_branch_sb, m_w_branch_fox, m_w_out, m_g_ffn, m_w_ffn_gate, m_w_ffn_up, m_w_ffn_down, m_g_final, v_w_ada, v_b_ada, v_g_mix, v_w_in, v_b_forget, v_b_gate, v_w_branch_sb, v_w_branch_fox, v_w_out, v_g_ffn, v_w_ffn_gate, v_w_ffn_up, v_w_ffn_down, v_g_final):
    bsz, seq, d = x.shape
    tok = bsz * seq
    nh = b_forget.shape[-1]
    d_in = w_in.shape[-1] * N_DEV
    d_att = (d_in - nh - 2 * d) // 6
    assert d_att == nh * HEAD_DIM and nh % 2 == 0
    np_ = nh // 2
    d_ff = w_ffn_gate.shape[-1] * N_DEV
    n_mod = w_ada.shape[-1] * N_DEV // d
    me = 4 * lax.axis_index("x") + 2 * lax.axis_index("y") + lax.axis_index("c")
    tq_att = _pick(seq, (512, 256, 128))
    tk_att = tq_att
    tq_fox = tk_fox = tq_att

    big = [w_in[0], w_branch_sb[0], w_branch_fox[0], w_out[0], w_ffn_gate[0], w_ffn_up[0], w_ffn_down[0]]
    big_m = [m_w_in[0], m_w_branch_sb[0], m_w_branch_fox[0], m_w_out[0], m_w_ffn_gate[0], m_w_ffn_up[0], m_w_ffn_down[0]]
    big_v = [v_w_in[0], v_w_branch_sb[0], v_w_branch_fox[0], v_w_out[0], v_w_ffn_gate[0], v_w_ffn_up[0], v_w_ffn_down[0]]

    w_in_g, c_g = _exchange([big[0].astype(BF16), _rows128(c)], True, "gather_w_in_c")
    w_in_f = _from_col_blocks(w_in_g)
    w_qkv = w_in_f[:, :6 * d_att]
    w_f = jnp.concatenate([w_in_f[:, 6 * d_att:6 * d_att + nh], jnp.zeros((d, LANES - nh), BF16)], axis=1)
    w_gl = w_in_f[:, 6 * d_att + nh:]

    c_all = c_g.reshape(N_DEV * bsz, d)
    nb_all = N_DEV * bsz
    ada_cols = w_ada.shape[-1]
    b_ada_loc = lax.dynamic_slice(b_ada, (0, me * ada_cols), (1, ada_cols))

    def mod_fn(c_v, w_v, b_v):
        return [jnp.dot(_silu(c_v), w_v, precision=lax.Precision.HIGHEST, preferred_element_type=F32) + b_v]

    (mod_part,) = _small(mod_fn, [c_all, w_ada[0], b_ada_loc], [((nb_all, ada_cols), F32)], "ada_mod")
    mod_all = _from_col_blocks(_exchange([_rows128(mod_part)], True, "gather_mod")[0].reshape(N_DEV, nb_all, ada_cols))
    mod = lax.dynamic_slice(mod_all, (me * bsz, 0), (bsz, n_mod * d))
    shift1, scale1, gate1, shift2, scale2, gate2 = [mod[:, i * d:(i + 1) * d].reshape(bsz, 1, d) for i in range(6)]

    def norm_mod_fn(x_v, sc, sh, g):
        n = x_v * lax.rsqrt(jnp.mean(x_v * x_v, axis=-1, keepdims=True) + RMS_EPS) * g
        return [n * (1.0 + sc) + sh]

    (h,) = _rowmap(norm_mod_fn, [x, scale1, shift1, g_mix], [("row", d, BF16)], "norm1")
    h2d = h.reshape(tok, d)
    qkv = _mm([(h2d, w_qkv)], BF16, "proj_qkv").reshape(bsz, seq, 6 * d_att)
    gl = _mm([(h2d, w_gl)], F32, "proj_gates").reshape(bsz, seq, 2 * d)
    fl = _mm([(h2d, w_f)], F32, "proj_forget").reshape(bsz, seq, LANES)

    bf_pad = jnp.concatenate([b_forget, jnp.zeros((1, LANES - nh), F32)], axis=1)
    cum, cumq = _cum_fwd(fl, bf_pad, np_, "cum_fwd")
    cumk = jnp.concatenate([cum[:, :, :nh].transpose(0, 2, 1).reshape(bsz, np_, 2, seq),
                            jnp.zeros((bsz, np_, 6, seq), F32)], axis=2)

    y_sb, tot_sb, gath = _sb_fwd(qkv, np_, tq_att, tk_att, "sb_fwd", gathered=[a.astype(BF16) for a in big[1:]])
    w_sb_f = _from_col_blocks(gath[0])
    w_fx_f = _from_col_blocks(gath[1])
    w_out_f = gath[2].reshape(d, d)
    w_g_f, w_u_f = _from_col_blocks(gath[3]), _from_col_blocks(gath[4])
    w_dn_f = gath[5].reshape(d_ff, d)
    y_fx, lse_fx = _fox_fwd(qkv, cumq, cumk, np_, tq_fox, tk_fox, "fox_fwd")

    def merge_fn(gl_v, bg, us, uf):
        gates = _sigmoid(gl_v + bg)
        return [gates[:, :d] * us + gates[:, d:] * uf, us, uf]

    merged, u_sb, u_fx = _rowmap(merge_fn, [gl, b_gate], [("row", d, BF16), ("row", d, F32), ("row", d, F32)], "merge",
                                 products=[(y_sb, w_sb_f), (y_fx, w_fx_f)])

    def resid_norm_fn(x_v, g1, sc, sh, g, mo_v):
        x1_v = x_v + g1 * mo_v
        n = x1_v * lax.rsqrt(jnp.mean(x1_v * x1_v, axis=-1, keepdims=True) + RMS_EPS) * g
        return [x1_v, n * (1.0 + sc) + sh, mo_v]

    x1, h2, mo = _rowmap(resid_norm_fn, [x, gate1, scale2, shift2, g_ffn],
                         [("row", d, F32), ("row", d, BF16), ("row", d, F32)], "norm2", products=[(merged, w_out_f)])
    h2_2d = h2.reshape(tok, d)

    def swiglu_fn(accs, _):
        a, u = accs
        return [_silu(a) * u, a, u]

    f, a_s, u_s = _mm([(h2_2d, w_g_f), (h2_2d, w_u_f)], [BF16, BF16, BF16], "ffn_in", epilogue=swiglu_fn)

    def head_fn(x1_v, g2, gf, tgt, ffn_v):
        x2 = x1_v + g2 * ffn_v
        rstd = lax.rsqrt(jnp.mean(x2 * x2, axis=-1, keepdims=True) + RMS_EPS)
        xh = x2 * rstd
        err = xh * gf - tgt
        loss_rows = 0.5 * jnp.mean(err * err, axis=-1, keepdims=True)
        dy = err * (1.0 / d)
        dxh = dy * gf
        dx2 = rstd * (dxh - xh * jnp.mean(dxh * xh, axis=-1, keepdims=True))
        return [dx2, dx2 * g2, jnp.sum(loss_rows, axis=0, keepdims=True) * jnp.ones((1, LANES), F32),
                jnp.sum(dy * xh, axis=0, keepdims=True), jnp.sum(dx2 * ffn_v, axis=0, keepdims=True)]

    dx2, dffn, loss_vec, dg_final, dgate2 = _rowmap(
        head_fn, [x1, gate2, g_final.reshape(1, d), loss_target],
        [("row", d, F32), ("row", d, BF16), ("global", LANES, F32), ("global", d, F32), ("batch", d, F32)], "head",
        products=[(f.reshape(bsz, seq, d_ff), w_dn_f)])

    dffn2d = dffn.reshape(tok, d)
    dw_dn = _mm_tn(f, dffn2d, "ffn_out_dw")

    def swiglu_bwd_fn(accs, saved):
        df_v, a, u = accs[0], saved[0].astype(F32), saved[1].astype(F32)
        sig = _sigmoid(a)
        return [df_v * u * sig * (1.0 + a * (1.0 - sig)), df_v * a * sig]

    da, du = _mm([(dffn2d, w_dn_f.T)], [BF16, BF16], "ffn_out_dx", extras=[a_s, u_s], epilogue=swiglu_bwd_fn)
    dw_gate, dw_up = _mm_tn(h2_2d, da, "ffn_gate_dw"), _mm_tn(h2_2d, du, "ffn_up_dw")
    def norm2_bwd_fn(x1_v, dx2_v, mo_v, sc, g1, g, dh_gate, dh_up):
        dh_v = dh_gate + dh_up
        rstd = lax.rsqrt(jnp.mean(x1_v * x1_v, axis=-1, keepdims=True) + RMS_EPS)
        xh = x1_v * rstd
        dn = dh_v * (1.0 + sc)
        dxh = dn * g
        dx1 = dx2_v + rstd * (dxh - xh * jnp.mean(dxh * xh, axis=-1, keepdims=True))
        return [dx1, dx1 * g1, jnp.sum(dh_v * (xh * g), axis=0, keepdims=True), jnp.sum(dh_v, axis=0, keepdims=True),
                jnp.sum(dn * xh, axis=0, keepdims=True), jnp.sum(dx1 * mo_v, axis=0, keepdims=True)]

    dx1, dmo, dscale2, dshift2, dg_ffn, dgate1 = _rowmap(
        norm2_bwd_fn, [x1, dx2, mo, scale2, gate1, g_ffn],
        [("row", d, F32), ("row", d, BF16), ("batch", d, F32), ("batch", d, F32), ("global", d, F32), ("batch", d, F32)],
        "norm2_bwd", ts=256,
        products=[(da.reshape(bsz, seq, d_ff), w_g_f.T), (du.reshape(bsz, seq, d_ff), w_u_f.T)])

    dmo2d = dmo.reshape(tok, d)
    dw_out = _mm_tn(merged.reshape(tok, d), dmo2d, "out_proj_dw")

    def merge_bwd_fn(gl_v, us, uf, bg, dm):
        gates = _sigmoid(gl_v + bg)
        gs, gf = gates[:, :d], gates[:, d:]
        dgl = jnp.concatenate([dm * us * gs * (1.0 - gs), dm * uf * gf * (1.0 - gf)], axis=1)
        return [dm * gs, dm * gf, dgl, jnp.sum(dgl, axis=0, keepdims=True)]

    du_sb, du_fx, dgl, db_gate = _rowmap(
        merge_bwd_fn, [gl, u_sb, u_fx, b_gate],
        [("row", d, BF16), ("row", d, BF16), ("row", 2 * d, BF16), ("global", 2 * d, F32)], "merge_bwd", ts=256,
        products=[(dmo, w_out_f.T)])
    du_sb2d, du_fx2d = du_sb.reshape(tok, d), du_fx.reshape(tok, d)
    dw_sb = _mm_tn(y_sb.reshape(tok, d_att), du_sb2d, "branch_sb_dw")
    dw_fx = _mm_tn(y_fx.reshape(tok, d_att), du_fx2d, "branch_fox_dw")
    dy_sb = _mm([(du_sb2d, w_sb_f.T)], F32, "branch_sb_dx").reshape(bsz, seq, d_att)
    dy_fx = _mm([(du_fx2d, w_fx_f.T)], F32, "branch_fox_dx").reshape(bsz, seq, d_att)

    blocks = [_col_blocks(dw_sb), _col_blocks(dw_fx), dw_out.reshape(N_DEV, d // N_DEV, d),
              _col_blocks(dw_gate), _col_blocks(dw_up), dw_dn.reshape(N_DEV, d_ff // N_DEV, d)]
    blocks = [b.astype(BF16) for b in blocks]
    dq_sb, dk_sb, dv_sb, got = _sb_bwd(qkv, dy_sb, tot_sb, np_, tq_att, tk_att, "sb_bwd", swapped=blocks)
    dq_fx, dk_fx, dv_fx, dck, dcq = _fox_bwd(qkv, dy_fx, y_fx, lse_fx, cumq, cumk, np_, tq_fox, tk_fox, "fox_bwd")
    dck_rows = dck[:, :, :2, :].reshape(bsz, nh, seq).transpose(0, 2, 1)
    dck_rows = jnp.concatenate([dck_rows, jnp.zeros((bsz, seq, LANES - nh), F32)], axis=2)
    dfl, db_f = _cum_bwd(dck_rows, dcq, fl, bf_pad, np_, "cum_bwd")

    dqkv = jnp.concatenate([dq_sb, dk_sb, dv_sb, dq_fx, dk_fx, dv_fx], axis=2).reshape(tok, 6 * d_att)
    dgl2d, dfl2d = dgl.reshape(tok, 2 * d), dfl.reshape(tok, LANES)
    dw_in = jnp.concatenate([_mm_tn(h2d, dqkv, "proj_qkv_dw"), _mm_tn(h2d, dfl2d, "proj_forget_dw")[:, :nh],
                             _mm_tn(h2d, dgl2d, "proj_gates_dw")], axis=1)
    def norm1_bwd_fn(x_v, dx1_v, sc, g, dh_qkv, dh_gl, dh_fl):
        dh_v = dh_qkv + dh_gl + dh_fl
        rstd = lax.rsqrt(jnp.mean(x_v * x_v, axis=-1, keepdims=True) + RMS_EPS)
        xh = x_v * rstd
        dn = dh_v * (1.0 + sc)
        dxh = dn * g
        dx = dx1_v + rstd * (dxh - xh * jnp.mean(dxh * xh, axis=-1, keepdims=True))
        return [dx, jnp.sum(dh_v * (xh * g), axis=0, keepdims=True), jnp.sum(dh_v, axis=0, keepdims=True),
                jnp.sum(dn * xh, axis=0, keepdims=True)]

    grad_x, dscale1, dshift1, dg_mix = _rowmap(
        norm1_bwd_fn, [x, dx1, scale1, g_mix],
        [("row", d, F32), ("batch", d, F32), ("batch", d, F32), ("global", d, F32)], "norm1_bwd", ts=256,
        products=[(dqkv.reshape(bsz, seq, 6 * d_att), w_qkv.T), (dgl, w_gl.T), (dfl, w_f.T)])

    dmod = jnp.concatenate([dshift1, dscale1, dgate1, dshift2, dscale2, dgate2], axis=2).reshape(bsz, n_mod * d)
    partial = [dg_mix, db_f, db_gate, dg_ffn, dg_final]
    n_dmod_rows = bsz * n_mod * d // LANES
    small_sent = _pack([dmod] + partial + [loss_vec], 8)
    small_all = _exchange([small_sent], True, "gather_small")[0]
    small_w = [b_ada, g_mix, jnp.concatenate([b_forget, jnp.zeros((1, LANES - nh), F32)], axis=1), b_gate, g_ffn,
               g_final.reshape(1, d)]
    small_m = [m_b_ada, m_g_mix, jnp.concatenate([m_b_forget, jnp.zeros((1, LANES - nh), F32)], axis=1), m_b_gate,
               m_g_ffn, m_g_final.reshape(1, d)]
    small_v = [v_b_ada, v_g_mix, jnp.concatenate([v_b_forget, jnp.zeros((1, LANES - nh), F32)], axis=1), v_b_gate,
               v_g_ffn, v_g_final.reshape(1, d)]
    small_shapes = [a.shape for a in small_w]
    n_ada_rows = n_mod * d // LANES
    n_part_rows = sum(a.shape[1] // LANES for a in partial)
    sw, sm, sv = _pack(small_w, 8), _pack(small_m, 8), _pack(small_v, 8)
    n_small_rows = sw.shape[0]

    def small_fn(all_v, w_v, m_v, v_v):
        g_ada = None
        g_rest = None
        for k in range(N_DEV):
            for b in range(bsz):
                part = all_v[k, b * n_ada_rows:(b + 1) * n_ada_rows]
                g_ada = part if g_ada is None else g_ada + part
            rest = all_v[k, n_dmod_rows:n_dmod_rows + n_part_rows + 1]
            g_rest = rest if g_rest is None else g_rest + rest
        pieces = [g_ada, g_rest[:n_part_rows]]
        if n_small_rows > n_ada_rows + n_part_rows:
            pieces.append(jnp.zeros((n_small_rows - n_ada_rows - n_part_rows, LANES), F32))
        g = jnp.concatenate(pieces, axis=0)
        return [g, *_adamw_math(w_v, g, m_v, v_v), jnp.broadcast_to(g_rest[n_part_rows:], (8, LANES))]

    shp = ((n_small_rows, LANES), F32)
    *small_out, loss_all = _small(small_fn, [small_all, sw, sm, sv], [shp] * 4 + [((8, LANES), F32)], "small_update")
    loss = loss_all[0, 0]
    small_g, small_d, small_nm, small_nv = [_unpack(o, small_shapes) for o in small_out]

    def fix_small(lst):
        b_ada_o, g_mix_o, b_f_o, b_gate_o, g_ffn_o, g_final_o = lst
        return [b_ada_o, g_mix_o, b_f_o[:, :nh], b_gate_o, g_ffn_o, g_final_o.reshape(d)]

    small_g, small_d, small_nm, small_nv = [fix_small(l) for l in (small_g, small_d, small_nm, small_nv)]

    dmod_all = small_all[:, :n_dmod_rows].reshape(nb_all, n_mod * d)
    dmod_cols = lax.dynamic_slice(dmod_all, (0, me * ada_cols), (nb_all, ada_cols))

    def ada_dw_fn(c_v, dm_v):
        return [lax.dot_general(_silu(c_v), dm_v, (((0,), (0,)), ((), ())), precision=lax.Precision.HIGHEST,
                                preferred_element_type=F32)]

    (dw_ada,) = _small(ada_dw_fn, [c_all, dmod_cols], [((d, ada_cols), F32)], "ada_dw")
    ada_out = _adamw(dw_ada[None], w_ada[0], m_w_ada[0], v_w_ada[0], "adamw_ada")
    ada_g, ada_d, ada_nm, ada_nv = [o[None] for o in ada_out]

    got = list(_exchange([_col_blocks(dw_in).astype(BF16)], False, "exchange_dw_in")) + list(got)
    names = ["w_in", "w_sb", "w_fox", "w_out", "w_gate", "w_up", "w_down"]
    big_out = [_adamw(g, w, m, v, "adamw_" + n, tr=256) for g, w, m, v, n in zip(got, big, big_m, big_v, names)]
    big_g, big_d, big_nm, big_nv = [[o[i][None] for o in big_out] for i in range(4)]

    def ordered(ada, small, bigs):
        b_ada_o, g_mix_o, b_f_o, b_gate_o, g_ffn_o, g_final_o = small
        w_in_o, w_sb_o, w_fx_o, w_out_o, w_gate_o, w_up_o, w_dn_o = bigs
        return [ada, b_ada_o, g_mix_o, w_in_o, b_f_o, b_gate_o, w_sb_o, w_fx_o, w_out_o, g_ffn_o, w_gate_o, w_up_o,
                w_dn_o, g_final_o]

    return (loss, grad_x, *ordered(ada_g, small_g, big_g), *ordered(ada_d, small_d, big_d),
            *ordered(ada_nm, small_nm, big_nm), *ordered(ada_nv, small_nv, big_nv))
```

```python
import jax
import jax.numpy as jnp
from jax import lax
from jax.experimental import pallas as pl
from jax.experimental.pallas import tpu as pltpu

F32 = jnp.float32
BF16 = jnp.bfloat16
HEAD_DIM = 64
LANES = 128
N_DEV = 8
RMS_EPS = 1e-6
ADAM_LR, ADAM_B1, ADAM_B2, ADAM_EPS, ADAM_WD, ADAM_STEP = 0.001, 0.9, 0.999, 1e-08, 0.01, 10
NEG = -1e30
MESH = pl.DeviceIdType.MESH


def _pick(n, cands):
    for c in cands:
        if n % c == 0:
            return c
    raise ValueError(f"no tile for {n} in {cands}")


def _arb(n):
    return pltpu.CompilerParams(dimension_semantics=("arbitrary",) * n)


def _dot(a, b):
    return jnp.dot(a, b, preferred_element_type=F32)


def _dot_nt(a, b):
    return lax.dot_general(a, b, (((1,), (1,)), ((), ())), preferred_element_type=F32)


def _dot_tn(a, b):
    return lax.dot_general(a, b, (((0,), (0,)), ((), ())), preferred_element_type=F32)


def _split2(v):
    hi = v.astype(BF16)
    return hi, (v - hi.astype(F32)).astype(BF16)


def _dot_split2(v, m):
    hi, lo = _split2(v)
    return _dot(hi, m) + _dot(lo, m)


def _tri_dot3(m, v):
    h1 = v.astype(BF16)
    r1 = v - h1.astype(F32)
    h2 = r1.astype(BF16)
    h3 = (r1 - h2.astype(F32)).astype(BF16)
    return _dot(m, h1) + _dot(m, h2) + _dot(m, h3)


def _sigmoid(v):
    return 1.0 / (1.0 + jnp.exp(-v))


def _silu(v):
    return v * _sigmoid(v)


VMEM_BLOCK_BUDGET = 44 << 20


def _col_tiles(n):
    return [n // q for q in range(1, n // LANES + 1) if n % q == 0 and (n // q) % LANES == 0]


def _size(dt):
    return jnp.dtype(dt).itemsize


def _mm(pairs, out_dtypes, name, tm=512, extras=(), epilogue=None):
    m, n = pairs[0][0].shape[0], pairs[0][1].shape[1]
    tm = _pick(m, (tm, 256, 128, 64, 32, 16, 8))
    lhs = []
    for a, _ in pairs:
        if not any(a is x for x in lhs):
            lhs.append(a)
    odts = out_dtypes if epilogue is not None else [out_dtypes]
    n_acc = len(pairs) if epilogue is not None else 1
    per_col = (sum(b.shape[0] * _size(b.dtype) for _, b in pairs) * 2
               + tm * 2 * (sum(_size(d) for d in odts) + sum(_size(e.dtype) for e in extras)) + tm * 4 * n_acc)
    fixed = 2 * sum(tm * a.shape[1] * _size(a.dtype) for a in lhs)
    tn = next((c for c in _col_tiles(n) if fixed + per_col * c <= VMEM_BLOCK_BUDGET), LANES)
    n_l, n_p, n_e = len(lhs), len(pairs), len(extras)

    def body(*refs):
        l_refs, b_refs, e_refs, o_refs = refs[:n_l], refs[n_l:n_l + n_p], refs[n_l + n_p:n_l + n_p + n_e], refs[n_l + n_p + n_e:]
        vals = [r[...].astype(BF16) for r in l_refs]
        accs = []
        for (a, _), b_ref in zip(pairs, b_refs):
            av = vals[next(i for i, x in enumerate(lhs) if x is a)]
            accs.append(_dot(av, b_ref[...].astype(BF16)))
        if epilogue is None:
            outs = [sum(accs[1:], accs[0])]
        else:
            outs = epilogue(accs, [r[...] for r in e_refs])
        for o_ref, v, dt in zip(o_refs, outs, odts):
            o_ref[...] = v.astype(dt)

    tile = pl.BlockSpec((tm, tn), lambda j, i: (i, j))
    res = pl.pallas_call(
        body, grid=(n // tn, m // tm),
        in_specs=[pl.BlockSpec((tm, a.shape[1]), lambda j, i: (i, 0)) for a in lhs]
        + [pl.BlockSpec((b.shape[0], tn), lambda j, i: (0, j)) for _, b in pairs] + [tile] * n_e,
        out_specs=[tile] * len(odts), out_shape=[jax.ShapeDtypeStruct((m, n), d) for d in odts],
        name=name, compiler_params=_arb(2),
    )(*lhs, *[b for _, b in pairs], *extras)
    return res if epilogue is not None else res[0]


def _mm_tn(a, b, name):
    t, m = a.shape
    n = b.shape[1]

    def fits(tm, tn, tk):
        return 2 * (tk * tm * _size(a.dtype) + tk * tn * _size(b.dtype) + tm * tn * 4) <= VMEM_BLOCK_BUDGET

    tm, tn, tk = next((tm, tn, tk) for tn in _col_tiles(n) for tm in _col_tiles(m) if tm <= 1536
                      for tk in (1024, 512, 256, 128) if t % tk == 0 and fits(tm, tn, tk))

    def body(a_ref, b_ref, o_ref):
        @pl.when(pl.program_id(2) == 0)
        def _():
            o_ref[...] = jnp.zeros_like(o_ref)

        o_ref[...] += _dot_tn(a_ref[...].astype(BF16), b_ref[...].astype(BF16))

    return pl.pallas_call(
        body, grid=(m // tm, n // tn, t // tk),
        in_specs=[pl.BlockSpec((tk, tm), lambda i, j, k: (k, i)), pl.BlockSpec((tk, tn), lambda i, j, k: (k, j))],
        out_specs=pl.BlockSpec((tm, tn), lambda i, j, k: (i, j)),
        out_shape=jax.ShapeDtypeStruct((m, n), F32), name=name, compiler_params=_arb(3),
    )(a, b)


def _rowmap(fn, ins, outs, name, ts=512, products=(), swapped=()):
    bsz, seq = next(a.shape[:2] for a in ins if a.ndim == 3 and a.shape[1] != 1)
    ts = _pick(seq, (ts, 256, 128, 64, 32, 16, 8))
    n_given = len(ins)
    ins = list(ins) + [t for pair in products for t in pair]
    n_in = len(ins)

    def in_spec(a):
        if a.ndim == 2:
            return pl.BlockSpec(a.shape, lambda b, s: (0, 0))
        if a.shape[1] == 1:
            return pl.BlockSpec((1, 1, a.shape[2]), lambda b, s: (b, 0, 0))
        return pl.BlockSpec((1, ts, a.shape[2]), lambda b, s: (b, s, 0))

    def out_spec(kind, w):
        if kind == "row":
            return pl.BlockSpec((1, ts, w), lambda b, s: (b, s, 0))
        if kind == "batch":
            return pl.BlockSpec((1, 1, w), lambda b, s: (b, 0, 0))
        return pl.BlockSpec((1, w), lambda b, s: (0, 0))

    def out_shape(kind, w, dt):
        shp = {"row": (bsz, seq, w), "batch": (bsz, 1, w), "global": (1, w)}[kind]
        return jax.ShapeDtypeStruct(shp, dt)

    n_s, n_out = len(swapped), len(outs)

    def body(*refs):
        b, s = pl.program_id(0), pl.program_id(1)
        x_refs, o_refs = refs[n_in:n_in + n_s], refs[n_in + n_s:n_in + n_s + n_out]
        got_refs, sems = refs[n_in + n_s + n_out:n_in + 2 * n_s + n_out], refs[n_in + 2 * n_s + n_out:]
        if n_s:
            @pl.when((b == 0) & (s == 0))
            def _():
                _exchange_start(x_refs, got_refs, sems, False)

        vals = [r[...] if a.ndim == 2 else r[0] for r, a in zip(refs[:n_in], ins)]
        prods = [_dot(vals[t].astype(BF16), vals[t + 1].astype(BF16)) for t in range(n_given, n_in, 2)]
        res = fn(*vals[:n_given], *prods)
        if n_s:
            @pl.when((b == bsz - 1) & (s == seq // ts - 1))
            def _():
                _exchange_start(x_refs, got_refs, sems, False, wait=True)

        for o_ref, (kind, _, dt), v in zip(o_refs, outs, res):
            if kind == "row":
                o_ref[0] = v.astype(dt)
            elif kind == "batch":
                @pl.when(s == 0)
                def _():
                    o_ref[...] = jnp.zeros_like(o_ref)

                o_ref[0] += v
            else:
                @pl.when((s == 0) & (b == 0))
                def _():
                    o_ref[...] = jnp.zeros_like(o_ref)

                o_ref[...] += v

    res = pl.pallas_call(
        body, grid=(bsz, seq // ts), in_specs=[in_spec(a) for a in ins] + [ANY_SPEC] * n_s,
        out_specs=[out_spec(k, w) for k, w, _ in outs] + [ANY_SPEC] * n_s,
        out_shape=[out_shape(*o) for o in outs] + _exchange_shapes(swapped, False),
        scratch_shapes=_exchange_sems(n_s) if n_s else [], name=name, compiler_params=_arb(2),
    )(*ins, *swapped)
    return res if not n_s else (*res[:n_out], list(res[n_out:]))


def _small(fn, ins, out_shapes, name):
    n_in = len(ins)

    def body(*refs):
        res = fn(*[r[...] for r in refs[:n_in]])
        for o_ref, v in zip(refs[n_in:], res):
            o_ref[...] = v

    return pl.pallas_call(body, out_shape=[jax.ShapeDtypeStruct(s, d) for s, d in out_shapes], name=name)(*ins)


def _mesh_pos():
    mx, my, mc = lax.axis_index("x"), lax.axis_index("y"), lax.axis_index("c")
    return mx, my, mc, 4 * mx + 2 * my + mc


def _peer(mx, my, mc, k):
    px = 1 - mx if k & 4 else mx
    py = 1 - my if k & 2 else my
    pc = 1 - mc if k & 1 else mc
    return (px, py, pc), 4 * px + 2 * py + pc


ANY_SPEC = pl.BlockSpec(memory_space=pl.ANY)


def _exchange_shapes(arrs, gather):
    return [jax.ShapeDtypeStruct((N_DEV,) + tuple(x.shape if gather else x.shape[1:]), x.dtype) for x in arrs]


def _exchange_sems(n_arr):
    return [pltpu.SemaphoreType.DMA((n_arr, N_DEV)), pltpu.SemaphoreType.DMA((n_arr, N_DEV)),
            pltpu.SemaphoreType.DMA((n_arr,))]


def _exchange_start(x_refs, out_refs, sems, gather, wait=False):
    send_sems, recv_sems, local_sems = sems
    mx, my, mc, me = _mesh_pos()
    owns, sends, recvs = [], [], []
    for a, (x_ref, out_ref) in enumerate(zip(x_refs, out_refs)):
        owns.append(pltpu.make_async_copy(x_ref if gather else x_ref.at[me], out_ref.at[me], local_sems.at[a]))
        for k in range(1, N_DEV):
            peer, pid = _peer(mx, my, mc, k)
            src = x_ref if gather else x_ref.at[pid]
            sends.append(pltpu.make_async_remote_copy(
                src_ref=src, dst_ref=out_ref.at[me], send_sem=send_sems.at[a, k], recv_sem=recv_sems.at[a, k],
                device_id=peer, device_id_type=MESH))
            if wait:
                recvs.append(pltpu.make_async_remote_copy(
                    src_ref=src, dst_ref=out_ref.at[pid], send_sem=send_sems.at[a, k], recv_sem=recv_sems.at[a, k],
                    device_id=peer, device_id_type=MESH))
    if not wait:
        for cp in owns + sends:
            cp.start()
        return
    for cp in recvs:
        cp.wait_recv()
    for cp in sends:
        cp.wait_send()
    for cp in owns:
        cp.wait()


def _exchange(arrs, gather, name):
    n_arr = len(arrs)

    def body(*refs):
        x_refs, out_refs, sems = refs[:n_arr], refs[n_arr:2 * n_arr], refs[2 * n_arr:]
        _exchange_start(x_refs, out_refs, sems, gather)
        _exchange_start(x_refs, out_refs, sems, gather, wait=True)

    return pl.pallas_call(
        body, out_shape=_exchange_shapes(arrs, gather), in_specs=[ANY_SPEC] * n_arr, out_specs=[ANY_SPEC] * n_arr,
        scratch_shapes=_exchange_sems(n_arr), name=name,
    )(*arrs)


QK_SCALE = HEAD_DIM ** -0.5


def _low_lanes():
    return lax.broadcasted_iota(jnp.int32, (1, LANES), 1) < HEAD_DIM


def _stack_heads(v, scale=None):
    lo = _low_lanes()
    zero = jnp.zeros_like(v)
    s = jnp.concatenate([jnp.where(lo, v, zero), jnp.where(lo, zero, v)], axis=0)
    return s if scale is None else s * scale


def _stack_cols(v):
    return jnp.concatenate([v[:, 0:1], v[:, HEAD_DIM:HEAD_DIM + 1]], axis=0)


def _unstack(v, tq):
    return jnp.where(_low_lanes(), v[:tq], v[tq:])


def _tile_pos(tq, tk, q0):
    rows = lax.broadcasted_iota(jnp.int32, (2 * tq, tk), 0)
    cols = lax.broadcasted_iota(jnp.int32, (2 * tq, tk), 1)
    return q0 + jnp.where(rows >= tq, rows - tq, rows), cols, rows < tq


def _tri(tk, cmp):
    r = lax.broadcasted_iota(jnp.int32, (tk, tk), 0)
    c = lax.broadcasted_iota(jnp.int32, (tk, tk), 1)
    return jnp.where(cmp(r, c), 1.0, 0.0).astype(BF16)


def _softplus(z):
    return jnp.maximum(z, 0.0) + jnp.log(1.0 + jnp.exp(-jnp.abs(z)))


PREFIX_BLOCK = 256


def _running(v, tri, later):
    blk = tri.shape[0]
    nb = v.shape[1] // blk
    parts = [v[:, b * blk:(b + 1) * blk] for b in range(nb)]
    outs, run = [None] * nb, None
    for b in (reversed(range(nb)) if later else range(nb)):
        inside = _dot_split2(parts[b], tri)
        outs[b] = inside if run is None else inside + run
        total = jnp.sum(parts[b], axis=1, keepdims=True)
        run = total if run is None else run + total
    return (outs[0] if nb == 1 else jnp.concatenate(outs, axis=1)), run


def _pair_specs(tq, seq, np_, off):
    return [pl.BlockSpec((1, tq, LANES), lambda b, p, i: (b, i, off + p)),
            pl.BlockSpec((1, seq, LANES), lambda b, p, i: (b, 0, off + np_ + p)),
            pl.BlockSpec((1, seq, LANES), lambda b, p, i: (b, 0, off + 2 * np_ + p))]


def _key_tiles(tile, carry, q0, tq, tk, upward):
    nfull = q0 // tk
    edge = range(tq // tk)
    if upward:
        carry = lax.fori_loop(0, nfull, lambda j, cr: tile(pl.multiple_of(j * tk, tk), cr, False), carry)
        for jm in edge:
            carry = tile(pl.multiple_of(q0 + jm * tk, tk), carry, True)
        return carry
    for jm in reversed(edge):
        carry = tile(pl.multiple_of(q0 + jm * tk, tk), carry, True)
    return lax.fori_loop(0, nfull, lambda jj, cr: tile(pl.multiple_of((nfull - 1 - jj) * tk, tk), cr, False), carry)


def _grid_ends(bsz, np_, nq):
    b, p, i = pl.program_id(0), pl.program_id(1), pl.program_id(2)
    return (b == 0) & (p == 0) & (i == 0), (b == bsz - 1) & (p == np_ - 1) & (i == nq - 1)


def _sb_fwd(qkv, np_, tq, tk, name, gathered=()):
    bsz, seq, _ = qkv.shape
    n_g = len(gathered)

    def body(*refs):
        q_ref, k_ref, v_ref = refs[:3]
        x_refs, (y_ref, tot_ref) = refs[3:3 + n_g], refs[3 + n_g:5 + n_g]
        out_refs, sems = refs[5 + n_g:5 + 2 * n_g], refs[5 + 2 * n_g:]
        first, last = _grid_ends(bsz, np_, seq // tq)
        if n_g:
            @pl.when(first)
            def _():
                _exchange_start(x_refs, out_refs, sems, True)

        q0 = pl.program_id(2) * tq
        tpos, cols, _ = _tile_pos(tq, tk, q0)
        msuf = _tri(min(tk, PREFIX_BLOCK), lambda a, b: a > b)
        qs = _stack_heads(q_ref[0], QK_SCALE)

        def tile(k0, carry, masked):
            tot, acc = carry
            z = _dot_nt(qs, k_ref[0, pl.ds(k0, tk), :])
            sp = _softplus(z)
            if masked:
                seen = (k0 + cols) < tpos
                sp = jnp.where(seen, sp, 0.0)
            sp_after, sp_tot = _running(sp, msuf, later=True)
            logw = z - sp - sp_after - tot
            if masked:
                logw = jnp.where(seen, logw, NEG)
            return tot + sp_tot, acc + _dot(jnp.exp(logw).astype(BF16), v_ref[0, pl.ds(k0, tk), :])

        init = (jnp.zeros((2 * tq, 1), F32), jnp.zeros((2 * tq, LANES), F32))
        tot, acc = _key_tiles(tile, init, q0, tq, tk, upward=False)
        y_ref[0] = _unstack(acc, tq)
        tot_ref[0, 0] = _unstack(tot, tq)
        if n_g:
            @pl.when(last)
            def _():
                _exchange_start(x_refs, out_refs, sems, True, wait=True)

    y, tot, *got = pl.pallas_call(
        body, grid=(bsz, np_, seq // tq), in_specs=_pair_specs(tq, seq, np_, 0) + [ANY_SPEC] * n_g,
        out_specs=[pl.BlockSpec((1, tq, LANES), lambda b, p, i: (b, i, p)),
                   pl.BlockSpec((1, 1, tq, LANES), lambda b, p, i: (b, p, i, 0))] + [ANY_SPEC] * n_g,
        out_shape=[jax.ShapeDtypeStruct((bsz, seq, np_ * LANES), F32),
                   jax.ShapeDtypeStruct((bsz, np_, seq, LANES), F32)] + _exchange_shapes(gathered, True),
        scratch_shapes=_exchange_sems(n_g) if n_g else [],
        name=name, compiler_params=_arb(3),
    )(qkv, qkv, qkv, *gathered)
    return y, tot, got


def _sb_bwd(qkv, dy, tot, np_, tq, tk, name, swapped=()):
    bsz, seq, _ = qkv.shape
    nq = seq // tq
    n_s = len(swapped)

    def body(*refs):
        q_ref, k_ref, v_ref, dy_ref, tot_ref = refs[:5]
        x_refs, (dq_ref, dk_ref, dv_ref) = refs[5:5 + n_s], refs[5 + n_s:8 + n_s]
        out_refs = refs[8 + n_s:8 + 2 * n_s]
        dk_acc, dv_acc = refs[8 + 2 * n_s:10 + 2 * n_s]
        sems = refs[10 + 2 * n_s:]
        first, last = _grid_ends(bsz, np_, nq)
        if n_s:
            @pl.when(first)
            def _():
                _exchange_start(x_refs, out_refs, sems, False)

        i = pl.program_id(2)
        q0 = i * tq

        @pl.when(i == 0)
        def _():
            dk_acc[...] = jnp.zeros_like(dk_acc)
            dv_acc[...] = jnp.zeros_like(dv_acc)

        tpos, cols, _ = _tile_pos(tq, tk, q0)
        mincl = _tri(min(tk, PREFIX_BLOCK), lambda a, b: a <= b)
        mexcl = _tri(min(tk, PREFIX_BLOCK), lambda a, b: a < b)
        qs = _stack_heads(q_ref[0], QK_SCALE)
        dys = _stack_heads(dy_ref[0].astype(BF16))
        tots = _stack_cols(tot_ref[0, 0])

        def tile(k0, carry, masked):
            c_sp, c_g, dq = carry
            kb = k_ref[0, pl.ds(k0, tk), :]
            z = _dot_nt(qs, kb)
            sp = _softplus(z)
            if masked:
                seen = (k0 + cols) < tpos
                sp = jnp.where(seen, sp, 0.0)
            sp_upto, sp_tot = _running(sp, mincl, later=False)
            logw = z - sp - (tots - c_sp - sp_upto)
            if masked:
                logw = jnp.where(seen, logw, NEG)
            w = jnp.exp(logw)
            g = w * _dot_nt(dys, v_ref[0, pl.ds(k0, tk), :])
            g_before, g_tot = _running(g, mexcl, later=False)
            beta = jnp.exp(jnp.minimum(z - sp, 0.0))
            dz = g - beta * (g + c_g + g_before)
            if masked:
                dz = jnp.where(seen, dz, 0.0)
            dzb = dz.astype(BF16)
            dk_acc[pl.ds(k0, tk), :] += _dot_tn(dzb, qs)
            dv_acc[pl.ds(k0, tk), :] += _dot_tn(w.astype(BF16), dys)
            return c_sp + sp_tot, c_g + g_tot, dq + _dot(dzb, kb)

        zero = jnp.zeros((2 * tq, 1), F32)
        _, _, dq = _key_tiles(tile, (zero, zero, jnp.zeros((2 * tq, LANES), F32)), q0, tq, tk, upward=True)
        dq_ref[0] = (_unstack(dq, tq) * QK_SCALE).astype(BF16)

        @pl.when(i == nq - 1)
        def _():
            dk_ref[0] = dk_acc[...].astype(BF16)
            dv_ref[0] = dv_acc[...].astype(BF16)

        if n_s:
            @pl.when(last)
            def _():
                _exchange_start(x_refs, out_refs, sems, False, wait=True)

    tile_spec = pl.BlockSpec((1, tq, LANES), lambda b, p, i: (b, i, p))
    whole = pl.BlockSpec((1, seq, LANES), lambda b, p, i: (b, 0, p))
    out = jax.ShapeDtypeStruct((bsz, seq, np_ * LANES), BF16)
    dq, dk, dv, *got = pl.pallas_call(
        body, grid=(bsz, np_, nq),
        in_specs=_pair_specs(tq, seq, np_, 0) + [tile_spec, pl.BlockSpec((1, 1, tq, LANES), lambda b, p, i: (b, p, i, 0))]
        + [ANY_SPEC] * n_s,
        out_specs=[tile_spec, whole, whole] + [ANY_SPEC] * n_s,
        out_shape=[out, out, out] + _exchange_shapes(swapped, False),
        scratch_shapes=[pltpu.VMEM((seq, LANES), F32), pltpu.VMEM((seq, LANES), F32)] + (_exchange_sems(n_s) if n_s else []),
        name=name, compiler_params=_arb(3),
    )(qkv, qkv, qkv, dy, tot, *swapped)
    return dq, dk, dv, got


def _fox_fwd(qkv, cumq, cumk, np_, tq, tk, name):
    bsz, seq, _ = qkv.shape

    def body(q_ref, k_ref, v_ref, cq_ref, ck_ref, y_ref, lse_ref):
        q0 = pl.program_id(2) * tq
        tpos, cols, top = _tile_pos(tq, tk, q0)
        qs = _stack_heads(q_ref[0], QK_SCALE)
        cq = _stack_cols(cq_ref[0, 0])

        def tile(k0, carry, masked):
            m, l, acc = carry
            ck = jnp.where(top, ck_ref[0, 0, 0:1, pl.ds(k0, tk)], ck_ref[0, 0, 1:2, pl.ds(k0, tk)])
            s = _dot_nt(qs, k_ref[0, pl.ds(k0, tk), :]) + (cq - ck)
            if masked:
                s = jnp.where((k0 + cols) <= tpos, s, NEG)
            m_new = jnp.maximum(m, jnp.max(s, axis=1, keepdims=True))
            p = jnp.exp(s - m_new)
            alpha = jnp.exp(m - m_new)
            return (m_new, alpha * l + jnp.sum(p, axis=1, keepdims=True),
                    alpha * acc + _dot(p.astype(BF16), v_ref[0, pl.ds(k0, tk), :]))

        init = (jnp.full((2 * tq, 1), NEG, F32), jnp.zeros((2 * tq, 1), F32), jnp.zeros((2 * tq, LANES), F32))
        m, l, acc = _key_tiles(tile, init, q0, tq, tk, upward=True)
        y_ref[0] = _unstack(acc / l, tq)
        lse_ref[0, 0] = _unstack(m + jnp.log(l), tq)

    row4 = pl.BlockSpec((1, 1, tq, LANES), lambda b, p, i: (b, p, i, 0))
    return pl.pallas_call(
        body, grid=(bsz, np_, seq // tq),
        in_specs=_pair_specs(tq, seq, np_, 3 * np_) + [row4, pl.BlockSpec((1, 1, 8, seq), lambda b, p, i: (b, p, 0, 0))],
        out_specs=[pl.BlockSpec((1, tq, LANES), lambda b, p, i: (b, i, p)), row4],
        out_shape=[jax.ShapeDtypeStruct((bsz, seq, np_ * LANES), F32),
                   jax.ShapeDtypeStruct((bsz, np_, seq, LANES), F32)],
        name=name, compiler_params=_arb(3),
    )(qkv, qkv, qkv, cumq, cumk)


def _fox_bwd(qkv, dy, y, lse, cumq, cumk, np_, tq, tk, name):
    bsz, seq, _ = qkv.shape
    nq = seq // tq

    def body(q_ref, k_ref, v_ref, dy_ref, y_ref, lse_ref, cq_ref, ck_ref,
             dq_ref, dk_ref, dv_ref, dck_ref, dcq_ref, dk_acc, dv_acc, dck_acc):
        i = pl.program_id(2)
        q0 = i * tq

        @pl.when(i == 0)
        def _():
            dk_acc[...] = jnp.zeros_like(dk_acc)
            dv_acc[...] = jnp.zeros_like(dv_acc)
            dck_acc[...] = jnp.zeros_like(dck_acc)

        tpos, cols, top = _tile_pos(tq, tk, q0)
        qs = _stack_heads(q_ref[0], QK_SCALE)
        dyf = dy_ref[0]
        dys = _stack_heads(dyf.astype(BF16))
        dyy = dyf * y_ref[0]
        lo = _low_lanes()
        delta = jnp.concatenate([jnp.sum(jnp.where(lo, dyy, 0.0), axis=1, keepdims=True),
                                 jnp.sum(jnp.where(lo, 0.0, dyy), axis=1, keepdims=True)], axis=0)
        cq = _stack_cols(cq_ref[0, 0])
        lse_s = _stack_cols(lse_ref[0, 0])

        def tile(k0, carry, masked):
            dq, row = carry
            kb = k_ref[0, pl.ds(k0, tk), :]
            ck = jnp.where(top, ck_ref[0, 0, 0:1, pl.ds(k0, tk)], ck_ref[0, 0, 1:2, pl.ds(k0, tk)])
            s = _dot_nt(qs, kb) + (cq - ck)
            if masked:
                s = jnp.where((k0 + cols) <= tpos, s, NEG)
            p = jnp.exp(s - lse_s)
            ds = p * (_dot_nt(dys, v_ref[0, pl.ds(k0, tk), :]) - delta)
            dsb = ds.astype(BF16)
            dk_acc[pl.ds(k0, tk), :] += _dot_tn(dsb, qs)
            dv_acc[pl.ds(k0, tk), :] += _dot_tn(p.astype(BF16), dys)
            dck_acc[0:1, pl.ds(k0, tk)] += -jnp.sum(ds[:tq], axis=0, keepdims=True)
            dck_acc[1:2, pl.ds(k0, tk)] += -jnp.sum(ds[tq:], axis=0, keepdims=True)
            return dq + _dot(dsb, kb), row + jnp.sum(ds, axis=1, keepdims=True)

        init = (jnp.zeros((2 * tq, LANES), F32), jnp.zeros((2 * tq, 1), F32))
        dq, row = _key_tiles(tile, init, q0, tq, tk, upward=True)
        dq_ref[0] = (_unstack(dq, tq) * QK_SCALE).astype(BF16)
        dcq_ref[0, 0] = _unstack(row, tq)

        @pl.when(i == nq - 1)
        def _():
            dk_ref[0] = dk_acc[...].astype(BF16)
            dv_ref[0] = dv_acc[...].astype(BF16)
            dck_ref[0, 0] = dck_acc[...]

    tile_spec = pl.BlockSpec((1, tq, LANES), lambda b, p, i: (b, i, p))
    whole = pl.BlockSpec((1, seq, LANES), lambda b, p, i: (b, 0, p))
    row4 = pl.BlockSpec((1, 1, tq, LANES), lambda b, p, i: (b, p, i, 0))
    key4 = pl.BlockSpec((1, 1, 8, seq), lambda b, p, i: (b, p, 0, 0))
    out = jax.ShapeDtypeStruct((bsz, seq, np_ * LANES), BF16)
    return pl.pallas_call(
        body, grid=(bsz, np_, nq),
        in_specs=_pair_specs(tq, seq, np_, 3 * np_) + [tile_spec, tile_spec, row4, row4, key4],
        out_specs=[tile_spec, whole, whole, key4, row4],
        out_shape=[out, out, out, jax.ShapeDtypeStruct((bsz, np_, 8, seq), F32),
                   jax.ShapeDtypeStruct((bsz, np_, seq, LANES), F32)],
        scratch_shapes=[pltpu.VMEM((seq, LANES), F32), pltpu.VMEM((seq, LANES), F32), pltpu.VMEM((8, seq), F32)],
        name=name, compiler_params=_arb(3),
    )(qkv, qkv, qkv, dy, y, lse, cumq, cumk)


def _cum_fwd(fl, bf, np_, name, tb=256):
    bsz, seq, _ = fl.shape
    tb = _pick(seq, (tb, 128))

    def body(fl_ref, bf_ref, o_ref, q_ref):
        tri = _tri(tb, lambda a, b: b <= a)
        lo = _low_lanes()

        def step(j, carry):
            r0 = pl.multiple_of(j * tb, tb)
            blk = _tri_dot3(tri, -_softplus(-(fl_ref[0, pl.ds(r0, tb), :] + bf_ref[...]))) + carry
            o_ref[0, pl.ds(r0, tb), :] = blk
            for p in range(np_):
                q_ref[0, p, pl.ds(r0, tb), :] = jnp.where(lo, blk[:, 2 * p:2 * p + 1], blk[:, 2 * p + 1:2 * p + 2])
            return blk[tb - 1:tb, :]

        lax.fori_loop(0, seq // tb, step, jnp.zeros((1, LANES), F32))

    return pl.pallas_call(
        body, grid=(bsz,),
        in_specs=[pl.BlockSpec((1, seq, LANES), lambda b: (b, 0, 0)), pl.BlockSpec((1, LANES), lambda b: (0, 0))],
        out_specs=[pl.BlockSpec((1, seq, LANES), lambda b: (b, 0, 0)),
                   pl.BlockSpec((1, np_, seq, LANES), lambda b: (b, 0, 0, 0))],
        out_shape=[jax.ShapeDtypeStruct(fl.shape, F32), jax.ShapeDtypeStruct((bsz, np_, seq, LANES), F32)],
        name=name, compiler_params=_arb(1),
    )(fl, bf)


def _cum_bwd(dck, dcq, fl, bf, np_, name, tb=256):
    bsz, seq, _ = fl.shape
    tb = _pick(seq, (tb, 128))
    nb = seq // tb

    def body(dck_ref, dcq_ref, fl_ref, bf_ref, o_ref, db_ref):
        @pl.when(pl.program_id(0) == 0)
        def _():
            db_ref[...] = jnp.zeros_like(db_ref)

        tri = _tri(tb, lambda a, b: b >= a)
        lane = lax.broadcasted_iota(jnp.int32, (1, LANES), 1)

        def step(jj, carry):
            tail, tot = carry
            r0 = pl.multiple_of((nb - 1 - jj) * tb, tb)
            dc = dck_ref[0, pl.ds(r0, tb), :]
            for p in range(np_):
                pair = dcq_ref[0, p, pl.ds(r0, tb), :]
                dc = dc + jnp.where(lane == 2 * p, pair, 0.0) + jnp.where(lane == 2 * p + 1, pltpu.roll(pair, HEAD_DIM, 1), 0.0)
            dlf = _tri_dot3(tri, dc) + tail
            dfl = dlf * _sigmoid(-(fl_ref[0, pl.ds(r0, tb), :] + bf_ref[...]))
            o_ref[0, pl.ds(r0, tb), :] = dfl
            return dlf[0:1, :], tot + jnp.sum(dfl, axis=0, keepdims=True)

        zero = jnp.zeros((1, LANES), F32)
        _, tot = lax.fori_loop(0, nb, step, (zero, zero))
        db_ref[...] += tot

    whole = pl.BlockSpec((1, seq, LANES), lambda b: (b, 0, 0))
    vec = pl.BlockSpec((1, LANES), lambda b: (0, 0))
    return pl.pallas_call(
        body, grid=(bsz,), in_specs=[whole, pl.BlockSpec((1, np_, seq, LANES), lambda b: (b, 0, 0, 0)), whole, vec],
        out_specs=[whole, vec],
        out_shape=[jax.ShapeDtypeStruct(fl.shape, F32), jax.ShapeDtypeStruct((1, LANES), F32)],
        name=name, compiler_params=_arb(1),
    )(dck, dcq, fl, bf)


def _adamw_math(w, g, m, v):
    m = ADAM_B1 * m + (1.0 - ADAM_B1) * g
    v = ADAM_B2 * v + (1.0 - ADAM_B2) * (g * g)
    m_hat = m / (1.0 - ADAM_B1 ** ADAM_STEP)
    v_hat = v / (1.0 - ADAM_B2 ** ADAM_STEP)
    return -ADAM_LR * (m_hat / (jnp.sqrt(v_hat) + ADAM_EPS) + ADAM_WD * w), m, v


def _adamw(gparts, w, m, v, name, tr=512):
    nslots, rows, cols = gparts.shape
    tr = _pick(rows, (tr, 256, 128, 64, 32, 16, 8))

    def body(g_ref, w_ref, m_ref, v_ref, go_ref, d_ref, mo_ref, vo_ref):
        g = g_ref[0].astype(F32)
        for k in range(1, nslots):
            g = g + g_ref[k].astype(F32)
        go_ref[...] = g
        d_ref[...], mo_ref[...], vo_ref[...] = _adamw_math(w_ref[...], g, m_ref[...], v_ref[...])

    blk = pl.BlockSpec((tr, cols), lambda i: (i, 0))
    shp = jax.ShapeDtypeStruct((rows, cols), F32)
    return pl.pallas_call(
        body, grid=(rows // tr,), in_specs=[pl.BlockSpec((nslots, tr, cols), lambda i: (0, i, 0)), blk, blk, blk],
        out_specs=[blk] * 4, out_shape=[shp] * 4, name=name, compiler_params=_arb(1),
    )(gparts, w, m, v)


def _rows128(a):
    return a.reshape(-1, LANES)


def _pad_rows(a, mult):
    extra = (-a.shape[0]) % mult
    return a if extra == 0 else jnp.concatenate([a, jnp.zeros((extra, a.shape[1]), a.dtype)], axis=0)


def _pack(arrs, mult):
    return _pad_rows(jnp.concatenate([_rows128(a) for a in arrs], axis=0), mult)


def _unpack(flat, shapes):
    out, off = [], 0
    for shp in shapes:
        n = 1
        for s in shp:
            n *= s
        out.append(flat[off:off + n // LANES].reshape(shp))
        off += n // LANES
    return out


def _col_blocks(full):
    k, n = full.shape
    return full.reshape(k, N_DEV, n // N_DEV).transpose(1, 0, 2)


def _from_col_blocks(blocks):
    _, k, n = blocks.shape
    return blocks.transpose(1, 0, 2).reshape(k, N_DEV * n)


def kernel(x, c, w_ada, b_ada, g_mix, w_in, b_forget, b_gate, w_branch_sb, w_branch_fox, w_out, g_ffn, w_ffn_gate, w_ffn_up, w_ffn_down, g_final, loss_target, m_w_ada, m_b_ada, m_g_mix, m_w_in, m_b_forget, m_b_gate, m_w_branch_sb, m_w_branch_fox, m_w_out, m_g_ffn, m_w_ffn_gate, m_w_ffn_up, m_w_ffn_down, m_g_final, v_w_ada, v_b_ada, v_g_mix, v_w_in, v_b_forget, v_b_gate, v_w_branch_sb, v_w_branch_fox, v_w_out, v_g_ffn, v_w_ffn_gate, v_w_ffn_up, v_w_ffn_down, v_g_final):
    bsz, seq, d = x.shape
    tok = bsz * seq
    nh = b_forget.shape[-1]
    d_in = w_in.shape[-1] * N_DEV
    d_att = (d_in - nh - 2 * d) // 6
    assert d_att == nh * HEAD_DIM and nh % 2 == 0
    np_ = nh // 2
    d_ff = w_ffn_gate.shape[-1] * N_DEV
    n_mod = w_ada.shape[-1] * N_DEV // d
    me = 4 * lax.axis_index("x") + 2 * lax.axis_index("y") + lax.axis_index("c")
    tq_att = _pick(seq, (512, 256, 128))
    tk_att = tq_att
    tq_fox = tk_fox = tq_att

    big = [w_in[0], w_branch_sb[0], w_branch_fox[0], w_out[0], w_ffn_gate[0], w_ffn_up[0], w_ffn_down[0]]
    big_m = [m_w_in[0], m_w_branch_sb[0], m_w_branch_fox[0], m_w_out[0], m_w_ffn_gate[0], m_w_ffn_up[0], m_w_ffn_down[0]]
    big_v = [v_w_in[0], v_w_branch_sb[0], v_w_branch_fox[0], v_w_out[0], v_w_ffn_gate[0], v_w_ffn_up[0], v_w_ffn_down[0]]

    w_in_g, c_g = _exchange([big[0].astype(BF16), _rows128(c)], True, "gather_w_in_c")
    w_in_f = _from_col_blocks(w_in_g)
    w_qkv = w_in_f[:, :6 * d_att]
    w_f = jnp.concatenate([w_in_f[:, 6 * d_att:6 * d_att + nh], jnp.zeros((d, LANES - nh), BF16)], axis=1)
    w_gl = w_in_f[:, 6 * d_att + nh:]

    c_all = c_g.reshape(N_DEV * bsz, d)
    nb_all = N_DEV * bsz
    ada_cols = w_ada.shape[-1]
    b_ada_loc = lax.dynamic_slice(b_ada, (0, me * ada_cols), (1, ada_cols))

    def mod_fn(c_v, w_v, b_v):
        return [jnp.dot(_silu(c_v), w_v, precision=lax.Precision.HIGHEST, preferred_element_type=F32) + b_v]

    (mod_part,) = _small(mod_fn, [c_all, w_ada[0], b_ada_loc], [((nb_all, ada_cols), F32)], "ada_mod")
    mod_all = _from_col_blocks(_exchange([_rows128(mod_part)], True, "gather_mod")[0].reshape(N_DEV, nb_all, ada_cols))
    mod = lax.dynamic_slice(mod_all, (me * bsz, 0), (bsz, n_mod * d))
    shift1, scale1, gate1, shift2, scale2, gate2 = [mod[:, i * d:(i + 1) * d].reshape(bsz, 1, d) for i in range(6)]

    def norm_mod_fn(x_v, sc, sh, g):
        n = x_v * lax.rsqrt(jnp.mean(x_v * x_v, axis=-1, keepdims=True) + RMS_EPS) * g
        return [n * (1.0 + sc) + sh]

    (h,) = _rowmap(norm_mod_fn, [x, scale1, shift1, g_mix], [("row", d, BF16)], "norm1")
    h2d = h.reshape(tok, d)
    qkv = _mm([(h2d, w_qkv)], BF16, "proj_qkv").reshape(bsz, seq, 6 * d_att)
    gl = _mm([(h2d, w_gl)], F32, "proj_gates").reshape(bsz, seq, 2 * d)
    fl = _mm([(h2d, w_f)], F32, "proj_forget").reshape(bsz, seq, LANES)

    bf_pad = jnp.concatenate([b_forget, jnp.zeros((1, LANES - nh), F32)], axis=1)
    cum, cumq = _cum_fwd(fl, bf_pad, np_, "cum_fwd")
    cumk = jnp.concatenate([cum[:, :, :nh].transpose(0, 2, 1).reshape(bsz, np_, 2, seq),
                            jnp.zeros((bsz, np_, 6, seq), F32)], axis=2)

    y_sb, tot_sb, gath = _sb_fwd(qkv, np_, tq_att, tk_att, "sb_fwd", gathered=[a.astype(BF16) for a in big[1:]])
    w_sb_f = _from_col_blocks(gath[0])
    w_fx_f = _from_col_blocks(gath[1])
    w_out_f = gath[2].reshape(d, d)
    w_g_f, w_u_f = _from_col_blocks(gath[3]), _from_col_blocks(gath[4])
    w_dn_f = gath[5].reshape(d_ff, d)
    y_fx, lse_fx = _fox_fwd(qkv, cumq, cumk, np_, tq_fox, tk_fox, "fox_fwd")

    def merge_fn(gl_v, bg, us, uf):
        gates = _sigmoid(gl_v + bg)
        return [gates[:, :d] * us + gates[:, d:] * uf, us, uf]

    merged, u_sb, u_fx = _rowmap(merge_fn, [gl, b_gate], [("row", d, BF16), ("row", d, F32), ("row", d, F32)], "merge",
                                 products=[(y_sb, w_sb_f), (y_fx, w_fx_f)])

    def resid_norm_fn(x_v, g1, sc, sh, g, mo_v):
        x1_v = x_v + g1 * mo_v
        n = x1_v * lax.rsqrt(jnp.mean(x1_v * x1_v, axis=-1, keepdims=True) + RMS_EPS) * g
        return [x1_v, n * (1.0 + sc) + sh, mo_v]

    x1, h2, mo = _rowmap(resid_norm_fn, [x, gate1, scale2, shift2, g_ffn],
                         [("row", d, F32), ("row", d, BF16), ("row", d, F32)], "norm2", products=[(merged, w_out_f)])
    h2_2d = h2.reshape(tok, d)

    def swiglu_fn(accs, _):
        a, u = accs
        return [_silu(a) * u, a, u]

    f, a_s, u_s = _mm([(h2_2d, w_g_f), (h2_2d, w_u_f)], [BF16, BF16, BF16], "ffn_in", epilogue=swiglu_fn)

    def head_fn(x1_v, g2, gf, tgt, ffn_v):
        x2 = x1_v + g2 * ffn_v
        rstd = lax.rsqrt(jnp.mean(x2 * x2, axis=-1, keepdims=True) + RMS_EPS)
        xh = x2 * rstd
        err = xh * gf - tgt
        loss_rows = 0.5 * jnp.mean(err * err, axis=-1, keepdims=True)
        dy = err * (1.0 / d)
        dxh = dy * gf
        dx2 = rstd * (dxh - xh * jnp.mean(dxh * xh, axis=-1, keepdims=True))
        return [dx2, dx2 * g2, jnp.sum(loss_rows, axis=0, keepdims=True) * jnp.ones((1, LANES), F32),
                jnp.sum(dy * xh, axis=0, keepdims=True), jnp.sum(dx2 * ffn_v, axis=0, keepdims=True)]

    dx2, dffn, loss_vec, dg_final, dgate2 = _rowmap(
        head_fn, [x1, gate2, g_final.reshape(1, d), loss_target],
        [("row", d, F32), ("row", d, BF16), ("global", LANES, F32), ("global", d, F32), ("batch", d, F32)], "head",
        products=[(f.reshape(bsz, seq, d_ff), w_dn_f)])

    dffn2d = dffn.reshape(tok, d)
    dw_dn = _mm_tn(f, dffn2d, "ffn_out_dw")

    def swiglu_bwd_fn(accs, saved):
        df_v, a, u = accs[0], saved[0].astype(F32), saved[1].astype(F32)
        sig = _sigmoid(a)
        return [df_v * u * sig * (1.0 + a * (1.0 - sig)), df_v * a * sig]

    da, du = _mm([(dffn2d, w_dn_f.T)], [BF16, BF16], "ffn_out_dx", extras=[a_s, u_s], epilogue=swiglu_bwd_fn)
    dw_gate, dw_up = _mm_tn(h2_2d, da, "ffn_gate_dw"), _mm_tn(h2_2d, du, "ffn_up_dw")
    def norm2_bwd_fn(x1_v, dx2_v, mo_v, sc, g1, g, dh_gate, dh_up):
        dh_v = dh_gate + dh_up
        rstd = lax.rsqrt(jnp.mean(x1_v * x1_v, axis=-1, keepdims=True) + RMS_EPS)
        xh = x1_v * rstd
        dn = dh_v * (1.0 + sc)
        dxh = dn * g
        dx1 = dx2_v + rstd * (dxh - xh * jnp.mean(dxh * xh, axis=-1, keepdims=True))
        return [dx1, dx1 * g1, jnp.sum(dh_v * (xh * g), axis=0, keepdims=True), jnp.sum(dh_v, axis=0, keepdims=True),
                jnp.sum(dn * xh, axis=0, keepdims=True), jnp.sum(dx1 * mo_v, axis=0, keepdims=True)]

    dx1, dmo, dscale2, dshift2, dg_ffn, dgate1 = _rowmap(
        norm2_bwd_fn, [x1, dx2, mo, scale2, gate1, g_ffn],
        [("row", d, F32), ("row", d, BF16), ("batch", d, F32), ("batch", d, F32), ("global", d, F32), ("batch", d, F32)],
        "norm2_bwd", ts=256,
        products=[(da.reshape(bsz, seq, d_ff), w_g_f.T), (du.reshape(bsz, seq, d_ff), w_u_f.T)])

    dmo2d = dmo.reshape(tok, d)
    dw_out = _mm_tn(merged.reshape(tok, d), dmo2d, "out_proj_dw")

    def merge_bwd_fn(gl_v, us, uf, bg, dm):
        gates = _sigmoid(gl_v + bg)
        gs, gf = gates[:, :d], gates[:, d:]
        dgl = jnp.concatenate([dm * us * gs * (1.0 - gs), dm * uf * gf * (1.0 - gf)], axis=1)
        return [dm * gs, dm * gf, dgl, jnp.sum(dgl, axis=0, keepdims=True)]

    du_sb, du_fx, dgl, db_gate = _rowmap(
        merge_bwd_fn, [gl, u_sb, u_fx, b_gate],
        [("row", d, BF16), ("row", d, BF16), ("row", 2 * d, BF16), ("global", 2 * d, F32)], "merge_bwd", ts=256,
        products=[(dmo, w_out_f.T)])
    du_sb2d, du_fx2d = du_sb.reshape(tok, d), du_fx.reshape(tok, d)
    dw_sb = _mm_tn(y_sb.reshape(tok, d_att), du_sb2d, "branch_sb_dw")
    dw_fx = _mm_tn(y_fx.reshape(tok, d_att), du_fx2d, "branch_fox_dw")
    dy_sb = _mm([(du_sb2d, w_sb_f.T)], F32, "branch_sb_dx").reshape(bsz, seq, d_att)
    dy_fx = _mm([(du_fx2d, w_fx_f.T)], F32, "branch_fox_dx").reshape(bsz, seq, d_att)

    blocks = [_col_blocks(dw_sb), _col_blocks(dw_fx), dw_out.reshape(N_DEV, d // N_DEV, d),
              _col_blocks(dw_gate), _col_blocks(dw_up), dw_dn.reshape(N_DEV, d_ff // N_DEV, d)]
    blocks = [b.astype(BF16) for b in blocks]
    dq_sb, dk_sb, dv_sb, got = _sb_bwd(qkv, dy_sb, tot_sb, np_, tq_att, tk_att, "sb_bwd", swapped=blocks)
    dq_fx, dk_fx, dv_fx, dck, dcq = _fox_bwd(qkv, dy_fx, y_fx, lse_fx, cumq, cumk, np_, tq_fox, tk_fox, "fox_bwd")
    dck_rows = dck[:, :, :2, :].reshape(bsz, nh, seq).transpose(0, 2, 1)
    dck_rows = jnp.concatenate([dck_rows, jnp.zeros((bsz, seq, LANES - nh), F32)], axis=2)
    dfl, db_f = _cum_bwd(dck_rows, dcq, fl, bf_pad, np_, "cum_bwd")

    dqkv = jnp.concatenate([dq_sb, dk_sb, dv_sb, dq_fx, dk_fx, dv_fx], axis=2).reshape(tok, 6 * d_att)
    dgl2d, dfl2d = dgl.reshape(tok, 2 * d), dfl.reshape(tok, LANES)
    dw_in = jnp.concatenate([_mm_tn(h2d, dqkv, "proj_qkv_dw"), _mm_tn(h2d, dfl2d, "proj_forget_dw")[:, :nh],
                             _mm_tn(h2d, dgl2d, "proj_gates_dw")], axis=1)
    def norm1_bwd_fn(x_v, dx1_v, sc, g, dh_qkv, dh_gl, dh_fl):
        dh_v = dh_qkv + dh_gl + dh_fl
        rstd = lax.rsqrt(jnp.mean(x_v * x_v, axis=-1, keepdims=True) + RMS_EPS)
        xh = x_v * rstd
        dn = dh_v * (1.0 + sc)
        dxh = dn * g
        dx = dx1_v + rstd * (dxh - xh * jnp.mean(dxh * xh, axis=-1, keepdims=True))
        return [dx, jnp.sum(dh_v * (xh * g), axis=0, keepdims=True), jnp.sum(dh_v, axis=0, keepdims=True),
                jnp.sum(dn * xh, axis=0, keepdims=True)]

    grad_x, dscale1, dshift1, dg_mix, got_in = _rowmap(
        norm1_bwd_fn, [x, dx1, scale1, g_mix],
        [("row", d, F32), ("batch", d, F32), ("batch", d, F32), ("global", d, F32)], "norm1_bwd", ts=256,
        products=[(dqkv.reshape(bsz, seq, 6 * d_att), w_qkv.T), (dgl, w_gl.T), (dfl, w_f.T)],
        swapped=[_col_blocks(dw_in).astype(BF16)])

    dmod = jnp.concatenate([dshift1, dscale1, dgate1, dshift2, dscale2, dgate2], axis=2).reshape(bsz, n_mod * d)
    partial = [dg_mix, db_f, db_gate, dg_ffn, dg_final]
    n_dmod_rows = bsz * n_mod * d // LANES
    small_sent = _pack([dmod] + partial + [loss_vec], 8)
    small_all = _exchange([small_sent], True, "gather_small")[0]
    small_w = [b_ada, g_mix, jnp.concatenate([b_forget, jnp.zeros((1, LANES - nh), F32)], axis=1), b_gate, g_ffn,
               g_final.reshape(1, d)]
    small_m = [m_b_ada, m_g_mix, jnp.concatenate([m_b_forget, jnp.zeros((1, LANES - nh), F32)], axis=1), m_b_gate,
               m_g_ffn, m_g_final.reshape(1, d)]
    small_v = [v_b_ada, v_g_mix, jnp.concatenate([v_b_forget, jnp.zeros((1, LANES - nh), F32)], axis=1), v_b_gate,
               v_g_ffn, v_g_final.reshape(1, d)]
    small_shapes = [a.shape for a in small_w]
    n_ada_rows = n_mod * d // LANES
    n_part_rows = sum(a.shape[1] // LANES for a in partial)
    sw, sm, sv = _pack(small_w, 8), _pack(small_m, 8), _pack(small_v, 8)
    n_small_rows = sw.shape[0]

    def small_fn(all_v, w_v, m_v, v_v):
        g_ada = None
        g_rest = None
        for k in range(N_DEV):
            for b in range(bsz):
                part = all_v[k, b * n_ada_rows:(b + 1) * n_ada_rows]
                g_ada = part if g_ada is None else g_ada + part
            rest = all_v[k, n_dmod_rows:n_dmod_rows + n_part_rows + 1]
            g_rest = rest if g_rest is None else g_rest + rest
        pieces = [g_ada, g_rest[:n_part_rows]]
        if n_small_rows > n_ada_rows + n_part_rows:
            pieces.append(jnp.zeros((n_small_rows - n_ada_rows - n_part_rows, LANES), F32))
        g = jnp.concatenate(pieces, axis=0)
        return [g, *_adamw_math(w_v, g, m_v, v_v), jnp.broadcast_to(g_rest[n_part_rows:], (8, LANES))]

    shp = ((n_small_rows, LANES), F32)
    *small_out, loss_all = _small(small_fn, [small_all, sw, sm, sv], [shp] * 4 + [((8, LANES), F32)], "small_update")
    loss = loss_all[0, 0]
    small_g, small_d, small_nm, small_nv = [_unpack(o, small_shapes) for o in small_out]

    def fix_small(lst):
        b_ada_o, g_mix_o, b_f_o, b_gate_o, g_ffn_o, g_final_o = lst
        return [b_ada_o, g_mix_o, b_f_o[:, :nh], b_gate_o, g_ffn_o, g_final_o.reshape(d)]

    small_g, small_d, small_nm, small_nv = [fix_small(l) for l in (small_g, small_d, small_nm, small_nv)]

    dmod_all = small_all[:, :n_dmod_rows].reshape(nb_all, n_mod * d)
    dmod_cols = lax.dynamic_slice(dmod_all, (0, me * ada_cols), (nb_all, ada_cols))

    def ada_dw_fn(c_v, dm_v):
        return [lax.dot_general(_silu(c_v), dm_v, (((0,), (0,)), ((), ())), precision=lax.Precision.HIGHEST,
                                preferred_element_type=F32)]

    (dw_ada,) = _small(ada_dw_fn, [c_all, dmod_cols], [((d, ada_cols), F32)], "ada_dw")
    ada_out = _adamw(dw_ada[None], w_ada[0], m_w_ada[0], v_w_ada[0], "adamw_ada")
    ada_g, ada_d, ada_nm, ada_nv = [o[None] for o in ada_out]

    got = got_in + list(got)
    names = ["w_in", "w_sb", "w_fox", "w_out", "w_gate", "w_up", "w_down"]
    big_out = [_adamw(g, w, m, v, "adamw_" + n, tr=256) for g, w, m, v, n in zip(got, big, big_m, big_v, names)]
    big_g, big_d, big_nm, big_nv = [[o[i][None] for o in big_out] for i in range(4)]

    def ordered(ada, small, bigs):
        b_ada_o, g_mix_o, b_f_o, b_gate_o, g_ffn_o, g_final_o = small
        w_in_o, w_sb_o, w_fx_o, w_out_o, w_gate_o, w_up_o, w_dn_o = bigs
        return [ada, b_ada_o, g_mix_o, w_in_o, b_f_o, b_gate_o, w_sb_o, w_fx_o, w_out_o, g_ffn_o, w_gate_o, w_up_o,
                w_dn_o, g_final_o]

    return (loss, grad_x, *ordered(ada_g, small_g, big_g), *ordered(ada_d, small_d, big_d),
            *ordered(ada_nm, small_nm, big_nm), *ordered(ada_nv, small_nv, big_nv))
```

```python
import jax
import jax.numpy as jnp
from jax import lax
from jax.experimental import pallas as pl
from jax.experimental.pallas import tpu as pltpu

F32 = jnp.float32
BF16 = jnp.bfloat16
HEAD_DIM = 64
LANES = 128
N_DEV = 8
RMS_EPS = 1e-6
ADAM_LR, ADAM_B1, ADAM_B2, ADAM_EPS, ADAM_WD, ADAM_STEP = 0.001, 0.9, 0.999, 1e-08, 0.01, 10
NEG = -1e30
MESH = pl.DeviceIdType.MESH


def _pick(n, cands):
    for c in cands:
        if n % c == 0:
            return c
    raise ValueError(f"no tile for {n} in {cands}")


def _arb(n):
    return pltpu.CompilerParams(dimension_semantics=("arbitrary",) * n)


def _dot(a, b):
    return jnp.dot(a, b, preferred_element_type=F32)


def _dot_nt(a, b):
    return lax.dot_general(a, b, (((1,), (1,)), ((), ())), preferred_element_type=F32)


def _dot_tn(a, b):
    return lax.dot_general(a, b, (((0,), (0,)), ((), ())), preferred_element_type=F32)


def _split2(v):
    hi = v.astype(BF16)
    return hi, (v - hi.astype(F32)).astype(BF16)


def _dot_split2(v, m):
    hi, lo = _split2(v)
    return _dot(hi, m) + _dot(lo, m)


def _tri_dot3(m, v):
    h1 = v.astype(BF16)
    r1 = v - h1.astype(F32)
    h2 = r1.astype(BF16)
    h3 = (r1 - h2.astype(F32)).astype(BF16)
    return _dot(m, h1) + _dot(m, h2) + _dot(m, h3)


def _sigmoid(v):
    return 1.0 / (1.0 + jnp.exp(-v))


def _silu(v):
    return v * _sigmoid(v)


VMEM_BLOCK_BUDGET = 44 << 20


def _col_tiles(n):
    return [n // q for q in range(1, n // LANES + 1) if n % q == 0 and (n // q) % LANES == 0]


def _size(dt):
    return jnp.dtype(dt).itemsize


def _mm(pairs, out_dtypes, name, tm=512, extras=(), epilogue=None):
    m, n = pairs[0][0].shape[0], pairs[0][1].shape[1]
    tm = _pick(m, (tm, 256, 128, 64, 32, 16, 8))
    lhs = []
    for a, _ in pairs:
        if not any(a is x for x in lhs):
            lhs.append(a)
    odts = out_dtypes if epilogue is not None else [out_dtypes]
    n_acc = len(pairs) if epilogue is not None else 1
    per_col = (sum(b.shape[0] * _size(b.dtype) for _, b in pairs) * 2
               + tm * 2 * (sum(_size(d) for d in odts) + sum(_size(e.dtype) for e in extras)) + tm * 4 * n_acc)
    fixed = 2 * sum(tm * a.shape[1] * _size(a.dtype) for a in lhs)
    tn = next((c for c in _col_tiles(n) if fixed + per_col * c <= VMEM_BLOCK_BUDGET), LANES)
    n_l, n_p, n_e = len(lhs), len(pairs), len(extras)

    def body(*refs):
        l_refs, b_refs, e_refs, o_refs = refs[:n_l], refs[n_l:n_l + n_p], refs[n_l + n_p:n_l + n_p + n_e], refs[n_l + n_p + n_e:]
        vals = [r[...].astype(BF16) for r in l_refs]
        accs = []
        for (a, _), b_ref in zip(pairs, b_refs):
            av = vals[next(i for i, x in enumerate(lhs) if x is a)]
            accs.append(_dot(av, b_ref[...].astype(BF16)))
        if epilogue is None:
            outs = [sum(accs[1:], accs[0])]
        else:
            outs = epilogue(accs, [r[...] for r in e_refs])
        for o_ref, v, dt in zip(o_refs, outs, odts):
            o_ref[...] = v.astype(dt)

    tile = pl.BlockSpec((tm, tn), lambda j, i: (i, j))
    res = pl.pallas_call(
        body, grid=(n // tn, m // tm),
        in_specs=[pl.BlockSpec((tm, a.shape[1]), lambda j, i: (i, 0)) for a in lhs]
        + [pl.BlockSpec((b.shape[0], tn), lambda j, i: (0, j)) for _, b in pairs] + [tile] * n_e,
        out_specs=[tile] * len(odts), out_shape=[jax.ShapeDtypeStruct((m, n), d) for d in odts],
        name=name, compiler_params=_arb(2),
    )(*lhs, *[b for _, b in pairs], *extras)
    return res if epilogue is not None else res[0]


def _mm_tn(a, b, name):
    t, m = a.shape
    n = b.shape[1]

    def fits(tm, tn, tk):
        return 2 * (tk * tm * _size(a.dtype) + tk * tn * _size(b.dtype) + tm * tn * 4) <= VMEM_BLOCK_BUDGET

    tm, tn, tk = next((tm, tn, tk) for tn in _col_tiles(n) for tm in _col_tiles(m) if tm <= 1536
                      for tk in (1024, 512, 256, 128) if t % tk == 0 and fits(tm, tn, tk))

    def body(a_ref, b_ref, o_ref):
        @pl.when(pl.program_id(2) == 0)
        def _():
            o_ref[...] = jnp.zeros_like(o_ref)

        o_ref[...] += _dot_tn(a_ref[...].astype(BF16), b_ref[...].astype(BF16))

    return pl.pallas_call(
        body, grid=(m // tm, n // tn, t // tk),
        in_specs=[pl.BlockSpec((tk, tm), lambda i, j, k: (k, i)), pl.BlockSpec((tk, tn), lambda i, j, k: (k, j))],
        out_specs=pl.BlockSpec((tm, tn), lambda i, j, k: (i, j)),
        out_shape=jax.ShapeDtypeStruct((m, n), F32), name=name, compiler_params=_arb(3),
    )(a, b)


def _rowmap(fn, ins, outs, name, ts=512, products=(), swapped=()):
    bsz, seq = next(a.shape[:2] for a in ins if a.ndim == 3 and a.shape[1] != 1)
    ts = _pick(seq, (ts, 256, 128, 64, 32, 16, 8))
    n_given = len(ins)
    ins = list(ins) + [t for pair in products for t in pair]
    n_in = len(ins)

    def in_spec(a):
        if a.ndim == 2:
            return pl.BlockSpec(a.shape, lambda b, s: (0, 0))
        if a.shape[1] == 1:
            return pl.BlockSpec((1, 1, a.shape[2]), lambda b, s: (b, 0, 0))
        return pl.BlockSpec((1, ts, a.shape[2]), lambda b, s: (b, s, 0))

    def out_spec(kind, w):
        if kind == "row":
            return pl.BlockSpec((1, ts, w), lambda b, s: (b, s, 0))
        if kind == "batch":
            return pl.BlockSpec((1, 1, w), lambda b, s: (b, 0, 0))
        return pl.BlockSpec((1, w), lambda b, s: (0, 0))

    def out_shape(kind, w, dt):
        shp = {"row": (bsz, seq, w), "batch": (bsz, 1, w), "global": (1, w)}[kind]
        return jax.ShapeDtypeStruct(shp, dt)

    n_s, n_out = len(swapped), len(outs)

    def body(*refs):
        b, s = pl.program_id(0), pl.program_id(1)
        x_refs, o_refs = refs[n_in:n_in + n_s], refs[n_in + n_s:n_in + n_s + n_out]
        got_refs, sems = refs[n_in + n_s + n_out:n_in + 2 * n_s + n_out], refs[n_in + 2 * n_s + n_out:]
        if n_s:
            @pl.when((b == 0) & (s == 0))
            def _():
                _exchange_start(x_refs, got_refs, sems, False)

        vals = [r[...] if a.ndim == 2 else r[0] for r, a in zip(refs[:n_in], ins)]
        prods = [_dot(vals[t].astype(BF16), vals[t + 1].astype(BF16)) for t in range(n_given, n_in, 2)]
        res = fn(*vals[:n_given], *prods)
        if n_s:
            @pl.when((b == bsz - 1) & (s == seq // ts - 1))
            def _():
                _exchange_start(x_refs, got_refs, sems, False, wait=True)

        for o_ref, (kind, _, dt), v in zip(o_refs, outs, res):
            if kind == "row":
                o_ref[0] = v.astype(dt)
            elif kind == "batch":
                @pl.when(s == 0)
                def _():
                    o_ref[...] = jnp.zeros_like(o_ref)

                o_ref[0] += v
            else:
                @pl.when((s == 0) & (b == 0))
                def _():
                    o_ref[...] = jnp.zeros_like(o_ref)

                o_ref[...] += v

    res = pl.pallas_call(
        body, grid=(bsz, seq // ts), in_specs=[in_spec(a) for a in ins] + [ANY_SPEC] * n_s,
        out_specs=[out_spec(k, w) for k, w, _ in outs] + [ANY_SPEC] * n_s,
        out_shape=[out_shape(*o) for o in outs] + _exchange_shapes(swapped, False),
        scratch_shapes=_exchange_sems(n_s) if n_s else [], name=name, compiler_params=_arb(2),
    )(*ins, *swapped)
    return res if not n_s else (*res[:n_out], list(res[n_out:]))


def _small(fn, ins, out_shapes, name):
    n_in = len(ins)

    def body(*refs):
        res = fn(*[r[...] for r in refs[:n_in]])
        for o_ref, v in zip(refs[n_in:], res):
            o_ref[...] = v

    return pl.pallas_call(body, out_shape=[jax.ShapeDtypeStruct(s, d) for s, d in out_shapes], name=name)(*ins)


def _mesh_pos():
    mx, my, mc = lax.axis_index("x"), lax.axis_index("y"), lax.axis_index("c")
    return mx, my, mc, 4 * mx + 2 * my + mc


def _peer(mx, my, mc, k):
    px = 1 - mx if k & 4 else mx
    py = 1 - my if k & 2 else my
    pc = 1 - mc if k & 1 else mc
    return (px, py, pc), 4 * px + 2 * py + pc


ANY_SPEC = pl.BlockSpec(memory_space=pl.ANY)


def _exchange_shapes(arrs, gather):
    return [jax.ShapeDtypeStruct((N_DEV,) + tuple(x.shape if gather else x.shape[1:]), x.dtype) for x in arrs]


def _exchange_sems(n_arr):
    return [pltpu.SemaphoreType.DMA((n_arr, N_DEV)), pltpu.SemaphoreType.DMA((n_arr, N_DEV)),
            pltpu.SemaphoreType.DMA((n_arr,))]


def _exchange_start(x_refs, out_refs, sems, gather, wait=False):
    send_sems, recv_sems, local_sems = sems
    mx, my, mc, me = _mesh_pos()
    owns, sends, recvs = [], [], []
    for a, (x_ref, out_ref) in enumerate(zip(x_refs, out_refs)):
        owns.append(pltpu.make_async_copy(x_ref if gather else x_ref.at[me], out_ref.at[me], local_sems.at[a]))
        for k in range(1, N_DEV):
            peer, pid = _peer(mx, my, mc, k)
            src = x_ref if gather else x_ref.at[pid]
            sends.append(pltpu.make_async_remote_copy(
                src_ref=src, dst_ref=out_ref.at[me], send_sem=send_sems.at[a, k], recv_sem=recv_sems.at[a, k],
                device_id=peer, device_id_type=MESH))
            if wait:
                recvs.append(pltpu.make_async_remote_copy(
                    src_ref=src, dst_ref=out_ref.at[pid], send_sem=send_sems.at[a, k], recv_sem=recv_sems.at[a, k],
                    device_id=peer, device_id_type=MESH))
    if not wait:
        for cp in owns + sends:
            cp.start()
        return
    for cp in recvs:
        cp.wait_recv()
    for cp in sends:
        cp.wait_send()
    for cp in owns:
        cp.wait()


def _exchange(arrs, gather, name):
    n_arr = len(arrs)

    def body(*refs):
        x_refs, out_refs, sems = refs[:n_arr], refs[n_arr:2 * n_arr], refs[2 * n_arr:]
        _exchange_start(x_refs, out_refs, sems, gather)
        _exchange_start(x_refs, out_refs, sems, gather, wait=True)

    return pl.pallas_call(
        body, out_shape=_exchange_shapes(arrs, gather), in_specs=[ANY_SPEC] * n_arr, out_specs=[ANY_SPEC] * n_arr,
        scratch_shapes=_exchange_sems(n_arr), name=name,
    )(*arrs)


QK_SCALE = HEAD_DIM ** -0.5


def _low_lanes():
    return lax.broadcasted_iota(jnp.int32, (1, LANES), 1) < HEAD_DIM


def _stack_heads(v, scale=None):
    lo = _low_lanes()
    zero = jnp.zeros_like(v)
    s = jnp.concatenate([jnp.where(lo, v, zero), jnp.where(lo, zero, v)], axis=0)
    return s if scale is None else s * scale


def _stack_cols(v):
    return jnp.concatenate([v[:, 0:1], v[:, HEAD_DIM:HEAD_DIM + 1]], axis=0)


def _unstack(v, tq):
    return jnp.where(_low_lanes(), v[:tq], v[tq:])


def _tile_pos(tq, tk, q0):
    rows = lax.broadcasted_iota(jnp.int32, (2 * tq, tk), 0)
    cols = lax.broadcasted_iota(jnp.int32, (2 * tq, tk), 1)
    return q0 + jnp.where(rows >= tq, rows - tq, rows), cols, rows < tq


def _tri(tk, cmp):
    r = lax.broadcasted_iota(jnp.int32, (tk, tk), 0)
    c = lax.broadcasted_iota(jnp.int32, (tk, tk), 1)
    return jnp.where(cmp(r, c), 1.0, 0.0).astype(BF16)


def _softplus(z):
    return jnp.maximum(z, 0.0) + jnp.log(1.0 + jnp.exp(-jnp.abs(z)))


PREFIX_BLOCK = 256


def _running(v, tri, later):
    blk = tri.shape[0]
    nb = v.shape[1] // blk
    parts = [v[:, b * blk:(b + 1) * blk] for b in range(nb)]
    outs, run = [None] * nb, None
    for b in (reversed(range(nb)) if later else range(nb)):
        inside = _dot_split2(parts[b], tri)
        outs[b] = inside if run is None else inside + run
        total = jnp.sum(parts[b], axis=1, keepdims=True)
        run = total if run is None else run + total
    return (outs[0] if nb == 1 else jnp.concatenate(outs, axis=1)), run


def _pair_specs(tq, seq, np_, off):
    return [pl.BlockSpec((1, tq, LANES), lambda b, p, i: (b, i, off + p)),
            pl.BlockSpec((1, seq, LANES), lambda b, p, i: (b, 0, off + np_ + p)),
            pl.BlockSpec((1, seq, LANES), lambda b, p, i: (b, 0, off + 2 * np_ + p))]


def _key_tiles(tile, carry, q0, tq, tk, upward):
    nfull = q0 // tk
    edge = range(tq // tk)
    if upward:
        carry = lax.fori_loop(0, nfull, lambda j, cr: tile(pl.multiple_of(j * tk, tk), cr, False), carry)
        for jm in edge:
            carry = tile(pl.multiple_of(q0 + jm * tk, tk), carry, True)
        return carry
    for jm in reversed(edge):
        carry = tile(pl.multiple_of(q0 + jm * tk, tk), carry, True)
    return lax.fori_loop(0, nfull, lambda jj, cr: tile(pl.multiple_of((nfull - 1 - jj) * tk, tk), cr, False), carry)


def _grid_ends(bsz, np_, nq):
    b, p, i = pl.program_id(0), pl.program_id(1), pl.program_id(2)
    return (b == 0) & (p == 0) & (i == 0), (b == bsz - 1) & (p == np_ - 1) & (i == nq - 1)


def _sb_fwd(qkv, np_, tq, tk, name, gathered=()):
    bsz, seq, _ = qkv.shape
    n_g = len(gathered)

    def body(*refs):
        q_ref, k_ref, v_ref = refs[:3]
        x_refs, (y_ref, tot_ref) = refs[3:3 + n_g], refs[3 + n_g:5 + n_g]
        out_refs, sems = refs[5 + n_g:5 + 2 * n_g], refs[5 + 2 * n_g:]
        first, last = _grid_ends(bsz, np_, seq // tq)
        if n_g:
            @pl.when(first)
            def _():
                _exchange_start(x_refs, out_refs, sems, True)

        q0 = pl.program_id(2) * tq
        tpos, cols, _ = _tile_pos(tq, tk, q0)
        msuf = _tri(min(tk, PREFIX_BLOCK), lambda a, b: a > b)
        qs = _stack_heads(q_ref[0], QK_SCALE)

        def tile(k0, carry, masked):
            tot, acc = carry
            z = _dot_nt(qs, k_ref[0, pl.ds(k0, tk), :])
            sp = _softplus(z)
            if masked:
                seen = (k0 + cols) < tpos
                sp = jnp.where(seen, sp, 0.0)
            sp_after, sp_tot = _running(sp, msuf, later=True)
            logw = z - sp - sp_after - tot
            if masked:
                logw = jnp.where(seen, logw, NEG)
            return tot + sp_tot, acc + _dot(jnp.exp(logw).astype(BF16), v_ref[0, pl.ds(k0, tk), :])

        init = (jnp.zeros((2 * tq, 1), F32), jnp.zeros((2 * tq, LANES), F32))
        tot, acc = _key_tiles(tile, init, q0, tq, tk, upward=False)
        y_ref[0] = _unstack(acc, tq)
        tot_ref[0, 0] = _unstack(tot, tq)
        if n_g:
            @pl.when(last)
            def _():
                _exchange_start(x_refs, out_refs, sems, True, wait=True)

    y, tot, *got = pl.pallas_call(
        body, grid=(bsz, np_, seq // tq), in_specs=_pair_specs(tq, seq, np_, 0) + [ANY_SPEC] * n_g,
        out_specs=[pl.BlockSpec((1, tq, LANES), lambda b, p, i: (b, i, p)),
                   pl.BlockSpec((1, 1, tq, LANES), lambda b, p, i: (b, p, i, 0))] + [ANY_SPEC] * n_g,
        out_shape=[jax.ShapeDtypeStruct((bsz, seq, np_ * LANES), F32),
                   jax.ShapeDtypeStruct((bsz, np_, seq, LANES), F32)] + _exchange_shapes(gathered, True),
        scratch_shapes=_exchange_sems(n_g) if n_g else [],
        name=name, compiler_params=_arb(3),
    )(qkv, qkv, qkv, *gathered)
    return y, tot, got


def _sb_bwd(qkv, dy, tot, np_, tq, tk, name, swapped=()):
    bsz, seq, _ = qkv.shape
    nq = seq // tq
    n_s = len(swapped)

    def body(*refs):
        q_ref, k_ref, v_ref, dy_ref, tot_ref = refs[:5]
        x_refs, (dq_ref, dk_ref, dv_ref) = refs[5:5 + n_s], refs[5 + n_s:8 + n_s]
        out_refs = refs[8 + n_s:8 + 2 * n_s]
        dk_acc, dv_acc = refs[8 + 2 * n_s:10 + 2 * n_s]
        sems = refs[10 + 2 * n_s:]
        first, last = _grid_ends(bsz, np_, nq)
        if n_s:
            @pl.when(first)
            def _():
                _exchange_start(x_refs, out_refs, sems, False)

        i = pl.program_id(2)
        q0 = i * tq

        @pl.when(i == 0)
        def _():
            dk_acc[...] = jnp.zeros_like(dk_acc)
            dv_acc[...] = jnp.zeros_like(dv_acc)

        tpos, cols, _ = _tile_pos(tq, tk, q0)
        mincl = _tri(min(tk, PREFIX_BLOCK), lambda a, b: a <= b)
        mexcl = _tri(min(tk, PREFIX_BLOCK), lambda a, b: a < b)
        qs = _stack_heads(q_ref[0], QK_SCALE)
        dys = _stack_heads(dy_ref[0].astype(BF16))
        tots = _stack_cols(tot_ref[0, 0])

        def tile(k0, carry, masked):
            c_sp, c_g, dq = carry
            kb = k_ref[0, pl.ds(k0, tk), :]
            z = _dot_nt(qs, kb)
            sp = _softplus(z)
            if masked:
                seen = (k0 + cols) < tpos
                sp = jnp.where(seen, sp, 0.0)
            sp_upto, sp_tot = _running(sp, mincl, later=False)
            logw = z - sp - (tots - c_sp - sp_upto)
            if masked:
                logw = jnp.where(seen, logw, NEG)
            w = jnp.exp(logw)
            g = w * _dot_nt(dys, v_ref[0, pl.ds(k0, tk), :])
            g_before, g_tot = _running(g, mexcl, later=False)
            beta = jnp.exp(jnp.minimum(z - sp, 0.0))
            dz = g - beta * (g + c_g + g_before)
            if masked:
                dz = jnp.where(seen, dz, 0.0)
            dzb = dz.astype(BF16)
            dk_acc[pl.ds(k0, tk), :] += _dot_tn(dzb, qs)
            dv_acc[pl.ds(k0, tk), :] += _dot_tn(w.astype(BF16), dys)
            return c_sp + sp_tot, c_g + g_tot, dq + _dot(dzb, kb)

        zero = jnp.zeros((2 * tq, 1), F32)
        _, _, dq = _key_tiles(tile, (zero, zero, jnp.zeros((2 * tq, LANES), F32)), q0, tq, tk, upward=True)
        dq_ref[0] = (_unstack(dq, tq) * QK_SCALE).astype(BF16)

        @pl.when(i == nq - 1)
        def _():
            dk_ref[0] = dk_acc[...].astype(BF16)
            dv_ref[0] = dv_acc[...].astype(BF16)

        if n_s:
            @pl.when(last)
            def _():
                _exchange_start(x_refs, out_refs, sems, False, wait=True)

    tile_spec = pl.BlockSpec((1, tq, LANES), lambda b, p, i: (b, i, p))
    whole = pl.BlockSpec((1, seq, LANES), lambda b, p, i: (b, 0, p))
    out = jax.ShapeDtypeStruct((bsz, seq, np_ * LANES), BF16)
    dq, dk, dv, *got = pl.pallas_call(
        body, grid=(bsz, np_, nq),
        in_specs=_pair_specs(tq, seq, np_, 0) + [tile_spec, pl.BlockSpec((1, 1, tq, LANES), lambda b, p, i: (b, p, i, 0))]
        + [ANY_SPEC] * n_s,
        out_specs=[tile_spec, whole, whole] + [ANY_SPEC] * n_s,
        out_shape=[out, out, out] + _exchange_shapes(swapped, False),
        scratch_shapes=[pltpu.VMEM((seq, LANES), F32), pltpu.VMEM((seq, LANES), F32)] + (_exchange_sems(n_s) if n_s else []),
        name=name, compiler_params=_arb(3),
    )(qkv, qkv, qkv, dy, tot, *swapped)
    return dq, dk, dv, got


def _fox_fwd(qkv, cumk, np_, tq, tk, name):
    bsz, seq, _ = qkv.shape

    def body(q_ref, k_ref, v_ref, ck_ref, y_ref, lse_ref):
        q0 = pl.program_id(2) * tq
        tpos, cols, top = _tile_pos(tq, tk, q0)
        qs = _stack_heads(q_ref[0], QK_SCALE)

        def tile(k0, carry, masked):
            m, l, acc = carry
            ck = jnp.where(top, ck_ref[0, 0, 0:1, pl.ds(k0, tk)], ck_ref[0, 0, 1:2, pl.ds(k0, tk)])
            s = _dot_nt(qs, k_ref[0, pl.ds(k0, tk), :]) - ck
            if masked:
                s = jnp.where((k0 + cols) <= tpos, s, NEG)
            m_new = jnp.maximum(m, jnp.max(s, axis=1, keepdims=True))
            p = jnp.exp(s - m_new)
            alpha = jnp.exp(m - m_new)
            return (m_new, alpha * l + jnp.sum(p, axis=1, keepdims=True),
                    alpha * acc + _dot(p.astype(BF16), v_ref[0, pl.ds(k0, tk), :]))

        init = (jnp.full((2 * tq, 1), NEG, F32), jnp.zeros((2 * tq, 1), F32), jnp.zeros((2 * tq, LANES), F32))
        m, l, acc = _key_tiles(tile, init, q0, tq, tk, upward=True)
        y_ref[0] = _unstack(acc / l, tq)
        lse_ref[0, 0] = _unstack(m + jnp.log(l), tq)

    row4 = pl.BlockSpec((1, 1, tq, LANES), lambda b, p, i: (b, p, i, 0))
    return pl.pallas_call(
        body, grid=(bsz, np_, seq // tq),
        in_specs=_pair_specs(tq, seq, np_, 3 * np_) + [pl.BlockSpec((1, 1, 8, seq), lambda b, p, i: (b, p, 0, 0))],
        out_specs=[pl.BlockSpec((1, tq, LANES), lambda b, p, i: (b, i, p)), row4],
        out_shape=[jax.ShapeDtypeStruct((bsz, seq, np_ * LANES), F32),
                   jax.ShapeDtypeStruct((bsz, np_, seq, LANES), F32)],
        name=name, compiler_params=_arb(3),
    )(qkv, qkv, qkv, cumk)


def _fox_bwd(qkv, dy, y, lse, cumk, np_, tq, tk, name):
    bsz, seq, _ = qkv.shape
    nq = seq // tq

    def body(q_ref, k_ref, v_ref, dy_ref, y_ref, lse_ref, ck_ref,
             dq_ref, dk_ref, dv_ref, dck_ref, dcq_ref, dk_acc, dv_acc, dck_acc):
        i = pl.program_id(2)
        q0 = i * tq

        @pl.when(i == 0)
        def _():
            dk_acc[...] = jnp.zeros_like(dk_acc)
            dv_acc[...] = jnp.zeros_like(dv_acc)
            dck_acc[...] = jnp.zeros_like(dck_acc)

        tpos, cols, top = _tile_pos(tq, tk, q0)
        qs = _stack_heads(q_ref[0], QK_SCALE)
        dyf = dy_ref[0]
        dys = _stack_heads(dyf.astype(BF16))
        dyy = dyf * y_ref[0]
        lo = _low_lanes()
        delta = jnp.concatenate([jnp.sum(jnp.where(lo, dyy, 0.0), axis=1, keepdims=True),
                                 jnp.sum(jnp.where(lo, 0.0, dyy), axis=1, keepdims=True)], axis=0)
        lse_s = _stack_cols(lse_ref[0, 0])

        def tile(k0, carry, masked):
            dq, row = carry
            kb = k_ref[0, pl.ds(k0, tk), :]
            ck = jnp.where(top, ck_ref[0, 0, 0:1, pl.ds(k0, tk)], ck_ref[0, 0, 1:2, pl.ds(k0, tk)])
            s = _dot_nt(qs, kb) - ck
            if masked:
                s = jnp.where((k0 + cols) <= tpos, s, NEG)
            p = jnp.exp(s - lse_s)
            ds = p * (_dot_nt(dys, v_ref[0, pl.ds(k0, tk), :]) - delta)
            dsb = ds.astype(BF16)
            dk_acc[pl.ds(k0, tk), :] += _dot_tn(dsb, qs)
            dv_acc[pl.ds(k0, tk), :] += _dot_tn(p.astype(BF16), dys)
            dck_acc[0:1, pl.ds(k0, tk)] += -jnp.sum(ds[:tq], axis=0, keepdims=True)
            dck_acc[1:2, pl.ds(k0, tk)] += -jnp.sum(ds[tq:], axis=0, keepdims=True)
            return dq + _dot(dsb, kb), row + jnp.sum(ds, axis=1, keepdims=True)

        init = (jnp.zeros((2 * tq, LANES), F32), jnp.zeros((2 * tq, 1), F32))
        dq, row = _key_tiles(tile, init, q0, tq, tk, upward=True)
        dq_ref[0] = (_unstack(dq, tq) * QK_SCALE).astype(BF16)
        dcq_ref[0, 0] = _unstack(row, tq)

        @pl.when(i == nq - 1)
        def _():
            dk_ref[0] = dk_acc[...].astype(BF16)
            dv_ref[0] = dv_acc[...].astype(BF16)
            dck_ref[0, 0] = dck_acc[...]

    tile_spec = pl.BlockSpec((1, tq, LANES), lambda b, p, i: (b, i, p))
    whole = pl.BlockSpec((1, seq, LANES), lambda b, p, i: (b, 0, p))
    row4 = pl.BlockSpec((1, 1, tq, LANES), lambda b, p, i: (b, p, i, 0))
    key4 = pl.BlockSpec((1, 1, 8, seq), lambda b, p, i: (b, p, 0, 0))
    out = jax.ShapeDtypeStruct((bsz, seq, np_ * LANES), BF16)
    return pl.pallas_call(
        body, grid=(bsz, np_, nq),
        in_specs=_pair_specs(tq, seq, np_, 3 * np_) + [tile_spec, tile_spec, row4, key4],
        out_specs=[tile_spec, whole, whole, key4, row4],
        out_shape=[out, out, out, jax.ShapeDtypeStruct((bsz, np_, 8, seq), F32),
                   jax.ShapeDtypeStruct((bsz, np_, seq, LANES), F32)],
        scratch_shapes=[pltpu.VMEM((seq, LANES), F32), pltpu.VMEM((seq, LANES), F32), pltpu.VMEM((8, seq), F32)],
        name=name, compiler_params=_arb(3),
    )(qkv, qkv, qkv, dy, y, lse, cumk)


def _cum_fwd(fl, bf, name, tb=256):
    bsz, seq, _ = fl.shape
    tb = _pick(seq, (tb, 128))

    def body(fl_ref, bf_ref, o_ref):
        tri = _tri(tb, lambda a, b: b <= a)

        def step(j, carry):
            r0 = pl.multiple_of(j * tb, tb)
            blk = _tri_dot3(tri, -_softplus(-(fl_ref[0, pl.ds(r0, tb), :] + bf_ref[...]))) + carry
            o_ref[0, pl.ds(r0, tb), :] = blk
            return blk[tb - 1:tb, :]

        lax.fori_loop(0, seq // tb, step, jnp.zeros((1, LANES), F32))

    return pl.pallas_call(
        body, grid=(bsz,),
        in_specs=[pl.BlockSpec((1, seq, LANES), lambda b: (b, 0, 0)), pl.BlockSpec((1, LANES), lambda b: (0, 0))],
        out_specs=pl.BlockSpec((1, seq, LANES), lambda b: (b, 0, 0)),
        out_shape=jax.ShapeDtypeStruct(fl.shape, F32), name=name, compiler_params=_arb(1),
    )(fl, bf)


def _cum_bwd(dck, dcq, fl, bf, np_, name, tb=256):
    bsz, seq, _ = fl.shape
    tb = _pick(seq, (tb, 128))
    nb = seq // tb

    def body(dck_ref, dcq_ref, fl_ref, bf_ref, o_ref, db_ref):
        @pl.when(pl.program_id(0) == 0)
        def _():
            db_ref[...] = jnp.zeros_like(db_ref)

        tri = _tri(tb, lambda a, b: b >= a)
        lane = lax.broadcasted_iota(jnp.int32, (1, LANES), 1)

        def step(jj, carry):
            tail, tot = carry
            r0 = pl.multiple_of((nb - 1 - jj) * tb, tb)
            dc = dck_ref[0, pl.ds(r0, tb), :]
            for p in range(np_):
                pair = dcq_ref[0, p, pl.ds(r0, tb), :]
                dc = dc + jnp.where(lane == 2 * p, pair, 0.0) + jnp.where(lane == 2 * p + 1, pltpu.roll(pair, HEAD_DIM, 1), 0.0)
            dlf = _tri_dot3(tri, dc) + tail
            dfl = dlf * _sigmoid(-(fl_ref[0, pl.ds(r0, tb), :] + bf_ref[...]))
            o_ref[0, pl.ds(r0, tb), :] = dfl
            return dlf[0:1, :], tot + jnp.sum(dfl, axis=0, keepdims=True)

        zero = jnp.zeros((1, LANES), F32)
        _, tot = lax.fori_loop(0, nb, step, (zero, zero))
        db_ref[...] += tot

    whole = pl.BlockSpec((1, seq, LANES), lambda b: (b, 0, 0))
    vec = pl.BlockSpec((1, LANES), lambda b: (0, 0))
    return pl.pallas_call(
        body, grid=(bsz,), in_specs=[whole, pl.BlockSpec((1, np_, seq, LANES), lambda b: (b, 0, 0, 0)), whole, vec],
        out_specs=[whole, vec],
        out_shape=[jax.ShapeDtypeStruct(fl.shape, F32), jax.ShapeDtypeStruct((1, LANES), F32)],
        name=name, compiler_params=_arb(1),
    )(dck, dcq, fl, bf)


def _adamw_math(w, g, m, v):
    m = ADAM_B1 * m + (1.0 - ADAM_B1) * g
    v = ADAM_B2 * v + (1.0 - ADAM_B2) * (g * g)
    m_hat = m / (1.0 - ADAM_B1 ** ADAM_STEP)
    v_hat = v / (1.0 - ADAM_B2 ** ADAM_STEP)
    return -ADAM_LR * (m_hat / (jnp.sqrt(v_hat) + ADAM_EPS) + ADAM_WD * w), m, v


def _adamw(gparts, w, m, v, name, tr=512):
    nslots, rows, cols = gparts.shape
    tr = _pick(rows, (tr, 256, 128, 64, 32, 16, 8))

    def body(g_ref, w_ref, m_ref, v_ref, go_ref, d_ref, mo_ref, vo_ref):
        g = g_ref[0].astype(F32)
        for k in range(1, nslots):
            g = g + g_ref[k].astype(F32)
        go_ref[...] = g
        d_ref[...], mo_ref[...], vo_ref[...] = _adamw_math(w_ref[...], g, m_ref[...], v_ref[...])

    blk = pl.BlockSpec((tr, cols), lambda i: (i, 0))
    shp = jax.ShapeDtypeStruct((rows, cols), F32)
    return pl.pallas_call(
        body, grid=(rows // tr,), in_specs=[pl.BlockSpec((nslots, tr, cols), lambda i: (0, i, 0)), blk, blk, blk],
        out_specs=[blk] * 4, out_shape=[shp] * 4, name=name, compiler_params=_arb(1),
    )(gparts, w, m, v)


def _rows128(a):
    return a.reshape(-1, LANES)


def _pad_rows(a, mult):
    extra = (-a.shape[0]) % mult
    return a if extra == 0 else jnp.concatenate([a, jnp.zeros((extra, a.shape[1]), a.dtype)], axis=0)


def _pack(arrs, mult):
    return _pad_rows(jnp.concatenate([_rows128(a) for a in arrs], axis=0), mult)


def _unpack(flat, shapes):
    out, off = [], 0
    for shp in shapes:
        n = 1
        for s in shp:
            n *= s
        out.append(flat[off:off + n // LANES].reshape(shp))
        off += n // LANES
    return out


def _col_blocks(full):
    k, n = full.shape
    return full.reshape(k, N_DEV, n // N_DEV).transpose(1, 0, 2)


def _from_col_blocks(blocks):
    _, k, n = blocks.shape
    return blocks.transpose(1, 0, 2).reshape(k, N_DEV * n)


def kernel(x, c, w_ada, b_ada, g_mix, w_in, b_forget, b_gate, w_branch_sb, w_branch_fox, w_out, g_ffn, w_ffn_gate, w_ffn_up, w_ffn_down, g_final, loss_target, m_w_ada, m_b_ada, m_g_mix, m_w_in, m_b_forget, m_b_gate, m_w_branch_sb, m_w_branch_fox, m_w_out, m_g_ffn, m_w_ffn_gate, m_w_ffn_up, m_w_ffn_down, m_g_final, v_w_ada, v_b_ada, v_g_mix, v_w_in, v_b_forget, v_b_gate, v_w_branch_sb, v_w_branch_fox, v_w_out, v_g_ffn, v_w_ffn_gate, v_w_ffn_up, v_w_ffn_down, v_g_final):
    bsz, seq, d = x.shape
    tok = bsz * seq
    nh = b_forget.shape[-1]
    d_in = w_in.shape[-1] * N_DEV
    d_att = (d_in - nh - 2 * d) // 6
    assert d_att == nh * HEAD_DIM and nh % 2 == 0
    np_ = nh // 2
    d_ff = w_ffn_gate.shape[-1] * N_DEV
    n_mod = w_ada.shape[-1] * N_DEV // d
    me = 4 * lax.axis_index("x") + 2 * lax.axis_index("y") + lax.axis_index("c")
    tq_att = _pick(seq, (512, 256, 128))
    tk_att = tq_att
    tq_fox = tk_fox = tq_att

    big = [w_in[0], w_branch_sb[0], w_branch_fox[0], w_out[0], w_ffn_gate[0], w_ffn_up[0], w_ffn_down[0]]
    big_m = [m_w_in[0], m_w_branch_sb[0], m_w_branch_fox[0], m_w_out[0], m_w_ffn_gate[0], m_w_ffn_up[0], m_w_ffn_down[0]]
    big_v = [v_w_in[0], v_w_branch_sb[0], v_w_branch_fox[0], v_w_out[0], v_w_ffn_gate[0], v_w_ffn_up[0], v_w_ffn_down[0]]

    w_in_g, c_g = _exchange([big[0].astype(BF16), _rows128(c)], True, "gather_w_in_c")
    w_in_f = _from_col_blocks(w_in_g)
    w_qkv = w_in_f[:, :6 * d_att]
    w_f = jnp.concatenate([w_in_f[:, 6 * d_att:6 * d_att + nh], jnp.zeros((d, LANES - nh), BF16)], axis=1)
    w_gl = w_in_f[:, 6 * d_att + nh:]

    c_all = c_g.reshape(N_DEV * bsz, d)
    nb_all = N_DEV * bsz
    ada_cols = w_ada.shape[-1]
    b_ada_loc = lax.dynamic_slice(b_ada, (0, me * ada_cols), (1, ada_cols))

    def mod_fn(c_v, w_v, b_v):
        return [jnp.dot(_silu(c_v), w_v, precision=lax.Precision.HIGHEST, preferred_element_type=F32) + b_v]

    (mod_part,) = _small(mod_fn, [c_all, w_ada[0], b_ada_loc], [((nb_all, ada_cols), F32)], "ada_mod")
    mod_all = _from_col_blocks(_exchange([_rows128(mod_part)], True, "gather_mod")[0].reshape(N_DEV, nb_all, ada_cols))
    mod = lax.dynamic_slice(mod_all, (me * bsz, 0), (bsz, n_mod * d))
    shift1, scale1, gate1, shift2, scale2, gate2 = [mod[:, i * d:(i + 1) * d].reshape(bsz, 1, d) for i in range(6)]

    def norm_mod_fn(x_v, sc, sh, g):
        n = x_v * lax.rsqrt(jnp.mean(x_v * x_v, axis=-1, keepdims=True) + RMS_EPS) * g
        return [n * (1.0 + sc) + sh]

    (h,) = _rowmap(norm_mod_fn, [x, scale1, shift1, g_mix], [("row", d, BF16)], "norm1")
    h2d = h.reshape(tok, d)
    qkv = _mm([(h2d, w_qkv)], BF16, "proj_qkv").reshape(bsz, seq, 6 * d_att)
    gl = _mm([(h2d, w_gl)], F32, "proj_gates").reshape(bsz, seq, 2 * d)
    fl = _mm([(h2d, w_f)], F32, "proj_forget").reshape(bsz, seq, LANES)

    bf_pad = jnp.concatenate([b_forget, jnp.zeros((1, LANES - nh), F32)], axis=1)
    cum = _cum_fwd(fl, bf_pad, "cum_fwd")
    cumk = jnp.concatenate([cum[:, :, :nh].transpose(0, 2, 1).reshape(bsz, np_, 2, seq),
                            jnp.zeros((bsz, np_, 6, seq), F32)], axis=2)

    y_sb, tot_sb, gath = _sb_fwd(qkv, np_, tq_att, tk_att, "sb_fwd", gathered=[a.astype(BF16) for a in big[1:]])
    w_sb_f = _from_col_blocks(gath[0])
    w_fx_f = _from_col_blocks(gath[1])
    w_out_f = gath[2].reshape(d, d)
    w_g_f, w_u_f = _from_col_blocks(gath[3]), _from_col_blocks(gath[4])
    w_dn_f = gath[5].reshape(d_ff, d)
    y_fx, lse_fx = _fox_fwd(qkv, cumk, np_, tq_fox, tk_fox, "fox_fwd")

    def merge_fn(gl_v, bg, us, uf):
        gates = _sigmoid(gl_v + bg)
        return [gates[:, :d] * us + gates[:, d:] * uf, us, uf]

    merged, u_sb, u_fx = _rowmap(merge_fn, [gl, b_gate], [("row", d, BF16), ("row", d, F32), ("row", d, F32)], "merge",
                                 products=[(y_sb, w_sb_f), (y_fx, w_fx_f)])

    def resid_norm_fn(x_v, g1, sc, sh, g, mo_v):
        x1_v = x_v + g1 * mo_v
        n = x1_v * lax.rsqrt(jnp.mean(x1_v * x1_v, axis=-1, keepdims=True) + RMS_EPS) * g
        return [x1_v, n * (1.0 + sc) + sh, mo_v]

    x1, h2, mo = _rowmap(resid_norm_fn, [x, gate1, scale2, shift2, g_ffn],
                         [("row", d, F32), ("row", d, BF16), ("row", d, F32)], "norm2", products=[(merged, w_out_f)])
    h2_2d = h2.reshape(tok, d)

    def swiglu_fn(accs, _):
        a, u = accs
        return [_silu(a) * u, a, u]

    f, a_s, u_s = _mm([(h2_2d, w_g_f), (h2_2d, w_u_f)], [BF16, BF16, BF16], "ffn_in", epilogue=swiglu_fn)

    def head_fn(x1_v, g2, gf, tgt, ffn_v):
        x2 = x1_v + g2 * ffn_v
        rstd = lax.rsqrt(jnp.mean(x2 * x2, axis=-1, keepdims=True) + RMS_EPS)
        xh = x2 * rstd
        err = xh * gf - tgt
        loss_rows = 0.5 * jnp.mean(err * err, axis=-1, keepdims=True)
        dy = err * (1.0 / d)
        dxh = dy * gf
        dx2 = rstd * (dxh - xh * jnp.mean(dxh * xh, axis=-1, keepdims=True))
        return [dx2, dx2 * g2, jnp.sum(loss_rows, axis=0, keepdims=True) * jnp.ones((1, LANES), F32),
                jnp.sum(dy * xh, axis=0, keepdims=True), jnp.sum(dx2 * ffn_v, axis=0, keepdims=True)]

    dx2, dffn, loss_vec, dg_final, dgate2 = _rowmap(
        head_fn, [x1, gate2, g_final.reshape(1, d), loss_target],
        [("row", d, F32), ("row", d, BF16), ("global", LANES, F32), ("global", d, F32), ("batch", d, F32)], "head",
        products=[(f.reshape(bsz, seq, d_ff), w_dn_f)])

    dffn2d = dffn.reshape(tok, d)
    dw_dn = _mm_tn(f, dffn2d, "ffn_out_dw")

    def swiglu_bwd_fn(accs, saved):
        df_v, a, u = accs[0], saved[0].astype(F32), saved[1].astype(F32)
        sig = _sigmoid(a)
        return [df_v * u * sig * (1.0 + a * (1.0 - sig)), df_v * a * sig]

    da, du = _mm([(dffn2d, w_dn_f.T)], [BF16, BF16], "ffn_out_dx", extras=[a_s, u_s], epilogue=swiglu_bwd_fn)
    dw_gate, dw_up = _mm_tn(h2_2d, da, "ffn_gate_dw"), _mm_tn(h2_2d, du, "ffn_up_dw")
    def norm2_bwd_fn(x1_v, dx2_v, mo_v, sc, g1, g, dh_gate, dh_up):
        dh_v = dh_gate + dh_up
        rstd = lax.rsqrt(jnp.mean(x1_v * x1_v, axis=-1, keepdims=True) + RMS_EPS)
        xh = x1_v * rstd
        dn = dh_v * (1.0 + sc)
        dxh = dn * g
        dx1 = dx2_v + rstd * (dxh - xh * jnp.mean(dxh * xh, axis=-1, keepdims=True))
        return [dx1, dx1 * g1, jnp.sum(dh_v * (xh * g), axis=0, keepdims=True), jnp.sum(dh_v, axis=0, keepdims=True),
                jnp.sum(dn * xh, axis=0, keepdims=True), jnp.sum(dx1 * mo_v, axis=0, keepdims=True)]

    dx1, dmo, dscale2, dshift2, dg_ffn, dgate1 = _rowmap(
        norm2_bwd_fn, [x1, dx2, mo, scale2, gate1, g_ffn],
        [("row", d, F32), ("row", d, BF16), ("batch", d, F32), ("batch", d, F32), ("global", d, F32), ("batch", d, F32)],
        "norm2_bwd", ts=256,
        products=[(da.reshape(bsz, seq, d_ff), w_g_f.T), (du.reshape(bsz, seq, d_ff), w_u_f.T)])

    dmo2d = dmo.reshape(tok, d)
    dw_out = _mm_tn(merged.reshape(tok, d), dmo2d, "out_proj_dw")

    def merge_bwd_fn(gl_v, us, uf, bg, dm):
        gates = _sigmoid(gl_v + bg)
        gs, gf = gates[:, :d], gates[:, d:]
        dgl = jnp.concatenate([dm * us * gs * (1.0 - gs), dm * uf * gf * (1.0 - gf)], axis=1)
        return [dm * gs, dm * gf, dgl, jnp.sum(dgl, axis=0, keepdims=True)]

    du_sb, du_fx, dgl, db_gate = _rowmap(
        merge_bwd_fn, [gl, u_sb, u_fx, b_gate],
        [("row", d, BF16), ("row", d, BF16), ("row", 2 * d, BF16), ("global", 2 * d, F32)], "merge_bwd", ts=256,
        products=[(dmo, w_out_f.T)])
    du_sb2d, du_fx2d = du_sb.reshape(tok, d), du_fx.reshape(tok, d)
    dw_sb = _mm_tn(y_sb.reshape(tok, d_att), du_sb2d, "branch_sb_dw")
    dw_fx = _mm_tn(y_fx.reshape(tok, d_att), du_fx2d, "branch_fox_dw")
    dy_sb = _mm([(du_sb2d, w_sb_f.T)], F32, "branch_sb_dx").reshape(bsz, seq, d_att)
    dy_fx = _mm([(du_fx2d, w_fx_f.T)], F32, "branch_fox_dx").reshape(bsz, seq, d_att)

    blocks = [_col_blocks(dw_sb), _col_blocks(dw_fx), dw_out.reshape(N_DEV, d // N_DEV, d),
              _col_blocks(dw_gate), _col_blocks(dw_up), dw_dn.reshape(N_DEV, d_ff // N_DEV, d)]
    blocks = [b.astype(BF16) for b in blocks]
    dq_sb, dk_sb, dv_sb, got = _sb_bwd(qkv, dy_sb, tot_sb, np_, tq_att, tk_att, "sb_bwd", swapped=blocks)
    dq_fx, dk_fx, dv_fx, dck, dcq = _fox_bwd(qkv, dy_fx, y_fx, lse_fx, cumk, np_, tq_fox, tk_fox, "fox_bwd")
    dck_rows = dck[:, :, :2, :].reshape(bsz, nh, seq).transpose(0, 2, 1)
    dck_rows = jnp.concatenate([dck_rows, jnp.zeros((bsz, seq, LANES - nh), F32)], axis=2)
    dfl, db_f = _cum_bwd(dck_rows, dcq, fl, bf_pad, np_, "cum_bwd")

    dqkv = jnp.concatenate([dq_sb, dk_sb, dv_sb, dq_fx, dk_fx, dv_fx], axis=2).reshape(tok, 6 * d_att)
    dgl2d, dfl2d = dgl.reshape(tok, 2 * d), dfl.reshape(tok, LANES)
    dw_in = jnp.concatenate([_mm_tn(h2d, dqkv, "proj_qkv_dw"), _mm_tn(h2d, dfl2d, "proj_forget_dw")[:, :nh],
                             _mm_tn(h2d, dgl2d, "proj_gates_dw")], axis=1)
    def norm1_bwd_fn(x_v, dx1_v, sc, g, dh_qkv, dh_gl, dh_fl):
        dh_v = dh_qkv + dh_gl + dh_fl
        rstd = lax.rsqrt(jnp.mean(x_v * x_v, axis=-1, keepdims=True) + RMS_EPS)
        xh = x_v * rstd
        dn = dh_v * (1.0 + sc)
        dxh = dn * g
        dx = dx1_v + rstd * (dxh - xh * jnp.mean(dxh * xh, axis=-1, keepdims=True))
        return [dx, jnp.sum(dh_v * (xh * g), axis=0, keepdims=True), jnp.sum(dh_v, axis=0, keepdims=True),
                jnp.sum(dn * xh, axis=0, keepdims=True)]

    grad_x, dscale1, dshift1, dg_mix, got_in = _rowmap(
        norm1_bwd_fn, [x, dx1, scale1, g_mix],
        [("row", d, F32), ("batch", d, F32), ("batch", d, F32), ("global", d, F32)], "norm1_bwd", ts=256,
        products=[(dqkv.reshape(bsz, seq, 6 * d_att), w_qkv.T), (dgl, w_gl.T), (dfl, w_f.T)],
        swapped=[_col_blocks(dw_in).astype(BF16)])

    dmod = jnp.concatenate([dshift1, dscale1, dgate1, dshift2, dscale2, dgate2], axis=2).reshape(bsz, n_mod * d)
    partial = [dg_mix, db_f, db_gate, dg_ffn, dg_final]
    n_dmod_rows = bsz * n_mod * d // LANES
    small_sent = _pack([dmod] + partial + [loss_vec], 8)
    small_all = _exchange([small_sent], True, "gather_small")[0]
    small_w = [b_ada, g_mix, jnp.concatenate([b_forget, jnp.zeros((1, LANES - nh), F32)], axis=1), b_gate, g_ffn,
               g_final.reshape(1, d)]
    small_m = [m_b_ada, m_g_mix, jnp.concatenate([m_b_forget, jnp.zeros((1, LANES - nh), F32)], axis=1), m_b_gate,
               m_g_ffn, m_g_final.reshape(1, d)]
    small_v = [v_b_ada, v_g_mix, jnp.concatenate([v_b_forget, jnp.zeros((1, LANES - nh), F32)], axis=1), v_b_gate,
               v_g_ffn, v_g_final.reshape(1, d)]
    small_shapes = [a.shape for a in small_w]
    n_ada_rows = n_mod * d // LANES
    n_part_rows = sum(a.shape[1] // LANES for a in partial)
    sw, sm, sv = _pack(small_w, 8), _pack(small_m, 8), _pack(small_v, 8)
    n_small_rows = sw.shape[0]

    def small_fn(all_v, w_v, m_v, v_v):
        g_ada = None
        g_rest = None
        for k in range(N_DEV):
            for b in range(bsz):
                part = all_v[k, b * n_ada_rows:(b + 1) * n_ada_rows]
                g_ada = part if g_ada is None else g_ada + part
            rest = all_v[k, n_dmod_rows:n_dmod_rows + n_part_rows + 1]
            g_rest = rest if g_rest is None else g_rest + rest
        pieces = [g_ada, g_rest[:n_part_rows]]
        if n_small_rows > n_ada_rows + n_part_rows:
            pieces.append(jnp.zeros((n_small_rows - n_ada_rows - n_part_rows, LANES), F32))
        g = jnp.concatenate(pieces, axis=0)
        return [g, *_adamw_math(w_v, g, m_v, v_v), jnp.broadcast_to(g_rest[n_part_rows:], (8, LANES))]

    shp = ((n_small_rows, LANES), F32)
    *small_out, loss_all = _small(small_fn, [small_all, sw, sm, sv], [shp] * 4 + [((8, LANES), F32)], "small_update")
    loss = loss_all[0, 0]
    small_g, small_d, small_nm, small_nv = [_unpack(o, small_shapes) for o in small_out]

    def fix_small(lst):
        b_ada_o, g_mix_o, b_f_o, b_gate_o, g_ffn_o, g_final_o = lst
        return [b_ada_o, g_mix_o, b_f_o[:, :nh], b_gate_o, g_ffn_o, g_final_o.reshape(d)]

    small_g, small_d, small_nm, small_nv = [fix_small(l) for l in (small_g, small_d, small_nm, small_nv)]

    dmod_all = small_all[:, :n_dmod_rows].reshape(nb_all, n_mod * d)
    dmod_cols = lax.dynamic_slice(dmod_all, (0, me * ada_cols), (nb_all, ada_cols))

    def ada_dw_fn(c_v, dm_v):
        return [lax.dot_general(_silu(c_v), dm_v, (((0,), (0,)), ((), ())), precision=lax.Precision.HIGHEST,
                                preferred_element_type=F32)]

    (dw_ada,) = _small(ada_dw_fn, [c_all, dmod_cols], [((d, ada_cols), F32)], "ada_dw")
    ada_out = _adamw(dw_ada[None], w_ada[0], m_w_ada[0], v_w_ada[0], "adamw_ada")
    ada_g, ada_d, ada_nm, ada_nv = [o[None] for o in ada_out]

    got = got_in + list(got)
    names = ["w_in", "w_sb", "w_fox", "w_out", "w_gate", "w_up", "w_down"]
    big_out = [_adamw(g, w, m, v, "adamw_" + n, tr=256) for g, w, m, v, n in zip(got, big, big_m, big_v, names)]
    big_g, big_d, big_nm, big_nv = [[o[i][None] for o in big_out] for i in range(4)]

    def ordered(ada, small, bigs):
        b_ada_o, g_mix_o, b_f_o, b_gate_o, g_ffn_o, g_final_o = small
        w_in_o, w_sb_o, w_fx_o, w_out_o, w_gate_o, w_up_o, w_dn_o = bigs
        return [ada, b_ada_o, g_mix_o, w_in_o, b_f_o, b_gate_o, w_sb_o, w_fx_o, w_out_o, g_ffn_o, w_gate_o, w_up_o,
                w_dn_o, g_final_o]

    return (loss, grad_x, *ordered(ada_g, small_g, big_g), *ordered(ada_d, small_d, big_d),
            *ordered(ada_nm, small_nm, big_nm), *ordered(ada_nv, small_nv, big_nv))
```

```python
import jax
import jax.numpy as jnp
from jax import lax
from jax.experimental import pallas as pl
from jax.experimental.pallas import tpu as pltpu

F32 = jnp.float32
BF16 = jnp.bfloat16
HEAD_DIM = 64
LANES = 128
N_DEV = 8
RMS_EPS = 1e-6
ADAM_LR, ADAM_B1, ADAM_B2, ADAM_EPS, ADAM_WD, ADAM_STEP = 0.001, 0.9, 0.999, 1e-08, 0.01, 10
NEG = -1e30
MESH = pl.DeviceIdType.MESH


def _pick(n, cands):
    for c in cands:
        if n % c == 0:
            return c
    raise ValueError(f"no tile for {n} in {cands}")


def _arb(n):
    return pltpu.CompilerParams(dimension_semantics=("arbitrary",) * n)


def _dot(a, b):
    return jnp.dot(a, b, preferred_element_type=F32)


def _dot_nt(a, b):
    return lax.dot_general(a, b, (((1,), (1,)), ((), ())), preferred_element_type=F32)


def _dot_tn(a, b):
    return lax.dot_general(a, b, (((0,), (0,)), ((), ())), preferred_element_type=F32)


def _split2(v):
    hi = v.astype(BF16)
    return hi, (v - hi.astype(F32)).astype(BF16)


def _dot_split2(v, m):
    hi, lo = _split2(v)
    return _dot(hi, m) + _dot(lo, m)


def _tri_dot3(m, v):
    h1 = v.astype(BF16)
    r1 = v - h1.astype(F32)
    h2 = r1.astype(BF16)
    h3 = (r1 - h2.astype(F32)).astype(BF16)
    return _dot(m, h1) + _dot(m, h2) + _dot(m, h3)


def _sigmoid(v):
    return 1.0 / (1.0 + jnp.exp(-v))


def _silu(v):
    return v * _sigmoid(v)


VMEM_BLOCK_BUDGET = 44 << 20


def _col_tiles(n):
    return [n // q for q in range(1, n // LANES + 1) if n % q == 0 and (n // q) % LANES == 0]


def _size(dt):
    return jnp.dtype(dt).itemsize


def _mm(pairs, out_dtypes, name, tm=512, extras=(), epilogue=None):
    m, n = pairs[0][0].shape[0], pairs[0][1].shape[1]
    tm = _pick(m, (tm, 256, 128, 64, 32, 16, 8))
    lhs = []
    for a, _ in pairs:
        if not any(a is x for x in lhs):
            lhs.append(a)
    odts = out_dtypes if epilogue is not None else [out_dtypes]
    n_acc = len(pairs) if epilogue is not None else 1
    per_col = (sum(b.shape[0] * _size(b.dtype) for _, b in pairs) * 2
               + tm * 2 * (sum(_size(d) for d in odts) + sum(_size(e.dtype) for e in extras)) + tm * 4 * n_acc)
    fixed = 2 * sum(tm * a.shape[1] * _size(a.dtype) for a in lhs)
    tn = next((c for c in _col_tiles(n) if fixed + per_col * c <= VMEM_BLOCK_BUDGET), LANES)
    n_l, n_p, n_e = len(lhs), len(pairs), len(extras)

    def body(*refs):
        l_refs, b_refs, e_refs, o_refs = refs[:n_l], refs[n_l:n_l + n_p], refs[n_l + n_p:n_l + n_p + n_e], refs[n_l + n_p + n_e:]
        vals = [r[...].astype(BF16) for r in l_refs]
        accs = []
        for (a, _), b_ref in zip(pairs, b_refs):
            av = vals[next(i for i, x in enumerate(lhs) if x is a)]
            accs.append(_dot(av, b_ref[...].astype(BF16)))
        if epilogue is None:
            outs = [sum(accs[1:], accs[0])]
        else:
            outs = epilogue(accs, [r[...] for r in e_refs])
        for o_ref, v, dt in zip(o_refs, outs, odts):
            o_ref[...] = v.astype(dt)

    tile = pl.BlockSpec((tm, tn), lambda j, i: (i, j))
    res = pl.pallas_call(
        body, grid=(n // tn, m // tm),
        in_specs=[pl.BlockSpec((tm, a.shape[1]), lambda j, i: (i, 0)) for a in lhs]
        + [pl.BlockSpec((b.shape[0], tn), lambda j, i: (0, j)) for _, b in pairs] + [tile] * n_e,
        out_specs=[tile] * len(odts), out_shape=[jax.ShapeDtypeStruct((m, n), d) for d in odts],
        name=name, compiler_params=_arb(2),
    )(*lhs, *[b for _, b in pairs], *extras)
    return res if epilogue is not None else res[0]


def _mm_tn(a, b, name):
    t, m = a.shape
    n = b.shape[1]

    def fits(tm, tn, tk):
        return 2 * (tk * tm * _size(a.dtype) + tk * tn * _size(b.dtype) + tm * tn * 4) <= VMEM_BLOCK_BUDGET

    tm, tn, tk = next((tm, tn, tk) for tn in _col_tiles(n) for tm in _col_tiles(m) if tm <= 1536
                      for tk in (1024, 512, 256, 128) if t % tk == 0 and fits(tm, tn, tk))

    def body(a_ref, b_ref, o_ref):
        @pl.when(pl.program_id(2) == 0)
        def _():
            o_ref[...] = jnp.zeros_like(o_ref)

        o_ref[...] += _dot_tn(a_ref[...].astype(BF16), b_ref[...].astype(BF16))

    return pl.pallas_call(
        body, grid=(m // tm, n // tn, t // tk),
        in_specs=[pl.BlockSpec((tk, tm), lambda i, j, k: (k, i)), pl.BlockSpec((tk, tn), lambda i, j, k: (k, j))],
        out_specs=pl.BlockSpec((tm, tn), lambda i, j, k: (i, j)),
        out_shape=jax.ShapeDtypeStruct((m, n), F32), name=name, compiler_params=_arb(3),
    )(a, b)


def _rowmap(fn, ins, outs, name, ts=512, products=(), swapped=()):
    bsz, seq = next(a.shape[:2] for a in ins if a.ndim == 3 and a.shape[1] != 1)
    ts = _pick(seq, (ts, 256, 128, 64, 32, 16, 8))
    n_given = len(ins)
    ins = list(ins) + [t for pair in products for t in pair]
    n_in = len(ins)

    def in_spec(a):
        if a.ndim == 2:
            return pl.BlockSpec(a.shape, lambda b, s: (0, 0))
        if a.shape[1] == 1:
            return pl.BlockSpec((1, 1, a.shape[2]), lambda b, s: (b, 0, 0))
        return pl.BlockSpec((1, ts, a.shape[2]), lambda b, s: (b, s, 0))

    def out_spec(kind, w):
        if kind == "row":
            return pl.BlockSpec((1, ts, w), lambda b, s: (b, s, 0))
        if kind == "batch":
            return pl.BlockSpec((1, 1, w), lambda b, s: (b, 0, 0))
        return pl.BlockSpec((1, w), lambda b, s: (0, 0))

    def out_shape(kind, w, dt):
        shp = {"row": (bsz, seq, w), "batch": (bsz, 1, w), "global": (1, w)}[kind]
        return jax.ShapeDtypeStruct(shp, dt)

    n_s, n_out = len(swapped), len(outs)

    def body(*refs):
        b, s = pl.program_id(0), pl.program_id(1)
        x_refs, o_refs = refs[n_in:n_in + n_s], refs[n_in + n_s:n_in + n_s + n_out]
        got_refs, sems = refs[n_in + n_s + n_out:n_in + 2 * n_s + n_out], refs[n_in + 2 * n_s + n_out:]
        if n_s:
            @pl.when((b == 0) & (s == 0))
            def _():
                _exchange_start(x_refs, got_refs, sems, False)

        vals = [r[...] if a.ndim == 2 else r[0] for r, a in zip(refs[:n_in], ins)]
        prods = [_dot(vals[t].astype(BF16), vals[t + 1].astype(BF16)) for t in range(n_given, n_in, 2)]
        res = fn(*vals[:n_given], *prods)
        if n_s:
            @pl.when((b == bsz - 1) & (s == seq // ts - 1))
            def _():
                _exchange_start(x_refs, got_refs, sems, False, wait=True)

        for o_ref, (kind, _, dt), v in zip(o_refs, outs, res):
            if kind == "row":
                o_ref[0] = v.astype(dt)
            elif kind == "batch":
                @pl.when(s == 0)
                def _():
                    o_ref[...] = jnp.zeros_like(o_ref)

                o_ref[0] += v
            else:
                @pl.when((s == 0) & (b == 0))
                def _():
                    o_ref[...] = jnp.zeros_like(o_ref)

                o_ref[...] += v

    res = pl.pallas_call(
        body, grid=(bsz, seq // ts), in_specs=[in_spec(a) for a in ins] + [ANY_SPEC] * n_s,
        out_specs=[out_spec(k, w) for k, w, _ in outs] + [ANY_SPEC] * n_s,
        out_shape=[out_shape(*o) for o in outs] + _exchange_shapes(swapped, False),
        scratch_shapes=_exchange_sems(n_s) if n_s else [], name=name, compiler_params=_arb(2),
    )(*ins, *swapped)
    return res if not n_s else (*res[:n_out], list(res[n_out:]))


def _small(fn, ins, out_shapes, name):
    n_in = len(ins)

    def body(*refs):
        res = fn(*[r[...] for r in refs[:n_in]])
        for o_ref, v in zip(refs[n_in:], res):
            o_ref[...] = v

    return pl.pallas_call(body, out_shape=[jax.ShapeDtypeStruct(s, d) for s, d in out_shapes], name=name)(*ins)


def _mesh_pos():
    mx, my, mc = lax.axis_index("x"), lax.axis_index("y"), lax.axis_index("c")
    return mx, my, mc, 4 * mx + 2 * my + mc


def _peer(mx, my, mc, k):
    px = 1 - mx if k & 4 else mx
    py = 1 - my if k & 2 else my
    pc = 1 - mc if k & 1 else mc
    return (px, py, pc), 4 * px + 2 * py + pc


ANY_SPEC = pl.BlockSpec(memory_space=pl.ANY)


def _exchange_shapes(arrs, gather):
    return [jax.ShapeDtypeStruct((N_DEV,) + tuple(x.shape if gather else x.shape[1:]), x.dtype) for x in arrs]


def _exchange_sems(n_arr):
    return [pltpu.SemaphoreType.DMA((n_arr, N_DEV)), pltpu.SemaphoreType.DMA((n_arr, N_DEV)),
            pltpu.SemaphoreType.DMA((n_arr,))]


def _exchange_start(x_refs, out_refs, sems, gather, wait=False):
    send_sems, recv_sems, local_sems = sems
    mx, my, mc, me = _mesh_pos()
    owns, sends, recvs = [], [], []
    for a, (x_ref, out_ref) in enumerate(zip(x_refs, out_refs)):
        owns.append(pltpu.make_async_copy(x_ref if gather else x_ref.at[me], out_ref.at[me], local_sems.at[a]))
        for k in range(1, N_DEV):
            peer, pid = _peer(mx, my, mc, k)
            src = x_ref if gather else x_ref.at[pid]
            sends.append(pltpu.make_async_remote_copy(
                src_ref=src, dst_ref=out_ref.at[me], send_sem=send_sems.at[a, k], recv_sem=recv_sems.at[a, k],
                device_id=peer, device_id_type=MESH))
            if wait:
                recvs.append(pltpu.make_async_remote_copy(
                    src_ref=src, dst_ref=out_ref.at[pid], send_sem=send_sems.at[a, k], recv_sem=recv_sems.at[a, k],
                    device_id=peer, device_id_type=MESH))
    if not wait:
        for cp in owns + sends:
            cp.start()
        return
    for cp in recvs:
        cp.wait_recv()
    for cp in sends:
        cp.wait_send()
    for cp in owns:
        cp.wait()


def _exchange(arrs, gather, name):
    n_arr = len(arrs)

    def body(*refs):
        x_refs, out_refs, sems = refs[:n_arr], refs[n_arr:2 * n_arr], refs[2 * n_arr:]
        _exchange_start(x_refs, out_refs, sems, gather)
        _exchange_start(x_refs, out_refs, sems, gather, wait=True)

    return pl.pallas_call(
        body, out_shape=_exchange_shapes(arrs, gather), in_specs=[ANY_SPEC] * n_arr, out_specs=[ANY_SPEC] * n_arr,
        scratch_shapes=_exchange_sems(n_arr), name=name,
    )(*arrs)


QK_SCALE = HEAD_DIM ** -0.5


def _low_lanes():
    return lax.broadcasted_iota(jnp.int32, (1, LANES), 1) < HEAD_DIM


def _stack_heads(v, scale=None):
    lo = _low_lanes()
    zero = jnp.zeros_like(v)
    s = jnp.concatenate([jnp.where(lo, v, zero), jnp.where(lo, zero, v)], axis=0)
    return s if scale is None else s * scale


def _stack_cols(v):
    return jnp.concatenate([v[:, 0:1], v[:, HEAD_DIM:HEAD_DIM + 1]], axis=0)


def _unstack(v, tq):
    return jnp.where(_low_lanes(), v[:tq], v[tq:])


def _tile_pos(tq, tk, q0):
    rows = lax.broadcasted_iota(jnp.int32, (2 * tq, tk), 0)
    cols = lax.broadcasted_iota(jnp.int32, (2 * tq, tk), 1)
    return q0 + jnp.where(rows >= tq, rows - tq, rows), cols, rows < tq


def _tri(tk, cmp):
    r = lax.broadcasted_iota(jnp.int32, (tk, tk), 0)
    c = lax.broadcasted_iota(jnp.int32, (tk, tk), 1)
    return jnp.where(cmp(r, c), 1.0, 0.0).astype(BF16)


def _softplus(z):
    return jnp.maximum(z, 0.0) + jnp.log(1.0 + jnp.exp(-jnp.abs(z)))


PREFIX_BLOCK = 256


def _running(v, tri, later):
    blk = tri.shape[0]
    nb = v.shape[1] // blk
    parts = [v[:, b * blk:(b + 1) * blk] for b in range(nb)]
    outs, run = [None] * nb, None
    for b in (reversed(range(nb)) if later else range(nb)):
        inside = _dot_split2(parts[b], tri)
        outs[b] = inside if run is None else inside + run
        total = jnp.sum(parts[b], axis=1, keepdims=True)
        run = total if run is None else run + total
    return (outs[0] if nb == 1 else jnp.concatenate(outs, axis=1)), run


def _pair_specs(tq, seq, np_, off):
    return [pl.BlockSpec((1, tq, LANES), lambda b, p, i: (b, i, off + p)),
            pl.BlockSpec((1, seq, LANES), lambda b, p, i: (b, 0, off + np_ + p)),
            pl.BlockSpec((1, seq, LANES), lambda b, p, i: (b, 0, off + 2 * np_ + p))]


def _key_tiles(tile, carry, q0, tq, tk, upward):
    nfull = q0 // tk
    edge = range(tq // tk)
    if upward:
        carry = lax.fori_loop(0, nfull, lambda j, cr: tile(pl.multiple_of(j * tk, tk), cr, False), carry)
        for jm in edge:
            carry = tile(pl.multiple_of(q0 + jm * tk, tk), carry, True)
        return carry
    for jm in reversed(edge):
        carry = tile(pl.multiple_of(q0 + jm * tk, tk), carry, True)
    return lax.fori_loop(0, nfull, lambda jj, cr: tile(pl.multiple_of((nfull - 1 - jj) * tk, tk), cr, False), carry)


def _grid_ends(bsz, np_, nq):
    b, p, i = pl.program_id(0), pl.program_id(1), pl.program_id(2)
    return (b == 0) & (p == 0) & (i == 0), (b == bsz - 1) & (p == np_ - 1) & (i == nq - 1)


def _sb_fwd(qkv, np_, tq, tk, name, gathered=()):
    bsz, seq, _ = qkv.shape
    n_g = len(gathered)

    def body(*refs):
        q_ref, k_ref, v_ref = refs[:3]
        x_refs, (y_ref, tot_ref) = refs[3:3 + n_g], refs[3 + n_g:5 + n_g]
        out_refs, sems = refs[5 + n_g:5 + 2 * n_g], refs[5 + 2 * n_g:]
        first, last = _grid_ends(bsz, np_, seq // tq)
        if n_g:
            @pl.when(first)
            def _():
                _exchange_start(x_refs, out_refs, sems, True)

        q0 = pl.program_id(2) * tq
        tpos, cols, _ = _tile_pos(tq, tk, q0)
        msuf = _tri(min(tk, PREFIX_BLOCK), lambda a, b: a > b)
        qs = _stack_heads(q_ref[0], QK_SCALE)

        def tile(k0, carry, masked):
            tot, acc = carry
            z = _dot_nt(qs, k_ref[0, pl.ds(k0, tk), :])
            sp = _softplus(z)
            if masked:
                seen = (k0 + cols) < tpos
                sp = jnp.where(seen, sp, 0.0)
            sp_after, sp_tot = _running(sp, msuf, later=True)
            logw = z - sp - sp_after - tot
            if masked:
                logw = jnp.where(seen, logw, NEG)
            return tot + sp_tot, acc + _dot(jnp.exp(logw).astype(BF16), v_ref[0, pl.ds(k0, tk), :])

        init = (jnp.zeros((2 * tq, 1), F32), jnp.zeros((2 * tq, LANES), F32))
        tot, acc = _key_tiles(tile, init, q0, tq, tk, upward=False)
        y_ref[0] = _unstack(acc, tq)
        tot_ref[0, 0] = _unstack(tot, tq)
        if n_g:
            @pl.when(last)
            def _():
                _exchange_start(x_refs, out_refs, sems, True, wait=True)

    y, tot, *got = pl.pallas_call(
        body, grid=(bsz, np_, seq // tq), in_specs=_pair_specs(tq, seq, np_, 0) + [ANY_SPEC] * n_g,
        out_specs=[pl.BlockSpec((1, tq, LANES), lambda b, p, i: (b, i, p)),
                   pl.BlockSpec((1, 1, tq, LANES), lambda b, p, i: (b, p, i, 0))] + [ANY_SPEC] * n_g,
        out_shape=[jax.ShapeDtypeStruct((bsz, seq, np_ * LANES), F32),
                   jax.ShapeDtypeStruct((bsz, np_, seq, LANES), F32)] + _exchange_shapes(gathered, True),
        scratch_shapes=_exchange_sems(n_g) if n_g else [],
        name=name, compiler_params=_arb(3),
    )(qkv, qkv, qkv, *gathered)
    return y, tot, got


def _sb_bwd(qkv, dy, tot, np_, tq, tk, name, swapped=()):
    bsz, seq, _ = qkv.shape
    nq = seq // tq
    n_s = len(swapped)

    def body(*refs):
        q_ref, k_ref, v_ref, dy_ref, tot_ref = refs[:5]
        x_refs, (dq_ref, dk_ref, dv_ref) = refs[5:5 + n_s], refs[5 + n_s:8 + n_s]
        out_refs = refs[8 + n_s:8 + 2 * n_s]
        dk_acc, dv_acc = refs[8 + 2 * n_s:10 + 2 * n_s]
        sems = refs[10 + 2 * n_s:]
        first, last = _grid_ends(bsz, np_, nq)
        if n_s:
            @pl.when(first)
            def _():
                _exchange_start(x_refs, out_refs, sems, False)

        i = pl.program_id(2)
        q0 = i * tq

        @pl.when(i == 0)
        def _():
            dk_acc[...] = jnp.zeros_like(dk_acc)
            dv_acc[...] = jnp.zeros_like(dv_acc)

        tpos, cols, _ = _tile_pos(tq, tk, q0)
        mincl = _tri(min(tk, PREFIX_BLOCK), lambda a, b: a <= b)
        mexcl = _tri(min(tk, PREFIX_BLOCK), lambda a, b: a < b)
        qs = _stack_heads(q_ref[0], QK_SCALE)
        dys = _stack_heads(dy_ref[0].astype(BF16))
        tots = _stack_cols(tot_ref[0, 0])

        def tile(k0, carry, masked):
            c_sp, c_g, dq = carry
            kb = k_ref[0, pl.ds(k0, tk), :]
            z = _dot_nt(qs, kb)
            sp = _softplus(z)
            if masked:
                seen = (k0 + cols) < tpos
                sp = jnp.where(seen, sp, 0.0)
            sp_upto, sp_tot = _running(sp, mincl, later=False)
            logw = z - sp - (tots - c_sp - sp_upto)
            if masked:
                logw = jnp.where(seen, logw, NEG)
            w = jnp.exp(logw)
            g = w * _dot_nt(dys, v_ref[0, pl.ds(k0, tk), :])
            g_before, g_tot = _running(g, mexcl, later=False)
            beta = jnp.exp(jnp.minimum(z - sp, 0.0))
            dz = g - beta * (g + c_g + g_before)
            if masked:
                dz = jnp.where(seen, dz, 0.0)
            dzb = dz.astype(BF16)
            dk_acc[pl.ds(k0, tk), :] += _dot_tn(dzb, qs)
            dv_acc[pl.ds(k0, tk), :] += _dot_tn(w.astype(BF16), dys)
            return c_sp + sp_tot, c_g + g_tot, dq + _dot(dzb, kb)

        zero = jnp.zeros((2 * tq, 1), F32)
        _, _, dq = _key_tiles(tile, (zero, zero, jnp.zeros((2 * tq, LANES), F32)), q0, tq, tk, upward=True)
        dq_ref[0] = (_unstack(dq, tq) * QK_SCALE).astype(BF16)

        @pl.when(i == nq - 1)
        def _():
            dk_ref[0] = dk_acc[...].astype(BF16)
            dv_ref[0] = dv_acc[...].astype(BF16)

        if n_s:
            @pl.when(last)
            def _():
                _exchange_start(x_refs, out_refs, sems, False, wait=True)

    tile_spec = pl.BlockSpec((1, tq, LANES), lambda b, p, i: (b, i, p))
    whole = pl.BlockSpec((1, seq, LANES), lambda b, p, i: (b, 0, p))
    out = jax.ShapeDtypeStruct((bsz, seq, np_ * LANES), BF16)
    dq, dk, dv, *got = pl.pallas_call(
        body, grid=(bsz, np_, nq),
        in_specs=_pair_specs(tq, seq, np_, 0) + [tile_spec, pl.BlockSpec((1, 1, tq, LANES), lambda b, p, i: (b, p, i, 0))]
        + [ANY_SPEC] * n_s,
        out_specs=[tile_spec, whole, whole] + [ANY_SPEC] * n_s,
        out_shape=[out, out, out] + _exchange_shapes(swapped, False),
        scratch_shapes=[pltpu.VMEM((seq, LANES), F32), pltpu.VMEM((seq, LANES), F32)] + (_exchange_sems(n_s) if n_s else []),
        name=name, compiler_params=_arb(3),
    )(qkv, qkv, qkv, dy, tot, *swapped)
    return dq, dk, dv, got


def _fox_fwd(qkv, cumk, np_, tq, tk, name):
    bsz, seq, _ = qkv.shape

    def body(q_ref, k_ref, v_ref, ck_ref, y_ref, lse_ref):
        q0 = pl.program_id(2) * tq
        tpos, cols, top = _tile_pos(tq, tk, q0)
        qs = _stack_heads(q_ref[0], QK_SCALE)

        def tile(k0, carry, masked):
            m, l, acc = carry
            ck = jnp.where(top, ck_ref[0, 0, 0:1, pl.ds(k0, tk)], ck_ref[0, 0, 1:2, pl.ds(k0, tk)])
            s = _dot_nt(qs, k_ref[0, pl.ds(k0, tk), :]) - ck
            if masked:
                s = jnp.where((k0 + cols) <= tpos, s, NEG)
            m_new = jnp.maximum(m, jnp.max(s, axis=1, keepdims=True))
            p = jnp.exp(s - m_new)
            alpha = jnp.exp(m - m_new)
            p_hi, p_lo = _split2(p)
            vb = v_ref[0, pl.ds(k0, tk), :]
            return m_new, alpha * l + jnp.sum(p, axis=1, keepdims=True), alpha * acc + (_dot(p_hi, vb) + _dot(p_lo, vb))

        init = (jnp.full((2 * tq, 1), NEG, F32), jnp.zeros((2 * tq, 1), F32), jnp.zeros((2 * tq, LANES), F32))
        m, l, acc = _key_tiles(tile, init, q0, tq, tk, upward=True)
        y_ref[0] = _unstack(acc / l, tq)
        lse_ref[0, 0] = _unstack(m + jnp.log(l), tq)

    row4 = pl.BlockSpec((1, 1, tq, LANES), lambda b, p, i: (b, p, i, 0))
    return pl.pallas_call(
        body, grid=(bsz, np_, seq // tq),
        in_specs=_pair_specs(tq, seq, np_, 3 * np_) + [pl.BlockSpec((1, 1, 8, seq), lambda b, p, i: (b, p, 0, 0))],
        out_specs=[pl.BlockSpec((1, tq, LANES), lambda b, p, i: (b, i, p)), row4],
        out_shape=[jax.ShapeDtypeStruct((bsz, seq, np_ * LANES), F32),
                   jax.ShapeDtypeStruct((bsz, np_, seq, LANES), F32)],
        name=name, compiler_params=_arb(3),
    )(qkv, qkv, qkv, cumk)


def _fox_bwd(qkv, dy, y, lse, cumk, np_, tq, tk, name):
    bsz, seq, _ = qkv.shape
    nq = seq // tq

    def body(q_ref, k_ref, v_ref, dy_ref, y_ref, lse_ref, ck_ref,
             dq_ref, dk_ref, dv_ref, dck_ref, dcq_ref, dk_acc, dv_acc, dck_acc):
        i = pl.program_id(2)
        q0 = i * tq

        @pl.when(i == 0)
        def _():
            dk_acc[...] = jnp.zeros_like(dk_acc)
            dv_acc[...] = jnp.zeros_like(dv_acc)
            dck_acc[...] = jnp.zeros_like(dck_acc)

        tpos, cols, top = _tile_pos(tq, tk, q0)
        qs = _stack_heads(q_ref[0], QK_SCALE)
        dyf = dy_ref[0]
        dys = _stack_heads(dyf.astype(BF16))
        dyy = dyf.astype(BF16).astype(F32) * y_ref[0]
        lo = _low_lanes()
        delta = jnp.concatenate([jnp.sum(jnp.where(lo, dyy, 0.0), axis=1, keepdims=True),
                                 jnp.sum(jnp.where(lo, 0.0, dyy), axis=1, keepdims=True)], axis=0)
        lse_s = _stack_cols(lse_ref[0, 0])

        def tile(k0, carry, masked):
            dq, row = carry
            kb = k_ref[0, pl.ds(k0, tk), :]
            ck = jnp.where(top, ck_ref[0, 0, 0:1, pl.ds(k0, tk)], ck_ref[0, 0, 1:2, pl.ds(k0, tk)])
            s = _dot_nt(qs, kb) - ck
            if masked:
                s = jnp.where((k0 + cols) <= tpos, s, NEG)
            p = jnp.exp(s - lse_s)
            ds = p * (_dot_nt(dys, v_ref[0, pl.ds(k0, tk), :]) - delta)
            dsb = ds.astype(BF16)
            dk_acc[pl.ds(k0, tk), :] += _dot_tn(dsb, qs)
            dv_acc[pl.ds(k0, tk), :] += _dot_tn(p.astype(BF16), dys)
            dck_acc[0:1, pl.ds(k0, tk)] += -jnp.sum(ds[:tq], axis=0, keepdims=True)
            dck_acc[1:2, pl.ds(k0, tk)] += -jnp.sum(ds[tq:], axis=0, keepdims=True)
            return dq + _dot(dsb, kb), row + jnp.sum(ds, axis=1, keepdims=True)

        init = (jnp.zeros((2 * tq, LANES), F32), jnp.zeros((2 * tq, 1), F32))
        dq, row = _key_tiles(tile, init, q0, tq, tk, upward=True)
        dq_ref[0] = (_unstack(dq, tq) * QK_SCALE).astype(BF16)
        dcq_ref[0, 0] = _unstack(row, tq)

        @pl.when(i == nq - 1)
        def _():
            dk_ref[0] = dk_acc[...].astype(BF16)
            dv_ref[0] = dv_acc[...].astype(BF16)
            dck_ref[0, 0] = dck_acc[...]

    tile_spec = pl.BlockSpec((1, tq, LANES), lambda b, p, i: (b, i, p))
    whole = pl.BlockSpec((1, seq, LANES), lambda b, p, i: (b, 0, p))
    row4 = pl.BlockSpec((1, 1, tq, LANES), lambda b, p, i: (b, p, i, 0))
    key4 = pl.BlockSpec((1, 1, 8, seq), lambda b, p, i: (b, p, 0, 0))
    out = jax.ShapeDtypeStruct((bsz, seq, np_ * LANES), BF16)
    return pl.pallas_call(
        body, grid=(bsz, np_, nq),
        in_specs=_pair_specs(tq, seq, np_, 3 * np_) + [tile_spec, tile_spec, row4, key4],
        out_specs=[tile_spec, whole, whole, key4, row4],
        out_shape=[out, out, out, jax.ShapeDtypeStruct((bsz, np_, 8, seq), F32),
                   jax.ShapeDtypeStruct((bsz, np_, seq, LANES), F32)],
        scratch_shapes=[pltpu.VMEM((seq, LANES), F32), pltpu.VMEM((seq, LANES), F32), pltpu.VMEM((8, seq), F32)],
        name=name, compiler_params=_arb(3),
    )(qkv, qkv, qkv, dy, y, lse, cumk)


def _cum_fwd(fl, bf, name, tb=256):
    bsz, seq, _ = fl.shape
    tb = _pick(seq, (tb, 128))

    def body(fl_ref, bf_ref, o_ref):
        tri = _tri(tb, lambda a, b: b <= a)

        def step(j, carry):
            r0 = pl.multiple_of(j * tb, tb)
            blk = _tri_dot3(tri, -_softplus(-(fl_ref[0, pl.ds(r0, tb), :] + bf_ref[...]))) + carry
            o_ref[0, pl.ds(r0, tb), :] = blk
            return blk[tb - 1:tb, :]

        lax.fori_loop(0, seq // tb, step, jnp.zeros((1, LANES), F32))

    return pl.pallas_call(
        body, grid=(bsz,),
        in_specs=[pl.BlockSpec((1, seq, LANES), lambda b: (b, 0, 0)), pl.BlockSpec((1, LANES), lambda b: (0, 0))],
        out_specs=pl.BlockSpec((1, seq, LANES), lambda b: (b, 0, 0)),
        out_shape=jax.ShapeDtypeStruct(fl.shape, F32), name=name, compiler_params=_arb(1),
    )(fl, bf)


def _cum_bwd(dck, dcq, fl, bf, np_, name, tb=256):
    bsz, seq, _ = fl.shape
    tb = _pick(seq, (tb, 128))
    nb = seq // tb

    def body(dck_ref, dcq_ref, fl_ref, bf_ref, o_ref, db_ref):
        @pl.when(pl.program_id(0) == 0)
        def _():
            db_ref[...] = jnp.zeros_like(db_ref)

        tri = _tri(tb, lambda a, b: b >= a)
        lane = lax.broadcasted_iota(jnp.int32, (1, LANES), 1)

        def step(jj, carry):
            tail, tot = carry
            r0 = pl.multiple_of((nb - 1 - jj) * tb, tb)
            dc = dck_ref[0, pl.ds(r0, tb), :]
            for p in range(np_):
                pair = dcq_ref[0, p, pl.ds(r0, tb), :]
                dc = dc + jnp.where(lane == 2 * p, pair, 0.0) + jnp.where(lane == 2 * p + 1, pltpu.roll(pair, HEAD_DIM, 1), 0.0)
            dlf = _tri_dot3(tri, dc) + tail
            dfl = dlf * _sigmoid(-(fl_ref[0, pl.ds(r0, tb), :] + bf_ref[...]))
            o_ref[0, pl.ds(r0, tb), :] = dfl
            return dlf[0:1, :], tot + jnp.sum(dfl, axis=0, keepdims=True)

        zero = jnp.zeros((1, LANES), F32)
        _, tot = lax.fori_loop(0, nb, step, (zero, zero))
        db_ref[...] += tot

    whole = pl.BlockSpec((1, seq, LANES), lambda b: (b, 0, 0))
    vec = pl.BlockSpec((1, LANES), lambda b: (0, 0))
    return pl.pallas_call(
        body, grid=(bsz,), in_specs=[whole, pl.BlockSpec((1, np_, seq, LANES), lambda b: (b, 0, 0, 0)), whole, vec],
        out_specs=[whole, vec],
        out_shape=[jax.ShapeDtypeStruct(fl.shape, F32), jax.ShapeDtypeStruct((1, LANES), F32)],
        name=name, compiler_params=_arb(1),
    )(dck, dcq, fl, bf)


def _adamw_math(w, g, m, v):
    m = ADAM_B1 * m + (1.0 - ADAM_B1) * g
    v = ADAM_B2 * v + (1.0 - ADAM_B2) * (g * g)
    m_hat = m / (1.0 - ADAM_B1 ** ADAM_STEP)
    v_hat = v / (1.0 - ADAM_B2 ** ADAM_STEP)
    return -ADAM_LR * (m_hat / (jnp.sqrt(v_hat) + ADAM_EPS) + ADAM_WD * w), m, v


def _adamw(gparts, w, m, v, name, tr=512):
    nslots, rows, cols = gparts.shape
    tr = _pick(rows, (tr, 256, 128, 64, 32, 16, 8))

    def body(g_ref, w_ref, m_ref, v_ref, go_ref, d_ref, mo_ref, vo_ref):
        g = g_ref[0].astype(F32)
        for k in range(1, nslots):
            g = g + g_ref[k].astype(F32)
        go_ref[...] = g
        d_ref[...], mo_ref[...], vo_ref[...] = _adamw_math(w_ref[...], g, m_ref[...], v_ref[...])

    blk = pl.BlockSpec((tr, cols), lambda i: (i, 0))
    shp = jax.ShapeDtypeStruct((rows, cols), F32)
    return pl.pallas_call(
        body, grid=(rows // tr,), in_specs=[pl.BlockSpec((nslots, tr, cols), lambda i: (0, i, 0)), blk, blk, blk],
        out_specs=[blk] * 4, out_shape=[shp] * 4, name=name, compiler_params=_arb(1),
    )(gparts, w, m, v)


def _rows128(a):
    return a.reshape(-1, LANES)


def _pad_rows(a, mult):
    extra = (-a.shape[0]) % mult
    return a if extra == 0 else jnp.concatenate([a, jnp.zeros((extra, a.shape[1]), a.dtype)], axis=0)


def _pack(arrs, mult):
    return _pad_rows(jnp.concatenate([_rows128(a) for a in arrs], axis=0), mult)


def _unpack(flat, shapes):
    out, off = [], 0
    for shp in shapes:
        n = 1
        for s in shp:
            n *= s
        out.append(flat[off:off + n // LANES].reshape(shp))
        off += n // LANES
    return out


def _col_blocks(full):
    k, n = full.shape
    return full.reshape(k, N_DEV, n // N_DEV).transpose(1, 0, 2)


def _from_col_blocks(blocks):
    _, k, n = blocks.shape
    return blocks.transpose(1, 0, 2).reshape(k, N_DEV * n)


def kernel(x, c, w_ada, b_ada, g_mix, w_in, b_forget, b_gate, w_branch_sb, w_branch_fox, w_out, g_ffn, w_ffn_gate, w_ffn_up, w_ffn_down, g_final, loss_target, m_w_ada, m_b_ada, m_g_mix, m_w_in, m_b_forget, m_b_gate, m_w_branch_sb, m_w_branch_fox, m_w_out, m_g_ffn, m_w_ffn_gate, m_w_ffn_up, m_w_ffn_down, m_g_final, v_w_ada, v_b_ada, v_g_mix, v_w_in, v_b_forget, v_b_gate, v_w_branch_sb, v_w_branch_fox, v_w_out, v_g_ffn, v_w_ffn_gate, v_w_ffn_up, v_w_ffn_down, v_g_final):
    bsz, seq, d = x.shape
    tok = bsz * seq
    nh = b_forget.shape[-1]
    d_in = w_in.shape[-1] * N_DEV
    d_att = (d_in - nh - 2 * d) // 6
    assert d_att == nh * HEAD_DIM and nh % 2 == 0
    np_ = nh // 2
    d_ff = w_ffn_gate.shape[-1] * N_DEV
    n_mod = w_ada.shape[-1] * N_DEV // d
    me = 4 * lax.axis_index("x") + 2 * lax.axis_index("y") + lax.axis_index("c")
    tq_att = _pick(seq, (512, 256, 128))
    tk_att = tq_att
    tq_fox = tk_fox = tq_att

    big = [w_in[0], w_branch_sb[0], w_branch_fox[0], w_out[0], w_ffn_gate[0], w_ffn_up[0], w_ffn_down[0]]
    big_m = [m_w_in[0], m_w_branch_sb[0], m_w_branch_fox[0], m_w_out[0], m_w_ffn_gate[0], m_w_ffn_up[0], m_w_ffn_down[0]]
    big_v = [v_w_in[0], v_w_branch_sb[0], v_w_branch_fox[0], v_w_out[0], v_w_ffn_gate[0], v_w_ffn_up[0], v_w_ffn_down[0]]

    w_in_g, c_g = _exchange([big[0].astype(BF16), _rows128(c)], True, "gather_w_in_c")
    w_in_f = _from_col_blocks(w_in_g)
    w_qkv = w_in_f[:, :6 * d_att]
    w_f = jnp.concatenate([w_in_f[:, 6 * d_att:6 * d_att + nh], jnp.zeros((d, LANES - nh), BF16)], axis=1)
    w_gl = w_in_f[:, 6 * d_att + nh:]

    c_all = c_g.reshape(N_DEV * bsz, d)
    nb_all = N_DEV * bsz
    ada_cols = w_ada.shape[-1]
    b_ada_loc = lax.dynamic_slice(b_ada, (0, me * ada_cols), (1, ada_cols))

    def mod_fn(c_v, w_v, b_v):
        return [jnp.dot(_silu(c_v), w_v, precision=lax.Precision.HIGHEST, preferred_element_type=F32) + b_v]

    (mod_part,) = _small(mod_fn, [c_all, w_ada[0], b_ada_loc], [((nb_all, ada_cols), F32)], "ada_mod")
    mod_all = _from_col_blocks(_exchange([_rows128(mod_part)], True, "gather_mod")[0].reshape(N_DEV, nb_all, ada_cols))
    mod = lax.dynamic_slice(mod_all, (me * bsz, 0), (bsz, n_mod * d))
    shift1, scale1, gate1, shift2, scale2, gate2 = [mod[:, i * d:(i + 1) * d].reshape(bsz, 1, d) for i in range(6)]

    def norm_mod_fn(x_v, sc, sh, g):
        n = x_v * lax.rsqrt(jnp.mean(x_v * x_v, axis=-1, keepdims=True) + RMS_EPS) * g
        return [n * (1.0 + sc) + sh]

    (h,) = _rowmap(norm_mod_fn, [x, scale1, shift1, g_mix], [("row", d, BF16)], "norm1")
    h2d = h.reshape(tok, d)
    qkv = _mm([(h2d, w_qkv)], BF16, "proj_qkv").reshape(bsz, seq, 6 * d_att)
    gl = _mm([(h2d, w_gl)], F32, "proj_gates").reshape(bsz, seq, 2 * d)
    fl = _mm([(h2d, w_f)], F32, "proj_forget").reshape(bsz, seq, LANES)

    bf_pad = jnp.concatenate([b_forget, jnp.zeros((1, LANES - nh), F32)], axis=1)
    cum = _cum_fwd(fl, bf_pad, "cum_fwd")
    cumk = jnp.concatenate([cum[:, :, :nh].transpose(0, 2, 1).reshape(bsz, np_, 2, seq),
                            jnp.zeros((bsz, np_, 6, seq), F32)], axis=2)

    y_sb, tot_sb, gath = _sb_fwd(qkv, np_, tq_att, tk_att, "sb_fwd", gathered=[a.astype(BF16) for a in big[1:]])
    w_sb_f = _from_col_blocks(gath[0])
    w_fx_f = _from_col_blocks(gath[1])
    w_out_f = gath[2].reshape(d, d)
    w_g_f, w_u_f = _from_col_blocks(gath[3]), _from_col_blocks(gath[4])
    w_dn_f = gath[5].reshape(d_ff, d)
    y_fx, lse_fx = _fox_fwd(qkv, cumk, np_, tq_fox, tk_fox, "fox_fwd")

    def merge_fn(gl_v, bg, us, uf):
        gates = _sigmoid(gl_v + bg)
        return [gates[:, :d] * us + gates[:, d:] * uf, us, uf]

    merged, u_sb, u_fx = _rowmap(merge_fn, [gl, b_gate], [("row", d, BF16), ("row", d, F32), ("row", d, F32)], "merge",
                                 products=[(y_sb, w_sb_f), (y_fx, w_fx_f)])

    def resid_norm_fn(x_v, g1, sc, sh, g, mo_v):
        x1_v = x_v + g1 * mo_v
        n = x1_v * lax.rsqrt(jnp.mean(x1_v * x1_v, axis=-1, keepdims=True) + RMS_EPS) * g
        return [x1_v, n * (1.0 + sc) + sh, mo_v]

    x1, h2, mo = _rowmap(resid_norm_fn, [x, gate1, scale2, shift2, g_ffn],
                         [("row", d, F32), ("row", d, BF16), ("row", d, F32)], "norm2", products=[(merged, w_out_f)])
    h2_2d = h2.reshape(tok, d)

    def swiglu_fn(accs, _):
        a, u = accs
        return [_silu(a) * u, a, u]

    f, a_s, u_s = _mm([(h2_2d, w_g_f), (h2_2d, w_u_f)], [BF16, BF16, BF16], "ffn_in", epilogue=swiglu_fn)

    def head_fn(x1_v, g2, gf, tgt, ffn_v):
        x2 = x1_v + g2 * ffn_v
        rstd = lax.rsqrt(jnp.mean(x2 * x2, axis=-1, keepdims=True) + RMS_EPS)
        xh = x2 * rstd
        err = xh * gf - tgt
        loss_rows = 0.5 * jnp.mean(err * err, axis=-1, keepdims=True)
        dy = err * (1.0 / d)
        dxh = dy * gf
        dx2 = rstd * (dxh - xh * jnp.mean(dxh * xh, axis=-1, keepdims=True))
        return [dx2, dx2 * g2, jnp.sum(loss_rows, axis=0, keepdims=True) * jnp.ones((1, LANES), F32),
                jnp.sum(dy * xh, axis=0, keepdims=True), jnp.sum(dx2 * ffn_v, axis=0, keepdims=True)]

    dx2, dffn, loss_vec, dg_final, dgate2 = _rowmap(
        head_fn, [x1, gate2, g_final.reshape(1, d), loss_target],
        [("row", d, F32), ("row", d, BF16), ("global", LANES, F32), ("global", d, F32), ("batch", d, F32)], "head",
        products=[(f.reshape(bsz, seq, d_ff), w_dn_f)])

    dffn2d = dffn.reshape(tok, d)
    dw_dn = _mm_tn(f, dffn2d, "ffn_out_dw")

    def swiglu_bwd_fn(accs, saved):
        df_v, a, u = accs[0], saved[0].astype(F32), saved[1].astype(F32)
        sig = _sigmoid(a)
        return [df_v * u * sig * (1.0 + a * (1.0 - sig)), df_v * a * sig]

    da, du = _mm([(dffn2d, w_dn_f.T)], [BF16, BF16], "ffn_out_dx", extras=[a_s, u_s], epilogue=swiglu_bwd_fn)
    dw_gate, dw_up = _mm_tn(h2_2d, da, "ffn_gate_dw"), _mm_tn(h2_2d, du, "ffn_up_dw")
    def norm2_bwd_fn(x1_v, dx2_v, mo_v, sc, g1, g, dh_gate, dh_up):
        dh_v = dh_gate + dh_up
        rstd = lax.rsqrt(jnp.mean(x1_v * x1_v, axis=-1, keepdims=True) + RMS_EPS)
        xh = x1_v * rstd
        dn = dh_v * (1.0 + sc)
        dxh = dn * g
        dx1 = dx2_v + rstd * (dxh - xh * jnp.mean(dxh * xh, axis=-1, keepdims=True))
        return [dx1, dx1 * g1, jnp.sum(dh_v * (xh * g), axis=0, keepdims=True), jnp.sum(dh_v, axis=0, keepdims=True),
                jnp.sum(dn * xh, axis=0, keepdims=True), jnp.sum(dx1 * mo_v, axis=0, keepdims=True)]

    dx1, dmo, dscale2, dshift2, dg_ffn, dgate1 = _rowmap(
        norm2_bwd_fn, [x1, dx2, mo, scale2, gate1, g_ffn],
        [("row", d, F32), ("row", d, BF16), ("batch", d, F32), ("batch", d, F32), ("global", d, F32), ("batch", d, F32)],
        "norm2_bwd", ts=256,
        products=[(da.reshape(bsz, seq, d_ff), w_g_f.T), (du.reshape(bsz, seq, d_ff), w_u_f.T)])

    dmo2d = dmo.reshape(tok, d)
    dw_out = _mm_tn(merged.reshape(tok, d), dmo2d, "out_proj_dw")

    def merge_bwd_fn(gl_v, us, uf, bg, dm):
        gates = _sigmoid(gl_v + bg)
        gs, gf = gates[:, :d], gates[:, d:]
        dgl = jnp.concatenate([dm * us * gs * (1.0 - gs), dm * uf * gf * (1.0 - gf)], axis=1)
        return [dm * gs, dm * gf, dgl, jnp.sum(dgl, axis=0, keepdims=True)]

    du_sb, du_fx, dgl, db_gate = _rowmap(
        merge_bwd_fn, [gl, u_sb, u_fx, b_gate],
        [("row", d, BF16), ("row", d, BF16), ("row", 2 * d, BF16), ("global", 2 * d, F32)], "merge_bwd", ts=256,
        products=[(dmo, w_out_f.T)])
    du_sb2d, du_fx2d = du_sb.reshape(tok, d), du_fx.reshape(tok, d)
    dw_sb = _mm_tn(y_sb.reshape(tok, d_att), du_sb2d, "branch_sb_dw")
    dw_fx = _mm_tn(y_fx.reshape(tok, d_att), du_fx2d, "branch_fox_dw")
    dy_sb = _mm([(du_sb2d, w_sb_f.T)], F32, "branch_sb_dx").reshape(bsz, seq, d_att)
    dy_fx = _mm([(du_fx2d, w_fx_f.T)], F32, "branch_fox_dx").reshape(bsz, seq, d_att)

    blocks = [_col_blocks(dw_sb), _col_blocks(dw_fx), dw_out.reshape(N_DEV, d // N_DEV, d),
              _col_blocks(dw_gate), _col_blocks(dw_up), dw_dn.reshape(N_DEV, d_ff // N_DEV, d)]
    blocks = [b.astype(BF16) for b in blocks]
    dq_sb, dk_sb, dv_sb, got = _sb_bwd(qkv, dy_sb, tot_sb, np_, tq_att, tk_att, "sb_bwd", swapped=blocks)
    dq_fx, dk_fx, dv_fx, dck, dcq = _fox_bwd(qkv, dy_fx, y_fx, lse_fx, cumk, np_, tq_fox, tk_fox, "fox_bwd")
    dck_rows = dck[:, :, :2, :].reshape(bsz, nh, seq).transpose(0, 2, 1)
    dck_rows = jnp.concatenate([dck_rows, jnp.zeros((bsz, seq, LANES - nh), F32)], axis=2)
    dfl, db_f = _cum_bwd(dck_rows, dcq, fl, bf_pad, np_, "cum_bwd")

    dqkv = jnp.concatenate([dq_sb, dk_sb, dv_sb, dq_fx, dk_fx, dv_fx], axis=2).reshape(tok, 6 * d_att)
    dgl2d, dfl2d = dgl.reshape(tok, 2 * d), dfl.reshape(tok, LANES)
    dw_in = jnp.concatenate([_mm_tn(h2d, dqkv, "proj_qkv_dw"), _mm_tn(h2d, dfl2d, "proj_forget_dw")[:, :nh],
                             _mm_tn(h2d, dgl2d, "proj_gates_dw")], axis=1)
    def norm1_bwd_fn(x_v, dx1_v, sc, g, dh_qkv, dh_gl, dh_fl):
        dh_v = dh_qkv + dh_gl + dh_fl
        rstd = lax.rsqrt(jnp.mean(x_v * x_v, axis=-1, keepdims=True) + RMS_EPS)
        xh = x_v * rstd
        dn = dh_v * (1.0 + sc)
        dxh = dn * g
        dx = dx1_v + rstd * (dxh - xh * jnp.mean(dxh * xh, axis=-1, keepdims=True))
        return [dx, jnp.sum(dh_v * (xh * g), axis=0, keepdims=True), jnp.sum(dh_v, axis=0, keepdims=True),
                jnp.sum(dn * xh, axis=0, keepdims=True)]

    grad_x, dscale1, dshift1, dg_mix, got_in = _rowmap(
        norm1_bwd_fn, [x, dx1, scale1, g_mix],
        [("row", d, F32), ("batch", d, F32), ("batch", d, F32), ("global", d, F32)], "norm1_bwd", ts=256,
        products=[(dqkv.reshape(bsz, seq, 6 * d_att), w_qkv.T), (dgl, w_gl.T), (dfl, w_f.T)],
        swapped=[_col_blocks(dw_in).astype(BF16)])

    dmod = jnp.concatenate([dshift1, dscale1, dgate1, dshift2, dscale2, dgate2], axis=2).reshape(bsz, n_mod * d)
    partial = [dg_mix, db_f, db_gate, dg_ffn, dg_final]
    n_dmod_rows = bsz * n_mod * d // LANES
    small_sent = _pack([dmod] + partial + [loss_vec], 8)
    small_all = _exchange([small_sent], True, "gather_small")[0]
    small_w = [b_ada, g_mix, jnp.concatenate([b_forget, jnp.zeros((1, LANES - nh), F32)], axis=1), b_gate, g_ffn,
               g_final.reshape(1, d)]
    small_m = [m_b_ada, m_g_mix, jnp.concatenate([m_b_forget, jnp.zeros((1, LANES - nh), F32)], axis=1), m_b_gate,
               m_g_ffn, m_g_final.reshape(1, d)]
    small_v = [v_b_ada, v_g_mix, jnp.concatenate([v_b_forget, jnp.zeros((1, LANES - nh), F32)], axis=1), v_b_gate,
               v_g_ffn, v_g_final.reshape(1, d)]
    small_shapes = [a.shape for a in small_w]
    n_ada_rows = n_mod * d // LANES
    n_part_rows = sum(a.shape[1] // LANES for a in partial)
    sw, sm, sv = _pack(small_w, 8), _pack(small_m, 8), _pack(small_v, 8)
    n_small_rows = sw.shape[0]

    def small_fn(all_v, w_v, m_v, v_v):
        g_ada = None
        g_rest = None
        for k in range(N_DEV):
            for b in range(bsz):
                part = all_v[k, b * n_ada_rows:(b + 1) * n_ada_rows]
                g_ada = part if g_ada is None else g_ada + part
            rest = all_v[k, n_dmod_rows:n_dmod_rows + n_part_rows + 1]
            g_rest = rest if g_rest is None else g_rest + rest
        pieces = [g_ada, g_rest[:n_part_rows]]
        if n_small_rows > n_ada_rows + n_part_rows:
            pieces.append(jnp.zeros((n_small_rows - n_ada_rows - n_part_rows, LANES), F32))
        g = jnp.concatenate(pieces, axis=0)
        return [g, *_adamw_math(w_v, g, m_v, v_v), jnp.broadcast_to(g_rest[n_part_rows:], (8, LANES))]

    shp = ((n_small_rows, LANES), F32)
    *small_out, loss_all = _small(small_fn, [small_all, sw, sm, sv], [shp] * 4 + [((8, LANES), F32)], "small_update")
    loss = loss_all[0, 0]
    small_g, small_d, small_nm, small_nv = [_unpack(o, small_shapes) for o in small_out]

    def fix_small(lst):
        b_ada_o, g_mix_o, b_f_o, b_gate_o, g_ffn_o, g_final_o = lst
        return [b_ada_o, g_mix_o, b_f_o[:, :nh], b_gate_o, g_ffn_o, g_final_o.reshape(d)]

    small_g, small_d, small_nm, small_nv = [fix_small(l) for l in (small_g, small_d, small_nm, small_nv)]

    dmod_all = small_all[:, :n_dmod_rows].reshape(nb_all, n_mod * d)
    dmod_cols = lax.dynamic_slice(dmod_all, (0, me * ada_cols), (nb_all, ada_cols))

    def ada_dw_fn(c_v, dm_v):
        return [lax.dot_general(_silu(c_v), dm_v, (((0,), (0,)), ((), ())), precision=lax.Precision.HIGHEST,
                                preferred_element_type=F32)]

    (dw_ada,) = _small(ada_dw_fn, [c_all, dmod_cols], [((d, ada_cols), F32)], "ada_dw")
    ada_out = _adamw(dw_ada[None], w_ada[0], m_w_ada[0], v_w_ada[0], "adamw_ada")
    ada_g, ada_d, ada_nm, ada_nv = [o[None] for o in ada_out]

    got = got_in + list(got)
    names = ["w_in", "w_sb", "w_fox", "w_out", "w_gate", "w_up", "w_down"]
    big_out = [_adamw(g, w, m, v, "adamw_" + n, tr=256) for g, w, m, v, n in zip(got, big, big_m, big_v, names)]
    big_g, big_d, big_nm, big_nv = [[o[i][None] for o in big_out] for i in range(4)]

    def ordered(ada, small, bigs):
        b_ada_o, g_mix_o, b_f_o, b_gate_o, g_ffn_o, g_final_o = small
        w_in_o, w_sb_o, w_fx_o, w_out_o, w_gate_o, w_up_o, w_dn_o = bigs
        return [ada, b_ada_o, g_mix_o, w_in_o, b_f_o, b_gate_o, w_sb_o, w_fx_o, w_out_o, g_ffn_o, w_gate_o, w_up_o,
                w_dn_o, g_final_o]

    return (loss, grad_x, *ordered(ada_g, small_g, big_g), *ordered(ada_d, small_d, big_d),
            *ordered(ada_nm, small_nm, big_nm), *ordered(ada_nv, small_nv, big_nv))
```

```python
import jax
import jax.numpy as jnp
from jax import lax
from jax.experimental import pallas as pl
from jax.experimental.pallas import tpu as pltpu

F32 = jnp.float32
BF16 = jnp.bfloat16
HEAD_DIM = 64
LANES = 128
N_DEV = 8
RMS_EPS = 1e-6
ADAM_LR, ADAM_B1, ADAM_B2, ADAM_EPS, ADAM_WD, ADAM_STEP = 0.001, 0.9, 0.999, 1e-08, 0.01, 10
NEG = -1e30
MESH = pl.DeviceIdType.MESH


def _pick(n, cands):
    for c in cands:
        if n % c == 0:
            return c
    raise ValueError(f"no tile for {n} in {cands}")


def _arb(n):
    return pltpu.CompilerParams(dimension_semantics=("arbitrary",) * n)


def _dot(a, b):
    return jnp.dot(a, b, preferred_element_type=F32)


def _dot_nt(a, b):
    return lax.dot_general(a, b, (((1,), (1,)), ((), ())), preferred_element_type=F32)


def _dot_tn(a, b):
    return lax.dot_general(a, b, (((0,), (0,)), ((), ())), preferred_element_type=F32)


def _split2(v):
    hi = v.astype(BF16)
    return hi, (v - hi.astype(F32)).astype(BF16)


def _dot_split2(v, m):
    hi, lo = _split2(v)
    return _dot(hi, m) + _dot(lo, m)


def _tri_dot3(m, v):
    h1 = v.astype(BF16)
    r1 = v - h1.astype(F32)
    h2 = r1.astype(BF16)
    h3 = (r1 - h2.astype(F32)).astype(BF16)
    return _dot(m, h1) + _dot(m, h2) + _dot(m, h3)


def _sigmoid(v):
    return 1.0 / (1.0 + jnp.exp(-v))


def _silu(v):
    return v * _sigmoid(v)


VMEM_BLOCK_BUDGET = 44 << 20


def _col_tiles(n):
    return [n // q for q in range(1, n // LANES + 1) if n % q == 0 and (n // q) % LANES == 0]


def _size(dt):
    return jnp.dtype(dt).itemsize


def _mm(pairs, out_dtypes, name, tm=512, extras=(), epilogue=None):
    m, n = pairs[0][0].shape[0], pairs[0][1].shape[1]
    tm = _pick(m, (tm, 256, 128, 64, 32, 16, 8))
    lhs = []
    for a, _ in pairs:
        if not any(a is x for x in lhs):
            lhs.append(a)
    odts = out_dtypes if epilogue is not None else [out_dtypes]
    n_acc = len(pairs) if epilogue is not None else 1
    per_col = (sum(b.shape[0] * _size(b.dtype) for _, b in pairs) * 2
               + tm * 2 * (sum(_size(d) for d in odts) + sum(_size(e.dtype) for e in extras)) + tm * 4 * n_acc)
    fixed = 2 * sum(tm * a.shape[1] * _size(a.dtype) for a in lhs)
    tn = next((c for c in _col_tiles(n) if fixed + per_col * c <= VMEM_BLOCK_BUDGET), LANES)
    n_l, n_p, n_e = len(lhs), len(pairs), len(extras)

    def body(*refs):
        l_refs, b_refs, e_refs, o_refs = refs[:n_l], refs[n_l:n_l + n_p], refs[n_l + n_p:n_l + n_p + n_e], refs[n_l + n_p + n_e:]
        vals = [r[...].astype(BF16) for r in l_refs]
        accs = []
        for (a, _), b_ref in zip(pairs, b_refs):
            av = vals[next(i for i, x in enumerate(lhs) if x is a)]
            accs.append(_dot(av, b_ref[...].astype(BF16)))
        if epilogue is None:
            outs = [sum(accs[1:], accs[0])]
        else:
            outs = epilogue(accs, [r[...] for r in e_refs])
        for o_ref, v, dt in zip(o_refs, outs, odts):
            o_ref[...] = v.astype(dt)

    tile = pl.BlockSpec((tm, tn), lambda j, i: (i, j))
    res = pl.pallas_call(
        body, grid=(n // tn, m // tm),
        in_specs=[pl.BlockSpec((tm, a.shape[1]), lambda j, i: (i, 0)) for a in lhs]
        + [pl.BlockSpec((b.shape[0], tn), lambda j, i: (0, j)) for _, b in pairs] + [tile] * n_e,
        out_specs=[tile] * len(odts), out_shape=[jax.ShapeDtypeStruct((m, n), d) for d in odts],
        name=name, compiler_params=_arb(2),
    )(*lhs, *[b for _, b in pairs], *extras)
    return res if epilogue is not None else res[0]


def _mm_tn(a, b, name):
    t, m = a.shape
    n = b.shape[1]

    def fits(tm, tn, tk):
        return 2 * (tk * tm * _size(a.dtype) + tk * tn * _size(b.dtype) + tm * tn * 4) <= VMEM_BLOCK_BUDGET

    tm, tn, tk = next((tm, tn, tk) for tn in _col_tiles(n) for tm in _col_tiles(m) if tm <= 1536
                      for tk in (1024, 512, 256, 128) if t % tk == 0 and fits(tm, tn, tk))

    def body(a_ref, b_ref, o_ref):
        @pl.when(pl.program_id(2) == 0)
        def _():
            o_ref[...] = jnp.zeros_like(o_ref)

        o_ref[...] += _dot_tn(a_ref[...].astype(BF16), b_ref[...].astype(BF16))

    return pl.pallas_call(
        body, grid=(m // tm, n // tn, t // tk),
        in_specs=[pl.BlockSpec((tk, tm), lambda i, j, k: (k, i)), pl.BlockSpec((tk, tn), lambda i, j, k: (k, j))],
        out_specs=pl.BlockSpec((tm, tn), lambda i, j, k: (i, j)),
        out_shape=jax.ShapeDtypeStruct((m, n), F32), name=name, compiler_params=_arb(3),
    )(a, b)


def _rowmap(fn, ins, outs, name, ts=512, products=(), swapped=()):
    bsz, seq = next(a.shape[:2] for a in ins if a.ndim == 3 and a.shape[1] != 1)
    ts = _pick(seq, (ts, 256, 128, 64, 32, 16, 8))
    n_given = len(ins)
    ins = list(ins) + [t for pair in products for t in pair]
    n_in = len(ins)

    def in_spec(a):
        if a.ndim == 2:
            return pl.BlockSpec(a.shape, lambda b, s: (0, 0))
        if a.shape[1] == 1:
            return pl.BlockSpec((1, 1, a.shape[2]), lambda b, s: (b, 0, 0))
        return pl.BlockSpec((1, ts, a.shape[2]), lambda b, s: (b, s, 0))

    def out_spec(kind, w):
        if kind == "row":
            return pl.BlockSpec((1, ts, w), lambda b, s: (b, s, 0))
        if kind == "batch":
            return pl.BlockSpec((1, 1, w), lambda b, s: (b, 0, 0))
        return pl.BlockSpec((1, w), lambda b, s: (0, 0))

    def out_shape(kind, w, dt):
        shp = {"row": (bsz, seq, w), "batch": (bsz, 1, w), "global": (1, w)}[kind]
        return jax.ShapeDtypeStruct(shp, dt)

    n_s, n_out = len(swapped), len(outs)

    def body(*refs):
        b, s = pl.program_id(0), pl.program_id(1)
        x_refs, o_refs = refs[n_in:n_in + n_s], refs[n_in + n_s:n_in + n_s + n_out]
        got_refs, sems = refs[n_in + n_s + n_out:n_in + 2 * n_s + n_out], refs[n_in + 2 * n_s + n_out:]
        if n_s:
            @pl.when((b == 0) & (s == 0))
            def _():
                _exchange_start(x_refs, got_refs, sems, False)

        vals = [r[...] if a.ndim == 2 else r[0] for r, a in zip(refs[:n_in], ins)]
        prods = [_dot(vals[t].astype(BF16), vals[t + 1].astype(BF16)) for t in range(n_given, n_in, 2)]
        res = fn(*vals[:n_given], *prods)
        if n_s:
            @pl.when((b == bsz - 1) & (s == seq // ts - 1))
            def _():
                _exchange_start(x_refs, got_refs, sems, False, wait=True)

        for o_ref, (kind, _, dt), v in zip(o_refs, outs, res):
            if kind == "row":
                o_ref[0] = v.astype(dt)
            elif kind == "batch":
                @pl.when(s == 0)
                def _():
                    o_ref[...] = jnp.zeros_like(o_ref)

                o_ref[0] += v
            else:
                @pl.when((s == 0) & (b == 0))
                def _():
                    o_ref[...] = jnp.zeros_like(o_ref)

                o_ref[...] += v

    res = pl.pallas_call(
        body, grid=(bsz, seq // ts), in_specs=[in_spec(a) for a in ins] + [ANY_SPEC] * n_s,
        out_specs=[out_spec(k, w) for k, w, _ in outs] + [ANY_SPEC] * n_s,
        out_shape=[out_shape(*o) for o in outs] + _exchange_shapes(swapped, False),
        scratch_shapes=_exchange_sems(n_s) if n_s else [], name=name, compiler_params=_arb(2),
    )(*ins, *swapped)
    return res if not n_s else (*res[:n_out], list(res[n_out:]))


def _small(fn, ins, out_shapes, name):
    n_in = len(ins)

    def body(*refs):
        res = fn(*[r[...] for r in refs[:n_in]])
        for o_ref, v in zip(refs[n_in:], res):
            o_ref[...] = v

    return pl.pallas_call(body, out_shape=[jax.ShapeDtypeStruct(s, d) for s, d in out_shapes], name=name)(*ins)


def _mesh_pos():
    mx, my, mc = lax.axis_index("x"), lax.axis_index("y"), lax.axis_index("c")
    return mx, my, mc, 4 * mx + 2 * my + mc


def _peer(mx, my, mc, k):
    px = 1 - mx if k & 4 else mx
    py = 1 - my if k & 2 else my
    pc = 1 - mc if k & 1 else mc
    return (px, py, pc), 4 * px + 2 * py + pc


ANY_SPEC = pl.BlockSpec(memory_space=pl.ANY)


def _exchange_shapes(arrs, gather):
    return [jax.ShapeDtypeStruct((N_DEV,) + tuple(x.shape if gather else x.shape[1:]), x.dtype) for x in arrs]


def _exchange_sems(n_arr):
    return [pltpu.SemaphoreType.DMA((n_arr, N_DEV)), pltpu.SemaphoreType.DMA((n_arr, N_DEV)),
            pltpu.SemaphoreType.DMA((n_arr,))]


def _exchange_start(x_refs, out_refs, sems, gather, wait=False):
    send_sems, recv_sems, local_sems = sems
    mx, my, mc, me = _mesh_pos()
    owns, sends, recvs = [], [], []
    for a, (x_ref, out_ref) in enumerate(zip(x_refs, out_refs)):
        owns.append(pltpu.make_async_copy(x_ref if gather else x_ref.at[me], out_ref.at[me], local_sems.at[a]))
        for k in range(1, N_DEV):
            peer, pid = _peer(mx, my, mc, k)
            src = x_ref if gather else x_ref.at[pid]
            sends.append(pltpu.make_async_remote_copy(
                src_ref=src, dst_ref=out_ref.at[me], send_sem=send_sems.at[a, k], recv_sem=recv_sems.at[a, k],
                device_id=peer, device_id_type=MESH))
            if wait:
                recvs.append(pltpu.make_async_remote_copy(
                    src_ref=src, dst_ref=out_ref.at[pid], send_sem=send_sems.at[a, k], recv_sem=recv_sems.at[a, k],
                    device_id=peer, device_id_type=MESH))
    if not wait:
        for cp in owns + sends:
            cp.start()
        return
    for cp in recvs:
        cp.wait_recv()
    for cp in sends:
        cp.wait_send()
    for cp in owns:
        cp.wait()


def _gather_by_chip(arrs, name):
    n_arr = len(arrs)

    def body(*refs):
        x_refs, out_refs = refs[:n_arr], refs[n_arr:2 * n_arr]
        send_sems, recv_sems, local_sems = refs[2 * n_arr:]
        mx, my, mc = lax.axis_index("x"), lax.axis_index("y"), lax.axis_index("c")
        me, sibling = (mx, my, mc), (mx, my, 1 - mc)
        chips = [(1 - mx, my), (mx, 1 - my), (1 - mx, 1 - my)]

        def copy(a, k, block, to, src=None):
            px, py, pc = block
            slot = out_refs[a].at[4 * px + 2 * py + pc]
            return pltpu.make_async_remote_copy(
                src_ref=slot if src is None else src, dst_ref=slot, send_sem=send_sems.at[a, k],
                recv_sem=recv_sems.at[a, k], device_id=to, device_id_type=MESH)

        owns = [pltpu.make_async_copy(x_refs[a], out_refs[a].at[4 * mx + 2 * my + mc], local_sems.at[a])
                for a in range(n_arr)]
        sent = [copy(a, 0, me, sibling, src=x_refs[a]) for a in range(n_arr)]
        sent += [copy(a, 1 + j, me, (*chip, mc), src=x_refs[a]) for a in range(n_arr) for j, chip in enumerate(chips)]
        for cp in owns + sent:
            cp.start()
        for a in range(n_arr):
            for j, chip in enumerate(chips):
                copy(a, 1 + j, (*chip, mc), me).wait_recv()
                passed = copy(a, 4 + j, (*chip, mc), sibling)
                passed.start()
                sent.append(passed)
        for a in range(n_arr):
            copy(a, 0, sibling, me).wait_recv()
            for j, chip in enumerate(chips):
                copy(a, 4 + j, (*chip, 1 - mc), me).wait_recv()
        for cp in sent:
            cp.wait_send()
        for cp in owns:
            cp.wait()

    return pl.pallas_call(
        body, out_shape=_exchange_shapes(arrs, True), in_specs=[ANY_SPEC] * n_arr, out_specs=[ANY_SPEC] * n_arr,
        scratch_shapes=_exchange_sems(n_arr), name=name,
    )(*arrs)


def _exchange(arrs, gather, name):
    n_arr = len(arrs)

    def body(*refs):
        x_refs, out_refs, sems = refs[:n_arr], refs[n_arr:2 * n_arr], refs[2 * n_arr:]
        _exchange_start(x_refs, out_refs, sems, gather)
        _exchange_start(x_refs, out_refs, sems, gather, wait=True)

    return pl.pallas_call(
        body, out_shape=_exchange_shapes(arrs, gather), in_specs=[ANY_SPEC] * n_arr, out_specs=[ANY_SPEC] * n_arr,
        scratch_shapes=_exchange_sems(n_arr), name=name,
    )(*arrs)


QK_SCALE = HEAD_DIM ** -0.5


def _low_lanes():
    return lax.broadcasted_iota(jnp.int32, (1, LANES), 1) < HEAD_DIM


def _stack_heads(v, scale=None):
    lo = _low_lanes()
    zero = jnp.zeros_like(v)
    s = jnp.concatenate([jnp.where(lo, v, zero), jnp.where(lo, zero, v)], axis=0)
    return s if scale is None else s * scale


def _stack_cols(v):
    return jnp.concatenate([v[:, 0:1], v[:, HEAD_DIM:HEAD_DIM + 1]], axis=0)


def _unstack(v, tq):
    return jnp.where(_low_lanes(), v[:tq], v[tq:])


def _tile_pos(tq, tk, q0):
    rows = lax.broadcasted_iota(jnp.int32, (2 * tq, tk), 0)
    cols = lax.broadcasted_iota(jnp.int32, (2 * tq, tk), 1)
    return q0 + jnp.where(rows >= tq, rows - tq, rows), cols, rows < tq


def _tri(tk, cmp):
    r = lax.broadcasted_iota(jnp.int32, (tk, tk), 0)
    c = lax.broadcasted_iota(jnp.int32, (tk, tk), 1)
    return jnp.where(cmp(r, c), 1.0, 0.0).astype(BF16)


def _softplus(z):
    return jnp.maximum(z, 0.0) + jnp.log(1.0 + jnp.exp(-jnp.abs(z)))


PREFIX_BLOCK = 256


def _running(v, tri, later):
    blk = tri.shape[0]
    nb = v.shape[1] // blk
    parts = [v[:, b * blk:(b + 1) * blk] for b in range(nb)]
    outs, run = [None] * nb, None
    for b in (reversed(range(nb)) if later else range(nb)):
        inside = _dot_split2(parts[b], tri)
        outs[b] = inside if run is None else inside + run
        total = jnp.sum(parts[b], axis=1, keepdims=True)
        run = total if run is None else run + total
    return (outs[0] if nb == 1 else jnp.concatenate(outs, axis=1)), run


def _pair_specs(tq, seq, np_, off):
    return [pl.BlockSpec((1, tq, LANES), lambda b, p, i: (b, i, off + p)),
            pl.BlockSpec((1, seq, LANES), lambda b, p, i: (b, 0, off + np_ + p)),
            pl.BlockSpec((1, seq, LANES), lambda b, p, i: (b, 0, off + 2 * np_ + p))]


def _key_tiles(tile, carry, q0, tq, tk, upward):
    nfull = q0 // tk
    edge = range(tq // tk)
    if upward:
        carry = lax.fori_loop(0, nfull, lambda j, cr: tile(pl.multiple_of(j * tk, tk), cr, False), carry)
        for jm in edge:
            carry = tile(pl.multiple_of(q0 + jm * tk, tk), carry, True)
        return carry
    for jm in reversed(edge):
        carry = tile(pl.multiple_of(q0 + jm * tk, tk), carry, True)
    return lax.fori_loop(0, nfull, lambda jj, cr: tile(pl.multiple_of((nfull - 1 - jj) * tk, tk), cr, False), carry)


def _grid_ends(bsz, np_, nq):
    b, p, i = pl.program_id(0), pl.program_id(1), pl.program_id(2)
    return (b == 0) & (p == 0) & (i == 0), (b == bsz - 1) & (p == np_ - 1) & (i == nq - 1)


def _sb_fwd(qkv, np_, tq, tk, name, gathered=()):
    bsz, seq, _ = qkv.shape
    n_g = len(gathered)

    def body(*refs):
        q_ref, k_ref, v_ref = refs[:3]
        x_refs, (y_ref, tot_ref) = refs[3:3 + n_g], refs[3 + n_g:5 + n_g]
        out_refs, sems = refs[5 + n_g:5 + 2 * n_g], refs[5 + 2 * n_g:]
        first, last = _grid_ends(bsz, np_, seq // tq)
        if n_g:
            @pl.when(first)
            def _():
                _exchange_start(x_refs, out_refs, sems, True)

        q0 = pl.program_id(2) * tq
        tpos, cols, _ = _tile_pos(tq, tk, q0)
        msuf = _tri(min(tk, PREFIX_BLOCK), lambda a, b: a > b)
        qs = _stack_heads(q_ref[0], QK_SCALE)

        def tile(k0, carry, masked):
            tot, acc = carry
            z = _dot_nt(qs, k_ref[0, pl.ds(k0, tk), :])
            sp = _softplus(z)
            if masked:
                seen = (k0 + cols) < tpos
                sp = jnp.where(seen, sp, 0.0)
            sp_after, sp_tot = _running(sp, msuf, later=True)
            logw = z - sp - sp_after - tot
            if masked:
                logw = jnp.where(seen, logw, NEG)
            return tot + sp_tot, acc + _dot(jnp.exp(logw).astype(BF16), v_ref[0, pl.ds(k0, tk), :])

        init = (jnp.zeros((2 * tq, 1), F32), jnp.zeros((2 * tq, LANES), F32))
        tot, acc = _key_tiles(tile, init, q0, tq, tk, upward=False)
        y_ref[0] = _unstack(acc, tq)
        tot_ref[0, 0] = _unstack(tot, tq)
        if n_g:
            @pl.when(last)
            def _():
                _exchange_start(x_refs, out_refs, sems, True, wait=True)

    y, tot, *got = pl.pallas_call(
        body, grid=(bsz, np_, seq // tq), in_specs=_pair_specs(tq, seq, np_, 0) + [ANY_SPEC] * n_g,
        out_specs=[pl.BlockSpec((1, tq, LANES), lambda b, p, i: (b, i, p)),
                   pl.BlockSpec((1, 1, tq, LANES), lambda b, p, i: (b, p, i, 0))] + [ANY_SPEC] * n_g,
        out_shape=[jax.ShapeDtypeStruct((bsz, seq, np_ * LANES), F32),
                   jax.ShapeDtypeStruct((bsz, np_, seq, LANES), F32)] + _exchange_shapes(gathered, True),
        scratch_shapes=_exchange_sems(n_g) if n_g else [],
        name=name, compiler_params=_arb(3),
    )(qkv, qkv, qkv, *gathered)
    return y, tot, got


def _sb_bwd(qkv, dy, tot, np_, tq, tk, name, swapped=()):
    bsz, seq, _ = qkv.shape
    nq = seq // tq
    n_s = len(swapped)

    def body(*refs):
        q_ref, k_ref, v_ref, dy_ref, tot_ref = refs[:5]
        x_refs, (dq_ref, dk_ref, dv_ref) = refs[5:5 + n_s], refs[5 + n_s:8 + n_s]
        out_refs = refs[8 + n_s:8 + 2 * n_s]
        dk_acc, dv_acc = refs[8 + 2 * n_s:10 + 2 * n_s]
        sems = refs[10 + 2 * n_s:]
        first, last = _grid_ends(bsz, np_, nq)
        if n_s:
            @pl.when(first)
            def _():
                _exchange_start(x_refs, out_refs, sems, False)

        i = pl.program_id(2)
        q0 = i * tq

        @pl.when(i == 0)
        def _():
            dk_acc[...] = jnp.zeros_like(dk_acc)
            dv_acc[...] = jnp.zeros_like(dv_acc)

        tpos, cols, _ = _tile_pos(tq, tk, q0)
        mincl = _tri(min(tk, PREFIX_BLOCK), lambda a, b: a <= b)
        mexcl = _tri(min(tk, PREFIX_BLOCK), lambda a, b: a < b)
        qs = _stack_heads(q_ref[0], QK_SCALE)
        dys = _stack_heads(dy_ref[0].astype(BF16))
        tots = _stack_cols(tot_ref[0, 0])

        def tile(k0, carry, masked):
            c_sp, c_g, dq = carry
            kb = k_ref[0, pl.ds(k0, tk), :]
            z = _dot_nt(qs, kb)
            sp = _softplus(z)
            if masked:
                seen = (k0 + cols) < tpos
                sp = jnp.where(seen, sp, 0.0)
            sp_upto, sp_tot = _running(sp, mincl, later=False)
            logw = z - sp - (tots - c_sp - sp_upto)
            if masked:
                logw = jnp.where(seen, logw, NEG)
            w = jnp.exp(logw)
            g = w * _dot_nt(dys, v_ref[0, pl.ds(k0, tk), :])
            g_before, g_tot = _running(g, mexcl, later=False)
            beta = jnp.exp(jnp.minimum(z - sp, 0.0))
            dz = g - beta * (g + c_g + g_before)
            if masked:
                dz = jnp.where(seen, dz, 0.0)
            dzb = dz.astype(BF16)
            dk_acc[pl.ds(k0, tk), :] += _dot_tn(dzb, qs)
            dv_acc[pl.ds(k0, tk), :] += _dot_tn(w.astype(BF16), dys)
            return c_sp + sp_tot, c_g + g_tot, dq + _dot(dzb, kb)

        zero = jnp.zeros((2 * tq, 1), F32)
        _, _, dq = _key_tiles(tile, (zero, zero, jnp.zeros((2 * tq, LANES), F32)), q0, tq, tk, upward=True)
        dq_ref[0] = (_unstack(dq, tq) * QK_SCALE).astype(BF16)

        @pl.when(i == nq - 1)
        def _():
            dk_ref[0] = dk_acc[...].astype(BF16)
            dv_ref[0] = dv_acc[...].astype(BF16)

        if n_s:
            @pl.when(last)
            def _():
                _exchange_start(x_refs, out_refs, sems, False, wait=True)

    tile_spec = pl.BlockSpec((1, tq, LANES), lambda b, p, i: (b, i, p))
    whole = pl.BlockSpec((1, seq, LANES), lambda b, p, i: (b, 0, p))
    out = jax.ShapeDtypeStruct((bsz, seq, np_ * LANES), BF16)
    dq, dk, dv, *got = pl.pallas_call(
        body, grid=(bsz, np_, nq),
        in_specs=_pair_specs(tq, seq, np_, 0) + [tile_spec, pl.BlockSpec((1, 1, tq, LANES), lambda b, p, i: (b, p, i, 0))]
        + [ANY_SPEC] * n_s,
        out_specs=[tile_spec, whole, whole] + [ANY_SPEC] * n_s,
        out_shape=[out, out, out] + _exchange_shapes(swapped, False),
        scratch_shapes=[pltpu.VMEM((seq, LANES), F32), pltpu.VMEM((seq, LANES), F32)] + (_exchange_sems(n_s) if n_s else []),
        name=name, compiler_params=_arb(3),
    )(qkv, qkv, qkv, dy, tot, *swapped)
    return dq, dk, dv, got


def _fox_fwd(qkv, cumk, np_, tq, tk, name):
    bsz, seq, _ = qkv.shape

    def body(q_ref, k_ref, v_ref, ck_ref, y_ref, lse_ref):
        q0 = pl.program_id(2) * tq
        tpos, cols, top = _tile_pos(tq, tk, q0)
        qs = _stack_heads(q_ref[0], QK_SCALE)

        def tile(k0, carry, masked):
            m, l, acc = carry
            ck = jnp.where(top, ck_ref[0, 0, 0:1, pl.ds(k0, tk)], ck_ref[0, 0, 1:2, pl.ds(k0, tk)])
            s = _dot_nt(qs, k_ref[0, pl.ds(k0, tk), :]) - ck
            if masked:
                s = jnp.where((k0 + cols) <= tpos, s, NEG)
            m_new = jnp.maximum(m, jnp.max(s, axis=1, keepdims=True))
            p = jnp.exp(s - m_new)
            alpha = jnp.exp(m - m_new)
            return (m_new, alpha * l + jnp.sum(p, axis=1, keepdims=True),
                    alpha * acc + _dot(p.astype(BF16), v_ref[0, pl.ds(k0, tk), :]))

        init = (jnp.full((2 * tq, 1), NEG, F32), jnp.zeros((2 * tq, 1), F32), jnp.zeros((2 * tq, LANES), F32))
        m, l, acc = _key_tiles(tile, init, q0, tq, tk, upward=True)
        y_ref[0] = _unstack(acc / l, tq)
        lse_ref[0, 0] = _unstack(m + jnp.log(l), tq)

    row4 = pl.BlockSpec((1, 1, tq, LANES), lambda b, p, i: (b, p, i, 0))
    return pl.pallas_call(
        body, grid=(bsz, np_, seq // tq),
        in_specs=_pair_specs(tq, seq, np_, 3 * np_) + [pl.BlockSpec((1, 1, 8, seq), lambda b, p, i: (b, p, 0, 0))],
        out_specs=[pl.BlockSpec((1, tq, LANES), lambda b, p, i: (b, i, p)), row4],
        out_shape=[jax.ShapeDtypeStruct((bsz, seq, np_ * LANES), F32),
                   jax.ShapeDtypeStruct((bsz, np_, seq, LANES), F32)],
        name=name, compiler_params=_arb(3),
    )(qkv, qkv, qkv, cumk)


def _fox_bwd(qkv, dy, y, lse, cumk, np_, tq, tk, name):
    bsz, seq, _ = qkv.shape
    nq = seq // tq

    def body(q_ref, k_ref, v_ref, dy_ref, y_ref, lse_ref, ck_ref,
             dq_ref, dk_ref, dv_ref, dck_ref, dcq_ref, dk_acc, dv_acc, dck_acc):
        i = pl.program_id(2)
        q0 = i * tq

        @pl.when(i == 0)
        def _():
            dk_acc[...] = jnp.zeros_like(dk_acc)
            dv_acc[...] = jnp.zeros_like(dv_acc)
            dck_acc[...] = jnp.zeros_like(dck_acc)

        tpos, cols, top = _tile_pos(tq, tk, q0)
        qs = _stack_heads(q_ref[0], QK_SCALE)
        dyf = dy_ref[0]
        dys = _stack_heads(dyf.astype(BF16))
        dyy = dyf * y_ref[0]
        lo = _low_lanes()
        delta = jnp.concatenate([jnp.sum(jnp.where(lo, dyy, 0.0), axis=1, keepdims=True),
                                 jnp.sum(jnp.where(lo, 0.0, dyy), axis=1, keepdims=True)], axis=0)
        lse_s = _stack_cols(lse_ref[0, 0])

        def tile(k0, carry, masked):
            dq, row = carry
            kb = k_ref[0, pl.ds(k0, tk), :]
            ck = jnp.where(top, ck_ref[0, 0, 0:1, pl.ds(k0, tk)], ck_ref[0, 0, 1:2, pl.ds(k0, tk)])
            s = _dot_nt(qs, kb) - ck
            if masked:
                s = jnp.where((k0 + cols) <= tpos, s, NEG)
            p = jnp.exp(s - lse_s)
            ds = p * (_dot_nt(dys, v_ref[0, pl.ds(k0, tk), :]) - delta)
            dsb = ds.astype(BF16)
            dk_acc[pl.ds(k0, tk), :] += _dot_tn(dsb, qs)
            dv_acc[pl.ds(k0, tk), :] += _dot_tn(p.astype(BF16), dys)
            dck_acc[0:1, pl.ds(k0, tk)] += -jnp.sum(ds[:tq], axis=0, keepdims=True)
            dck_acc[1:2, pl.ds(k0, tk)] += -jnp.sum(ds[tq:], axis=0, keepdims=True)
            return dq + _dot(dsb, kb), row + jnp.sum(ds, axis=1, keepdims=True)

        init = (jnp.zeros((2 * tq, LANES), F32), jnp.zeros((2 * tq, 1), F32))
        dq, row = _key_tiles(tile, init, q0, tq, tk, upward=True)
        dq_ref[0] = (_unstack(dq, tq) * QK_SCALE).astype(BF16)
        dcq_ref[0, 0] = _unstack(row, tq)

        @pl.when(i == nq - 1)
        def _():
            dk_ref[0] = dk_acc[...].astype(BF16)
            dv_ref[0] = dv_acc[...].astype(BF16)
            dck_ref[0, 0] = dck_acc[...]

    tile_spec = pl.BlockSpec((1, tq, LANES), lambda b, p, i: (b, i, p))
    whole = pl.BlockSpec((1, seq, LANES), lambda b, p, i: (b, 0, p))
    row4 = pl.BlockSpec((1, 1, tq, LANES), lambda b, p, i: (b, p, i, 0))
    key4 = pl.BlockSpec((1, 1, 8, seq), lambda b, p, i: (b, p, 0, 0))
    out = jax.ShapeDtypeStruct((bsz, seq, np_ * LANES), BF16)
    return pl.pallas_call(
        body, grid=(bsz, np_, nq),
        in_specs=_pair_specs(tq, seq, np_, 3 * np_) + [tile_spec, tile_spec, row4, key4],
        out_specs=[tile_spec, whole, whole, key4, row4],
        out_shape=[out, out, out, jax.ShapeDtypeStruct((bsz, np_, 8, seq), F32),
                   jax.ShapeDtypeStruct((bsz, np_, seq, LANES), F32)],
        scratch_shapes=[pltpu.VMEM((seq, LANES), F32), pltpu.VMEM((seq, LANES), F32), pltpu.VMEM((8, seq), F32)],
        name=name, compiler_params=_arb(3),
    )(qkv, qkv, qkv, dy, y, lse, cumk)


def _cum_fwd(fl, bf, name, tb=256):
    bsz, seq, _ = fl.shape
    tb = _pick(seq, (tb, 128))

    def body(fl_ref, bf_ref, o_ref):
        tri = _tri(tb, lambda a, b: b <= a)

        def step(j, carry):
            r0 = pl.multiple_of(j * tb, tb)
            blk = _tri_dot3(tri, -_softplus(-(fl_ref[0, pl.ds(r0, tb), :] + bf_ref[...]))) + carry
            o_ref[0, pl.ds(r0, tb), :] = blk
            return blk[tb - 1:tb, :]

        lax.fori_loop(0, seq // tb, step, jnp.zeros((1, LANES), F32))

    return pl.pallas_call(
        body, grid=(bsz,),
        in_specs=[pl.BlockSpec((1, seq, LANES), lambda b: (b, 0, 0)), pl.BlockSpec((1, LANES), lambda b: (0, 0))],
        out_specs=pl.BlockSpec((1, seq, LANES), lambda b: (b, 0, 0)),
        out_shape=jax.ShapeDtypeStruct(fl.shape, F32), name=name, compiler_params=_arb(1),
    )(fl, bf)


def _cum_bwd(dck, dcq, fl, bf, np_, name, tb=256):
    bsz, seq, _ = fl.shape
    tb = _pick(seq, (tb, 128))
    nb = seq // tb

    def body(dck_ref, dcq_ref, fl_ref, bf_ref, o_ref, db_ref):
        @pl.when(pl.program_id(0) == 0)
        def _():
            db_ref[...] = jnp.zeros_like(db_ref)

        tri = _tri(tb, lambda a, b: b >= a)
        lane = lax.broadcasted_iota(jnp.int32, (1, LANES), 1)

        def step(jj, carry):
            tail, tot = carry
            r0 = pl.multiple_of((nb - 1 - jj) * tb, tb)
            dc = dck_ref[0, pl.ds(r0, tb), :]
            for p in range(np_):
                pair = dcq_ref[0, p, pl.ds(r0, tb), :]
                dc = dc + jnp.where(lane == 2 * p, pair, 0.0) + jnp.where(lane == 2 * p + 1, pltpu.roll(pair, HEAD_DIM, 1), 0.0)
            dlf = _tri_dot3(tri, dc) + tail
            dfl = dlf * _sigmoid(-(fl_ref[0, pl.ds(r0, tb), :] + bf_ref[...]))
            o_ref[0, pl.ds(r0, tb), :] = dfl
            return dlf[0:1, :], tot + jnp.sum(dfl, axis=0, keepdims=True)

        zero = jnp.zeros((1, LANES), F32)
        _, tot = lax.fori_loop(0, nb, step, (zero, zero))
        db_ref[...] += tot

    whole = pl.BlockSpec((1, seq, LANES), lambda b: (b, 0, 0))
    vec = pl.BlockSpec((1, LANES), lambda b: (0, 0))
    return pl.pallas_call(
        body, grid=(bsz,), in_specs=[whole, pl.BlockSpec((1, np_, seq, LANES), lambda b: (b, 0, 0, 0)), whole, vec],
        out_specs=[whole, vec],
        out_shape=[jax.ShapeDtypeStruct(fl.shape, F32), jax.ShapeDtypeStruct((1, LANES), F32)],
        name=name, compiler_params=_arb(1),
    )(dck, dcq, fl, bf)


def _adamw_math(w, g, m, v):
    m = ADAM_B1 * m + (1.0 - ADAM_B1) * g
    v = ADAM_B2 * v + (1.0 - ADAM_B2) * (g * g)
    m_hat = m / (1.0 - ADAM_B1 ** ADAM_STEP)
    v_hat = v / (1.0 - ADAM_B2 ** ADAM_STEP)
    return -ADAM_LR * (m_hat / (jnp.sqrt(v_hat) + ADAM_EPS) + ADAM_WD * w), m, v


def _adamw(gparts, w, m, v, name, tr=512):
    nslots, rows, cols = gparts.shape
    tr = _pick(rows, (tr, 256, 128, 64, 32, 16, 8))

    def body(g_ref, w_ref, m_ref, v_ref, go_ref, d_ref, mo_ref, vo_ref):
        g = g_ref[0].astype(F32)
        for k in range(1, nslots):
            g = g + g_ref[k].astype(F32)
        go_ref[...] = g
        d_ref[...], mo_ref[...], vo_ref[...] = _adamw_math(w_ref[...], g, m_ref[...], v_ref[...])

    blk = pl.BlockSpec((tr, cols), lambda i: (i, 0))
    shp = jax.ShapeDtypeStruct((rows, cols), F32)
    return pl.pallas_call(
        body, grid=(rows // tr,), in_specs=[pl.BlockSpec((nslots, tr, cols), lambda i: (0, i, 0)), blk, blk, blk],
        out_specs=[blk] * 4, out_shape=[shp] * 4, name=name, compiler_params=_arb(1),
    )(gparts, w, m, v)


def _rows128(a):
    return a.reshape(-1, LANES)


def _pad_rows(a, mult):
    extra = (-a.shape[0]) % mult
    return a if extra == 0 else jnp.concatenate([a, jnp.zeros((extra, a.shape[1]), a.dtype)], axis=0)


def _pack(arrs, mult):
    return _pad_rows(jnp.concatenate([_rows128(a) for a in arrs], axis=0), mult)


def _unpack(flat, shapes):
    out, off = [], 0
    for shp in shapes:
        n = 1
        for s in shp:
            n *= s
        out.append(flat[off:off + n // LANES].reshape(shp))
        off += n // LANES
    return out


def _col_blocks(full):
    k, n = full.shape
    return full.reshape(k, N_DEV, n // N_DEV).transpose(1, 0, 2)


def _from_col_blocks(blocks):
    _, k, n = blocks.shape
    return blocks.transpose(1, 0, 2).reshape(k, N_DEV * n)


def kernel(x, c, w_ada, b_ada, g_mix, w_in, b_forget, b_gate, w_branch_sb, w_branch_fox, w_out, g_ffn, w_ffn_gate, w_ffn_up, w_ffn_down, g_final, loss_target, m_w_ada, m_b_ada, m_g_mix, m_w_in, m_b_forget, m_b_gate, m_w_branch_sb, m_w_branch_fox, m_w_out, m_g_ffn, m_w_ffn_gate, m_w_ffn_up, m_w_ffn_down, m_g_final, v_w_ada, v_b_ada, v_g_mix, v_w_in, v_b_forget, v_b_gate, v_w_branch_sb, v_w_branch_fox, v_w_out, v_g_ffn, v_w_ffn_gate, v_w_ffn_up, v_w_ffn_down, v_g_final):
    bsz, seq, d = x.shape
    tok = bsz * seq
    nh = b_forget.shape[-1]
    d_in = w_in.shape[-1] * N_DEV
    d_att = (d_in - nh - 2 * d) // 6
    assert d_att == nh * HEAD_DIM and nh % 2 == 0
    np_ = nh // 2
    d_ff = w_ffn_gate.shape[-1] * N_DEV
    n_mod = w_ada.shape[-1] * N_DEV // d
    me = 4 * lax.axis_index("x") + 2 * lax.axis_index("y") + lax.axis_index("c")
    tq_att = _pick(seq, (512, 256, 128))
    tk_att = tq_att
    tq_fox = tk_fox = tq_att

    big = [w_in[0], w_branch_sb[0], w_branch_fox[0], w_out[0], w_ffn_gate[0], w_ffn_up[0], w_ffn_down[0]]
    big_m = [m_w_in[0], m_w_branch_sb[0], m_w_branch_fox[0], m_w_out[0], m_w_ffn_gate[0], m_w_ffn_up[0], m_w_ffn_down[0]]
    big_v = [v_w_in[0], v_w_branch_sb[0], v_w_branch_fox[0], v_w_out[0], v_w_ffn_gate[0], v_w_ffn_up[0], v_w_ffn_down[0]]

    w_in_g, c_g = _gather_by_chip([big[0].astype(BF16), _rows128(c)], "gather_w_in_c")
    w_in_f = _from_col_blocks(w_in_g)
    w_qkv = w_in_f[:, :6 * d_att]
    w_f = jnp.concatenate([w_in_f[:, 6 * d_att:6 * d_att + nh], jnp.zeros((d, LANES - nh), BF16)], axis=1)
    w_gl = w_in_f[:, 6 * d_att + nh:]

    c_all = c_g.reshape(N_DEV * bsz, d)
    nb_all = N_DEV * bsz
    ada_cols = w_ada.shape[-1]
    b_ada_loc = lax.dynamic_slice(b_ada, (0, me * ada_cols), (1, ada_cols))

    def mod_fn(c_v, w_v, b_v):
        return [jnp.dot(_silu(c_v), w_v, precision=lax.Precision.HIGHEST, preferred_element_type=F32) + b_v]

    (mod_part,) = _small(mod_fn, [c_all, w_ada[0], b_ada_loc], [((nb_all, ada_cols), F32)], "ada_mod")
    mod_all = _from_col_blocks(_exchange([_rows128(mod_part)], True, "gather_mod")[0].reshape(N_DEV, nb_all, ada_cols))
    mod = lax.dynamic_slice(mod_all, (me * bsz, 0), (bsz, n_mod * d))
    shift1, scale1, gate1, shift2, scale2, gate2 = [mod[:, i * d:(i + 1) * d].reshape(bsz, 1, d) for i in range(6)]

    def norm_mod_fn(x_v, sc, sh, g):
        n = x_v * lax.rsqrt(jnp.mean(x_v * x_v, axis=-1, keepdims=True) + RMS_EPS) * g
        return [n * (1.0 + sc) + sh]

    (h,) = _rowmap(norm_mod_fn, [x, scale1, shift1, g_mix], [("row", d, BF16)], "norm1")
    h2d = h.reshape(tok, d)
    qkv = _mm([(h2d, w_qkv)], BF16, "proj_qkv").reshape(bsz, seq, 6 * d_att)
    gl = _mm([(h2d, w_gl)], F32, "proj_gates").reshape(bsz, seq, 2 * d)
    fl = _mm([(h2d, w_f)], F32, "proj_forget").reshape(bsz, seq, LANES)

    bf_pad = jnp.concatenate([b_forget, jnp.zeros((1, LANES - nh), F32)], axis=1)
    cum = _cum_fwd(fl, bf_pad, "cum_fwd")
    cumk = jnp.concatenate([cum[:, :, :nh].transpose(0, 2, 1).reshape(bsz, np_, 2, seq),
                            jnp.zeros((bsz, np_, 6, seq), F32)], axis=2)

    y_sb, tot_sb, gath = _sb_fwd(qkv, np_, tq_att, tk_att, "sb_fwd", gathered=[a.astype(BF16) for a in big[1:]])
    w_sb_f = _from_col_blocks(gath[0])
    w_fx_f = _from_col_blocks(gath[1])
    w_out_f = gath[2].reshape(d, d)
    w_g_f, w_u_f = _from_col_blocks(gath[3]), _from_col_blocks(gath[4])
    w_dn_f = gath[5].reshape(d_ff, d)
    y_fx, lse_fx = _fox_fwd(qkv, cumk, np_, tq_fox, tk_fox, "fox_fwd")

    def merge_fn(gl_v, bg, us, uf):
        gates = _sigmoid(gl_v + bg)
        return [gates[:, :d] * us + gates[:, d:] * uf, us, uf]

    merged, u_sb, u_fx = _rowmap(merge_fn, [gl, b_gate], [("row", d, BF16), ("row", d, F32), ("row", d, F32)], "merge",
                                 products=[(y_sb, w_sb_f), (y_fx, w_fx_f)])

    def resid_norm_fn(x_v, g1, sc, sh, g, mo_v):
        x1_v = x_v + g1 * mo_v
        n = x1_v * lax.rsqrt(jnp.mean(x1_v * x1_v, axis=-1, keepdims=True) + RMS_EPS) * g
        return [x1_v, n * (1.0 + sc) + sh, mo_v]

    x1, h2, mo = _rowmap(resid_norm_fn, [x, gate1, scale2, shift2, g_ffn],
                         [("row", d, F32), ("row", d, BF16), ("row", d, F32)], "norm2", products=[(merged, w_out_f)])
    h2_2d = h2.reshape(tok, d)

    def swiglu_fn(accs, _):
        a, u = accs
        return [_silu(a) * u, a, u]

    f, a_s, u_s = _mm([(h2_2d, w_g_f), (h2_2d, w_u_f)], [BF16, BF16, BF16], "ffn_in", epilogue=swiglu_fn)

    def head_fn(x1_v, g2, gf, tgt, ffn_v):
        x2 = x1_v + g2 * ffn_v
        rstd = lax.rsqrt(jnp.mean(x2 * x2, axis=-1, keepdims=True) + RMS_EPS)
        xh = x2 * rstd
        err = xh * gf - tgt
        loss_rows = 0.5 * jnp.mean(err * err, axis=-1, keepdims=True)
        dy = err * (1.0 / d)
        dxh = dy * gf
        dx2 = rstd * (dxh - xh * jnp.mean(dxh * xh, axis=-1, keepdims=True))
        return [dx2, dx2 * g2, jnp.sum(loss_rows, axis=0, keepdims=True) * jnp.ones((1, LANES), F32),
                jnp.sum(dy * xh, axis=0, keepdims=True), jnp.sum(dx2 * ffn_v, axis=0, keepdims=True)]

    dx2, dffn, loss_vec, dg_final, dgate2 = _rowmap(
        head_fn, [x1, gate2, g_final.reshape(1, d), loss_target],
        [("row", d, F32), ("row", d, BF16), ("global", LANES, F32), ("global", d, F32), ("batch", d, F32)], "head",
        products=[(f.reshape(bsz, seq, d_ff), w_dn_f)])

    dffn2d = dffn.reshape(tok, d)
    dw_dn = _mm_tn(f, dffn2d, "ffn_out_dw")

    def swiglu_bwd_fn(accs, saved):
        df_v, a, u = accs[0], saved[0].astype(F32), saved[1].astype(F32)
        sig = _sigmoid(a)
        return [df_v * u * sig * (1.0 + a * (1.0 - sig)), df_v * a * sig]

    da, du = _mm([(dffn2d, w_dn_f.T)], [BF16, BF16], "ffn_out_dx", extras=[a_s, u_s], epilogue=swiglu_bwd_fn)
    dw_gate, dw_up = _mm_tn(h2_2d, da, "ffn_gate_dw"), _mm_tn(h2_2d, du, "ffn_up_dw")
    def norm2_bwd_fn(x1_v, dx2_v, mo_v, sc, g1, g, dh_gate, dh_up):
        dh_v = dh_gate + dh_up
        rstd = lax.rsqrt(jnp.mean(x1_v * x1_v, axis=-1, keepdims=True) + RMS_EPS)
        xh = x1_v * rstd
        dn = dh_v * (1.0 + sc)
        dxh = dn * g
        dx1 = dx2_v + rstd * (dxh - xh * jnp.mean(dxh * xh, axis=-1, keepdims=True))
        return [dx1, dx1 * g1, jnp.sum(dh_v * (xh * g), axis=0, keepdims=True), jnp.sum(dh_v, axis=0, keepdims=True),
                jnp.sum(dn * xh, axis=0, keepdims=True), jnp.sum(dx1 * mo_v, axis=0, keepdims=True)]

    dx1, dmo, dscale2, dshift2, dg_ffn, dgate1 = _rowmap(
        norm2_bwd_fn, [x1, dx2, mo, scale2, gate1, g_ffn],
        [("row", d, F32), ("row", d, BF16), ("batch", d, F32), ("batch", d, F32), ("global", d, F32), ("batch", d, F32)],
        "norm2_bwd", ts=256,
        products=[(da.reshape(bsz, seq, d_ff), w_g_f.T), (du.reshape(bsz, seq, d_ff), w_u_f.T)])

    dmo2d = dmo.reshape(tok, d)
    dw_out = _mm_tn(merged.reshape(tok, d), dmo2d, "out_proj_dw")

    def merge_bwd_fn(gl_v, us, uf, bg, dm):
        gates = _sigmoid(gl_v + bg)
        gs, gf = gates[:, :d], gates[:, d:]
        dgl = jnp.concatenate([dm * us * gs * (1.0 - gs), dm * uf * gf * (1.0 - gf)], axis=1)
        return [dm * gs, dm * gf, dgl, jnp.sum(dgl, axis=0, keepdims=True)]

    du_sb, du_fx, dgl, db_gate = _rowmap(
        merge_bwd_fn, [gl, u_sb, u_fx, b_gate],
        [("row", d, BF16), ("row", d, BF16), ("row", 2 * d, BF16), ("global", 2 * d, F32)], "merge_bwd", ts=256,
        products=[(dmo, w_out_f.T)])
    du_sb2d, du_fx2d = du_sb.reshape(tok, d), du_fx.reshape(tok, d)
    dw_sb = _mm_tn(y_sb.reshape(tok, d_att), du_sb2d, "branch_sb_dw")
    dw_fx = _mm_tn(y_fx.reshape(tok, d_att), du_fx2d, "branch_fox_dw")
    dy_sb = _mm([(du_sb2d, w_sb_f.T)], F32, "branch_sb_dx").reshape(bsz, seq, d_att)
    dy_fx = _mm([(du_fx2d, w_fx_f.T)], F32, "branch_fox_dx").reshape(bsz, seq, d_att)

    blocks = [_col_blocks(dw_sb), _col_blocks(dw_fx), dw_out.reshape(N_DEV, d // N_DEV, d),
              _col_blocks(dw_gate), _col_blocks(dw_up), dw_dn.reshape(N_DEV, d_ff // N_DEV, d)]
    blocks = [b.astype(BF16) for b in blocks]
    dq_sb, dk_sb, dv_sb, got = _sb_bwd(qkv, dy_sb, tot_sb, np_, tq_att, tk_att, "sb_bwd", swapped=blocks)
    dq_fx, dk_fx, dv_fx, dck, dcq = _fox_bwd(qkv, dy_fx, y_fx, lse_fx, cumk, np_, tq_fox, tk_fox, "fox_bwd")
    dck_rows = dck[:, :, :2, :].reshape(bsz, nh, seq).transpose(0, 2, 1)
    dck_rows = jnp.concatenate([dck_rows, jnp.zeros((bsz, seq, LANES - nh), F32)], axis=2)
    dfl, db_f = _cum_bwd(dck_rows, dcq, fl, bf_pad, np_, "cum_bwd")

    dqkv = jnp.concatenate([dq_sb, dk_sb, dv_sb, dq_fx, dk_fx, dv_fx], axis=2).reshape(tok, 6 * d_att)
    dgl2d, dfl2d = dgl.reshape(tok, 2 * d), dfl.reshape(tok, LANES)
    dw_in = jnp.concatenate([_mm_tn(h2d, dqkv, "proj_qkv_dw"), _mm_tn(h2d, dfl2d, "proj_forget_dw")[:, :nh],
                             _mm_tn(h2d, dgl2d, "proj_gates_dw")], axis=1)
    def norm1_bwd_fn(x_v, dx1_v, sc, g, dh_qkv, dh_gl, dh_fl):
        dh_v = dh_qkv + dh_gl + dh_fl
        rstd = lax.rsqrt(jnp.mean(x_v * x_v, axis=-1, keepdims=True) + RMS_EPS)
        xh = x_v * rstd
        dn = dh_v * (1.0 + sc)
        dxh = dn * g
        dx = dx1_v + rstd * (dxh - xh * jnp.mean(dxh * xh, axis=-1, keepdims=True))
        return [dx, jnp.sum(dh_v * (xh * g), axis=0, keepdims=True), jnp.sum(dh_v, axis=0, keepdims=True),
                jnp.sum(dn * xh, axis=0, keepdims=True)]

    grad_x, dscale1, dshift1, dg_mix, got_in = _rowmap(
        norm1_bwd_fn, [x, dx1, scale1, g_mix],
        [("row", d, F32), ("batch", d, F32), ("batch", d, F32), ("global", d, F32)], "norm1_bwd", ts=256,
        products=[(dqkv.reshape(bsz, seq, 6 * d_att), w_qkv.T), (dgl, w_gl.T), (dfl, w_f.T)],
        swapped=[_col_blocks(dw_in).astype(BF16)])

    dmod = jnp.concatenate([dshift1, dscale1, dgate1, dshift2, dscale2, dgate2], axis=2).reshape(bsz, n_mod * d)
    partial = [dg_mix, db_f, db_gate, dg_ffn, dg_final]
    n_dmod_rows = bsz * n_mod * d // LANES
    small_sent = _pack([dmod] + partial + [loss_vec], 8)
    small_all = _exchange([small_sent], True, "gather_small")[0]
    small_w = [b_ada, g_mix, jnp.concatenate([b_forget, jnp.zeros((1, LANES - nh), F32)], axis=1), b_gate, g_ffn,
               g_final.reshape(1, d)]
    small_m = [m_b_ada, m_g_mix, jnp.concatenate([m_b_forget, jnp.zeros((1, LANES - nh), F32)], axis=1), m_b_gate,
               m_g_ffn, m_g_final.reshape(1, d)]
    small_v = [v_b_ada, v_g_mix, jnp.concatenate([v_b_forget, jnp.zeros((1, LANES - nh), F32)], axis=1), v_b_gate,
               v_g_ffn, v_g_final.reshape(1, d)]
    small_shapes = [a.shape for a in small_w]
    n_ada_rows = n_mod * d // LANES
    n_part_rows = sum(a.shape[1] // LANES for a in partial)
    sw, sm, sv = _pack(small_w, 8), _pack(small_m, 8), _pack(small_v, 8)
    n_small_rows = sw.shape[0]

    def small_fn(all_v, w_v, m_v, v_v):
        g_ada = None
        g_rest = None
        for k in range(N_DEV):
            for b in range(bsz):
                part = all_v[k, b * n_ada_rows:(b + 1) * n_ada_rows]
                g_ada = part if g_ada is None else g_ada + part
            rest = all_v[k, n_dmod_rows:n_dmod_rows + n_part_rows + 1]
            g_rest = rest if g_rest is None else g_rest + rest
        pieces = [g_ada, g_rest[:n_part_rows]]
        if n_small_rows > n_ada_rows + n_part_rows:
            pieces.append(jnp.zeros((n_small_rows - n_ada_rows - n_part_rows, LANES), F32))
        g = jnp.concatenate(pieces, axis=0)
        return [g, *_adamw_math(w_v, g, m_v, v_v), jnp.broadcast_to(g_rest[n_part_rows:], (8, LANES))]

    shp = ((n_small_rows, LANES), F32)
    *small_out, loss_all = _small(small_fn, [small_all, sw, sm, sv], [shp] * 4 + [((8, LANES), F32)], "small_update")
    loss = loss_all[0, 0]
    small_g, small_d, small_nm, small_nv = [_unpack(o, small_shapes) for o in small_out]

    def fix_small(lst):
        b_ada_o, g_mix_o, b_f_o, b_gate_o, g_ffn_o, g_final_o = lst
        return [b_ada_o, g_mix_o, b_f_o[:, :nh], b_gate_o, g_ffn_o, g_final_o.reshape(d)]

    small_g, small_d, small_nm, small_nv = [fix_small(l) for l in (small_g, small_d, small_nm, small_nv)]

    dmod_all = small_all[:, :n_dmod_rows].reshape(nb_all, n_mod * d)
    dmod_cols = lax.dynamic_slice(dmod_all, (0, me * ada_cols), (nb_all, ada_cols))

    def ada_dw_fn(c_v, dm_v):
        return [lax.dot_general(_silu(c_v), dm_v, (((0,), (0,)), ((), ())), precision=lax.Precision.HIGHEST,
                                preferred_element_type=F32)]

    (dw_ada,) = _small(ada_dw_fn, [c_all, dmod_cols], [((d, ada_cols), F32)], "ada_dw")
    ada_out = _adamw(dw_ada[None], w_ada[0], m_w_ada[0], v_w_ada[0], "adamw_ada")
    ada_g, ada_d, ada_nm, ada_nv = [o[None] for o in ada_out]

    got = got_in + list(got)
    names = ["w_in", "w_sb", "w_fox", "w_out", "w_gate", "w_up", "w_down"]
    big_out = [_adamw(g, w, m, v, "adamw_" + n, tr=256) for g, w, m, v, n in zip(got, big, big_m, big_v, names)]
    big_g, big_d, big_nm, big_nv = [[o[i][None] for o in big_out] for i in range(4)]

    def ordered(ada, small, bigs):
        b_ada_o, g_mix_o, b_f_o, b_gate_o, g_ffn_o, g_final_o = small
        w_in_o, w_sb_o, w_fx_o, w_out_o, w_gate_o, w_up_o, w_dn_o = bigs
        return [ada, b_ada_o, g_mix_o, w_in_o, b_f_o, b_gate_o, w_sb_o, w_fx_o, w_out_o, g_ffn_o, w_gate_o, w_up_o,
                w_dn_o, g_final_o]

    return (loss, grad_x, *ordered(ada_g, small_g, big_g), *ordered(ada_d, small_d, big_d),
            *ordered(ada_nm, small_nm, big_nm), *ordered(ada_nv, small_nv, big_nv))
```

```python
import jax
import jax.numpy as jnp
from jax import lax
from jax.experimental import pallas as pl
from jax.experimental.pallas import tpu as pltpu

F32 = jnp.float32
BF16 = jnp.bfloat16
HEAD_DIM = 64
LANES = 128
N_DEV = 8
RMS_EPS = 1e-6
ADAM_LR, ADAM_B1, ADAM_B2, ADAM_EPS, ADAM_WD, ADAM_STEP = 0.001, 0.9, 0.999, 1e-08, 0.01, 10
NEG = -1e30
MESH = pl.DeviceIdType.MESH


def _pick(n, cands):
    for c in cands:
        if n % c == 0:
            return c
    raise ValueError(f"no tile for {n} in {cands}")


def _arb(n):
    return pltpu.CompilerParams(dimension_semantics=("arbitrary",) * n)


def _dot(a, b):
    return jnp.dot(a, b, preferred_element_type=F32)


def _dot_nt(a, b):
    return lax.dot_general(a, b, (((1,), (1,)), ((), ())), preferred_element_type=F32)


def _dot_tn(a, b):
    return lax.dot_general(a, b, (((0,), (0,)), ((), ())), preferred_element_type=F32)


def _split2(v):
    hi = v.astype(BF16)
    return hi, (v - hi.astype(F32)).astype(BF16)


def _dot_split2(v, m):
    hi, lo = _split2(v)
    return _dot(hi, m) + _dot(lo, m)


def _tri_dot3(m, v):
    h1 = v.astype(BF16)
    r1 = v - h1.astype(F32)
    h2 = r1.astype(BF16)
    h3 = (r1 - h2.astype(F32)).astype(BF16)
    return _dot(m, h1) + _dot(m, h2) + _dot(m, h3)


def _sigmoid(v):
    return 1.0 / (1.0 + jnp.exp(-v))


def _silu(v):
    return v * _sigmoid(v)


VMEM_BLOCK_BUDGET = 44 << 20


def _col_tiles(n):
    return [n // q for q in range(1, n // LANES + 1) if n % q == 0 and (n // q) % LANES == 0]


def _size(dt):
    return jnp.dtype(dt).itemsize


def _mm(pairs, out_dtypes, name, tm=512, extras=(), epilogue=None):
    m, n = pairs[0][0].shape[0], pairs[0][1].shape[1]
    tm = _pick(m, (tm, 256, 128, 64, 32, 16, 8))
    lhs = []
    for a, _ in pairs:
        if not any(a is x for x in lhs):
            lhs.append(a)
    odts = out_dtypes if epilogue is not None else [out_dtypes]
    n_acc = len(pairs) if epilogue is not None else 1
    per_col = (sum(b.shape[0] * _size(b.dtype) for _, b in pairs) * 2
               + tm * 2 * (sum(_size(d) for d in odts) + sum(_size(e.dtype) for e in extras)) + tm * 4 * n_acc)
    fixed = 2 * sum(tm * a.shape[1] * _size(a.dtype) for a in lhs)
    tn = next((c for c in _col_tiles(n) if fixed + per_col * c <= VMEM_BLOCK_BUDGET), LANES)
    n_l, n_p, n_e = len(lhs), len(pairs), len(extras)

    def body(*refs):
        l_refs, b_refs, e_refs, o_refs = refs[:n_l], refs[n_l:n_l + n_p], refs[n_l + n_p:n_l + n_p + n_e], refs[n_l + n_p + n_e:]
        vals = [r[...].astype(BF16) for r in l_refs]
        accs = []
        for (a, _), b_ref in zip(pairs, b_refs):
            av = vals[next(i for i, x in enumerate(lhs) if x is a)]
            accs.append(_dot(av, b_ref[...].astype(BF16)))
        if epilogue is None:
            outs = [sum(accs[1:], accs[0])]
        else:
            outs = epilogue(accs, [r[...] for r in e_refs])
        for o_ref, v, dt in zip(o_refs, outs, odts):
            o_ref[...] = v.astype(dt)

    tile = pl.BlockSpec((tm, tn), lambda j, i: (i, j))
    res = pl.pallas_call(
        body, grid=(n // tn, m // tm),
        in_specs=[pl.BlockSpec((tm, a.shape[1]), lambda j, i: (i, 0)) for a in lhs]
        + [pl.BlockSpec((b.shape[0], tn), lambda j, i: (0, j)) for _, b in pairs] + [tile] * n_e,
        out_specs=[tile] * len(odts), out_shape=[jax.ShapeDtypeStruct((m, n), d) for d in odts],
        name=name, compiler_params=_arb(2),
    )(*lhs, *[b for _, b in pairs], *extras)
    return res if epilogue is not None else res[0]


def _mm_tn(a, b, name):
    t, m = a.shape
    n = b.shape[1]

    def fits(tm, tn, tk):
        return 2 * (tk * tm * _size(a.dtype) + tk * tn * _size(b.dtype) + tm * tn * 4) <= VMEM_BLOCK_BUDGET

    tm, tn, tk = next((tm, tn, tk) for tn in _col_tiles(n) for tm in _col_tiles(m) if tm <= 1536
                      for tk in (1024, 512, 256, 128) if t % tk == 0 and fits(tm, tn, tk))

    def body(a_ref, b_ref, o_ref):
        @pl.when(pl.program_id(2) == 0)
        def _():
            o_ref[...] = jnp.zeros_like(o_ref)

        o_ref[...] += _dot_tn(a_ref[...].astype(BF16), b_ref[...].astype(BF16))

    return pl.pallas_call(
        body, grid=(m // tm, n // tn, t // tk),
        in_specs=[pl.BlockSpec((tk, tm), lambda i, j, k: (k, i)), pl.BlockSpec((tk, tn), lambda i, j, k: (k, j))],
        out_specs=pl.BlockSpec((tm, tn), lambda i, j, k: (i, j)),
        out_shape=jax.ShapeDtypeStruct((m, n), F32), name=name, compiler_params=_arb(3),
    )(a, b)


def _rowmap(fn, ins, outs, name, ts=512, products=(), swapped=()):
    bsz, seq = next(a.shape[:2] for a in ins if a.ndim == 3 and a.shape[1] != 1)
    ts = _pick(seq, (ts, 256, 128, 64, 32, 16, 8))
    n_given = len(ins)
    ins = list(ins) + [t for pair in products for t in pair[:2]]
    w_rows = [len(pair) == 3 and pair[2] for pair in products]
    n_in = len(ins)

    def in_spec(a):
        if a.ndim == 2:
            return pl.BlockSpec(a.shape, lambda b, s: (0, 0))
        if a.shape[1] == 1:
            return pl.BlockSpec((1, 1, a.shape[2]), lambda b, s: (b, 0, 0))
        return pl.BlockSpec((1, ts, a.shape[2]), lambda b, s: (b, s, 0))

    def out_spec(kind, w):
        if kind == "row":
            return pl.BlockSpec((1, ts, w), lambda b, s: (b, s, 0))
        if kind == "batch":
            return pl.BlockSpec((1, 1, w), lambda b, s: (b, 0, 0))
        return pl.BlockSpec((1, w), lambda b, s: (0, 0))

    def out_shape(kind, w, dt):
        shp = {"row": (bsz, seq, w), "batch": (bsz, 1, w), "global": (1, w)}[kind]
        return jax.ShapeDtypeStruct(shp, dt)

    n_s, n_out = len(swapped), len(outs)

    def body(*refs):
        b, s = pl.program_id(0), pl.program_id(1)
        x_refs, o_refs = refs[n_in:n_in + n_s], refs[n_in + n_s:n_in + n_s + n_out]
        got_refs, sems = refs[n_in + n_s + n_out:n_in + 2 * n_s + n_out], refs[n_in + 2 * n_s + n_out:]
        if n_s:
            @pl.when((b == 0) & (s == 0))
            def _():
                _exchange_start(x_refs, got_refs, sems, False)

        vals = [r[...] if a.ndim == 2 else r[0] for r, a in zip(refs[:n_in], ins)]
        prods = [(_dot_nt if nt else _dot)(vals[t].astype(BF16), vals[t + 1].astype(BF16))
                 for t, nt in zip(range(n_given, n_in, 2), w_rows)]
        res = fn(*vals[:n_given], *prods)
        if n_s:
            @pl.when((b == bsz - 1) & (s == seq // ts - 1))
            def _():
                _exchange_start(x_refs, got_refs, sems, False, wait=True)

        for o_ref, (kind, _, dt), v in zip(o_refs, outs, res):
            if kind == "row":
                o_ref[0] = v.astype(dt)
            elif kind == "batch":
                @pl.when(s == 0)
                def _():
                    o_ref[...] = jnp.zeros_like(o_ref)

                o_ref[0] += v
            else:
                @pl.when((s == 0) & (b == 0))
                def _():
                    o_ref[...] = jnp.zeros_like(o_ref)

                o_ref[...] += v

    res = pl.pallas_call(
        body, grid=(bsz, seq // ts), in_specs=[in_spec(a) for a in ins] + [ANY_SPEC] * n_s,
        out_specs=[out_spec(k, w) for k, w, _ in outs] + [ANY_SPEC] * n_s,
        out_shape=[out_shape(*o) for o in outs] + _exchange_shapes(swapped, False),
        scratch_shapes=_exchange_sems(n_s) if n_s else [], name=name, compiler_params=_arb(2),
    )(*ins, *swapped)
    return res if not n_s else (*res[:n_out], list(res[n_out:]))


def _small(fn, ins, out_shapes, name):
    n_in = len(ins)

    def body(*refs):
        res = fn(*[r[...] for r in refs[:n_in]])
        for o_ref, v in zip(refs[n_in:], res):
            o_ref[...] = v

    return pl.pallas_call(body, out_shape=[jax.ShapeDtypeStruct(s, d) for s, d in out_shapes], name=name)(*ins)


def _mesh_pos():
    mx, my, mc = lax.axis_index("x"), lax.axis_index("y"), lax.axis_index("c")
    return mx, my, mc, 4 * mx + 2 * my + mc


def _peer(mx, my, mc, k):
    px = 1 - mx if k & 4 else mx
    py = 1 - my if k & 2 else my
    pc = 1 - mc if k & 1 else mc
    return (px, py, pc), 4 * px + 2 * py + pc


ANY_SPEC = pl.BlockSpec(memory_space=pl.ANY)


def _exchange_shapes(arrs, gather):
    return [jax.ShapeDtypeStruct((N_DEV,) + tuple(x.shape if gather else x.shape[1:]), x.dtype) for x in arrs]


def _exchange_sems(n_arr):
    return [pltpu.SemaphoreType.DMA((n_arr, N_DEV)), pltpu.SemaphoreType.DMA((n_arr, N_DEV)),
            pltpu.SemaphoreType.DMA((n_arr,))]


def _exchange_start(x_refs, out_refs, sems, gather, wait=False):
    send_sems, recv_sems, local_sems = sems
    mx, my, mc, me = _mesh_pos()
    owns, sends, recvs = [], [], []
    for a, (x_ref, out_ref) in enumerate(zip(x_refs, out_refs)):
        owns.append(pltpu.make_async_copy(x_ref if gather else x_ref.at[me], out_ref.at[me], local_sems.at[a]))
        for k in range(1, N_DEV):
            peer, pid = _peer(mx, my, mc, k)
            src = x_ref if gather else x_ref.at[pid]
            sends.append(pltpu.make_async_remote_copy(
                src_ref=src, dst_ref=out_ref.at[me], send_sem=send_sems.at[a, k], recv_sem=recv_sems.at[a, k],
                device_id=peer, device_id_type=MESH))
            if wait:
                recvs.append(pltpu.make_async_remote_copy(
                    src_ref=src, dst_ref=out_ref.at[pid], send_sem=send_sems.at[a, k], recv_sem=recv_sems.at[a, k],
                    device_id=peer, device_id_type=MESH))
    if not wait:
        for cp in owns + sends:
            cp.start()
        return
    for cp in recvs:
        cp.wait_recv()
    for cp in sends:
        cp.wait_send()
    for cp in owns:
        cp.wait()


def _gather_by_chip(arrs, name):
    n_arr = len(arrs)

    def body(*refs):
        x_refs, out_refs = refs[:n_arr], refs[n_arr:2 * n_arr]
        send_sems, recv_sems, local_sems = refs[2 * n_arr:]
        mx, my, mc = lax.axis_index("x"), lax.axis_index("y"), lax.axis_index("c")
        me, sibling = (mx, my, mc), (mx, my, 1 - mc)
        chips = [(1 - mx, my), (mx, 1 - my), (1 - mx, 1 - my)]

        def copy(a, k, block, to, src=None):
            px, py, pc = block
            slot = out_refs[a].at[4 * px + 2 * py + pc]
            return pltpu.make_async_remote_copy(
                src_ref=slot if src is None else src, dst_ref=slot, send_sem=send_sems.at[a, k],
                recv_sem=recv_sems.at[a, k], device_id=to, device_id_type=MESH)

        owns = [pltpu.make_async_copy(x_refs[a], out_refs[a].at[4 * mx + 2 * my + mc], local_sems.at[a])
                for a in range(n_arr)]
        sent = [copy(a, 0, me, sibling, src=x_refs[a]) for a in range(n_arr)]
        sent += [copy(a, 1 + j, me, (*chip, mc), src=x_refs[a]) for a in range(n_arr) for j, chip in enumerate(chips)]
        for cp in owns + sent:
            cp.start()
        for a in range(n_arr):
            for j, chip in enumerate(chips):
                copy(a, 1 + j, (*chip, mc), me).wait_recv()
                passed = copy(a, 4 + j, (*chip, mc), sibling)
                passed.start()
                sent.append(passed)
        for a in range(n_arr):
            copy(a, 0, sibling, me).wait_recv()
            for j, chip in enumerate(chips):
                copy(a, 4 + j, (*chip, 1 - mc), me).wait_recv()
        for cp in sent:
            cp.wait_send()
        for cp in owns:
            cp.wait()

    return pl.pallas_call(
        body, out_shape=_exchange_shapes(arrs, True), in_specs=[ANY_SPEC] * n_arr, out_specs=[ANY_SPEC] * n_arr,
        scratch_shapes=_exchange_sems(n_arr), name=name,
    )(*arrs)


def _exchange(arrs, gather, name):
    n_arr = len(arrs)

    def body(*refs):
        x_refs, out_refs, sems = refs[:n_arr], refs[n_arr:2 * n_arr], refs[2 * n_arr:]
        _exchange_start(x_refs, out_refs, sems, gather)
        _exchange_start(x_refs, out_refs, sems, gather, wait=True)

    return pl.pallas_call(
        body, out_shape=_exchange_shapes(arrs, gather), in_specs=[ANY_SPEC] * n_arr, out_specs=[ANY_SPEC] * n_arr,
        scratch_shapes=_exchange_sems(n_arr), name=name,
    )(*arrs)


QK_SCALE = HEAD_DIM ** -0.5


def _low_lanes():
    return lax.broadcasted_iota(jnp.int32, (1, LANES), 1) < HEAD_DIM


def _stack_heads(v, scale=None):
    lo = _low_lanes()
    zero = jnp.zeros_like(v)
    s = jnp.concatenate([jnp.where(lo, v, zero), jnp.where(lo, zero, v)], axis=0)
    return s if scale is None else s * scale


def _stack_cols(v):
    return jnp.concatenate([v[:, 0:1], v[:, HEAD_DIM:HEAD_DIM + 1]], axis=0)


def _unstack(v, tq):
    return jnp.where(_low_lanes(), v[:tq], v[tq:])


def _tile_pos(tq, tk, q0):
    rows = lax.broadcasted_iota(jnp.int32, (2 * tq, tk), 0)
    cols = lax.broadcasted_iota(jnp.int32, (2 * tq, tk), 1)
    return q0 + jnp.where(rows >= tq, rows - tq, rows), cols, rows < tq


def _tri(tk, cmp):
    r = lax.broadcasted_iota(jnp.int32, (tk, tk), 0)
    c = lax.broadcasted_iota(jnp.int32, (tk, tk), 1)
    return jnp.where(cmp(r, c), 1.0, 0.0).astype(BF16)


def _softplus(z):
    return jnp.maximum(z, 0.0) + jnp.log(1.0 + jnp.exp(-jnp.abs(z)))


PREFIX_BLOCK = 256


def _running(v, tri, later):
    blk = tri.shape[0]
    nb = v.shape[1] // blk
    parts = [v[:, b * blk:(b + 1) * blk] for b in range(nb)]
    outs, run = [None] * nb, None
    for b in (reversed(range(nb)) if later else range(nb)):
        inside = _dot_split2(parts[b], tri)
        outs[b] = inside if run is None else inside + run
        total = jnp.sum(parts[b], axis=1, keepdims=True)
        run = total if run is None else run + total
    return (outs[0] if nb == 1 else jnp.concatenate(outs, axis=1)), run


def _pair_specs(tq, seq, np_, off):
    return [pl.BlockSpec((1, tq, LANES), lambda b, p, i: (b, i, off + p)),
            pl.BlockSpec((1, seq, LANES), lambda b, p, i: (b, 0, off + np_ + p)),
            pl.BlockSpec((1, seq, LANES), lambda b, p, i: (b, 0, off + 2 * np_ + p))]


def _key_tiles(tile, carry, q0, tq, tk, upward):
    nfull = q0 // tk
    edge = range(tq // tk)
    if upward:
        carry = lax.fori_loop(0, nfull, lambda j, cr: tile(pl.multiple_of(j * tk, tk), cr, False), carry)
        for jm in edge:
            carry = tile(pl.multiple_of(q0 + jm * tk, tk), carry, True)
        return carry
    for jm in reversed(edge):
        carry = tile(pl.multiple_of(q0 + jm * tk, tk), carry, True)
    return lax.fori_loop(0, nfull, lambda jj, cr: tile(pl.multiple_of((nfull - 1 - jj) * tk, tk), cr, False), carry)


def _grid_ends(bsz, np_, nq):
    b, p, i = pl.program_id(0), pl.program_id(1), pl.program_id(2)
    return (b == 0) & (p == 0) & (i == 0), (b == bsz - 1) & (p == np_ - 1) & (i == nq - 1)


def _sb_fwd(qkv, np_, tq, tk, name, gathered=()):
    bsz, seq, _ = qkv.shape
    n_g = len(gathered)

    def body(*refs):
        q_ref, k_ref, v_ref = refs[:3]
        x_refs, (y_ref, tot_ref) = refs[3:3 + n_g], refs[3 + n_g:5 + n_g]
        out_refs, sems = refs[5 + n_g:5 + 2 * n_g], refs[5 + 2 * n_g:]
        first, last = _grid_ends(bsz, np_, seq // tq)
        if n_g:
            @pl.when(first)
            def _():
                _exchange_start(x_refs, out_refs, sems, True)

        q0 = pl.program_id(2) * tq
        tpos, cols, _ = _tile_pos(tq, tk, q0)
        msuf = _tri(min(tk, PREFIX_BLOCK), lambda a, b: a > b)
        qs = _stack_heads(q_ref[0], QK_SCALE)

        def tile(k0, carry, masked):
            tot, acc = carry
            z = _dot_nt(qs, k_ref[0, pl.ds(k0, tk), :])
            sp = _softplus(z)
            if masked:
                seen = (k0 + cols) < tpos
                sp = jnp.where(seen, sp, 0.0)
            sp_after, sp_tot = _running(sp, msuf, later=True)
            logw = z - sp - sp_after - tot
            if masked:
                logw = jnp.where(seen, logw, NEG)
            return tot + sp_tot, acc + _dot(jnp.exp(logw).astype(BF16), v_ref[0, pl.ds(k0, tk), :])

        init = (jnp.zeros((2 * tq, 1), F32), jnp.zeros((2 * tq, LANES), F32))
        tot, acc = _key_tiles(tile, init, q0, tq, tk, upward=False)
        y_ref[0] = _unstack(acc, tq)
        tot_ref[0, 0] = _unstack(tot, tq)
        if n_g:
            @pl.when(last)
            def _():
                _exchange_start(x_refs, out_refs, sems, True, wait=True)

    y, tot, *got = pl.pallas_call(
        body, grid=(bsz, np_, seq // tq), in_specs=_pair_specs(tq, seq, np_, 0) + [ANY_SPEC] * n_g,
        out_specs=[pl.BlockSpec((1, tq, LANES), lambda b, p, i: (b, i, p)),
                   pl.BlockSpec((1, 1, tq, LANES), lambda b, p, i: (b, p, i, 0))] + [ANY_SPEC] * n_g,
        out_shape=[jax.ShapeDtypeStruct((bsz, seq, np_ * LANES), F32),
                   jax.ShapeDtypeStruct((bsz, np_, seq, LANES), F32)] + _exchange_shapes(gathered, True),
        scratch_shapes=_exchange_sems(n_g) if n_g else [],
        name=name, compiler_params=_arb(3),
    )(qkv, qkv, qkv, *gathered)
    return y, tot, got


def _sb_bwd(qkv, dy, tot, np_, tq, tk, name, swapped=()):
    bsz, seq, _ = qkv.shape
    nq = seq // tq
    n_s = len(swapped)

    def body(*refs):
        q_ref, k_ref, v_ref, dy_ref, tot_ref = refs[:5]
        x_refs, (dq_ref, dk_ref, dv_ref) = refs[5:5 + n_s], refs[5 + n_s:8 + n_s]
        out_refs = refs[8 + n_s:8 + 2 * n_s]
        dk_acc, dv_acc = refs[8 + 2 * n_s:10 + 2 * n_s]
        sems = refs[10 + 2 * n_s:]
        first, last = _grid_ends(bsz, np_, nq)
        if n_s:
            @pl.when(first)
            def _():
                _exchange_start(x_refs, out_refs, sems, False)

        i = pl.program_id(2)
        q0 = i * tq

        @pl.when(i == 0)
        def _():
            dk_acc[...] = jnp.zeros_like(dk_acc)
            dv_acc[...] = jnp.zeros_like(dv_acc)

        tpos, cols, _ = _tile_pos(tq, tk, q0)
        mincl = _tri(min(tk, PREFIX_BLOCK), lambda a, b: a <= b)
        mexcl = _tri(min(tk, PREFIX_BLOCK), lambda a, b: a < b)
        qs = _stack_heads(q_ref[0], QK_SCALE)
        dys = _stack_heads(dy_ref[0].astype(BF16))
        tots = _stack_cols(tot_ref[0, 0])

        def tile(k0, carry, masked):
            c_sp, c_g, dq = carry
            kb = k_ref[0, pl.ds(k0, tk), :]
            z = _dot_nt(qs, kb)
            sp = _softplus(z)
            if masked:
                seen = (k0 + cols) < tpos
                sp = jnp.where(seen, sp, 0.0)
            sp_upto, sp_tot = _running(sp, mincl, later=False)
            logw = z - sp - (tots - c_sp - sp_upto)
            if masked:
                logw = jnp.where(seen, logw, NEG)
            w = jnp.exp(logw)
            g = w * _dot_nt(dys, v_ref[0, pl.ds(k0, tk), :])
            g_before, g_tot = _running(g, mexcl, later=False)
            beta = jnp.exp(jnp.minimum(z - sp, 0.0))
            dz = g - beta * (g + c_g + g_before)
            if masked:
                dz = jnp.where(seen, dz, 0.0)
            dzb = dz.astype(BF16)
            dk_acc[pl.ds(k0, tk), :] += _dot_tn(dzb, qs)
            dv_acc[pl.ds(k0, tk), :] += _dot_tn(w.astype(BF16), dys)
            return c_sp + sp_tot, c_g + g_tot, dq + _dot(dzb, kb)

        zero = jnp.zeros((2 * tq, 1), F32)
        _, _, dq = _key_tiles(tile, (zero, zero, jnp.zeros((2 * tq, LANES), F32)), q0, tq, tk, upward=True)
        dq_ref[0] = (_unstack(dq, tq) * QK_SCALE).astype(BF16)

        @pl.when(i == nq - 1)
        def _():
            dk_ref[0] = dk_acc[...].astype(BF16)
            dv_ref[0] = dv_acc[...].astype(BF16)

        if n_s:
            @pl.when(last)
            def _():
                _exchange_start(x_refs, out_refs, sems, False, wait=True)

    tile_spec = pl.BlockSpec((1, tq, LANES), lambda b, p, i: (b, i, p))
    whole = pl.BlockSpec((1, seq, LANES), lambda b, p, i: (b, 0, p))
    out = jax.ShapeDtypeStruct((bsz, seq, np_ * LANES), BF16)
    dq, dk, dv, *got = pl.pallas_call(
        body, grid=(bsz, np_, nq),
        in_specs=_pair_specs(tq, seq, np_, 0) + [tile_spec, pl.BlockSpec((1, 1, tq, LANES), lambda b, p, i: (b, p, i, 0))]
        + [ANY_SPEC] * n_s,
        out_specs=[tile_spec, whole, whole] + [ANY_SPEC] * n_s,
        out_shape=[out, out, out] + _exchange_shapes(swapped, False),
        scratch_shapes=[pltpu.VMEM((seq, LANES), F32), pltpu.VMEM((seq, LANES), F32)] + (_exchange_sems(n_s) if n_s else []),
        name=name, compiler_params=_arb(3),
    )(qkv, qkv, qkv, dy, tot, *swapped)
    return dq, dk, dv, got


def _fox_fwd(qkv, cumk, np_, tq, tk, name):
    bsz, seq, _ = qkv.shape

    def body(q_ref, k_ref, v_ref, ck_ref, y_ref, lse_ref):
        q0 = pl.program_id(2) * tq
        tpos, cols, top = _tile_pos(tq, tk, q0)
        qs = _stack_heads(q_ref[0], QK_SCALE)

        def tile(k0, carry, masked):
            m, l, acc = carry
            ck = jnp.where(top, ck_ref[0, 0, 0:1, pl.ds(k0, tk)], ck_ref[0, 0, 1:2, pl.ds(k0, tk)])
            s = _dot_nt(qs, k_ref[0, pl.ds(k0, tk), :]) - ck
            if masked:
                s = jnp.where((k0 + cols) <= tpos, s, NEG)
            m_new = jnp.maximum(m, jnp.max(s, axis=1, keepdims=True))
            p = jnp.exp(s - m_new)
            alpha = jnp.exp(m - m_new)
            return (m_new, alpha * l + jnp.sum(p, axis=1, keepdims=True),
                    alpha * acc + _dot(p.astype(BF16), v_ref[0, pl.ds(k0, tk), :]))

        init = (jnp.full((2 * tq, 1), NEG, F32), jnp.zeros((2 * tq, 1), F32), jnp.zeros((2 * tq, LANES), F32))
        m, l, acc = _key_tiles(tile, init, q0, tq, tk, upward=True)
        y_ref[0] = _unstack(acc / l, tq)
        lse_ref[0, 0] = _unstack(m + jnp.log(l), tq)

    row4 = pl.BlockSpec((1, 1, tq, LANES), lambda b, p, i: (b, p, i, 0))
    return pl.pallas_call(
        body, grid=(bsz, np_, seq // tq),
        in_specs=_pair_specs(tq, seq, np_, 3 * np_) + [pl.BlockSpec((1, 1, 8, seq), lambda b, p, i: (b, p, 0, 0))],
        out_specs=[pl.BlockSpec((1, tq, LANES), lambda b, p, i: (b, i, p)), row4],
        out_shape=[jax.ShapeDtypeStruct((bsz, seq, np_ * LANES), F32),
                   jax.ShapeDtypeStruct((bsz, np_, seq, LANES), F32)],
        name=name, compiler_params=_arb(3),
    )(qkv, qkv, qkv, cumk)


def _fox_bwd(qkv, dy, y, lse, cumk, np_, tq, tk, name):
    bsz, seq, _ = qkv.shape
    nq = seq // tq

    def body(q_ref, k_ref, v_ref, dy_ref, y_ref, lse_ref, ck_ref,
             dq_ref, dk_ref, dv_ref, dck_ref, dcq_ref, dk_acc, dv_acc, dck_acc):
        i = pl.program_id(2)
        q0 = i * tq

        @pl.when(i == 0)
        def _():
            dk_acc[...] = jnp.zeros_like(dk_acc)
            dv_acc[...] = jnp.zeros_like(dv_acc)
            dck_acc[...] = jnp.zeros_like(dck_acc)

        tpos, cols, top = _tile_pos(tq, tk, q0)
        qs = _stack_heads(q_ref[0], QK_SCALE)
        dyf = dy_ref[0]
        dys = _stack_heads(dyf.astype(BF16))
        dyy = dyf * y_ref[0]
        lo = _low_lanes()
        delta = jnp.concatenate([jnp.sum(jnp.where(lo, dyy, 0.0), axis=1, keepdims=True),
                                 jnp.sum(jnp.where(lo, 0.0, dyy), axis=1, keepdims=True)], axis=0)
        lse_s = _stack_cols(lse_ref[0, 0])

        def tile(k0, carry, masked):
            dq, row = carry
            kb = k_ref[0, pl.ds(k0, tk), :]
            ck = jnp.where(top, ck_ref[0, 0, 0:1, pl.ds(k0, tk)], ck_ref[0, 0, 1:2, pl.ds(k0, tk)])
            s = _dot_nt(qs, kb) - ck
            if masked:
                s = jnp.where((k0 + cols) <= tpos, s, NEG)
            p = jnp.exp(s - lse_s)
            ds = p * (_dot_nt(dys, v_ref[0, pl.ds(k0, tk), :]) - delta)
            dsb = ds.astype(BF16)
            dk_acc[pl.ds(k0, tk), :] += _dot_tn(dsb, qs)
            dv_acc[pl.ds(k0, tk), :] += _dot_tn(p.astype(BF16), dys)
            dck_acc[0:1, pl.ds(k0, tk)] += -jnp.sum(ds[:tq], axis=0, keepdims=True)
            dck_acc[1:2, pl.ds(k0, tk)] += -jnp.sum(ds[tq:], axis=0, keepdims=True)
            return dq + _dot(dsb, kb), row + jnp.sum(ds, axis=1, keepdims=True)

        init = (jnp.zeros((2 * tq, LANES), F32), jnp.zeros((2 * tq, 1), F32))
        dq, row = _key_tiles(tile, init, q0, tq, tk, upward=True)
        dq_ref[0] = (_unstack(dq, tq) * QK_SCALE).astype(BF16)
        dcq_ref[0, 0] = _unstack(row, tq)

        @pl.when(i == nq - 1)
        def _():
            dk_ref[0] = dk_acc[...].astype(BF16)
            dv_ref[0] = dv_acc[...].astype(BF16)
            dck_ref[0, 0] = dck_acc[...]

    tile_spec = pl.BlockSpec((1, tq, LANES), lambda b, p, i: (b, i, p))
    whole = pl.BlockSpec((1, seq, LANES), lambda b, p, i: (b, 0, p))
    row4 = pl.BlockSpec((1, 1, tq, LANES), lambda b, p, i: (b, p, i, 0))
    key4 = pl.BlockSpec((1, 1, 8, seq), lambda b, p, i: (b, p, 0, 0))
    out = jax.ShapeDtypeStruct((bsz, seq, np_ * LANES), BF16)
    return pl.pallas_call(
        body, grid=(bsz, np_, nq),
        in_specs=_pair_specs(tq, seq, np_, 3 * np_) + [tile_spec, tile_spec, row4, key4],
        out_specs=[tile_spec, whole, whole, key4, row4],
        out_shape=[out, out, out, jax.ShapeDtypeStruct((bsz, np_, 8, seq), F32),
                   jax.ShapeDtypeStruct((bsz, np_, seq, LANES), F32)],
        scratch_shapes=[pltpu.VMEM((seq, LANES), F32), pltpu.VMEM((seq, LANES), F32), pltpu.VMEM((8, seq), F32)],
        name=name, compiler_params=_arb(3),
    )(qkv, qkv, qkv, dy, y, lse, cumk)


def _cum_fwd(fl, bf, name, tb=256):
    bsz, seq, _ = fl.shape
    tb = _pick(seq, (tb, 128))

    def body(fl_ref, bf_ref, o_ref):
        tri = _tri(tb, lambda a, b: b <= a)

        def step(j, carry):
            r0 = pl.multiple_of(j * tb, tb)
            blk = _tri_dot3(tri, -_softplus(-(fl_ref[0, pl.ds(r0, tb), :] + bf_ref[...]))) + carry
            o_ref[0, pl.ds(r0, tb), :] = blk
            return blk[tb - 1:tb, :]

        lax.fori_loop(0, seq // tb, step, jnp.zeros((1, LANES), F32))

    return pl.pallas_call(
        body, grid=(bsz,),
        in_specs=[pl.BlockSpec((1, seq, LANES), lambda b: (b, 0, 0)), pl.BlockSpec((1, LANES), lambda b: (0, 0))],
        out_specs=pl.BlockSpec((1, seq, LANES), lambda b: (b, 0, 0)),
        out_shape=jax.ShapeDtypeStruct(fl.shape, F32), name=name, compiler_params=_arb(1),
    )(fl, bf)


def _cum_bwd(dck, dcq, fl, bf, np_, name, tb=256):
    bsz, seq, _ = fl.shape
    tb = _pick(seq, (tb, 128))
    nb = seq // tb

    def body(dck_ref, dcq_ref, fl_ref, bf_ref, o_ref, db_ref):
        @pl.when(pl.program_id(0) == 0)
        def _():
            db_ref[...] = jnp.zeros_like(db_ref)

        tri = _tri(tb, lambda a, b: b >= a)
        lane = lax.broadcasted_iota(jnp.int32, (1, LANES), 1)

        def step(jj, carry):
            tail, tot = carry
            r0 = pl.multiple_of((nb - 1 - jj) * tb, tb)
            dc = dck_ref[0, pl.ds(r0, tb), :]
            for p in range(np_):
                pair = dcq_ref[0, p, pl.ds(r0, tb), :]
                dc = dc + jnp.where(lane == 2 * p, pair, 0.0) + jnp.where(lane == 2 * p + 1, pltpu.roll(pair, HEAD_DIM, 1), 0.0)
            dlf = _tri_dot3(tri, dc) + tail
            dfl = dlf * _sigmoid(-(fl_ref[0, pl.ds(r0, tb), :] + bf_ref[...]))
            o_ref[0, pl.ds(r0, tb), :] = dfl
            return dlf[0:1, :], tot + jnp.sum(dfl, axis=0, keepdims=True)

        zero = jnp.zeros((1, LANES), F32)
        _, tot = lax.fori_loop(0, nb, step, (zero, zero))
        db_ref[...] += tot

    whole = pl.BlockSpec((1, seq, LANES), lambda b: (b, 0, 0))
    vec = pl.BlockSpec((1, LANES), lambda b: (0, 0))
    return pl.pallas_call(
        body, grid=(bsz,), in_specs=[whole, pl.BlockSpec((1, np_, seq, LANES), lambda b: (b, 0, 0, 0)), whole, vec],
        out_specs=[whole, vec],
        out_shape=[jax.ShapeDtypeStruct(fl.shape, F32), jax.ShapeDtypeStruct((1, LANES), F32)],
        name=name, compiler_params=_arb(1),
    )(dck, dcq, fl, bf)


def _adamw_math(w, g, m, v):
    m = ADAM_B1 * m + (1.0 - ADAM_B1) * g
    v = ADAM_B2 * v + (1.0 - ADAM_B2) * (g * g)
    m_hat = m / (1.0 - ADAM_B1 ** ADAM_STEP)
    v_hat = v / (1.0 - ADAM_B2 ** ADAM_STEP)
    return -ADAM_LR * (m_hat / (jnp.sqrt(v_hat) + ADAM_EPS) + ADAM_WD * w), m, v


def _adamw(gparts, w, m, v, name, tr=512):
    nslots, rows, cols = gparts.shape
    tr = _pick(rows, (tr, 256, 128, 64, 32, 16, 8))

    def body(g_ref, w_ref, m_ref, v_ref, go_ref, d_ref, mo_ref, vo_ref):
        g = g_ref[0].astype(F32)
        for k in range(1, nslots):
            g = g + g_ref[k].astype(F32)
        go_ref[...] = g
        d_ref[...], mo_ref[...], vo_ref[...] = _adamw_math(w_ref[...], g, m_ref[...], v_ref[...])

    blk = pl.BlockSpec((tr, cols), lambda i: (i, 0))
    shp = jax.ShapeDtypeStruct((rows, cols), F32)
    return pl.pallas_call(
        body, grid=(rows // tr,), in_specs=[pl.BlockSpec((nslots, tr, cols), lambda i: (0, i, 0)), blk, blk, blk],
        out_specs=[blk] * 4, out_shape=[shp] * 4, name=name, compiler_params=_arb(1),
    )(gparts, w, m, v)


def _rows128(a):
    return a.reshape(-1, LANES)


def _pad_rows(a, mult):
    extra = (-a.shape[0]) % mult
    return a if extra == 0 else jnp.concatenate([a, jnp.zeros((extra, a.shape[1]), a.dtype)], axis=0)


def _pack(arrs, mult):
    return _pad_rows(jnp.concatenate([_rows128(a) for a in arrs], axis=0), mult)


def _unpack(flat, shapes):
    out, off = [], 0
    for shp in shapes:
        n = 1
        for s in shp:
            n *= s
        out.append(flat[off:off + n // LANES].reshape(shp))
        off += n // LANES
    return out


def _col_blocks(full):
    k, n = full.shape
    return full.reshape(k, N_DEV, n // N_DEV).transpose(1, 0, 2)


def _from_col_blocks(blocks):
    _, k, n = blocks.shape
    return blocks.transpose(1, 0, 2).reshape(k, N_DEV * n)


def kernel(x, c, w_ada, b_ada, g_mix, w_in, b_forget, b_gate, w_branch_sb, w_branch_fox, w_out, g_ffn, w_ffn_gate, w_ffn_up, w_ffn_down, g_final, loss_target, m_w_ada, m_b_ada, m_g_mix, m_w_in, m_b_forget, m_b_gate, m_w_branch_sb, m_w_branch_fox, m_w_out, m_g_ffn, m_w_ffn_gate, m_w_ffn_up, m_w_ffn_down, m_g_final, v_w_ada, v_b_ada, v_g_mix, v_w_in, v_b_forget, v_b_gate, v_w_branch_sb, v_w_branch_fox, v_w_out, v_g_ffn, v_w_ffn_gate, v_w_ffn_up, v_w_ffn_down, v_g_final):
    bsz, seq, d = x.shape
    tok = bsz * seq
    nh = b_forget.shape[-1]
    d_in = w_in.shape[-1] * N_DEV
    d_att = (d_in - nh - 2 * d) // 6
    assert d_att == nh * HEAD_DIM and nh % 2 == 0
    np_ = nh // 2
    d_ff = w_ffn_gate.shape[-1] * N_DEV
    n_mod = w_ada.shape[-1] * N_DEV // d
    me = 4 * lax.axis_index("x") + 2 * lax.axis_index("y") + lax.axis_index("c")
    tq_att = _pick(seq, (512, 256, 128))
    tk_att = tq_att
    tq_fox = tk_fox = tq_att

    big = [w_in[0], w_branch_sb[0], w_branch_fox[0], w_out[0], w_ffn_gate[0], w_ffn_up[0], w_ffn_down[0]]
    big_m = [m_w_in[0], m_w_branch_sb[0], m_w_branch_fox[0], m_w_out[0], m_w_ffn_gate[0], m_w_ffn_up[0], m_w_ffn_down[0]]
    big_v = [v_w_in[0], v_w_branch_sb[0], v_w_branch_fox[0], v_w_out[0], v_w_ffn_gate[0], v_w_ffn_up[0], v_w_ffn_down[0]]

    w_in_g, c_g = _gather_by_chip([big[0].astype(BF16), _rows128(c)], "gather_w_in_c")
    w_in_f = _from_col_blocks(w_in_g)
    w_qkv = w_in_f[:, :6 * d_att]
    w_f = jnp.concatenate([w_in_f[:, 6 * d_att:6 * d_att + nh], jnp.zeros((d, LANES - nh), BF16)], axis=1)
    w_gl = w_in_f[:, 6 * d_att + nh:]

    c_all = c_g.reshape(N_DEV * bsz, d)
    nb_all = N_DEV * bsz
    ada_cols = w_ada.shape[-1]
    b_ada_loc = lax.dynamic_slice(b_ada, (0, me * ada_cols), (1, ada_cols))

    def mod_fn(c_v, w_v, b_v):
        return [jnp.dot(_silu(c_v), w_v, precision=lax.Precision.HIGHEST, preferred_element_type=F32) + b_v]

    (mod_part,) = _small(mod_fn, [c_all, w_ada[0], b_ada_loc], [((nb_all, ada_cols), F32)], "ada_mod")
    mod_all = _from_col_blocks(_exchange([_rows128(mod_part)], True, "gather_mod")[0].reshape(N_DEV, nb_all, ada_cols))
    mod = lax.dynamic_slice(mod_all, (me * bsz, 0), (bsz, n_mod * d))
    shift1, scale1, gate1, shift2, scale2, gate2 = [mod[:, i * d:(i + 1) * d].reshape(bsz, 1, d) for i in range(6)]

    def norm_mod_fn(x_v, sc, sh, g):
        n = x_v * lax.rsqrt(jnp.mean(x_v * x_v, axis=-1, keepdims=True) + RMS_EPS) * g
        return [n * (1.0 + sc) + sh]

    (h,) = _rowmap(norm_mod_fn, [x, scale1, shift1, g_mix], [("row", d, BF16)], "norm1")
    h2d = h.reshape(tok, d)
    qkv = _mm([(h2d, w_qkv)], BF16, "proj_qkv").reshape(bsz, seq, 6 * d_att)
    gl = _mm([(h2d, w_gl)], F32, "proj_gates").reshape(bsz, seq, 2 * d)
    fl = _mm([(h2d, w_f)], F32, "proj_forget").reshape(bsz, seq, LANES)

    bf_pad = jnp.concatenate([b_forget, jnp.zeros((1, LANES - nh), F32)], axis=1)
    cum = _cum_fwd(fl, bf_pad, "cum_fwd")
    cumk = jnp.concatenate([cum[:, :, :nh].transpose(0, 2, 1).reshape(bsz, np_, 2, seq),
                            jnp.zeros((bsz, np_, 6, seq), F32)], axis=2)

    y_sb, tot_sb, gath = _sb_fwd(qkv, np_, tq_att, tk_att, "sb_fwd", gathered=[a.astype(BF16) for a in big[1:]])
    w_sb_f = _from_col_blocks(gath[0])
    w_fx_f = _from_col_blocks(gath[1])
    w_out_f = gath[2].reshape(d, d)
    w_g_f, w_u_f = _from_col_blocks(gath[3]), _from_col_blocks(gath[4])
    w_dn_f = gath[5].reshape(d_ff, d)
    y_fx, lse_fx = _fox_fwd(qkv, cumk, np_, tq_fox, tk_fox, "fox_fwd")

    def merge_fn(gl_v, bg, us, uf):
        gates = _sigmoid(gl_v + bg)
        return [gates[:, :d] * us + gates[:, d:] * uf, us, uf]

    merged, u_sb, u_fx = _rowmap(merge_fn, [gl, b_gate], [("row", d, BF16), ("row", d, F32), ("row", d, F32)], "merge",
                                 products=[(y_sb, w_sb_f), (y_fx, w_fx_f)])

    def resid_norm_fn(x_v, g1, sc, sh, g, mo_v):
        x1_v = x_v + g1 * mo_v
        n = x1_v * lax.rsqrt(jnp.mean(x1_v * x1_v, axis=-1, keepdims=True) + RMS_EPS) * g
        return [x1_v, n * (1.0 + sc) + sh, mo_v]

    x1, h2, mo = _rowmap(resid_norm_fn, [x, gate1, scale2, shift2, g_ffn],
                         [("row", d, F32), ("row", d, BF16), ("row", d, F32)], "norm2", products=[(merged, w_out_f)])
    h2_2d = h2.reshape(tok, d)

    def swiglu_fn(accs, _):
        a, u = accs
        return [_silu(a) * u, a, u]

    f, a_s, u_s = _mm([(h2_2d, w_g_f), (h2_2d, w_u_f)], [BF16, BF16, BF16], "ffn_in", epilogue=swiglu_fn)

    def head_fn(x1_v, g2, gf, tgt, ffn_v):
        x2 = x1_v + g2 * ffn_v
        rstd = lax.rsqrt(jnp.mean(x2 * x2, axis=-1, keepdims=True) + RMS_EPS)
        xh = x2 * rstd
        err = xh * gf - tgt
        loss_rows = 0.5 * jnp.mean(err * err, axis=-1, keepdims=True)
        dy = err * (1.0 / d)
        dxh = dy * gf
        dx2 = rstd * (dxh - xh * jnp.mean(dxh * xh, axis=-1, keepdims=True))
        return [dx2, dx2 * g2, jnp.sum(loss_rows, axis=0, keepdims=True) * jnp.ones((1, LANES), F32),
                jnp.sum(dy * xh, axis=0, keepdims=True), jnp.sum(dx2 * ffn_v, axis=0, keepdims=True)]

    dx2, dffn, loss_vec, dg_final, dgate2 = _rowmap(
        head_fn, [x1, gate2, g_final.reshape(1, d), loss_target],
        [("row", d, F32), ("row", d, BF16), ("global", LANES, F32), ("global", d, F32), ("batch", d, F32)], "head",
        products=[(f.reshape(bsz, seq, d_ff), w_dn_f)])

    dffn2d = dffn.reshape(tok, d)
    dw_dn = _mm_tn(f, dffn2d, "ffn_out_dw")

    def swiglu_bwd_fn(accs, saved):
        df_v, a, u = accs[0], saved[0].astype(F32), saved[1].astype(F32)
        sig = _sigmoid(a)
        return [df_v * u * sig * (1.0 + a * (1.0 - sig)), df_v * a * sig]

    da, du = _mm([(dffn2d, w_dn_f.T)], [BF16, BF16], "ffn_out_dx", extras=[a_s, u_s], epilogue=swiglu_bwd_fn)
    dw_gate, dw_up = _mm_tn(h2_2d, da, "ffn_gate_dw"), _mm_tn(h2_2d, du, "ffn_up_dw")
    def norm2_bwd_fn(x1_v, dx2_v, mo_v, sc, g1, g, dh_gate, dh_up):
        dh_v = dh_gate + dh_up
        rstd = lax.rsqrt(jnp.mean(x1_v * x1_v, axis=-1, keepdims=True) + RMS_EPS)
        xh = x1_v * rstd
        dn = dh_v * (1.0 + sc)
        dxh = dn * g
        dx1 = dx2_v + rstd * (dxh - xh * jnp.mean(dxh * xh, axis=-1, keepdims=True))
        return [dx1, dx1 * g1, jnp.sum(dh_v * (xh * g), axis=0, keepdims=True), jnp.sum(dh_v, axis=0, keepdims=True),
                jnp.sum(dn * xh, axis=0, keepdims=True), jnp.sum(dx1 * mo_v, axis=0, keepdims=True)]

    dx1, dmo, dscale2, dshift2, dg_ffn, dgate1 = _rowmap(
        norm2_bwd_fn, [x1, dx2, mo, scale2, gate1, g_ffn],
        [("row", d, F32), ("row", d, BF16), ("batch", d, F32), ("batch", d, F32), ("global", d, F32), ("batch", d, F32)],
        "norm2_bwd", ts=256,
        products=[(da.reshape(bsz, seq, d_ff), w_g_f, True), (du.reshape(bsz, seq, d_ff), w_u_f, True)])

    dmo2d = dmo.reshape(tok, d)
    dw_out = _mm_tn(merged.reshape(tok, d), dmo2d, "out_proj_dw")

    def merge_bwd_fn(gl_v, us, uf, bg, dm):
        gates = _sigmoid(gl_v + bg)
        gs, gf = gates[:, :d], gates[:, d:]
        dgl = jnp.concatenate([dm * us * gs * (1.0 - gs), dm * uf * gf * (1.0 - gf)], axis=1)
        return [dm * gs, dm * gf, dgl, jnp.sum(dgl, axis=0, keepdims=True)]

    du_sb, du_fx, dgl, db_gate = _rowmap(
        merge_bwd_fn, [gl, u_sb, u_fx, b_gate],
        [("row", d, BF16), ("row", d, BF16), ("row", 2 * d, BF16), ("global", 2 * d, F32)], "merge_bwd", ts=256,
        products=[(dmo, w_out_f, True)])
    du_sb2d, du_fx2d = du_sb.reshape(tok, d), du_fx.reshape(tok, d)
    dw_sb = _mm_tn(y_sb.reshape(tok, d_att), du_sb2d, "branch_sb_dw")
    dw_fx = _mm_tn(y_fx.reshape(tok, d_att), du_fx2d, "branch_fox_dw")
    dy_sb = _mm([(du_sb2d, w_sb_f.T)], F32, "branch_sb_dx").reshape(bsz, seq, d_att)
    dy_fx = _mm([(du_fx2d, w_fx_f.T)], F32, "branch_fox_dx").reshape(bsz, seq, d_att)

    blocks = [_col_blocks(dw_sb), _col_blocks(dw_fx), dw_out.reshape(N_DEV, d // N_DEV, d),
              _col_blocks(dw_gate), _col_blocks(dw_up), dw_dn.reshape(N_DEV, d_ff // N_DEV, d)]
    blocks = [b.astype(BF16) for b in blocks]
    dq_sb, dk_sb, dv_sb, got = _sb_bwd(qkv, dy_sb, tot_sb, np_, tq_att, tk_att, "sb_bwd", swapped=blocks)
    dq_fx, dk_fx, dv_fx, dck, dcq = _fox_bwd(qkv, dy_fx, y_fx, lse_fx, cumk, np_, tq_fox, tk_fox, "fox_bwd")
    dck_rows = dck[:, :, :2, :].reshape(bsz, nh, seq).transpose(0, 2, 1)
    dck_rows = jnp.concatenate([dck_rows, jnp.zeros((bsz, seq, LANES - nh), F32)], axis=2)
    dfl, db_f = _cum_bwd(dck_rows, dcq, fl, bf_pad, np_, "cum_bwd")

    dqkv = jnp.concatenate([dq_sb, dk_sb, dv_sb, dq_fx, dk_fx, dv_fx], axis=2).reshape(tok, 6 * d_att)
    dgl2d, dfl2d = dgl.reshape(tok, 2 * d), dfl.reshape(tok, LANES)
    dw_in = jnp.concatenate([_mm_tn(h2d, dqkv, "proj_qkv_dw"), _mm_tn(h2d, dfl2d, "proj_forget_dw")[:, :nh],
                             _mm_tn(h2d, dgl2d, "proj_gates_dw")], axis=1)
    def norm1_bwd_fn(x_v, dx1_v, sc, g, dh_qkv, dh_gl, dh_fl):
        dh_v = dh_qkv + dh_gl + dh_fl
        rstd = lax.rsqrt(jnp.mean(x_v * x_v, axis=-1, keepdims=True) + RMS_EPS)
        xh = x_v * rstd
        dn = dh_v * (1.0 + sc)
        dxh = dn * g
        dx = dx1_v + rstd * (dxh - xh * jnp.mean(dxh * xh, axis=-1, keepdims=True))
        return [dx, jnp.sum(dh_v * (xh * g), axis=0, keepdims=True), jnp.sum(dh_v, axis=0, keepdims=True),
                jnp.sum(dn * xh, axis=0, keepdims=True)]

    grad_x, dscale1, dshift1, dg_mix, got_in = _rowmap(
        norm1_bwd_fn, [x, dx1, scale1, g_mix],
        [("row", d, F32), ("batch", d, F32), ("batch", d, F32), ("global", d, F32)], "norm1_bwd", ts=256,
        products=[(dqkv.reshape(bsz, seq, 6 * d_att), w_qkv, True), (dgl, w_gl, True), (dfl, w_f, True)],
        swapped=[_col_blocks(dw_in).astype(BF16)])

    dmod = jnp.concatenate([dshift1, dscale1, dgate1, dshift2, dscale2, dgate2], axis=2).reshape(bsz, n_mod * d)
    partial = [dg_mix, db_f, db_gate, dg_ffn, dg_final]
    n_dmod_rows = bsz * n_mod * d // LANES
    small_sent = _pack([dmod] + partial + [loss_vec], 8)
    small_all = _exchange([small_sent], True, "gather_small")[0]
    small_w = [b_ada, g_mix, jnp.concatenate([b_forget, jnp.zeros((1, LANES - nh), F32)], axis=1), b_gate, g_ffn,
               g_final.reshape(1, d)]
    small_m = [m_b_ada, m_g_mix, jnp.concatenate([m_b_forget, jnp.zeros((1, LANES - nh), F32)], axis=1), m_b_gate,
               m_g_ffn, m_g_final.reshape(1, d)]
    small_v = [v_b_ada, v_g_mix, jnp.concatenate([v_b_forget, jnp.zeros((1, LANES - nh), F32)], axis=1), v_b_gate,
               v_g_ffn, v_g_final.reshape(1, d)]
    small_shapes = [a.shape for a in small_w]
    n_ada_rows = n_mod * d // LANES
    n_part_rows = sum(a.shape[1] // LANES for a in partial)
    sw, sm, sv = _pack(small_w, 8), _pack(small_m, 8), _pack(small_v, 8)
    n_small_rows = sw.shape[0]

    def small_fn(all_v, w_v, m_v, v_v):
        g_ada = None
        g_rest = None
        for k in range(N_DEV):
            for b in range(bsz):
                part = all_v[k, b * n_ada_rows:(b + 1) * n_ada_rows]
                g_ada = part if g_ada is None else g_ada + part
            rest = all_v[k, n_dmod_rows:n_dmod_rows + n_part_rows + 1]
            g_rest = rest if g_rest is None else g_rest + rest
        pieces = [g_ada, g_rest[:n_part_rows]]
        if n_small_rows > n_ada_rows + n_part_rows:
            pieces.append(jnp.zeros((n_small_rows - n_ada_rows - n_part_rows, LANES), F32))
        g = jnp.concatenate(pieces, axis=0)
        return [g, *_adamw_math(w_v, g, m_v, v_v), jnp.broadcast_to(g_rest[n_part_rows:], (8, LANES))]

    shp = ((n_small_rows, LANES), F32)
    *small_out, loss_all = _small(small_fn, [small_all, sw, sm, sv], [shp] * 4 + [((8, LANES), F32)], "small_update")
    loss = loss_all[0, 0]
    small_g, small_d, small_nm, small_nv = [_unpack(o, small_shapes) for o in small_out]

    def fix_small(lst):
        b_ada_o, g_mix_o, b_f_o, b_gate_o, g_ffn_o, g_final_o = lst
        return [b_ada_o, g_mix_o, b_f_o[:, :nh], b_gate_o, g_ffn_o, g_final_o.reshape(d)]

    small_g, small_d, small_nm, small_nv = [fix_small(l) for l in (small_g, small_d, small_nm, small_nv)]

    dmod_all = small_all[:, :n_dmod_rows].reshape(nb_all, n_mod * d)
    dmod_cols = lax.dynamic_slice(dmod_all, (0, me * ada_cols), (nb_all, ada_cols))

    def ada_dw_fn(c_v, dm_v):
        return [lax.dot_general(_silu(c_v), dm_v, (((0,), (0,)), ((), ())), precision=lax.Precision.HIGHEST,
                                preferred_element_type=F32)]

    (dw_ada,) = _small(ada_dw_fn, [c_all, dmod_cols], [((d, ada_cols), F32)], "ada_dw")
    ada_out = _adamw(dw_ada[None], w_ada[0], m_w_ada[0], v_w_ada[0], "adamw_ada")
    ada_g, ada_d, ada_nm, ada_nv = [o[None] for o in ada_out]

    got = got_in + list(got)
    names = ["w_in", "w_sb", "w_fox", "w_out", "w_gate", "w_up", "w_down"]
    big_out = [_adamw(g, w, m, v, "adamw_" + n, tr=256) for g, w, m, v, n in zip(got, big, big_m, big_v, names)]
    big_g, big_d, big_nm, big_nv = [[o[i][None] for o in big_out] for i in range(4)]

    def ordered(ada, small, bigs):
        b_ada_o, g_mix_o, b_f_o, b_gate_o, g_ffn_o, g_final_o = small
        w_in_o, w_sb_o, w_fx_o, w_out_o, w_gate_o, w_up_o, w_dn_o = bigs
        return [ada, b_ada_o, g_mix_o, w_in_o, b_f_o, b_gate_o, w_sb_o, w_fx_o, w_out_o, g_ffn_o, w_gate_o, w_up_o,
                w_dn_o, g_final_o]

    return (loss, grad_x, *ordered(ada_g, small_g, big_g), *ordered(ada_d, small_d, big_d),
            *ordered(ada_nm, small_nm, big_nm), *ordered(ada_nv, small_nv, big_nv))
```
